```python
import math
import jax, jax.numpy as jnp
from jax import lax
import numpy as np

D_MODEL = 1024
BATCH = 2
SEQ = 8192
DEPTH = 4

GRID_W = 64
CTX_LEN = 256
HEAD_DIM = 64
ATTN_HEADS = 8
ATTN_KV_HEADS = 2
GQA_GROUP = ATTN_HEADS // ATTN_KV_HEADS
ATTN_WIDTH = ATTN_HEADS * HEAD_DIM
KV_WIDTH = ATTN_KV_HEADS * HEAD_DIM
HYENA_WIDTH = D_MODEL // 4
NA_HEADS = 4
NA_WIDTH = NA_HEADS * HEAD_DIM
MIX_WIDTH = ATTN_WIDTH + HYENA_WIDTH + NA_WIDTH
IN_WIDTH = ATTN_WIDTH + 2 * KV_WIDTH + 3 * HYENA_WIDTH + 3 * NA_WIDTH
FFN_HIDDEN = 2816
Q_BLOCK = 128
NA_ROWS = 8
NA_COLS = 16
SHORT_CONV = 3
FILTER_BANDS = 16
FILTER_EMB = 2 * FILTER_BANDS + 1
FILTER_HIDDEN = 64
DECAY_TARGET = 1e-2
FAST_DECAY_PCT = 0.3
SLOW_DECAY_PCT = 1.5
ROPE_THETA = 10000.0
EPS = 1e-6
N_MOD = 9

kernel_name = 'hybrid_attn_hyena_natten_dit'


def rms_norm(x, g):
    xf = x.astype(jnp.float32)
    y = xf * lax.rsqrt(jnp.mean(xf * xf, axis=-1, keepdims=True) + EPS)
    return (y * g.astype(jnp.float32)).astype(x.dtype)


def modulate(n, shift, scale):
    return n * (1 + scale) + shift


def adaln_params(cond, w, b):
    m = jax.nn.silu(cond) @ w + b
    return jnp.split(m, N_MOD, axis=-1)


def swiglu(x, wg, wu, wd):
    return (jax.nn.silu(x @ wg) * (x @ wu)) @ wd


def to_heads(t, n_heads):
    b, n, _ = t.shape
    return t.reshape(b, n, n_heads, HEAD_DIM).transpose(0, 2, 1, 3)


def from_heads(t):
    b, h, n, dh = t.shape
    return t.transpose(0, 2, 1, 3).reshape(b, n, h * dh)


def split_proj(p):
    sizes = [ATTN_WIDTH, KV_WIDTH, KV_WIDTH, 3 * HYENA_WIDTH, NA_WIDTH, NA_WIDTH, NA_WIDTH]
    return jnp.split(p, np.cumsum(sizes)[:-1].tolist(), axis=-1)


def grid_rope_tables(n, dtype):
    t = jnp.arange(n, dtype=jnp.int32)
    rows = (t // GRID_W).astype(jnp.float32)
    cols = (t % GRID_W).astype(jnp.float32)
    nf = HEAD_DIM // 4
    inv = ROPE_THETA ** (-jnp.arange(nf, dtype=jnp.float32) / nf)
    ang = jnp.stack([rows[:, None] * inv, cols[:, None] * inv], axis=1)
    return jnp.cos(ang).astype(dtype), jnp.sin(ang).astype(dtype)


def axial_rope(x, cos, sin):
    xs = x.reshape(x.shape[:-1] + (2, 2, HEAD_DIM // 4))
    x1 = xs[..., 0, :]
    x2 = xs[..., 1, :]
    out = jnp.stack([x1 * cos - x2 * sin, x1 * sin + x2 * cos], axis=-2)
    return out.reshape(x.shape)


def dense_attention(q, k, v):
    s = jnp.einsum('bhgqd,bhkd->bhgqk', q, k).astype(jnp.float32) * (HEAD_DIM ** -0.5)
    p = jax.nn.softmax(s, axis=-1).astype(v.dtype)
    return jnp.einsum('bhgqk,bhkd->bhgqd', p, v)


def gqa_latent(q, k, v, kc, vc):
    b, hkv, g, s, dh = q.shape
    k_all = jnp.concatenate([kc, k], axis=2)
    v_all = jnp.concatenate([vc, v], axis=2)
    nb = s // Q_BLOCK
    qb = jnp.moveaxis(q.reshape(b, hkv, g, nb, Q_BLOCK, dh), 3, 0)
    o = lax.map(lambda qblk: dense_attention(qblk, k_all, v_all), qb)
    return jnp.moveaxis(o, 0, 3).reshape(b, hkv, g, s, dh)


def neighbourhood_attention(q, k, v, kc, vc, rpb):
    b, h, s, dh = q.shape
    rows = s // GRID_W
    nr = min(NA_ROWS, rows)
    kg = k.reshape(b, h, rows, GRID_W, dh)
    vg = v.reshape(b, h, rows, GRID_W, dh)
    q_rows = jnp.moveaxis(q.reshape(b, h, rows, GRID_W, dh), 2, 0)
    cols = jnp.arange(GRID_W)
    col_start = jnp.clip(cols - NA_COLS // 2, 0, GRID_W - NA_COLS)
    col_idx = col_start[:, None] + jnp.arange(NA_COLS)[None, :]
    rpb_cols = rpb[:, :, col_idx - cols[:, None] + NA_COLS - 1]
    scale = HEAD_DIM ** -0.5
    n_nb = nr * NA_COLS

    def row_block(args):
        r, q_row = args
        rs = jnp.clip(r - NA_ROWS // 2, 0, rows - nr)
        k_sel = lax.dynamic_slice_in_dim(kg, rs, nr, axis=2)[:, :, :, col_idx]
        v_sel = lax.dynamic_slice_in_dim(vg, rs, nr, axis=2)[:, :, :, col_idx]
        row_off = rs + jnp.arange(nr) - r + NA_ROWS - 1
        bias = jnp.take(rpb_cols, row_off, axis=1).transpose(0, 2, 1, 3)
        s_nb = jnp.einsum('bhqd,bhrqcd->bhqrc', q_row, k_sel).astype(jnp.float32) * scale
        s_nb = (s_nb + bias[None].astype(jnp.float32)).reshape(b, h, GRID_W, n_nb)
        s_ctx = jnp.einsum('bhqd,bhkd->bhqk', q_row, kc).astype(jnp.float32) * scale
        p = jax.nn.softmax(jnp.concatenate([s_nb, s_ctx], axis=-1), axis=-1).astype(v.dtype)
        p_nb = p[..., :n_nb].reshape(b, h, GRID_W, nr, NA_COLS)
        return (jnp.einsum('bhqrc,bhrqcd->bhqd', p_nb, v_sel)
                + jnp.einsum('bhqk,bhkd->bhqd', p[..., n_nb:], vc))

    o = lax.map(row_block, (jnp.arange(rows, dtype=jnp.int32), q_rows))
    return jnp.moveaxis(o, 0, 2).reshape(b, h, s, dh)


def implicit_filters(n, w1, b1, w2, b2, w3, freq):
    pos = jnp.arange(n, dtype=jnp.float32)
    t = pos / max(n - 1, 1)
    bands = jnp.linspace(1e-4, FILTER_BANDS - 1, FILTER_BANDS, dtype=jnp.float32)
    ang = (2.0 * math.pi / n) * pos[:, None] * bands[None, :]
    feats = jnp.concatenate([t[:, None], jnp.cos(ang), -jnp.sin(ang)], axis=-1)
    hdn = jnp.sin(freq * (feats @ w1 + b1))
    hdn = jnp.sin(freq * (hdn @ w2 + b2))
    filt = (hdn @ w3).astype(jnp.float32).reshape(n, 2, HYENA_WIDTH)
    deltas = jnp.linspace(math.log(DECAY_TARGET) / SLOW_DECAY_PCT, math.log(DECAY_TARGET) / FAST_DECAY_PCT,
                          HYENA_WIDTH, dtype=jnp.float32)
    decay = jnp.exp(-t[:, None] * jnp.abs(deltas)[None, :])
    return filt * decay[:, None, :]


def hyena_mixer(u, conv_w, conv_b, w1, b1, w2, b2, w3, freq, skip):
    b, n, ch = u.shape
    uc = lax.conv_general_dilated(u, conv_w[:, None, :], window_strides=(1,), padding=[(1, 1)],
                                  dimension_numbers=('NWC', 'WIO', 'NWC'), feature_group_count=ch) + conv_b
    x0, x1, v = jnp.split(uc, 3, axis=-1)
    filt = implicit_filters(n, w1, b1, w2, b2, w3, freq)
    k = jnp.concatenate([filt[:, 0], jnp.zeros((1, HYENA_WIDTH), jnp.float32), filt[:0:-1, 1]], axis=0)
    k = k / jnp.sum(jnp.abs(k), axis=0, keepdims=True)
    z = (v * x1).astype(jnp.float32)
    y = jnp.fft.irfft(jnp.fft.rfft(z, n=2 * n, axis=1) * jnp.fft.rfft(k, n=2 * n, axis=0)[None],
                      n=2 * n, axis=1)[:, :n]
    y = y + z * skip.astype(jnp.float32)
    return y.astype(u.dtype) * x0


def setup_inputs(seed: int = 0) -> dict:
    key = jax.random.key(seed)
    keys = iter(jax.random.split(key, 40))
    D = D_MODEL

    def nrm(shape, s):
        return jax.random.normal(next(keys), shape, jnp.float32) * s

    return {
        'x': nrm((BATCH, SEQ, D), 1.0),
        'c': nrm((BATCH, D), 1.0),
        'ctx': nrm((BATCH, CTX_LEN, D), 1.0),
        'c_ctx': nrm((D,), 1.0),
        'w_ada': nrm((DEPTH, D, N_MOD * D), 0.5 * D ** -0.5),
        'b_ada': nrm((DEPTH, N_MOD * D), 0.01),
        'g_ffn1': 1.0 + nrm((DEPTH, D), 0.02),
        'w_ffn1_gate': nrm((DEPTH, D, FFN_HIDDEN), D ** -0.5),
        'w_ffn1_up': nrm((DEPTH, D, FFN_HIDDEN), D ** -0.5),
        'w_ffn1_down': nrm((DEPTH, FFN_HIDDEN, D), FFN_HIDDEN ** -0.5),
        'g_mix': 1.0 + nrm((DEPTH, D), 0.02),
        'w_in': nrm((DEPTH, D, IN_WIDTH), D ** -0.5),
        'w_out': nrm((DEPTH, MIX_WIDTH, D), MIX_WIDTH ** -0.5),
        'g_q_attn': 1.0 + nrm((DEPTH, HEAD_DIM), 0.02),
        'g_k_attn': 1.0 + nrm((DEPTH, HEAD_DIM), 0.02),
        'conv_w': nrm((DEPTH, SHORT_CONV, 3 * HYENA_WIDTH), SHORT_CONV ** -0.5),
        'conv_b': nrm((DEPTH, 3 * HYENA_WIDTH), 0.01),
        'filt_w1': nrm((DEPTH, FILTER_EMB, FILTER_HIDDEN), FILTER_EMB ** -0.5),
        'filt_b1': nrm((DEPTH, FILTER_HIDDEN), 0.02),
        'filt_w2': nrm((DEPTH, FILTER_HIDDEN, FILTER_HIDDEN), FILTER_HIDDEN ** -0.5),
        'filt_b2': nrm((DEPTH, FILTER_HIDDEN), 0.02),
        'filt_w3': nrm((DEPTH, FILTER_HIDDEN, 2 * HYENA_WIDTH), FILTER_HIDDEN ** -0.5),
        'filt_freq': 1.0 + nrm((DEPTH, FILTER_HIDDEN), 0.02),
        'hyena_skip': nrm((DEPTH, HYENA_WIDTH), 1.0),
        'g_q_na': 1.0 + nrm((DEPTH, HEAD_DIM), 0.02),
        'g_k_na': 1.0 + nrm((DEPTH, HEAD_DIM), 0.02),
        'na_rpb': nrm((DEPTH, NA_HEADS, 2 * NA_ROWS - 1, 2 * NA_COLS - 1), 0.1),
        'g_ffn2': 1.0 + nrm((DEPTH, D), 0.02),
        'w_ffn2_gate': nrm((DEPTH, D, FFN_HIDDEN), D ** -0.5),
        'w_ffn2_up': nrm((DEPTH, D, FFN_HIDDEN), D ** -0.5),
        'w_ffn2_down': nrm((DEPTH, FFN_HIDDEN, D), FFN_HIDDEN ** -0.5),
    }


def reference(x, c, ctx, c_ctx, w_ada, b_ada, g_ffn1, w_ffn1_gate, w_ffn1_up, w_ffn1_down, g_mix, w_in, w_out,
              g_q_attn, g_k_attn, conv_w, conv_b, filt_w1, filt_b1, filt_w2, filt_b2, filt_w3, filt_freq,
              hyena_skip, g_q_na, g_k_na, na_rpb, g_ffn2, w_ffn2_gate, w_ffn2_up, w_ffn2_down):
    b, s, _ = x.shape
    n_ctx = ctx.shape[1]
    cos, sin = grid_rope_tables(s, x.dtype)
    h, hc = x, ctx
    for l in range(DEPTH):
        last = l == DEPTH - 1
        sh1, sc1, gt1, sh2, sc2, gt2, sh3, sc3, gt3 = [m[:, None, :] for m in adaln_params(c, w_ada[l], b_ada[l])]
        csh1, csc1, cgt1, csh2, csc2, cgt2, csh3, csc3, cgt3 = adaln_params(c_ctx, w_ada[l], b_ada[l])
        ffn1 = (w_ffn1_gate[l], w_ffn1_up[l], w_ffn1_down[l])
        ffn2 = (w_ffn2_gate[l], w_ffn2_up[l], w_ffn2_down[l])
        filt = (filt_w1[l], filt_b1[l], filt_w2[l], filt_b2[l], filt_w3[l], filt_freq[l], hyena_skip[l])

        h = h + 0.5 * gt1 * swiglu(modulate(rms_norm(h, g_ffn1[l]), sh1, sc1), *ffn1)
        hc = hc + 0.5 * cgt1 * swiglu(modulate(rms_norm(hc, g_ffn1[l]), csh1, csc1), *ffn1)

        px = modulate(rms_norm(h, g_mix[l]), sh2, sc2) @ w_in[l]
        pc = modulate(rms_norm(hc, g_mix[l]), csh2, csc2) @ w_in[l]
        aq, ak, av, hy, nq, nk, nv = split_proj(px)
        caq, cak, cav, chy, cnq, cnk, cnv = split_proj(pc)

        kca = rms_norm(to_heads(cak, ATTN_KV_HEADS), g_k_attn[l])
        vca = to_heads(cav, ATTN_KV_HEADS)
        kcn = rms_norm(to_heads(cnk, NA_HEADS), g_k_na[l])
        vcn = to_heads(cnv, NA_HEADS)

        qa = axial_rope(rms_norm(to_heads(aq, ATTN_HEADS), g_q_attn[l]), cos, sin)
        qa = qa.reshape(b, ATTN_KV_HEADS, GQA_GROUP, s, HEAD_DIM)
        ka = axial_rope(rms_norm(to_heads(ak, ATTN_KV_HEADS), g_k_attn[l]), cos, sin)
        ya = gqa_latent(qa, ka, to_heads(av, ATTN_KV_HEADS), kca, vca)
        ya = from_heads(ya.reshape(b, ATTN_HEADS, s, HEAD_DIM))
        yb = hyena_mixer(hy, conv_w[l], conv_b[l], *filt)
        qn = rms_norm(to_heads(nq, NA_HEADS), g_q_na[l])
        kn = rms_norm(to_heads(nk, NA_HEADS), g_k_na[l])
        yc = from_heads(neighbourhood_attention(qn, kn, to_heads(nv, NA_HEADS), kcn, vcn, na_rpb[l]))

        h = h + gt2 * (jnp.concatenate([ya, yb, yc], axis=-1) @ w_out[l])
        h = h + 0.5 * gt3 * swiglu(modulate(rms_norm(h, g_ffn2[l]), sh3, sc3), *ffn2)

        if not last:
            qca = rms_norm(to_heads(caq, ATTN_HEADS), g_q_attn[l]).reshape(b, ATTN_KV_HEADS, GQA_GROUP, n_ctx, HEAD_DIM)
            yca = from_heads(dense_attention(qca, kca, vca).reshape(b, ATTN_HEADS, n_ctx, HEAD_DIM))
            ycb = hyena_mixer(chy, conv_w[l], conv_b[l], *filt)
            qcn = rms_norm(to_heads(cnq, NA_HEADS), g_q_na[l])[:, :, None]
            ycn = from_heads(dense_attention(qcn, kcn, vcn)[:, :, 0])
            hc = hc + cgt2 * (jnp.concatenate([yca, ycb, ycn], axis=-1) @ w_out[l])
            hc = hc + 0.5 * cgt3 * swiglu(modulate(rms_norm(hc, g_ffn2[l]), csh3, csc3), *ffn2)
    return h
```

```python
import functools
import math

import numpy as np
import jax
import jax.numpy as jnp
from jax import lax
from jax.experimental import pallas as pl
from jax.experimental.pallas import tpu as pltpu

F32 = jnp.float32
BF16 = jnp.bfloat16
HIGHEST = lax.Precision.HIGHEST

HEAD_DIM = 64
GRID_W = 64
ATTN_HEADS = 8
ATTN_KV_HEADS = 2
NA_HEADS = 4
NA_ROWS = 8
NA_COLS = 16
FILTER_BANDS = 16
DECAY_TARGET = 1e-2
FAST_DECAY_PCT = 0.3
SLOW_DECAY_PCT = 1.5
ROPE_THETA = 10000.0
EPS = 1e-6
N_MOD = 9

LANES = 128
TOKEN_TILE = 512
FFN_CHUNK = 256
ATTN_TQ = 256
ATTN_TK = 512
NA_BLOCK_ROWS = 8
NA_KEY_ROWS = 16
DFT_N2 = 128
NEG_BIG = -1e30
VMEM_LIMIT = 56 * 1024 * 1024


def _cparams(sem):
    return pltpu.CompilerParams(dimension_semantics=sem, vmem_limit_bytes=VMEM_LIMIT)


def _ada_kernel(c_ref, w_ref, b_ref, o_ref):
    c = c_ref[...]
    s = c * jax.nn.sigmoid(c)
    o_ref[0] = jnp.dot(s, w_ref[0], precision=HIGHEST, preferred_element_type=F32) + b_ref[0]


def _adaln_all(cond8, w_ada, b_ada):
    depth, d, nd = w_ada.shape
    tn = 1024
    return pl.pallas_call(
        _ada_kernel,
        grid=(depth, nd // tn),
        in_specs=[
            pl.BlockSpec((8, d), lambda l, j: (0, 0)),
            pl.BlockSpec((1, d, tn), lambda l, j: (l, 0, j)),
            pl.BlockSpec((1, 1, tn), lambda l, j: (l, 0, j)),
        ],
        out_specs=pl.BlockSpec((1, 8, tn), lambda l, j: (l, 0, j)),
        out_shape=jax.ShapeDtypeStruct((depth, 8, nd), F32),
        compiler_params=_cparams(("arbitrary", "arbitrary")),
        name="adaln",
    )(cond8, w_ada, b_ada.reshape(depth, 1, nd))


def _norm_mod(x, g, sh, sc):
    ms = jnp.mean(x * x, axis=-1, keepdims=True)
    y = x * lax.rsqrt(ms + EPS) * g
    return y * (1.0 + sc) + sh


def _mod_specs(tiles_per_batch, n_batch, base, idxs):
    def spec(which):
        return pl.BlockSpec(
            (1, 1, _D),
            lambda i, w=which: (base + jnp.minimum(i // tiles_per_batch, n_batch) * N_MOD + w, 0, 0))
    return [spec(w) for w in idxs]


_D = 1024


def _ffn_kernel(h_ref, sh_ref, sc_ref, gt_ref, g_ref, wg_ref, wu_ref, wd_ref, o_ref, xb_ref, acc_ref):
    x = h_ref[...]
    xb_ref[...] = _norm_mod(x, g_ref[...], sh_ref[0], sc_ref[0]).astype(BF16)
    acc_ref[...] = jnp.zeros_like(acc_ref)

    def body(k, carry):
        xb = xb_ref[...]
        a = jnp.dot(xb, wg_ref[k], preferred_element_type=F32)
        u = jnp.dot(xb, wu_ref[k], preferred_element_type=F32)
        act = (a * jax.nn.sigmoid(a) * u).astype(BF16)
        acc_ref[...] += jnp.dot(act, wd_ref[k], preferred_element_type=F32)
        return carry

    lax.fori_loop(0, wg_ref.shape[0], body, 0)
    o_ref[...] = x + 0.5 * gt_ref[0] * acc_ref[...]


def _ffn(h, mods, mod_base, which, g, wg3, wu3, wd3, n_rows, tiles_per_batch, n_batch):
    d = h.shape[1]
    tm = TOKEN_TILE
    const3 = lambda i: (0, 0, 0)
    return pl.pallas_call(
        _ffn_kernel,
        grid=(n_rows // tm,),
        in_specs=[pl.BlockSpec((tm, d), lambda i: (i, 0))]
        + _mod_specs(tiles_per_batch, n_batch, mod_base, which)
        + [
            pl.BlockSpec((1, d), lambda i: (0, 0)),
            pl.BlockSpec(wg3.shape, const3, pipeline_mode=pl.Buffered(1)),
            pl.BlockSpec(wu3.shape, const3, pipeline_mode=pl.Buffered(1)),
            pl.BlockSpec(wd3.shape, const3, pipeline_mode=pl.Buffered(1)),
        ],
        out_specs=pl.BlockSpec((tm, d), lambda i: (i, 0)),
        out_shape=jax.ShapeDtypeStruct((n_rows, d), F32),
        scratch_shapes=[pltpu.VMEM((tm, d), BF16), pltpu.VMEM((tm, d), F32)],
        compiler_params=_cparams(("arbitrary",)),
        name="ffn",
    )(h, mods, mods, mods, g, wg3, wu3, wd3)


N_NORM = 1152
N_ROPE = 640
Q_CHUNKS = (0, 1, 2, 3, 5, 6)


def _inproj_kernel(h_ref, sh_ref, sc_ref, g_ref, w_ref, gv_ref, hm_ref, cos_ref, sin_ref,
                   q_ref, k_ref, nq_ref, nk_ref, v_ref, nv_ref, hy_ref):
    xb = _norm_mod(h_ref[...], g_ref[...], sh_ref[0], sc_ref[0]).astype(BF16)
    hm = hm_ref[...]
    cosv = cos_ref[...]
    sinv = sin_ref[...]
    lane = lax.broadcasted_iota(jnp.int32, cosv.shape, 1)
    first_half = (lane % 32) < 16
    outs = []
    for c in range(N_NORM // LANES):
        p = jnp.dot(xb, w_ref[:, c * LANES:(c + 1) * LANES], preferred_element_type=F32)
        p2 = p * p
        hi = p2.astype(BF16)
        lo = (p2 - hi.astype(F32)).astype(BF16)
        ms = jnp.dot(hi, hm, preferred_element_type=F32) + jnp.dot(lo, hm, preferred_element_type=F32)
        y = p * lax.rsqrt(ms + EPS) * gv_ref[:, c * LANES:(c + 1) * LANES]
        if c < N_ROPE // LANES:
            swapped = jnp.where(first_half, pltpu.roll(y, LANES - 16, 1), pltpu.roll(y, 16, 1))
            y = y * cosv + swapped * sinv
        if c in Q_CHUNKS:
            y = y * (HEAD_DIM ** -0.5)
        outs.append(y.astype(BF16))
    for c in range(4):
        q_ref[:, c * LANES:(c + 1) * LANES] = outs[c]
    k_ref[...] = outs[4]
    nq_ref[:, 0:LANES] = outs[5]
    nq_ref[:, LANES:2 * LANES] = outs[6]
    nk_ref[:, 0:LANES] = outs[7]
    nk_ref[:, LANES:2 * LANES] = outs[8]
    v_ref[...] = jnp.dot(xb, w_ref[:, 1152:1280], preferred_element_type=F32).astype(BF16)
    nv_ref[...] = jnp.dot(xb, w_ref[:, 1280:1536], preferred_element_type=F32).astype(BF16)
    hy_ref[...] = jnp.dot(xb, w_ref[:, 1536:2304], preferred_element_type=F32)


def _inproj(h, mods, mod_base, g, w, gvec, headmean, cos_t, sin_t, tiles_per_batch, n_batch):
    n_rows, d = h.shape
    tm = TOKEN_TILE
    row = lambda i: (i, 0)
    const = lambda i: (0, 0)
    widths = (512, 128, 256, 256, 128, 256)
    out_shape = [jax.ShapeDtypeStruct((n_rows, wd), BF16) for wd in widths]
    out_shape.append(jax.ShapeDtypeStruct((n_rows, 768), F32))
    out_specs = [pl.BlockSpec((tm, wd), row) for wd in widths] + [pl.BlockSpec((tm, 768), row)]
    return pl.pallas_call(
        _inproj_kernel,
        grid=(n_rows // tm,),
        in_specs=[pl.BlockSpec((tm, d), row)]
        + _mod_specs(tiles_per_batch, n_batch, mod_base, (3, 4))
        + [
            pl.BlockSpec((1, d), const),
            pl.BlockSpec(w.shape, const, pipeline_mode=pl.Buffered(1)),
            pl.BlockSpec(gvec.shape, const),
            pl.BlockSpec(headmean.shape, const),
            pl.BlockSpec((tm, LANES), row),
            pl.BlockSpec((tm, LANES), row),
        ],
        out_specs=out_specs,
        out_shape=out_shape,
        compiler_params=_cparams(("arbitrary",)),
        name="inproj",
    )(h, mods, mods, g, w, gvec, headmean, cos_t, sin_t)


def _stack_heads(q):
    lane = lax.broadcasted_iota(jnp.int32, q.shape, 1)
    zero = jnp.zeros_like(q)
    return jnp.concatenate([jnp.where(lane < HEAD_DIM, q, zero), jnp.where(lane >= HEAD_DIM, q, zero)], axis=0)


def _unstack_heads(o2, tq):
    lane = lax.broadcasted_iota(jnp.int32, (tq, LANES), 1)
    return jnp.where(lane < HEAD_DIM, o2[:tq], o2[tq:])


def _qk(q2, k):
    return lax.dot_general(q2, k, (((1,), (1,)), ((), ())), preferred_element_type=F32)


def _attn_kernel(q_ref, kc_ref, vc_ref, *rest, n_latent_chunks):
    if n_latent_chunks:
        kl_ref, vl_ref, o_ref = rest
    else:
        (o_ref,) = rest
    tq = q_ref.shape[0]
    q2 = _stack_heads(q_ref[...])
    s = _qk(q2, kc_ref[...])
    m = jnp.max(s, axis=-1, keepdims=True)
    p = jnp.exp(s - m)
    l = jnp.sum(p, axis=-1, keepdims=True)
    acc = jnp.dot(p.astype(BF16), vc_ref[...], preferred_element_type=F32)

    if n_latent_chunks:
        def body(j, carry):
            m, l, acc = carry
            start = pl.multiple_of(j * ATTN_TK, ATTN_TK)
            s = _qk(q2, kl_ref[pl.ds(start, ATTN_TK), :])
            m_new = jnp.maximum(m, jnp.max(s, axis=-1, keepdims=True))
            alpha = jnp.exp(m - m_new)
            p = jnp.exp(s - m_new)
            l = alpha * l + jnp.sum(p, axis=-1, keepdims=True)
            acc = alpha * acc + jnp.dot(p.astype(BF16), vl_ref[pl.ds(start, ATTN_TK), :],
                                        preferred_element_type=F32)
            return m_new, l, acc

        m, l, acc = lax.fori_loop(0, n_latent_chunks, body, (m, l, acc))
    o_ref[...] = _unstack_heads(acc / l, tq).astype(o_ref.dtype)


def _attention(q, k, v, out_rows, n_batch, seq, n_ctx, latent, k_chunked):
    chunks = q.shape[1] // LANES
    ctx_blk0 = n_batch * seq // n_ctx
    kj = (lambda j: j) if k_chunked else (lambda j: 0)
    if latent:
        tq = ATTN_TQ
        grid = (n_batch, chunks, seq // tq)
        qmap = lambda b, j, i: (b * (seq // tq) + i, j)
        kv_specs = [pl.BlockSpec((seq, LANES), lambda b, j, i: (b, kj(j)))] * 2
        kv_args = (k, v)
    else:
        tq = n_ctx
        grid = (n_batch, chunks, 1)
        qmap = lambda b, j, i: (ctx_blk0 + b, j)
        kv_specs, kv_args = [], ()
    ctx_spec = pl.BlockSpec((n_ctx, LANES), lambda b, j, i: (ctx_blk0 + b, kj(j)))
    return pl.pallas_call(
        functools.partial(_attn_kernel, n_latent_chunks=seq // ATTN_TK if latent else 0),
        grid=grid,
        in_specs=[pl.BlockSpec((tq, LANES), qmap), ctx_spec, ctx_spec] + kv_specs,
        out_specs=pl.BlockSpec((tq, LANES), qmap),
        out_shape=jax.ShapeDtypeStruct((out_rows, q.shape[1]), BF16),
        compiler_params=_cparams(("arbitrary", "arbitrary", "arbitrary")),
        name="attn_latent" if latent else "attn_ctx",
    )(q, k, v, *kv_args)


def _na_kernel(q_ref, kc_ref, vc_ref, kl_ref, vl_ref, bias_ref, o_ref, *, grid_rows):
    i = pl.program_id(2)
    tq = q_ref.shape[0]
    nk = NA_KEY_ROWS * GRID_W
    ks = jnp.clip(i * NA_BLOCK_ROWS - NA_ROWS // 2, 0, grid_rows - NA_KEY_ROWS)
    start = pl.multiple_of(ks * GRID_W, GRID_W)
    q2 = _stack_heads(q_ref[...])
    s_nb = _qk(q2, kl_ref[pl.ds(start, nk), :]) + bias_ref[0].reshape(2 * tq, nk)
    s_cx = _qk(q2, kc_ref[...])
    m = jnp.maximum(jnp.max(s_nb, axis=-1, keepdims=True), jnp.max(s_cx, axis=-1, keepdims=True))
    p_nb = jnp.exp(s_nb - m)
    p_cx = jnp.exp(s_cx - m)
    l = jnp.sum(p_nb, axis=-1, keepdims=True) + jnp.sum(p_cx, axis=-1, keepdims=True)
    o = (jnp.dot(p_nb.astype(BF16), vl_ref[pl.ds(start, nk), :], preferred_element_type=F32)
         + jnp.dot(p_cx.astype(BF16), vc_ref[...], preferred_element_type=F32))
    o_ref[...] = _unstack_heads(o / l, tq).astype(o_ref.dtype)


def _na_block_start(blk, grid_rows):
    return int(np.clip(blk * NA_BLOCK_ROWS - NA_ROWS // 2, 0, grid_rows - NA_KEY_ROWS))


def _na_bias_tables(rpb, grid_rows):
    nblk = grid_rows // NA_BLOCK_ROWS
    tabs = []
    for blk in (0, min(1, nblk - 1), nblk - 1):
        r0 = blk * NA_BLOCK_ROWS
        ks = _na_block_start(blk, grid_rows)
        qr = r0 + jnp.arange(NA_BLOCK_ROWS)[:, None, None, None]
        qc = jnp.arange(GRID_W)[None, :, None, None]
        kr = ks + jnp.arange(NA_KEY_ROWS)[None, None, :, None]
        kc = jnp.arange(GRID_W)[None, None, None, :]
        rs = jnp.clip(qr - NA_ROWS // 2, 0, grid_rows - NA_ROWS)
        cs = jnp.clip(qc - NA_COLS // 2, 0, GRID_W - NA_COLS)
        valid = (kr >= rs) & (kr < rs + NA_ROWS) & (kc >= cs) & (kc < cs + NA_COLS)
        ri = jnp.clip(kr - qr + NA_ROWS - 1, 0, 2 * NA_ROWS - 2)
        ci = jnp.clip(kc - qc + NA_COLS - 1, 0, 2 * NA_COLS - 2)
        ri, ci, valid = jnp.broadcast_arrays(ri, ci, valid)
        b = jnp.where(valid[None], rpb[:, ri, ci], NEG_BIG)
        tabs.append(b.reshape(rpb.shape[0], NA_BLOCK_ROWS * GRID_W, NA_KEY_ROWS * GRID_W))
    return jnp.stack(tabs)


def _na_latent(nq, nk, nv, bias, n_rows_out, n_batch, seq, n_ctx):
    chunks = nq.shape[1] // LANES
    tq = NA_BLOCK_ROWS * GRID_W
    nblk = seq // tq
    ctx_blk0 = n_batch * seq // n_ctx
    qmap = lambda b, j, i: (b * nblk + i, j)
    ctx_spec = pl.BlockSpec((n_ctx, LANES), lambda b, j, i: (ctx_blk0 + b, j))
    lat_spec = pl.BlockSpec((seq, LANES), lambda b, j, i: (b, j))
    variant = lambda i: jnp.where(i == 0, 0, jnp.where(i == nblk - 1, 2, 1))
    bias_spec = pl.BlockSpec((1, 2, tq, NA_KEY_ROWS * GRID_W), lambda b, j, i: (variant(i), j, 0, 0))
    return pl.pallas_call(
        functools.partial(_na_kernel, grid_rows=seq // GRID_W),
        grid=(n_batch, chunks, nblk),
        in_specs=[pl.BlockSpec((tq, LANES), qmap), ctx_spec, ctx_spec, lat_spec, lat_spec, bias_spec],
        out_specs=pl.BlockSpec((tq, LANES), qmap),
        out_shape=jax.ShapeDtypeStruct((n_rows_out, nq.shape[1]), BF16),
        compiler_params=_cparams(("arbitrary", "arbitrary", "arbitrary")),
        name="na_latent",
    )(nq, nk, nv, nk, nv, bias)


def _hconv_kernel(x_ref, prev_ref, next_ref, w_ref, b_ref, z_ref, x0_ref, *, tiles_per_seq, n_latent_tiles):
    i = pl.program_id(0)
    x = x_ref[...]
    t = x.shape[0]
    row = lax.broadcasted_iota(jnp.int32, x.shape, 0)
    first = jnp.logical_or(i % tiles_per_seq == 0, i >= n_latent_tiles)
    last = jnp.logical_or(i % tiles_per_seq == tiles_per_seq - 1, i >= n_latent_tiles)
    prev_row = jnp.where(first, 0.0, prev_ref[7:8, :])
    next_row = jnp.where(last, 0.0, next_ref[0:1, :])
    up = jnp.where(row == 0, prev_row, pltpu.roll(x, 1, 0))
    dn = jnp.where(row == t - 1, next_row, pltpu.roll(x, t - 1, 0))
    uc = up * w_ref[0:1, :] + x * w_ref[1:2, :] + dn * w_ref[2:3, :] + b_ref[...]
    cw = uc.shape[1] // 3
    x0_ref[...] = uc[:, :cw]
    z_ref[...] = uc[:, 2 * cw:] * uc[:, cw:2 * cw]


def _hconv(hy, conv_w, conv_b, seq, n_ctx, n_batch):
    n_rows, ch = hy.shape
    tt = n_ctx
    n_tiles = n_rows // tt
    cw = ch // 3
    return pl.pallas_call(
        functools.partial(_hconv_kernel, tiles_per_seq=seq // tt, n_latent_tiles=n_batch * seq // tt),
        grid=(n_tiles,),
        in_specs=[
            pl.BlockSpec((tt, ch), lambda i: (i, 0)),
            pl.BlockSpec((8, ch), lambda i: (jnp.maximum(i * (tt // 8) - 1, 0), 0)),
            pl.BlockSpec((8, ch), lambda i: (jnp.minimum((i + 1) * (tt // 8), n_rows // 8 - 1), 0)),
            pl.BlockSpec((3, ch), lambda i: (0, 0)),
            pl.BlockSpec((1, ch), lambda i: (0, 0)),
        ],
        out_specs=[pl.BlockSpec((tt, cw), lambda i: (i, 0))] * 2,
        out_shape=[jax.ShapeDtypeStruct((n_rows, cw), F32)] * 2,
        compiler_params=_cparams(("arbitrary",)),
        name="hyena_conv",
    )(hy, hy, hy, conv_w, conv_b)


def _hfilt_kernel(f_ref, w1_ref, b1_ref, w2_ref, b2_ref, w3_ref, fr_ref, dl_ref, k_ref, l1_ref, *, n):
    i = pl.program_id(0)
    f = f_ref[...]
    fr = fr_ref[...]
    h1 = jnp.sin(fr * (jnp.dot(f, w1_ref[...], precision=HIGHEST, preferred_element_type=F32) + b1_ref[...]))
    h2 = jnp.sin(fr * (jnp.dot(h1, w2_ref[...], precision=HIGHEST, preferred_element_type=F32) + b2_ref[...]))
    o = jnp.dot(h2, w3_ref[...], precision=HIGHEST, preferred_element_type=F32)
    cw = o.shape[1] // 2
    t = f.shape[0]
    row = i * t + lax.broadcasted_iota(jnp.int32, (t, cw), 0)
    decay = jnp.exp(-f[:, 0:1] * jnp.abs(dl_ref[...]))
    k = jnp.where(row < n, o[:, :cw], o[:, cw:]) * decay
    k = jnp.where(row == n, 0.0, k)
    k_ref[...] = k

    @pl.when(i == 0)
    def _():
        l1_ref[...] = jnp.zeros_like(l1_ref)

    l1_ref[...] += jnp.sum(jnp.abs(k), axis=0, keepdims=True)


def _filter_feats(n):
    rows = np.arange(2 * n)
    pos = np.where(rows <= n, rows, 2 * n - rows).astype(np.float32)
    pos = jnp.asarray(np.minimum(pos, n - 1))
    t = pos / max(n - 1, 1)
    bands = jnp.linspace(1e-4, FILTER_BANDS - 1, FILTER_BANDS, dtype=F32)
    ang = (2.0 * math.pi / n) * pos[:, None] * bands[None, :]
    feats = jnp.concatenate([t[:, None], jnp.cos(ang), -jnp.sin(ang)], axis=-1)
    return jnp.pad(feats, ((0, 0), (0, LANES - feats.shape[1])))


def _pad_to(a, shape):
    return jnp.pad(a, [(0, s - d) for s, d in zip(shape, a.shape)])


def _hfilter(n, w1, b1, w2, b2, w3, freq, deltas):
    feats = _filter_feats(n)
    w1, b1, freq = _pad_to(w1, (LANES, LANES)), _pad_to(b1, (1, LANES)), _pad_to(freq, (1, LANES))
    w2, b2, w3 = _pad_to(w2, (LANES, LANES)), _pad_to(b2, (1, LANES)), _pad_to(w3, (LANES, w3.shape[1]))
    rows, emb = feats.shape
    t = min(1024, rows)
    cw = w3.shape[1] // 2
    const = lambda i: (0, 0)
    full = lambda a: pl.BlockSpec(a.shape, const)
    return pl.pallas_call(
        functools.partial(_hfilt_kernel, n=n),
        grid=(rows // t,),
        in_specs=[pl.BlockSpec((t, emb), lambda i: (i, 0)), full(w1), full(b1), full(w2), full(b2), full(w3),
                  full(freq), full(deltas)],
        out_specs=[pl.BlockSpec((t, cw), lambda i: (i, 0)), pl.BlockSpec((1, cw), const)],
        out_shape=[jax.ShapeDtypeStruct((rows, cw), F32), jax.ShapeDtypeStruct((1, cw), F32)],
        compiler_params=_cparams(("arbitrary",)),
        name="hyena_filter",
    )(feats, w1, b1, w2, b2, w3, freq, deltas)


def _dft_outer_kernel(f_ref, x_ref, o_ref):
    o_ref[0] = jnp.dot(f_ref[...], x_ref[0], precision=HIGHEST, preferred_element_type=F32)


def _dft_outer(fmat, x):
    nb, k, cols = x.shape
    r = fmat.shape[0]
    tc = 4096
    return pl.pallas_call(
        _dft_outer_kernel,
        grid=(nb, cols // tc),
        in_specs=[pl.BlockSpec((r, k), lambda b, j: (0, 0)), pl.BlockSpec((1, k, tc), lambda b, j: (b, 0, j))],
        out_specs=pl.BlockSpec((1, r, tc), lambda b, j: (b, 0, j)),
        out_shape=jax.ShapeDtypeStruct((nb, r, cols), F32),
        compiler_params=_cparams(("arbitrary", "arbitrary")),
        name="dft_outer",
    )(fmat, x)


def _inner_matrices(fc_ref, fs_ref, tc_ref, ts_ref):
    fc, fs = fc_ref[...], fs_ref[...]
    tc, ts = tc_ref[0], ts_ref[0]
    gr = fc * tc - fs * ts
    gi = -(fc * ts + fs * tc)
    return gr, gi


def _real_form(gr, gi):
    return jnp.concatenate([jnp.concatenate([gr, -gi], axis=1), jnp.concatenate([gi, gr], axis=1)], axis=0)


def _dft_inner_filter_kernel(fc_ref, fs_ref, tc_ref, ts_ref, b_ref, o_ref):
    g = _real_form(*_inner_matrices(fc_ref, fs_ref, tc_ref, ts_ref))
    n2, c = b_ref.shape[3], b_ref.shape[4]
    o_ref[0] = jnp.dot(g, b_ref[0, :, 0].reshape(2 * n2, c), precision=HIGHEST, preferred_element_type=F32)


def _dft_inner_conv_kernel(fc_ref, fs_ref, tc_ref, ts_ref, b_ref, kh_ref, sc_ref, o_ref):
    gr, gi = _inner_matrices(fc_ref, fs_ref, tc_ref, ts_ref)
    g = _real_form(gr, gi)
    gh = _real_form(gr.T, -gi.T)
    nb, n2, c = b_ref.shape[0], b_ref.shape[3], b_ref.shape[4]
    kh = kh_ref[0]
    kr, ki = kh[:n2], kh[n2:]
    scale = sc_ref[...]
    for b in range(nb):
        x = jnp.dot(g, b_ref[b, :, 0].reshape(2 * n2, c), precision=HIGHEST, preferred_element_type=F32)
        xr, xi = x[:n2], x[n2:]
        y = jnp.concatenate([(xr * kr - xi * ki) * scale, (xr * ki + xi * kr) * scale], axis=0)
        d = jnp.dot(gh, y, precision=HIGHEST, preferred_element_type=F32)
        o_ref[b, :, 0] = d.reshape(2, n2, c)


def _dft_tables(n1_len):
    n_total = n1_len * DFT_N2
    a = np.arange(DFT_N2)
    ang = 2.0 * np.pi * np.outer(a, a) / DFT_N2
    k1 = np.arange(n1_len)
    tw = 2.0 * np.pi * np.outer(k1, a) / n_total
    o = 2.0 * np.pi * np.outer(k1, k1) / n1_len
    f32 = lambda v: jnp.asarray(v.astype(np.float32))
    outer_fwd = np.concatenate([np.cos(o), -np.sin(o)], axis=0)
    outer_inv = np.concatenate([np.cos(o), -np.sin(o)], axis=1)
    return dict(
        fc=f32(np.cos(ang)), fs=f32(np.sin(ang)),
        tc=f32(np.cos(tw)).reshape(n1_len, 1, DFT_N2), ts=f32(np.sin(tw)).reshape(n1_len, 1, DFT_N2),
        outer_fwd_full=f32(outer_fwd), outer_fwd_half=f32(outer_fwd[:, :n1_len // 2]),
        outer_inv_half=f32(outer_inv[:n1_len // 2]),
    )


def _inner_specs(n1_len):
    sq = pl.BlockSpec((DFT_N2, DFT_N2), lambda k: (0, 0))
    tw = pl.BlockSpec((1, 1, DFT_N2), lambda k: (k, 0, 0))
    return [sq, sq, tw, tw]


def _filter_spectrum(kfull, tabs, n1_len):
    c = kfull.shape[1]
    b = _dft_outer(tabs["outer_fwd_full"], kfull.reshape(1, n1_len, DFT_N2 * c))
    b = b.reshape(1, 2, n1_len, DFT_N2, c)
    return pl.pallas_call(
        _dft_inner_filter_kernel,
        grid=(n1_len,),
        in_specs=_inner_specs(n1_len) + [pl.BlockSpec((1, 2, 1, DFT_N2, c), lambda k: (0, 0, k, 0, 0))],
        out_specs=pl.BlockSpec((1, 2 * DFT_N2, c), lambda k: (k, 0, 0)),
        out_shape=jax.ShapeDtypeStruct((n1_len, 2 * DFT_N2, c), F32),
        compiler_params=_cparams(("arbitrary",)),
        name="dft_inner_filter",
    )(tabs["fc"], tabs["fs"], tabs["tc"], tabs["ts"], b)


def _long_conv(z, khat, scale, tabs, n1_len):
    nb, n, c = z.shape
    bz = _dft_outer(tabs["outer_fwd_half"], z.reshape(nb, n1_len // 2, DFT_N2 * c))
    bz = bz.reshape(nb, 2, n1_len, DFT_N2, c)
    blk = pl.BlockSpec((nb, 2, 1, DFT_N2, c), lambda k: (0, 0, k, 0, 0))
    d = pl.pallas_call(
        _dft_inner_conv_kernel,
        grid=(n1_len,),
        in_specs=_inner_specs(n1_len) + [blk, pl.BlockSpec((1, 2 * DFT_N2, c), lambda k: (k, 0, 0)),
                                         pl.BlockSpec((1, c), lambda k: (0, 0))],
        out_specs=blk,
        out_shape=jax.ShapeDtypeStruct(bz.shape, F32),
        compiler_params=_cparams(("arbitrary",)),
        name="dft_inner_conv",
    )(tabs["fc"], tabs["fs"], tabs["tc"], tabs["ts"], bz, khat, scale)
    y = _dft_outer(tabs["outer_inv_half"], d.reshape(nb, 2 * n1_len, DFT_N2 * c))
    return y.reshape(nb, n, c)


def _hyena_ctx_kernel(ff_ref, fi_ref, k_ref, l1_ref, z_ref, o_ref):
    ff = ff_ref[...]
    n = z_ref.shape[0]
    big = ff.shape[1]
    kh = jnp.dot(ff, k_ref[...], precision=HIGHEST, preferred_element_type=F32)
    x = jnp.dot(ff[:, :n], z_ref[...], precision=HIGHEST, preferred_element_type=F32)
    kr, ki = kh[:big], kh[big:]
    xr, xi = x[:big], x[big:]
    scale = 1.0 / (l1_ref[...] * big)
    y = jnp.concatenate([(xr * kr - xi * ki) * scale, (xr * ki + xi * kr) * scale], axis=0)
    o_ref[...] = jnp.dot(fi_ref[...], y, precision=HIGHEST, preferred_element_type=F32)


def _hyena_ctx(z, kfull, l1, row0_blk, n_batch, n):
    big = 2 * n
    a = np.arange(big)
    ang = 2.0 * np.pi * np.outer(a, a) / big
    ff = jnp.asarray(np.concatenate([np.cos(ang), -np.sin(ang)], axis=0).astype(np.float32))
    fi = jnp.asarray(np.concatenate([np.cos(ang), -np.sin(ang)], axis=1)[:n].astype(np.float32))
    c = z.shape[1]
    const = lambda b: (0, 0)
    return pl.pallas_call(
        _hyena_ctx_kernel,
        grid=(n_batch,),
        in_specs=[pl.BlockSpec(ff.shape, const), pl.BlockSpec(fi.shape, const), pl.BlockSpec(kfull.shape, const),
                  pl.BlockSpec((1, c), const), pl.BlockSpec((n, c), lambda b: (row0_blk + b, 0))],
        out_specs=pl.BlockSpec((n, c), lambda b: (b, 0)),
        out_shape=jax.ShapeDtypeStruct((n_batch * n, c), F32),
        compiler_params=_cparams(("arbitrary",)),
        name="hyena_ctx",
    )(ff, fi, kfull, l1, z)


def _outproj_kernel(h_ref, gt_ref, ya_ref, yconv_ref, z_ref, x0_ref, skip_ref, yc_ref, wa_ref, wb_ref, wc_ref, o_ref):
    yb = ((yconv_ref[...] + z_ref[...] * skip_ref[...]) * x0_ref[...]).astype(BF16)
    mix = (jnp.dot(ya_ref[...], wa_ref[...], preferred_element_type=F32)
           + jnp.dot(yb, wb_ref[...], preferred_element_type=F32)
           + jnp.dot(yc_ref[...], wc_ref[...], preferred_element_type=F32))
    o_ref[...] = h_ref[...] + gt_ref[0] * mix


def _outproj(h, mods, mod_base, ya, yconv, z, x0, skip, yc, wa, wb, wc, n_rows, tiles_per_batch, n_batch):
    d = h.shape[1]
    tm = TOKEN_TILE
    row = lambda i: (i, 0)
    const = lambda i: (0, 0)
    rows = lambda a: pl.BlockSpec((tm, a.shape[1]), row)
    full = lambda a: pl.BlockSpec(a.shape, const)
    return pl.pallas_call(
        _outproj_kernel,
        grid=(n_rows // tm,),
        in_specs=[rows(h)] + _mod_specs(tiles_per_batch, n_batch, mod_base, (5,))
        + [rows(ya), rows(yconv), rows(z), rows(x0), full(skip), rows(yc), full(wa), full(wb), full(wc)],
        out_specs=pl.BlockSpec((tm, d), row),
        out_shape=jax.ShapeDtypeStruct((n_rows, d), F32),
        compiler_params=_cparams(("arbitrary",)),
        name="outproj",
    )(h, mods, ya, yconv, z, x0, skip, yc, wa, wb, wc)


def _rope_tables(seq, n_batch, n_ctx_rows):
    t = jnp.arange(seq, dtype=jnp.int32)
    rows = (t // GRID_W).astype(F32)
    cols = (t % GRID_W).astype(F32)
    nf = HEAD_DIM // 4
    inv = ROPE_THETA ** (-jnp.arange(nf, dtype=F32) / nf)
    ar, ac = rows[:, None] * inv, cols[:, None] * inv
    cos = jnp.concatenate([jnp.cos(ar), jnp.cos(ar), jnp.cos(ac), jnp.cos(ac)], axis=1)
    sin = jnp.concatenate([-jnp.sin(ar), jnp.sin(ar), -jnp.sin(ac), jnp.sin(ac)], axis=1)
    cos = jnp.tile(cos, (n_batch, LANES // HEAD_DIM))
    sin = jnp.tile(sin, (n_batch, LANES // HEAD_DIM))
    cos = jnp.concatenate([cos, jnp.ones((n_ctx_rows, LANES), F32)], axis=0)
    sin = jnp.concatenate([sin, jnp.zeros((n_ctx_rows, LANES), F32)], axis=0)
    return cos, sin


def _head_mean_matrix():
    a = np.arange(LANES)
    return jnp.asarray((a[:, None] // HEAD_DIM == a[None, :] // HEAD_DIM).astype(np.float32) / HEAD_DIM, dtype=BF16)


def _chunk3(w, axis):
    if axis == 1:
        d, hdn = w.shape
        return w.reshape(d, hdn // FFN_CHUNK, FFN_CHUNK).transpose(1, 0, 2).astype(BF16)
    hdn, d = w.shape
    return w.reshape(hdn // FFN_CHUNK, FFN_CHUNK, d).astype(BF16)


def kernel(x, c, ctx, c_ctx, w_ada, b_ada, g_ffn1, w_ffn1_gate, w_ffn1_up, w_ffn1_down, g_mix, w_in, w_out, g_q_attn, g_k_attn, conv_w, conv_b, filt_w1, filt_b1, filt_w2, filt_b2, filt_w3, filt_freq, hyena_skip, g_q_na, g_k_na, na_rpb, g_ffn2, w_ffn2_gate, w_ffn2_up, w_ffn2_down):
    n_batch, seq, d = x.shape
    n_ctx = ctx.shape[1]
    depth = w_ada.shape[0]
    assert d == _D and n_batch + 1 <= 8 and seq % TOKEN_TILE == 0 and seq % (NA_BLOCK_ROWS * GRID_W) == 0
    assert n_ctx == 256 and (n_batch * n_ctx) % TOKEN_TILE == 0
    n_lat = n_batch * seq
    n_all = n_lat + n_batch * n_ctx
    tiles_per_batch = seq // TOKEN_TILE
    hy_w = conv_w.shape[2] // 3
    n1_len = 2 * seq // DFT_N2

    cond8 = jnp.zeros((8, d), F32).at[:n_batch].set(c).at[n_batch].set(c_ctx)
    mods = _adaln_all(cond8, w_ada, b_ada).reshape(depth * 8 * N_MOD, 1, d)
    h = jnp.concatenate([x.reshape(n_lat, d), ctx.reshape(n_batch * n_ctx, d)], axis=0)

    cos_t, sin_t = _rope_tables(seq, n_batch, n_batch * n_ctx)
    headmean = _head_mean_matrix()
    tabs = _dft_tables(n1_len)
    deltas = jnp.linspace(math.log(DECAY_TARGET) / SLOW_DECAY_PCT, math.log(DECAY_TARGET) / FAST_DECAY_PCT,
                          hy_w, dtype=F32).reshape(1, hy_w)

    q_heads = [hd for pair in range(ATTN_HEADS // 2) for hd in (pair, pair + ATTN_HEADS // 2)]
    q_cols = np.concatenate([np.arange(hd * HEAD_DIM, (hd + 1) * HEAD_DIM) for hd in q_heads])
    seg = lambda a, b: np.arange(a, b)
    in_perm = np.concatenate([q_cols, seg(512, 640), seg(1536, 1792), seg(1792, 2048), seg(640, 768),
                              seg(2048, 2304), seg(768, 1536)])

    for l in range(depth):
        last = l == depth - 1
        base = l * 8 * N_MOD
        rep = lambda g, k: jnp.tile(g, k)
        gvec = jnp.concatenate([rep(g_q_attn[l], ATTN_HEADS), rep(g_k_attn[l], ATTN_KV_HEADS),
                                rep(g_q_na[l], NA_HEADS), rep(g_k_na[l], NA_HEADS)]).reshape(1, N_NORM)

        h = _ffn(h, mods, base, (0, 1, 2), g_ffn1[l].reshape(1, d), _chunk3(w_ffn1_gate[l], 1),
                 _chunk3(w_ffn1_up[l], 1), _chunk3(w_ffn1_down[l], 0), n_all, tiles_per_batch, n_batch)

        q, k, nq, nk, v, nv, hy = _inproj(h, mods, base, g_mix[l].reshape(1, d), w_in[l][:, in_perm].astype(BF16),
                                          gvec, headmean, cos_t, sin_t, tiles_per_batch, n_batch)

        out_rows = n_lat if last else n_all
        ya = _attention(q, k, v, out_rows, n_batch, seq, n_ctx, latent=True, k_chunked=False)
        bias = _na_bias_tables(na_rpb[l], seq // GRID_W)
        yc = _na_latent(nq, nk, nv, bias, out_rows, n_batch, seq, n_ctx)

        z, x0 = _hconv(hy, conv_w[l], conv_b[l].reshape(1, -1), seq, n_ctx, n_batch)
        fargs = (filt_w1[l], filt_b1[l].reshape(1, -1), filt_w2[l], filt_b2[l].reshape(1, -1), filt_w3[l],
                 filt_freq[l].reshape(1, -1), deltas)
        kfull, l1 = _hfilter(seq, *fargs)
        khat = _filter_spectrum(kfull, tabs, n1_len)
        yconv = _long_conv(z[:n_lat].reshape(n_batch, seq, hy_w), khat, 1.0 / (l1 * (2 * seq)), tabs, n1_len)
        yconv = yconv.reshape(n_lat, hy_w)

        if not last:
            ya_c = _attention(q, k, v, n_all, n_batch, seq, n_ctx, latent=False, k_chunked=False)
            yc_c = _attention(nq, nk, nv, n_all, n_batch, seq, n_ctx, latent=False, k_chunked=True)
            ya = lax.dynamic_update_slice(ya, ya_c[n_lat:], (n_lat, 0))
            yc = lax.dynamic_update_slice(yc, yc_c[n_lat:], (n_lat, 0))
            kfull_c, l1_c = _hfilter(n_ctx, *fargs)
            yconv_c = _hyena_ctx(z, kfull_c, l1_c, n_lat // n_ctx, n_batch, n_ctx)
            yconv = jnp.concatenate([yconv, yconv_c], axis=0)

        w_o = w_out[l]
        wa = w_o[:512][q_cols].astype(BF16)
        wb = w_o[512:512 + hy_w].astype(BF16)
        wc = w_o[512 + hy_w:].astype(BF16)
        h = _outproj(h, mods, base, ya, yconv, z, x0, hyena_skip[l].reshape(1, hy_w), yc, wa, wb, wc,
                     out_rows, tiles_per_batch, n_batch)
        h = _ffn(h, mods, base, (6, 7, 8), g_ffn2[l].reshape(1, d), _chunk3(w_ffn2_gate[l], 1),
                 _chunk3(w_ffn2_up[l], 1), _chunk3(w_ffn2_down[l], 0), out_rows, tiles_per_batch, n_batch)

    return h[:n_lat].reshape(n_batch, seq, d)
```

```python
import functools
import math

import numpy as np
import jax
import jax.numpy as jnp
from jax import lax
from jax.experimental import pallas as pl
from jax.experimental.pallas import tpu as pltpu

F32 = jnp.float32
BF16 = jnp.bfloat16
HIGHEST = lax.Precision.HIGHEST

HEAD_DIM = 64
GRID_W = 64
ATTN_HEADS = 8
ATTN_KV_HEADS = 2
NA_HEADS = 4
NA_ROWS = 8
NA_COLS = 16
FILTER_BANDS = 16
DECAY_TARGET = 1e-2
FAST_DECAY_PCT = 0.3
SLOW_DECAY_PCT = 1.5
ROPE_THETA = 10000.0
EPS = 1e-6
N_MOD = 9

LANES = 128
TOKEN_TILE = 512
FFN_CHUNK = 256
ATTN_TQ = 256
ATTN_TK = 512
ATTN_UNROLL = 8
NA_BLOCK_ROWS = 8
NA_KEY_ROWS = 16
DFT_N2 = 128
NEG_BIG = -1e30
VMEM_LIMIT = 56 * 1024 * 1024


def _cparams(sem):
    return pltpu.CompilerParams(dimension_semantics=sem, vmem_limit_bytes=VMEM_LIMIT)


def _ada_kernel(c_ref, w_ref, b_ref, o_ref):
    c = c_ref[...]
    s = c * jax.nn.sigmoid(c)
    o_ref[0] = jnp.dot(s, w_ref[0], precision=HIGHEST, preferred_element_type=F32) + b_ref[0]


def _adaln_all(cond8, w_ada, b_ada):
    depth, d, nd = w_ada.shape
    tn = 1024
    return pl.pallas_call(
        _ada_kernel,
        grid=(depth, nd // tn),
        in_specs=[
            pl.BlockSpec((8, d), lambda l, j: (0, 0)),
            pl.BlockSpec((1, d, tn), lambda l, j: (l, 0, j)),
            pl.BlockSpec((1, 1, tn), lambda l, j: (l, 0, j)),
        ],
        out_specs=pl.BlockSpec((1, 8, tn), lambda l, j: (l, 0, j)),
        out_shape=jax.ShapeDtypeStruct((depth, 8, nd), F32),
        compiler_params=_cparams(("arbitrary", "arbitrary")),
        name="adaln",
    )(cond8, w_ada, b_ada.reshape(depth, 1, nd))


def _norm_mod(x, g, sh, sc):
    ms = jnp.mean(x * x, axis=-1, keepdims=True)
    y = x * lax.rsqrt(ms + EPS) * g
    return y * (1.0 + sc) + sh


def _mod_specs(tiles_per_batch, n_batch, base, idxs):
    def spec(which):
        return pl.BlockSpec(
            (1, 1, _D),
            lambda i, w=which: (base + jnp.minimum(i // tiles_per_batch, n_batch) * N_MOD + w, 0, 0))
    return [spec(w) for w in idxs]


_D = 1024


def _ffn_kernel(h_ref, sh_ref, sc_ref, gt_ref, g_ref, wg_ref, wu_ref, wd_ref, o_ref, xb_ref, acc_ref):
    x = h_ref[...]
    xb_ref[...] = _norm_mod(x, g_ref[...], sh_ref[0], sc_ref[0]).astype(BF16)
    acc_ref[...] = jnp.zeros_like(acc_ref)

    def body(k, carry):
        xb = xb_ref[...]
        a = jnp.dot(xb, wg_ref[k], preferred_element_type=F32)
        u = jnp.dot(xb, wu_ref[k], preferred_element_type=F32)
        act = (a * jax.nn.sigmoid(a) * u).astype(BF16)
        acc_ref[...] += jnp.dot(act, wd_ref[k], preferred_element_type=F32)
        return carry

    lax.fori_loop(0, wg_ref.shape[0], body, 0)
    o_ref[...] = x + 0.5 * gt_ref[0] * acc_ref[...]


def _ffn(h, mods, mod_base, which, g, wg3, wu3, wd3, n_rows, tiles_per_batch, n_batch):
    d = h.shape[1]
    tm = TOKEN_TILE
    const3 = lambda i: (0, 0, 0)
    return pl.pallas_call(
        _ffn_kernel,
        grid=(n_rows // tm,),
        in_specs=[pl.BlockSpec((tm, d), lambda i: (i, 0))]
        + _mod_specs(tiles_per_batch, n_batch, mod_base, which)
        + [
            pl.BlockSpec((1, d), lambda i: (0, 0)),
            pl.BlockSpec(wg3.shape, const3, pipeline_mode=pl.Buffered(1)),
            pl.BlockSpec(wu3.shape, const3, pipeline_mode=pl.Buffered(1)),
            pl.BlockSpec(wd3.shape, const3, pipeline_mode=pl.Buffered(1)),
        ],
        out_specs=pl.BlockSpec((tm, d), lambda i: (i, 0)),
        out_shape=jax.ShapeDtypeStruct((n_rows, d), F32),
        scratch_shapes=[pltpu.VMEM((tm, d), BF16), pltpu.VMEM((tm, d), F32)],
        compiler_params=_cparams(("arbitrary",)),
        name="ffn",
    )(h, mods, mods, mods, g, wg3, wu3, wd3)


N_NORM = 1152
N_ROPE = 640
ATTN_Q_CHUNKS = (0, 1, 2, 3)
NA_Q_CHUNKS = (5, 6)
LOG2E = math.log2(math.e)


def _inproj_kernel(h_ref, sh_ref, sc_ref, g_ref, w_ref, gv_ref, hm_ref, cos_ref, sin_ref,
                   q_ref, k_ref, nq_ref, nk_ref, v_ref, nv_ref, hy_ref):
    xb = _norm_mod(h_ref[...], g_ref[...], sh_ref[0], sc_ref[0]).astype(BF16)
    hm = hm_ref[...]
    cosv = cos_ref[...]
    sinv = sin_ref[...]
    lane = lax.broadcasted_iota(jnp.int32, cosv.shape, 1)
    first_half = (lane % 32) < 16
    outs = []
    for c in range(N_NORM // LANES):
        p = jnp.dot(xb, w_ref[:, c * LANES:(c + 1) * LANES], preferred_element_type=F32)
        p2 = p * p
        hi = p2.astype(BF16)
        lo = (p2 - hi.astype(F32)).astype(BF16)
        ms = jnp.dot(hi, hm, preferred_element_type=F32) + jnp.dot(lo, hm, preferred_element_type=F32)
        y = p * lax.rsqrt(ms + EPS) * gv_ref[:, c * LANES:(c + 1) * LANES]
        if c < N_ROPE // LANES:
            swapped = jnp.where(first_half, pltpu.roll(y, LANES - 16, 1), pltpu.roll(y, 16, 1))
            y = y * cosv + swapped * sinv
        if c in ATTN_Q_CHUNKS:
            y = y * (HEAD_DIM ** -0.5 * LOG2E)
        elif c in NA_Q_CHUNKS:
            y = y * (HEAD_DIM ** -0.5)
        outs.append(y.astype(BF16))
    for c in range(4):
        q_ref[:, c * LANES:(c + 1) * LANES] = outs[c]
    k_ref[...] = outs[4]
    nq_ref[:, 0:LANES] = outs[5]
    nq_ref[:, LANES:2 * LANES] = outs[6]
    nk_ref[:, 0:LANES] = outs[7]
    nk_ref[:, LANES:2 * LANES] = outs[8]
    v_ref[...] = jnp.dot(xb, w_ref[:, 1152:1280], preferred_element_type=F32).astype(BF16)
    nv_ref[...] = jnp.dot(xb, w_ref[:, 1280:1536], preferred_element_type=F32).astype(BF16)
    hy_ref[...] = jnp.dot(xb, w_ref[:, 1536:2304], preferred_element_type=F32)


def _inproj(h, mods, mod_base, g, w, gvec, headmean, cos_t, sin_t, tiles_per_batch, n_batch):
    n_rows, d = h.shape
    tm = TOKEN_TILE
    row = lambda i: (i, 0)
    const = lambda i: (0, 0)
    widths = (512, 128, 256, 256, 128, 256)
    out_shape = [jax.ShapeDtypeStruct((n_rows, wd), BF16) for wd in widths]
    out_shape.append(jax.ShapeDtypeStruct((n_rows, 768), F32))
    out_specs = [pl.BlockSpec((tm, wd), row) for wd in widths] + [pl.BlockSpec((tm, 768), row)]
    return pl.pallas_call(
        _inproj_kernel,
        grid=(n_rows // tm,),
        in_specs=[pl.BlockSpec((tm, d), row)]
        + _mod_specs(tiles_per_batch, n_batch, mod_base, (3, 4))
        + [
            pl.BlockSpec((1, d), const),
            pl.BlockSpec(w.shape, const, pipeline_mode=pl.Buffered(1)),
            pl.BlockSpec(gvec.shape, const),
            pl.BlockSpec(headmean.shape, const),
            pl.BlockSpec((tm, LANES), row),
            pl.BlockSpec((tm, LANES), row),
        ],
        out_specs=out_specs,
        out_shape=out_shape,
        compiler_params=_cparams(("arbitrary",)),
        name="inproj",
    )(h, mods, mods, g, w, gvec, headmean, cos_t, sin_t)


def _stack_heads(q):
    lane = lax.broadcasted_iota(jnp.int32, q.shape, 1)
    zero = jnp.zeros_like(q)
    return jnp.concatenate([jnp.where(lane < HEAD_DIM, q, zero), jnp.where(lane >= HEAD_DIM, q, zero)], axis=0)


def _unstack_heads(o2, tq):
    lane = lax.broadcasted_iota(jnp.int32, (tq, LANES), 1)
    return jnp.where(lane < HEAD_DIM, o2[:tq], o2[tq:])


def _qk(q2, k):
    return lax.dot_general(q2, k, (((1,), (1,)), ((), ())), preferred_element_type=F32)


def _attn_kernel(q_ref, kc_ref, vc_ref, *rest, n_latent_chunks, base2):
    if n_latent_chunks:
        kl_ref, vl_ref, o_ref = rest
    else:
        (o_ref,) = rest
    ex = jnp.exp2 if base2 else jnp.exp
    tq = q_ref.shape[0]
    q2 = _stack_heads(q_ref[...])
    s = _qk(q2, kc_ref[...])
    m = jnp.max(s, axis=-1, keepdims=True)
    p = ex(s - m)
    l = jnp.sum(p, axis=-1, keepdims=True)
    acc = jnp.dot(p.astype(BF16), vc_ref[...], preferred_element_type=F32)

    if n_latent_chunks:
        def body(j, carry):
            m, l, acc = carry
            start = pl.multiple_of(j * ATTN_TK, ATTN_TK)
            s = _qk(q2, kl_ref[pl.ds(start, ATTN_TK), :])
            m_new = jnp.maximum(m, jnp.max(s, axis=-1, keepdims=True))
            alpha = ex(m - m_new)
            p = ex(s - m_new)
            l = alpha * l + jnp.sum(p, axis=-1, keepdims=True)
            acc = alpha * acc + jnp.dot(p.astype(BF16), vl_ref[pl.ds(start, ATTN_TK), :],
                                        preferred_element_type=F32)
            return m_new, l, acc

        m, l, acc = lax.fori_loop(0, n_latent_chunks, body, (m, l, acc), unroll=ATTN_UNROLL)
    o_ref[...] = _unstack_heads(acc / l, tq).astype(o_ref.dtype)


def _attention(q, k, v, out_rows, n_batch, seq, n_ctx, latent, k_chunked):
    chunks = q.shape[1] // LANES
    ctx_blk0 = n_batch * seq // n_ctx
    kj = (lambda j: j) if k_chunked else (lambda j: 0)
    if latent:
        tq = ATTN_TQ
        grid = (n_batch, chunks, seq // tq)
        qmap = lambda b, j, i: (b * (seq // tq) + i, j)
        kv_specs = [pl.BlockSpec((seq, LANES), lambda b, j, i: (b, kj(j)))] * 2
        kv_args = (k, v)
    else:
        tq = n_ctx
        grid = (n_batch, chunks, 1)
        qmap = lambda b, j, i: (ctx_blk0 + b, j)
        kv_specs, kv_args = [], ()
    ctx_spec = pl.BlockSpec((n_ctx, LANES), lambda b, j, i: (ctx_blk0 + b, kj(j)))
    return pl.pallas_call(
        functools.partial(_attn_kernel, n_latent_chunks=seq // ATTN_TK if latent else 0, base2=not k_chunked),
        grid=grid,
        in_specs=[pl.BlockSpec((tq, LANES), qmap), ctx_spec, ctx_spec] + kv_specs,
        out_specs=pl.BlockSpec((tq, LANES), qmap),
        out_shape=jax.ShapeDtypeStruct((out_rows, q.shape[1]), BF16),
        compiler_params=_cparams(("arbitrary", "arbitrary", "arbitrary")),
        name="attn_latent" if latent else "attn_ctx",
    )(q, k, v, *kv_args)


def _na_kernel(q_ref, kc_ref, vc_ref, kl_ref, vl_ref, bias_ref, o_ref, *, grid_rows):
    i = pl.program_id(2)
    tq = q_ref.shape[0]
    nk = NA_KEY_ROWS * GRID_W
    ks = jnp.clip(i * NA_BLOCK_ROWS - NA_ROWS // 2, 0, grid_rows - NA_KEY_ROWS)
    start = pl.multiple_of(ks * GRID_W, GRID_W)
    q2 = _stack_heads(q_ref[...])
    s_nb = _qk(q2, kl_ref[pl.ds(start, nk), :]) + bias_ref[0].reshape(2 * tq, nk)
    s_cx = _qk(q2, kc_ref[...])
    m = jnp.maximum(jnp.max(s_nb, axis=-1, keepdims=True), jnp.max(s_cx, axis=-1, keepdims=True))
    p_nb = jnp.exp(s_nb - m)
    p_cx = jnp.exp(s_cx - m)
    l = jnp.sum(p_nb, axis=-1, keepdims=True) + jnp.sum(p_cx, axis=-1, keepdims=True)
    o = (jnp.dot(p_nb.astype(BF16), vl_ref[pl.ds(start, nk), :], preferred_element_type=F32)
         + jnp.dot(p_cx.astype(BF16), vc_ref[...], preferred_element_type=F32))
    o_ref[...] = _unstack_heads(o / l, tq).astype(o_ref.dtype)


def _na_block_start(blk, grid_rows):
    return int(np.clip(blk * NA_BLOCK_ROWS - NA_ROWS // 2, 0, grid_rows - NA_KEY_ROWS))


def _na_bias_tables(rpb, grid_rows):
    nblk = grid_rows // NA_BLOCK_ROWS
    n_heads = rpb.shape[0]
    qc = np.arange(GRID_W)[:, None]
    kc = np.arange(GRID_W)[None, :]
    cs = np.clip(qc - NA_COLS // 2, 0, GRID_W - NA_COLS)
    col_ok = (kc >= cs) & (kc < cs + NA_COLS)
    col_hot = ((kc - qc + NA_COLS - 1)[:, :, None] == np.arange(2 * NA_COLS - 1)) & col_ok[:, :, None]
    by_col = jnp.einsum("abe,hde->hdab", jnp.asarray(col_hot, F32), rpb, precision=HIGHEST)
    tabs = []
    for blk in (0, min(1, nblk - 1), nblk - 1):
        qr = blk * NA_BLOCK_ROWS + np.arange(NA_BLOCK_ROWS)[:, None]
        kr = _na_block_start(blk, grid_rows) + np.arange(NA_KEY_ROWS)[None, :]
        rs = np.clip(qr - NA_ROWS // 2, 0, grid_rows - NA_ROWS)
        row_ok = (kr >= rs) & (kr < rs + NA_ROWS)
        row_hot = ((kr - qr + NA_ROWS - 1)[:, :, None] == np.arange(2 * NA_ROWS - 1)) & row_ok[:, :, None]
        b = jnp.einsum("qkd,hdab->hqakb", jnp.asarray(row_hot, F32), by_col, precision=HIGHEST)
        valid = row_ok[:, None, :, None] & col_ok[None, :, None, :]
        b = jnp.where(jnp.asarray(valid)[None], b, NEG_BIG)
        tabs.append(b.reshape(n_heads, NA_BLOCK_ROWS * GRID_W, NA_KEY_ROWS * GRID_W))
    return jnp.stack(tabs)


def _na_latent(nq, nk, nv, bias, n_rows_out, n_batch, seq, n_ctx):
    chunks = nq.shape[1] // LANES
    tq = NA_BLOCK_ROWS * GRID_W
    nblk = seq // tq
    ctx_blk0 = n_batch * seq // n_ctx
    qmap = lambda b, j, i: (b * nblk + i, j)
    ctx_spec = pl.BlockSpec((n_ctx, LANES), lambda b, j, i: (ctx_blk0 + b, j))
    lat_spec = pl.BlockSpec((seq, LANES), lambda b, j, i: (b, j))
    variant = lambda i: jnp.where(i == 0, 0, jnp.where(i == nblk - 1, 2, 1))
    bias_spec = pl.BlockSpec((1, 2, tq, NA_KEY_ROWS * GRID_W), lambda b, j, i: (variant(i), j, 0, 0))
    return pl.pallas_call(
        functools.partial(_na_kernel, grid_rows=seq // GRID_W),
        grid=(n_batch, chunks, nblk),
        in_specs=[pl.BlockSpec((tq, LANES), qmap), ctx_spec, ctx_spec, lat_spec, lat_spec, bias_spec],
        out_specs=pl.BlockSpec((tq, LANES), qmap),
        out_shape=jax.ShapeDtypeStruct((n_rows_out, nq.shape[1]), BF16),
        compiler_params=_cparams(("arbitrary", "arbitrary", "arbitrary")),
        name="na_latent",
    )(nq, nk, nv, nk, nv, bias)


def _hconv_kernel(x_ref, prev_ref, next_ref, w_ref, b_ref, z_ref, x0_ref, *, tiles_per_seq, n_latent_tiles):
    i = pl.program_id(0)
    x = x_ref[...]
    t = x.shape[0]
    row = lax.broadcasted_iota(jnp.int32, x.shape, 0)
    first = jnp.logical_or(i % tiles_per_seq == 0, i >= n_latent_tiles)
    last = jnp.logical_or(i % tiles_per_seq == tiles_per_seq - 1, i >= n_latent_tiles)
    prev_row = jnp.where(first, 0.0, prev_ref[7:8, :])
    next_row = jnp.where(last, 0.0, next_ref[0:1, :])
    up = jnp.where(row == 0, prev_row, pltpu.roll(x, 1, 0))
    dn = jnp.where(row == t - 1, next_row, pltpu.roll(x, t - 1, 0))
    uc = up * w_ref[0:1, :] + x * w_ref[1:2, :] + dn * w_ref[2:3, :] + b_ref[...]
    cw = uc.shape[1] // 3
    x0_ref[...] = uc[:, :cw]
    z_ref[...] = uc[:, 2 * cw:] * uc[:, cw:2 * cw]


def _hconv(hy, conv_w, conv_b, seq, n_ctx, n_batch):
    n_rows, ch = hy.shape
    tt = n_ctx
    n_tiles = n_rows // tt
    cw = ch // 3
    return pl.pallas_call(
        functools.partial(_hconv_kernel, tiles_per_seq=seq // tt, n_latent_tiles=n_batch * seq // tt),
        grid=(n_tiles,),
        in_specs=[
            pl.BlockSpec((tt, ch), lambda i: (i, 0)),
            pl.BlockSpec((8, ch), lambda i: (jnp.maximum(i * (tt // 8) - 1, 0), 0)),
            pl.BlockSpec((8, ch), lambda i: (jnp.minimum((i + 1) * (tt // 8), n_rows // 8 - 1), 0)),
            pl.BlockSpec((3, ch), lambda i: (0, 0)),
            pl.BlockSpec((1, ch), lambda i: (0, 0)),
        ],
        out_specs=[pl.BlockSpec((tt, cw), lambda i: (i, 0))] * 2,
        out_shape=[jax.ShapeDtypeStruct((n_rows, cw), F32)] * 2,
        compiler_params=_cparams(("arbitrary",)),
        name="hyena_conv",
    )(hy, hy, hy, conv_w, conv_b)


def _hfilt_kernel(f_ref, w1_ref, b1_ref, w2_ref, b2_ref, w3_ref, fr_ref, dl_ref, k_ref, l1_ref, *, n):
    i = pl.program_id(0)
    f = f_ref[...]
    fr = fr_ref[...]
    h1 = jnp.sin(fr * (jnp.dot(f, w1_ref[...], precision=HIGHEST, preferred_element_type=F32) + b1_ref[...]))
    h2 = jnp.sin(fr * (jnp.dot(h1, w2_ref[...], precision=HIGHEST, preferred_element_type=F32) + b2_ref[...]))
    o = jnp.dot(h2, w3_ref[...], precision=HIGHEST, preferred_element_type=F32)
    cw = o.shape[1] // 2
    t = f.shape[0]
    row = i * t + lax.broadcasted_iota(jnp.int32, (t, cw), 0)
    decay = jnp.exp(-f[:, 0:1] * jnp.abs(dl_ref[...]))
    k = jnp.where(row < n, o[:, :cw], o[:, cw:]) * decay
    k = jnp.where(row == n, 0.0, k)
    k_ref[...] = k

    @pl.when(i == 0)
    def _():
        l1_ref[...] = jnp.zeros_like(l1_ref)

    l1_ref[...] += jnp.sum(jnp.abs(k), axis=0, keepdims=True)


def _filter_feats(n):
    rows = np.arange(2 * n)
    pos = np.where(rows <= n, rows, 2 * n - rows).astype(np.float32)
    pos = jnp.asarray(np.minimum(pos, n - 1))
    t = pos / max(n - 1, 1)
    bands = jnp.linspace(1e-4, FILTER_BANDS - 1, FILTER_BANDS, dtype=F32)
    ang = (2.0 * math.pi / n) * pos[:, None] * bands[None, :]
    feats = jnp.concatenate([t[:, None], jnp.cos(ang), -jnp.sin(ang)], axis=-1)
    return jnp.pad(feats, ((0, 0), (0, LANES - feats.shape[1])))


def _pad_to(a, shape):
    return jnp.pad(a, [(0, s - d) for s, d in zip(shape, a.shape)])


def _hfilter(n, w1, b1, w2, b2, w3, freq, deltas):
    feats = _filter_feats(n)
    w1, b1, freq = _pad_to(w1, (LANES, LANES)), _pad_to(b1, (1, LANES)), _pad_to(freq, (1, LANES))
    w2, b2, w3 = _pad_to(w2, (LANES, LANES)), _pad_to(b2, (1, LANES)), _pad_to(w3, (LANES, w3.shape[1]))
    rows, emb = feats.shape
    t = min(1024, rows)
    cw = w3.shape[1] // 2
    const = lambda i: (0, 0)
    full = lambda a: pl.BlockSpec(a.shape, const)
    return pl.pallas_call(
        functools.partial(_hfilt_kernel, n=n),
        grid=(rows // t,),
        in_specs=[pl.BlockSpec((t, emb), lambda i: (i, 0)), full(w1), full(b1), full(w2), full(b2), full(w3),
                  full(freq), full(deltas)],
        out_specs=[pl.BlockSpec((t, cw), lambda i: (i, 0)), pl.BlockSpec((1, cw), const)],
        out_shape=[jax.ShapeDtypeStruct((rows, cw), F32), jax.ShapeDtypeStruct((1, cw), F32)],
        compiler_params=_cparams(("arbitrary",)),
        name="hyena_filter",
    )(feats, w1, b1, w2, b2, w3, freq, deltas)


def _dft_outer_kernel(f_ref, x_ref, o_ref):
    o_ref[0] = jnp.dot(f_ref[...], x_ref[0], precision=HIGHEST, preferred_element_type=F32)


def _dft_outer(fmat, x):
    nb, k, cols = x.shape
    r = fmat.shape[0]
    tc = 4096
    return pl.pallas_call(
        _dft_outer_kernel,
        grid=(nb, cols // tc),
        in_specs=[pl.BlockSpec((r, k), lambda b, j: (0, 0)), pl.BlockSpec((1, k, tc), lambda b, j: (b, 0, j))],
        out_specs=pl.BlockSpec((1, r, tc), lambda b, j: (b, 0, j)),
        out_shape=jax.ShapeDtypeStruct((nb, r, cols), F32),
        compiler_params=_cparams(("arbitrary", "arbitrary")),
        name="dft_outer",
    )(fmat, x)


def _inner_matrices(fc_ref, fs_ref, tc_ref, ts_ref):
    fc, fs = fc_ref[...], fs_ref[...]
    tc, ts = tc_ref[0], ts_ref[0]
    gr = fc * tc - fs * ts
    gi = -(fc * ts + fs * tc)
    return gr, gi


def _real_form(gr, gi):
    return jnp.concatenate([jnp.concatenate([gr, -gi], axis=1), jnp.concatenate([gi, gr], axis=1)], axis=0)


def _dft_inner_filter_kernel(fc_ref, fs_ref, tc_ref, ts_ref, b_ref, o_ref):
    g = _real_form(*_inner_matrices(fc_ref, fs_ref, tc_ref, ts_ref))
    n2, c = b_ref.shape[3], b_ref.shape[4]
    o_ref[0] = jnp.dot(g, b_ref[0, :, 0].reshape(2 * n2, c), precision=HIGHEST, preferred_element_type=F32)


def _dft_inner_conv_kernel(fc_ref, fs_ref, tc_ref, ts_ref, b_ref, kh_ref, sc_ref, o_ref):
    gr, gi = _inner_matrices(fc_ref, fs_ref, tc_ref, ts_ref)
    g = _real_form(gr, gi)
    gh = _real_form(gr.T, -gi.T)
    nb, n2, c = b_ref.shape[0], b_ref.shape[3], b_ref.shape[4]
    kh = kh_ref[0]
    kr, ki = kh[:n2], kh[n2:]
    scale = sc_ref[...]
    for b in range(nb):
        x = jnp.dot(g, b_ref[b, :, 0].reshape(2 * n2, c), precision=HIGHEST, preferred_element_type=F32)
        xr, xi = x[:n2], x[n2:]
        y = jnp.concatenate([(xr * kr - xi * ki) * scale, (xr * ki + xi * kr) * scale], axis=0)
        d = jnp.dot(gh, y, precision=HIGHEST, preferred_element_type=F32)
        o_ref[b, :, 0] = d.reshape(2, n2, c)


def _dft_tables(n1_len):
    n_total = n1_len * DFT_N2
    a = np.arange(DFT_N2)
    ang = 2.0 * np.pi * np.outer(a, a) / DFT_N2
    k1 = np.arange(n1_len)
    tw = 2.0 * np.pi * np.outer(k1, a) / n_total
    o = 2.0 * np.pi * np.outer(k1, k1) / n1_len
    f32 = lambda v: jnp.asarray(v.astype(np.float32))
    outer_fwd = np.concatenate([np.cos(o), -np.sin(o)], axis=0)
    outer_inv = np.concatenate([np.cos(o), -np.sin(o)], axis=1)
    return dict(
        fc=f32(np.cos(ang)), fs=f32(np.sin(ang)),
        tc=f32(np.cos(tw)).reshape(n1_len, 1, DFT_N2), ts=f32(np.sin(tw)).reshape(n1_len, 1, DFT_N2),
        outer_fwd_full=f32(outer_fwd), outer_fwd_half=f32(outer_fwd[:, :n1_len // 2]),
        outer_inv_half=f32(outer_inv[:n1_len // 2]),
    )


def _inner_specs(n1_len):
    sq = pl.BlockSpec((DFT_N2, DFT_N2), lambda k: (0, 0))
    tw = pl.BlockSpec((1, 1, DFT_N2), lambda k: (k, 0, 0))
    return [sq, sq, tw, tw]


def _filter_spectrum(kfull, tabs, n1_len):
    c = kfull.shape[1]
    b = _dft_outer(tabs["outer_fwd_full"], kfull.reshape(1, n1_len, DFT_N2 * c))
    b = b.reshape(1, 2, n1_len, DFT_N2, c)
    return pl.pallas_call(
        _dft_inner_filter_kernel,
        grid=(n1_len,),
        in_specs=_inner_specs(n1_len) + [pl.BlockSpec((1, 2, 1, DFT_N2, c), lambda k: (0, 0, k, 0, 0))],
        out_specs=pl.BlockSpec((1, 2 * DFT_N2, c), lambda k: (k, 0, 0)),
        out_shape=jax.ShapeDtypeStruct((n1_len, 2 * DFT_N2, c), F32),
        compiler_params=_cparams(("arbitrary",)),
        name="dft_inner_filter",
    )(tabs["fc"], tabs["fs"], tabs["tc"], tabs["ts"], b)


def _long_conv(z, khat, scale, tabs, n1_len):
    nb, n, c = z.shape
    bz = _dft_outer(tabs["outer_fwd_half"], z.reshape(nb, n1_len // 2, DFT_N2 * c))
    bz = bz.reshape(nb, 2, n1_len, DFT_N2, c)
    blk = pl.BlockSpec((nb, 2, 1, DFT_N2, c), lambda k: (0, 0, k, 0, 0))
    d = pl.pallas_call(
        _dft_inner_conv_kernel,
        grid=(n1_len,),
        in_specs=_inner_specs(n1_len) + [blk, pl.BlockSpec((1, 2 * DFT_N2, c), lambda k: (k, 0, 0)),
                                         pl.BlockSpec((1, c), lambda k: (0, 0))],
        out_specs=blk,
        out_shape=jax.ShapeDtypeStruct(bz.shape, F32),
        compiler_params=_cparams(("arbitrary",)),
        name="dft_inner_conv",
    )(tabs["fc"], tabs["fs"], tabs["tc"], tabs["ts"], bz, khat, scale)
    y = _dft_outer(tabs["outer_inv_half"], d.reshape(nb, 2 * n1_len, DFT_N2 * c))
    return y.reshape(nb, n, c)


def _hyena_ctx_kernel(ff_ref, fi_ref, k_ref, l1_ref, z_ref, o_ref):
    ff = ff_ref[...]
    n = z_ref.shape[0]
    big = ff.shape[1]
    kh = jnp.dot(ff, k_ref[...], precision=HIGHEST, preferred_element_type=F32)
    x = jnp.dot(ff[:, :n], z_ref[...], precision=HIGHEST, preferred_element_type=F32)
    kr, ki = kh[:big], kh[big:]
    xr, xi = x[:big], x[big:]
    scale = 1.0 / (l1_ref[...] * big)
    y = jnp.concatenate([(xr * kr - xi * ki) * scale, (xr * ki + xi * kr) * scale], axis=0)
    o_ref[...] = jnp.dot(fi_ref[...], y, precision=HIGHEST, preferred_element_type=F32)


def _hyena_ctx(z, kfull, l1, row0_blk, n_batch, n):
    big = 2 * n
    a = np.arange(big)
    ang = 2.0 * np.pi * np.outer(a, a) / big
    ff = jnp.asarray(np.concatenate([np.cos(ang), -np.sin(ang)], axis=0).astype(np.float32))
    fi = jnp.asarray(np.concatenate([np.cos(ang), -np.sin(ang)], axis=1)[:n].astype(np.float32))
    c = z.shape[1]
    const = lambda b: (0, 0)
    return pl.pallas_call(
        _hyena_ctx_kernel,
        grid=(n_batch,),
        in_specs=[pl.BlockSpec(ff.shape, const), pl.BlockSpec(fi.shape, const), pl.BlockSpec(kfull.shape, const),
                  pl.BlockSpec((1, c), const), pl.BlockSpec((n, c), lambda b: (row0_blk + b, 0))],
        out_specs=pl.BlockSpec((n, c), lambda b: (b, 0)),
        out_shape=jax.ShapeDtypeStruct((n_batch * n, c), F32),
        compiler_params=_cparams(("arbitrary",)),
        name="hyena_ctx",
    )(ff, fi, kfull, l1, z)


def _outproj_kernel(h_ref, gt_ref, ya_ref, yconv_ref, z_ref, x0_ref, skip_ref, yc_ref, wa_ref, wb_ref, wc_ref, o_ref):
    yb = ((yconv_ref[...] + z_ref[...] * skip_ref[...]) * x0_ref[...]).astype(BF16)
    mix = (jnp.dot(ya_ref[...], wa_ref[...], preferred_element_type=F32)
           + jnp.dot(yb, wb_ref[...], preferred_element_type=F32)
           + jnp.dot(yc_ref[...], wc_ref[...], preferred_element_type=F32))
    o_ref[...] = h_ref[...] + gt_ref[0] * mix


def _outproj(h, mods, mod_base, ya, yconv, z, x0, skip, yc, wa, wb, wc, n_rows, tiles_per_batch, n_batch):
    d = h.shape[1]
    tm = TOKEN_TILE
    row = lambda i: (i, 0)
    const = lambda i: (0, 0)
    rows = lambda a: pl.BlockSpec((tm, a.shape[1]), row)
    full = lambda a: pl.BlockSpec(a.shape, const)
    return pl.pallas_call(
        _outproj_kernel,
        grid=(n_rows // tm,),
        in_specs=[rows(h)] + _mod_specs(tiles_per_batch, n_batch, mod_base, (5,))
        + [rows(ya), rows(yconv), rows(z), rows(x0), full(skip), rows(yc), full(wa), full(wb), full(wc)],
        out_specs=pl.BlockSpec((tm, d), row),
        out_shape=jax.ShapeDtypeStruct((n_rows, d), F32),
        compiler_params=_cparams(("arbitrary",)),
        name="outproj",
    )(h, mods, ya, yconv, z, x0, skip, yc, wa, wb, wc)


def _rope_tables(seq, n_batch, n_ctx_rows):
    t = jnp.arange(seq, dtype=jnp.int32)
    rows = (t // GRID_W).astype(F32)
    cols = (t % GRID_W).astype(F32)
    nf = HEAD_DIM // 4
    inv = ROPE_THETA ** (-jnp.arange(nf, dtype=F32) / nf)
    ar, ac = rows[:, None] * inv, cols[:, None] * inv
    cos = jnp.concatenate([jnp.cos(ar), jnp.cos(ar), jnp.cos(ac), jnp.cos(ac)], axis=1)
    sin = jnp.concatenate([-jnp.sin(ar), jnp.sin(ar), -jnp.sin(ac), jnp.sin(ac)], axis=1)
    cos = jnp.tile(cos, (n_batch, LANES // HEAD_DIM))
    sin = jnp.tile(sin, (n_batch, LANES // HEAD_DIM))
    cos = jnp.concatenate([cos, jnp.ones((n_ctx_rows, LANES), F32)], axis=0)
    sin = jnp.concatenate([sin, jnp.zeros((n_ctx_rows, LANES), F32)], axis=0)
    return cos, sin


def _head_mean_matrix():
    a = np.arange(LANES)
    return jnp.asarray((a[:, None] // HEAD_DIM == a[None, :] // HEAD_DIM).astype(np.float32) / HEAD_DIM, dtype=BF16)


def _chunk3(w, axis):
    if axis == 1:
        d, hdn = w.shape
        return w.reshape(d, hdn // FFN_CHUNK, FFN_CHUNK).transpose(1, 0, 2).astype(BF16)
    hdn, d = w.shape
    return w.reshape(hdn // FFN_CHUNK, FFN_CHUNK, d).astype(BF16)


def kernel(x, c, ctx, c_ctx, w_ada, b_ada, g_ffn1, w_ffn1_gate, w_ffn1_up, w_ffn1_down, g_mix, w_in, w_out, g_q_attn, g_k_attn, conv_w, conv_b, filt_w1, filt_b1, filt_w2, filt_b2, filt_w3, filt_freq, hyena_skip, g_q_na, g_k_na, na_rpb, g_ffn2, w_ffn2_gate, w_ffn2_up, w_ffn2_down):
    n_batch, seq, d = x.shape
    n_ctx = ctx.shape[1]
    depth = w_ada.shape[0]
    assert d == _D and n_batch + 1 <= 8 and seq % TOKEN_TILE == 0 and seq % (NA_BLOCK_ROWS * GRID_W) == 0
    assert n_ctx == 256 and (n_batch * n_ctx) % TOKEN_TILE == 0
    n_lat = n_batch * seq
    n_all = n_lat + n_batch * n_ctx
    tiles_per_batch = seq // TOKEN_TILE
    hy_w = conv_w.shape[2] // 3
    n1_len = 2 * seq // DFT_N2

    cond8 = jnp.zeros((8, d), F32).at[:n_batch].set(c).at[n_batch].set(c_ctx)
    mods = _adaln_all(cond8, w_ada, b_ada).reshape(depth * 8 * N_MOD, 1, d)
    h = jnp.concatenate([x.reshape(n_lat, d), ctx.reshape(n_batch * n_ctx, d)], axis=0)

    cos_t, sin_t = _rope_tables(seq, n_batch, n_batch * n_ctx)
    headmean = _head_mean_matrix()
    tabs = _dft_tables(n1_len)
    deltas = jnp.linspace(math.log(DECAY_TARGET) / SLOW_DECAY_PCT, math.log(DECAY_TARGET) / FAST_DECAY_PCT,
                          hy_w, dtype=F32).reshape(1, hy_w)

    q_heads = [hd for pair in range(ATTN_HEADS // 2) for hd in (pair, pair + ATTN_HEADS // 2)]
    q_segs = [(hd * HEAD_DIM, (hd + 1) * HEAD_DIM) for hd in q_heads]
    in_segs = q_segs + [(512, 640), (1536, 1792), (1792, 2048), (640, 768), (2048, 2304), (768, 1536)]

    for l in range(depth):
        last = l == depth - 1
        base = l * 8 * N_MOD
        rep = lambda g, k: jnp.tile(g, k)
        gvec = jnp.concatenate([rep(g_q_attn[l], ATTN_HEADS), rep(g_k_attn[l], ATTN_KV_HEADS),
                                rep(g_q_na[l], NA_HEADS), rep(g_k_na[l], NA_HEADS)]).reshape(1, N_NORM)

        h = _ffn(h, mods, base, (0, 1, 2), g_ffn1[l].reshape(1, d), _chunk3(w_ffn1_gate[l], 1),
                 _chunk3(w_ffn1_up[l], 1), _chunk3(w_ffn1_down[l], 0), n_all, tiles_per_batch, n_batch)

        w_in_l = jnp.concatenate([w_in[l][:, a:b] for a, b in in_segs], axis=1).astype(BF16)
        q, k, nq, nk, v, nv, hy = _inproj(h, mods, base, g_mix[l].reshape(1, d), w_in_l,
                                          gvec, headmean, cos_t, sin_t, tiles_per_batch, n_batch)

        out_rows = n_lat if last else n_all
        ya = _attention(q, k, v, out_rows, n_batch, seq, n_ctx, latent=True, k_chunked=False)
        bias = _na_bias_tables(na_rpb[l], seq // GRID_W)
        yc = _na_latent(nq, nk, nv, bias, out_rows, n_batch, seq, n_ctx)

        z, x0 = _hconv(hy, conv_w[l], conv_b[l].reshape(1, -1), seq, n_ctx, n_batch)
        fargs = (filt_w1[l], filt_b1[l].reshape(1, -1), filt_w2[l], filt_b2[l].reshape(1, -1), filt_w3[l],
                 filt_freq[l].reshape(1, -1), deltas)
        kfull, l1 = _hfilter(seq, *fargs)
        khat = _filter_spectrum(kfull, tabs, n1_len)
        yconv = _long_conv(z[:n_lat].reshape(n_batch, seq, hy_w), khat, 1.0 / (l1 * (2 * seq)), tabs, n1_len)
        yconv = yconv.reshape(n_lat, hy_w)

        if not last:
            ya_c = _attention(q, k, v, n_all, n_batch, seq, n_ctx, latent=False, k_chunked=False)
            yc_c = _attention(nq, nk, nv, n_all, n_batch, seq, n_ctx, latent=False, k_chunked=True)
            ya = lax.dynamic_update_slice(ya, ya_c[n_lat:], (n_lat, 0))
            yc = lax.dynamic_update_slice(yc, yc_c[n_lat:], (n_lat, 0))
            kfull_c, l1_c = _hfilter(n_ctx, *fargs)
            yconv_c = _hyena_ctx(z, kfull_c, l1_c, n_lat // n_ctx, n_batch, n_ctx)
            yconv = jnp.concatenate([yconv, yconv_c], axis=0)

        w_o = w_out[l]
        wa = jnp.concatenate([w_o[a:b] for a, b in q_segs], axis=0).astype(BF16)
        wb = w_o[512:512 + hy_w].astype(BF16)
        wc = w_o[512 + hy_w:].astype(BF16)
        h = _outproj(h, mods, base, ya, yconv, z, x0, hyena_skip[l].reshape(1, hy_w), yc, wa, wb, wc,
                     out_rows, tiles_per_batch, n_batch)
        h = _ffn(h, mods, base, (6, 7, 8), g_ffn2[l].reshape(1, d), _chunk3(w_ffn2_gate[l], 1),
                 _chunk3(w_ffn2_up[l], 1), _chunk3(w_ffn2_down[l], 0), out_rows, tiles_per_batch, n_batch)

    return h[:n_lat].reshape(n_batch, seq, d)
```

```python
import functools
import math

import numpy as np
import jax
import jax.numpy as jnp
from jax import lax
from jax.experimental import pallas as pl
from jax.experimental.pallas import tpu as pltpu

F32 = jnp.float32
BF16 = jnp.bfloat16
HIGHEST = lax.Precision.HIGHEST

HEAD_DIM = 64
GRID_W = 64
ATTN_HEADS = 8
ATTN_KV_HEADS = 2
NA_HEADS = 4
NA_ROWS = 8
NA_COLS = 16
FILTER_BANDS = 16
DECAY_TARGET = 1e-2
FAST_DECAY_PCT = 0.3
SLOW_DECAY_PCT = 1.5
ROPE_THETA = 10000.0
EPS = 1e-6
N_MOD = 9

LANES = 128
TOKEN_TILE = 512
FFN_CHUNK = 256
ATTN_TQ = 256
ATTN_TK = 2048
NA_BLOCK_ROWS = 8
NA_KEY_ROWS = 16
DFT_N2 = 128
NEG_BIG = -1e30
VMEM_LIMIT = 56 * 1024 * 1024


def _cparams(sem):
    return pltpu.CompilerParams(dimension_semantics=sem, vmem_limit_bytes=VMEM_LIMIT)


def _ada_kernel(c_ref, w_ref, b_ref, o_ref):
    c = c_ref[...]
    s = c * jax.nn.sigmoid(c)
    o_ref[0] = jnp.dot(s, w_ref[0], precision=HIGHEST, preferred_element_type=F32) + b_ref[0]


def _adaln_all(cond8, w_ada, b_ada):
    depth, d, nd = w_ada.shape
    tn = 1024
    return pl.pallas_call(
        _ada_kernel,
        grid=(depth, nd // tn),
        in_specs=[
            pl.BlockSpec((8, d), lambda l, j: (0, 0)),
            pl.BlockSpec((1, d, tn), lambda l, j: (l, 0, j)),
            pl.BlockSpec((1, 1, tn), lambda l, j: (l, 0, j)),
        ],
        out_specs=pl.BlockSpec((1, 8, tn), lambda l, j: (l, 0, j)),
        out_shape=jax.ShapeDtypeStruct((depth, 8, nd), F32),
        compiler_params=_cparams(("arbitrary", "arbitrary")),
        name="adaln",
    )(cond8, w_ada, b_ada.reshape(depth, 1, nd))


def _norm_mod(x, g, sh, sc):
    ms = jnp.mean(x * x, axis=-1, keepdims=True)
    y = x * lax.rsqrt(ms + EPS) * g
    return y * (1.0 + sc) + sh


def _mod_specs(tiles_per_batch, n_batch, base, idxs):
    def spec(which):
        return pl.BlockSpec(
            (1, 1, _D),
            lambda i, w=which: (base + jnp.minimum(i // tiles_per_batch, n_batch) * N_MOD + w, 0, 0))
    return [spec(w) for w in idxs]


_D = 1024


def _ffn_kernel(h_ref, sh_ref, sc_ref, gt_ref, g_ref, wg_ref, wu_ref, wd_ref, o_ref, xb_ref, acc_ref):
    x = h_ref[...]
    xb_ref[...] = _norm_mod(x, g_ref[...], sh_ref[0], sc_ref[0]).astype(BF16)

    def chunk(k):
        xb = xb_ref[...]
        a = jnp.dot(xb, wg_ref[k], preferred_element_type=F32)
        u = jnp.dot(xb, wu_ref[k], preferred_element_type=F32)
        act = (a * jax.nn.sigmoid(a) * u).astype(BF16)
        return jnp.dot(act, wd_ref[k], preferred_element_type=F32)

    def body(k, carry):
        acc_ref[...] += chunk(k)
        return carry

    acc_ref[...] = chunk(0)
    lax.fori_loop(1, wg_ref.shape[0], body, 0)
    o_ref[...] = x + 0.5 * gt_ref[0] * acc_ref[...]


def _ffn(h, mods, mod_base, which, g, wg3, wu3, wd3, n_rows, tiles_per_batch, n_batch):
    d = h.shape[1]
    tm = TOKEN_TILE
    const3 = lambda i: (0, 0, 0)
    return pl.pallas_call(
        _ffn_kernel,
        grid=(n_rows // tm,),
        in_specs=[pl.BlockSpec((tm, d), lambda i: (i, 0))]
        + _mod_specs(tiles_per_batch, n_batch, mod_base, which)
        + [
            pl.BlockSpec((1, d), lambda i: (0, 0)),
            pl.BlockSpec(wg3.shape, const3, pipeline_mode=pl.Buffered(1)),
            pl.BlockSpec(wu3.shape, const3, pipeline_mode=pl.Buffered(1)),
            pl.BlockSpec(wd3.shape, const3, pipeline_mode=pl.Buffered(1)),
        ],
        out_specs=pl.BlockSpec((tm, d), lambda i: (i, 0)),
        out_shape=jax.ShapeDtypeStruct((n_rows, d), F32),
        scratch_shapes=[pltpu.VMEM((tm, d), BF16), pltpu.VMEM((tm, d), F32)],
        compiler_params=_cparams(("arbitrary",)),
        name="ffn",
    )(h, mods, mods, mods, g, wg3, wu3, wd3)


N_NORM = 1152
N_ROPE = 640
ATTN_Q_CHUNKS = (0, 1, 2, 3)
NA_Q_CHUNKS = (5, 6)
LOG2E = math.log2(math.e)


def _inproj_kernel(h_ref, sh_ref, sc_ref, g_ref, w_ref, gv_ref, hm_ref, cos_ref, sin_ref,
                   q_ref, k_ref, nq_ref, nk_ref, v_ref, nv_ref, hy_ref):
    xb = _norm_mod(h_ref[...], g_ref[...], sh_ref[0], sc_ref[0]).astype(BF16)
    hm = hm_ref[...]
    cosv = cos_ref[...]
    sinv = sin_ref[...]
    lane = lax.broadcasted_iota(jnp.int32, cosv.shape, 1)
    first_half = (lane % 32) < 16
    outs = []
    p_norm = jnp.dot(xb, w_ref[:, 0:N_NORM], preferred_element_type=F32)
    for c in range(N_NORM // LANES):
        p = p_norm[:, c * LANES:(c + 1) * LANES]
        ms = jnp.dot((p * p).astype(BF16), hm, preferred_element_type=F32)
        y = p * lax.rsqrt(ms + EPS) * gv_ref[:, c * LANES:(c + 1) * LANES]
        if c < N_ROPE // LANES:
            swapped = jnp.where(first_half, pltpu.roll(y, LANES - 16, 1), pltpu.roll(y, 16, 1))
            y = y * cosv + swapped * sinv
        if c in ATTN_Q_CHUNKS:
            y = y * (HEAD_DIM ** -0.5 * LOG2E)
        elif c in NA_Q_CHUNKS:
            y = y * (HEAD_DIM ** -0.5)
        outs.append(y.astype(BF16))
    for c in range(4):
        q_ref[:, c * LANES:(c + 1) * LANES] = outs[c]
    k_ref[...] = outs[4]
    nq_ref[:, 0:LANES] = outs[5]
    nq_ref[:, LANES:2 * LANES] = outs[6]
    nk_ref[:, 0:LANES] = outs[7]
    nk_ref[:, LANES:2 * LANES] = outs[8]
    p_rest = jnp.dot(xb, w_ref[:, N_NORM:], preferred_element_type=F32)
    v = p_rest[:, 0:LANES]
    v_ref[:, 0:LANES] = jnp.where(lane < HEAD_DIM, v, 1.0).astype(BF16)
    v_ref[:, LANES:2 * LANES] = jnp.where(lane < HEAD_DIM, 1.0, v).astype(BF16)
    nv_ref[...] = p_rest[:, LANES:3 * LANES].astype(BF16)
    hy_ref[...] = p_rest[:, 3 * LANES:]


def _inproj(h, mods, mod_base, g, w, gvec, headmean, cos_t, sin_t, tiles_per_batch, n_batch):
    n_rows, d = h.shape
    tm = TOKEN_TILE
    row = lambda i: (i, 0)
    const = lambda i: (0, 0)
    widths = (512, 128, 256, 256, 256, 256)
    out_shape = [jax.ShapeDtypeStruct((n_rows, wd), BF16) for wd in widths]
    out_shape.append(jax.ShapeDtypeStruct((n_rows, 768), F32))
    out_specs = [pl.BlockSpec((tm, wd), row) for wd in widths] + [pl.BlockSpec((tm, 768), row)]
    return pl.pallas_call(
        _inproj_kernel,
        grid=(n_rows // tm,),
        in_specs=[pl.BlockSpec((tm, d), row)]
        + _mod_specs(tiles_per_batch, n_batch, mod_base, (3, 4))
        + [
            pl.BlockSpec((1, d), const),
            pl.BlockSpec(w.shape, const, pipeline_mode=pl.Buffered(1)),
            pl.BlockSpec(gvec.shape, const),
            pl.BlockSpec(headmean.shape, const),
            pl.BlockSpec((tm, LANES), row),
            pl.BlockSpec((tm, LANES), row),
        ],
        out_specs=out_specs,
        out_shape=out_shape,
        compiler_params=_cparams(("arbitrary",)),
        name="inproj",
    )(h, mods, mods, g, w, gvec, headmean, cos_t, sin_t)


def _stack_heads(q):
    lane = lax.broadcasted_iota(jnp.int32, q.shape, 1)
    zero = jnp.zeros_like(q)
    return jnp.concatenate([jnp.where(lane < HEAD_DIM, q, zero), jnp.where(lane >= HEAD_DIM, q, zero)], axis=0)


def _unstack_heads(o2, tq):
    lane = lax.broadcasted_iota(jnp.int32, (tq, LANES), 1)
    return jnp.where(lane < HEAD_DIM, o2[:tq], o2[tq:])


def _qk(q2, k):
    return lax.dot_general(q2, k, (((1,), (1,)), ((), ())), preferred_element_type=F32)


def _gqa_kernel(q_ref, kc_ref, vc_ref, *rest, n_latent_chunks, tk):
    if n_latent_chunks:
        kl_ref, vl_ref, o_ref = rest
    else:
        (o_ref,) = rest
    tq = q_ref.shape[0]
    q2 = _stack_heads(q_ref[...])

    def pv(p, v_ref, rows):
        return jnp.concatenate(
            [jnp.dot(p[:tq], v_ref[rows, 0:LANES], preferred_element_type=F32),
             jnp.dot(p[tq:], v_ref[rows, LANES:2 * LANES], preferred_element_type=F32)], axis=0)

    s = _qk(q2, kc_ref[...])
    m = jnp.max(s, axis=-1, keepdims=True)
    acc = pv(jnp.exp2((s - m).astype(BF16)), vc_ref, slice(None))

    if n_latent_chunks:
        def body(j, carry):
            m, acc = carry
            rows = pl.ds(pl.multiple_of(j * tk, tk), tk)
            s = _qk(q2, kl_ref[rows, :])
            m_new = jnp.maximum(m, jnp.max(s, axis=-1, keepdims=True))
            p = jnp.exp2((s - m_new).astype(BF16))
            return m_new, jnp.exp2(m - m_new) * acc + pv(p, vl_ref, rows)

        m, acc = lax.fori_loop(0, n_latent_chunks, body, (m, acc), unroll=True)
    lane = lax.broadcasted_iota(jnp.int32, (tq, LANES), 1)
    num = jnp.where(lane < HEAD_DIM, acc[:tq], acc[tq:])
    den = pltpu.roll(jnp.where(lane < HEAD_DIM, acc[tq:], acc[:tq]), HEAD_DIM, 1)
    o_ref[...] = (num / den).astype(o_ref.dtype)


def _pair_ctx_kernel(q_ref, k_ref, v_ref, o_ref):
    tq = q_ref.shape[0]
    q2 = _stack_heads(q_ref[...])
    s = _qk(q2, k_ref[...])
    p = jnp.exp(s - jnp.max(s, axis=-1, keepdims=True))
    l = jnp.sum(p, axis=-1, keepdims=True)
    o = jnp.dot(p.astype(BF16), v_ref[...], preferred_element_type=F32)
    o_ref[...] = _unstack_heads(o / l, tq).astype(o_ref.dtype)


def _gqa(q, k, v2, out_rows, n_batch, seq, n_ctx, latent):
    chunks = q.shape[1] // LANES
    ctx_blk0 = n_batch * seq // n_ctx
    tk = min(ATTN_TK, seq)
    if latent:
        tq = ATTN_TQ
        grid = (n_batch, chunks, seq // tq)
        qmap = lambda b, j, i: (b * (seq // tq) + i, j)
        kv_specs = [pl.BlockSpec((seq, LANES), lambda b, j, i: (b, 0)),
                    pl.BlockSpec((seq, 2 * LANES), lambda b, j, i: (b, 0))]
        kv_args = (k, v2)
    else:
        tq = n_ctx
        grid = (n_batch, chunks, 1)
        qmap = lambda b, j, i: (ctx_blk0 + b, j)
        kv_specs, kv_args = [], ()
    return pl.pallas_call(
        functools.partial(_gqa_kernel, n_latent_chunks=seq // tk if latent else 0, tk=tk),
        grid=grid,
        in_specs=[pl.BlockSpec((tq, LANES), qmap),
                  pl.BlockSpec((n_ctx, LANES), lambda b, j, i: (ctx_blk0 + b, 0)),
                  pl.BlockSpec((n_ctx, 2 * LANES), lambda b, j, i: (ctx_blk0 + b, 0))] + kv_specs,
        out_specs=pl.BlockSpec((tq, LANES), qmap),
        out_shape=jax.ShapeDtypeStruct((out_rows, q.shape[1]), BF16),
        compiler_params=_cparams(("arbitrary", "arbitrary", "arbitrary")),
        name="attn_latent" if latent else "attn_ctx",
    )(q, k, v2, *kv_args)


def _pair_ctx_attention(q, k, v, out_rows, n_batch, seq, n_ctx):
    chunks = q.shape[1] // LANES
    ctx_blk0 = n_batch * seq // n_ctx
    spec = pl.BlockSpec((n_ctx, LANES), lambda b, j: (ctx_blk0 + b, j))
    return pl.pallas_call(
        _pair_ctx_kernel,
        grid=(n_batch, chunks),
        in_specs=[spec, spec, spec],
        out_specs=spec,
        out_shape=jax.ShapeDtypeStruct((out_rows, q.shape[1]), BF16),
        compiler_params=_cparams(("arbitrary", "arbitrary")),
        name="na_ctx",
    )(q, k, v)


def _na_kernel(q_ref, kc_ref, vc_ref, kl_ref, vl_ref, bias_ref, o_ref, *, grid_rows):
    i = pl.program_id(2)
    tq = q_ref.shape[0]
    nk = NA_KEY_ROWS * GRID_W
    ks = jnp.clip(i * NA_BLOCK_ROWS - NA_ROWS // 2, 0, grid_rows - NA_KEY_ROWS)
    start = pl.multiple_of(ks * GRID_W, GRID_W)
    q2 = _stack_heads(q_ref[...])
    s_nb = _qk(q2, kl_ref[pl.ds(start, nk), :]) + bias_ref[0].reshape(2 * tq, nk)
    s_cx = _qk(q2, kc_ref[...])
    m = jnp.maximum(jnp.max(s_nb, axis=-1, keepdims=True), jnp.max(s_cx, axis=-1, keepdims=True))
    p_nb = jnp.exp(s_nb - m)
    p_cx = jnp.exp(s_cx - m)
    l = jnp.sum(p_nb, axis=-1, keepdims=True) + jnp.sum(p_cx, axis=-1, keepdims=True)
    o = (jnp.dot(p_nb.astype(BF16), vl_ref[pl.ds(start, nk), :], preferred_element_type=F32)
         + jnp.dot(p_cx.astype(BF16), vc_ref[...], preferred_element_type=F32))
    o_ref[...] = _unstack_heads(o / l, tq).astype(o_ref.dtype)


def _na_block_start(blk, grid_rows):
    return int(np.clip(blk * NA_BLOCK_ROWS - NA_ROWS // 2, 0, grid_rows - NA_KEY_ROWS))


def _na_bias_tables(rpb, grid_rows):
    nblk = grid_rows // NA_BLOCK_ROWS
    n_heads = rpb.shape[0]
    qc = np.arange(GRID_W)[:, None]
    kc = np.arange(GRID_W)[None, :]
    cs = np.clip(qc - NA_COLS // 2, 0, GRID_W - NA_COLS)
    col_ok = (kc >= cs) & (kc < cs + NA_COLS)
    col_hot = ((kc - qc + NA_COLS - 1)[:, :, None] == np.arange(2 * NA_COLS - 1)) & col_ok[:, :, None]
    by_col = jnp.einsum("abe,hde->hdab", jnp.asarray(col_hot, F32), rpb, precision=HIGHEST)
    tabs = []
    for blk in (0, min(1, nblk - 1), nblk - 1):
        qr = blk * NA_BLOCK_ROWS + np.arange(NA_BLOCK_ROWS)[:, None]
        kr = _na_block_start(blk, grid_rows) + np.arange(NA_KEY_ROWS)[None, :]
        rs = np.clip(qr - NA_ROWS // 2, 0, grid_rows - NA_ROWS)
        row_ok = (kr >= rs) & (kr < rs + NA_ROWS)
        row_hot = ((kr - qr + NA_ROWS - 1)[:, :, None] == np.arange(2 * NA_ROWS - 1)) & row_ok[:, :, None]
        b = jnp.einsum("qkd,hdab->hqakb", jnp.asarray(row_hot, F32), by_col, precision=HIGHEST)
        valid = row_ok[:, None, :, None] & col_ok[None, :, None, :]
        b = jnp.where(jnp.asarray(valid)[None], b, NEG_BIG)
        tabs.append(b.reshape(n_heads, NA_BLOCK_ROWS * GRID_W, NA_KEY_ROWS * GRID_W))
    return jnp.stack(tabs)


def _na_latent(nq, nk, nv, bias, n_rows_out, n_batch, seq, n_ctx):
    chunks = nq.shape[1] // LANES
    tq = NA_BLOCK_ROWS * GRID_W
    nblk = seq // tq
    ctx_blk0 = n_batch * seq // n_ctx
    qmap = lambda b, j, i: (b * nblk + i, j)
    ctx_spec = pl.BlockSpec((n_ctx, LANES), lambda b, j, i: (ctx_blk0 + b, j))
    lat_spec = pl.BlockSpec((seq, LANES), lambda b, j, i: (b, j))
    variant = lambda i: jnp.where(i == 0, 0, jnp.where(i == nblk - 1, 2, 1))
    bias_spec = pl.BlockSpec((1, 2, tq, NA_KEY_ROWS * GRID_W), lambda b, j, i: (variant(i), j, 0, 0))
    return pl.pallas_call(
        functools.partial(_na_kernel, grid_rows=seq // GRID_W),
        grid=(n_batch, chunks, nblk),
        in_specs=[pl.BlockSpec((tq, LANES), qmap), ctx_spec, ctx_spec, lat_spec, lat_spec, bias_spec],
        out_specs=pl.BlockSpec((tq, LANES), qmap),
        out_shape=jax.ShapeDtypeStruct((n_rows_out, nq.shape[1]), BF16),
        compiler_params=_cparams(("arbitrary", "arbitrary", "arbitrary")),
        name="na_latent",
    )(nq, nk, nv, nk, nv, bias)


def _hconv_kernel(x_ref, prev_ref, next_ref, w_ref, b_ref, z_ref, x0_ref, *, tiles_per_seq, n_latent_tiles):
    i = pl.program_id(0)
    x = x_ref[...]
    t = x.shape[0]
    row = lax.broadcasted_iota(jnp.int32, x.shape, 0)
    first = jnp.logical_or(i % tiles_per_seq == 0, i >= n_latent_tiles)
    last = jnp.logical_or(i % tiles_per_seq == tiles_per_seq - 1, i >= n_latent_tiles)
    prev_row = jnp.where(first, 0.0, prev_ref[7:8, :])
    next_row = jnp.where(last, 0.0, next_ref[0:1, :])
    up = jnp.where(row == 0, prev_row, pltpu.roll(x, 1, 0))
    dn = jnp.where(row == t - 1, next_row, pltpu.roll(x, t - 1, 0))
    uc = up * w_ref[0:1, :] + x * w_ref[1:2, :] + dn * w_ref[2:3, :] + b_ref[...]
    cw = uc.shape[1] // 3
    x0_ref[...] = uc[:, :cw]
    z_ref[...] = uc[:, 2 * cw:] * uc[:, cw:2 * cw]


def _hconv(hy, conv_w, conv_b, seq, n_ctx, n_batch):
    n_rows, ch = hy.shape
    tt = n_ctx
    n_tiles = n_rows // tt
    cw = ch // 3
    return pl.pallas_call(
        functools.partial(_hconv_kernel, tiles_per_seq=seq // tt, n_latent_tiles=n_batch * seq // tt),
        grid=(n_tiles,),
        in_specs=[
            pl.BlockSpec((tt, ch), lambda i: (i, 0)),
            pl.BlockSpec((8, ch), lambda i: (jnp.maximum(i * (tt // 8) - 1, 0), 0)),
            pl.BlockSpec((8, ch), lambda i: (jnp.minimum((i + 1) * (tt // 8), n_rows // 8 - 1), 0)),
            pl.BlockSpec((3, ch), lambda i: (0, 0)),
            pl.BlockSpec((1, ch), lambda i: (0, 0)),
        ],
        out_specs=[pl.BlockSpec((tt, cw), lambda i: (i, 0))] * 2,
        out_shape=[jax.ShapeDtypeStruct((n_rows, cw), F32)] * 2,
        compiler_params=_cparams(("arbitrary",)),
        name="hyena_conv",
    )(hy, hy, hy, conv_w, conv_b)


def _hfilt_kernel(f_ref, w1_ref, b1_ref, w2_ref, b2_ref, w3_ref, fr_ref, dl_ref, k_ref, l1_ref, *, n):
    i = pl.program_id(0)
    f = f_ref[...]
    fr = fr_ref[...]
    h1 = jnp.sin(fr * (jnp.dot(f, w1_ref[...], precision=HIGHEST, preferred_element_type=F32) + b1_ref[...]))
    h2 = jnp.sin(fr * (jnp.dot(h1, w2_ref[...], precision=HIGHEST, preferred_element_type=F32) + b2_ref[...]))
    o = jnp.dot(h2, w3_ref[...], precision=HIGHEST, preferred_element_type=F32)
    cw = o.shape[1] // 2
    t = f.shape[0]
    row = i * t + lax.broadcasted_iota(jnp.int32, (t, cw), 0)
    decay = jnp.exp(-f[:, 0:1] * jnp.abs(dl_ref[...]))
    k = jnp.where(row < n, o[:, :cw], o[:, cw:]) * decay
    k = jnp.where(row == n, 0.0, k)
    k_ref[...] = k

    @pl.when(i == 0)
    def _():
        l1_ref[...] = jnp.zeros_like(l1_ref)

    l1_ref[...] += jnp.sum(jnp.abs(k), axis=0, keepdims=True)


def _filter_feats(n):
    rows = np.arange(2 * n)
    pos = np.where(rows <= n, rows, 2 * n - rows).astype(np.float32)
    pos = jnp.asarray(np.minimum(pos, n - 1))
    t = pos / max(n - 1, 1)
    bands = jnp.linspace(1e-4, FILTER_BANDS - 1, FILTER_BANDS, dtype=F32)
    ang = (2.0 * math.pi / n) * pos[:, None] * bands[None, :]
    feats = jnp.concatenate([t[:, None], jnp.cos(ang), -jnp.sin(ang)], axis=-1)
    return jnp.pad(feats, ((0, 0), (0, LANES - feats.shape[1])))


def _pad_to(a, shape):
    return jnp.pad(a, [(0, s - d) for s, d in zip(shape, a.shape)])


def _hfilter(n, w1, b1, w2, b2, w3, freq, deltas):
    feats = _filter_feats(n)
    w1, b1, freq = _pad_to(w1, (LANES, LANES)), _pad_to(b1, (1, LANES)), _pad_to(freq, (1, LANES))
    w2, b2, w3 = _pad_to(w2, (LANES, LANES)), _pad_to(b2, (1, LANES)), _pad_to(w3, (LANES, w3.shape[1]))
    rows, emb = feats.shape
    t = min(1024, rows)
    cw = w3.shape[1] // 2
    const = lambda i: (0, 0)
    full = lambda a: pl.BlockSpec(a.shape, const)
    return pl.pallas_call(
        functools.partial(_hfilt_kernel, n=n),
        grid=(rows // t,),
        in_specs=[pl.BlockSpec((t, emb), lambda i: (i, 0)), full(w1), full(b1), full(w2), full(b2), full(w3),
                  full(freq), full(deltas)],
        out_specs=[pl.BlockSpec((t, cw), lambda i: (i, 0)), pl.BlockSpec((1, cw), const)],
        out_shape=[jax.ShapeDtypeStruct((rows, cw), F32), jax.ShapeDtypeStruct((1, cw), F32)],
        compiler_params=_cparams(("arbitrary",)),
        name="hyena_filter",
    )(feats, w1, b1, w2, b2, w3, freq, deltas)


def _split_bf16(a):
    hi = a.astype(BF16)
    return hi, (a - hi.astype(F32)).astype(BF16)


def _dot3(a_split, b):
    ah, al = a_split
    bh, bl = _split_bf16(b)
    return (jnp.dot(ah, bh, preferred_element_type=F32) + jnp.dot(ah, bl, preferred_element_type=F32)
            + jnp.dot(al, bh, preferred_element_type=F32))


def _dft_outer_kernel(f_ref, x_ref, o_ref):
    o_ref[0] = _dot3(_split_bf16(f_ref[...]), x_ref[0])


def _dft_outer(fmat, x):
    nb, k, cols = x.shape
    r = fmat.shape[0]
    tc = 4096
    return pl.pallas_call(
        _dft_outer_kernel,
        grid=(nb, cols // tc),
        in_specs=[pl.BlockSpec((r, k), lambda b, j: (0, 0)), pl.BlockSpec((1, k, tc), lambda b, j: (b, 0, j))],
        out_specs=pl.BlockSpec((1, r, tc), lambda b, j: (b, 0, j)),
        out_shape=jax.ShapeDtypeStruct((nb, r, cols), F32),
        compiler_params=_cparams(("arbitrary", "arbitrary")),
        name="dft_outer",
    )(fmat, x)


def _inner_matrices(fc_ref, fs_ref, tc_ref, ts_ref):
    fc, fs = fc_ref[...], fs_ref[...]
    tc, ts = tc_ref[0], ts_ref[0]
    gr = fc * tc - fs * ts
    gi = -(fc * ts + fs * tc)
    return gr, gi


def _real_form(gr, gi):
    return jnp.concatenate([jnp.concatenate([gr, -gi], axis=1), jnp.concatenate([gi, gr], axis=1)], axis=0)


def _dft_inner_filter_kernel(fc_ref, fs_ref, tc_ref, ts_ref, b_ref, o_ref):
    g = _real_form(*_inner_matrices(fc_ref, fs_ref, tc_ref, ts_ref))
    n2, c = b_ref.shape[3], b_ref.shape[4]
    o_ref[0] = _dot3(_split_bf16(g), b_ref[0, :, 0].reshape(2 * n2, c))


def _dft_inner_conv_kernel(fc_ref, fs_ref, tc_ref, ts_ref, b_ref, kh_ref, sc_ref, o_ref):
    gr, gi = _inner_matrices(fc_ref, fs_ref, tc_ref, ts_ref)
    g = _split_bf16(_real_form(gr, gi))
    gh = _split_bf16(_real_form(gr.T, -gi.T))
    nb, n2, c = b_ref.shape[0], b_ref.shape[3], b_ref.shape[4]
    kh = kh_ref[0]
    kr, ki = kh[:n2], kh[n2:]
    scale = sc_ref[...]
    for b in range(nb):
        x = _dot3(g, b_ref[b, :, 0].reshape(2 * n2, c))
        xr, xi = x[:n2], x[n2:]
        y = jnp.concatenate([(xr * kr - xi * ki) * scale, (xr * ki + xi * kr) * scale], axis=0)
        d = _dot3(gh, y)
        o_ref[b, :, 0] = d.reshape(2, n2, c)


def _dft_tables(n1_len):
    n_total = n1_len * DFT_N2
    a = np.arange(DFT_N2)
    ang = 2.0 * np.pi * np.outer(a, a) / DFT_N2
    k1 = np.arange(n1_len)
    tw = 2.0 * np.pi * np.outer(k1, a) / n_total
    o = 2.0 * np.pi * np.outer(k1, k1) / n1_len
    f32 = lambda v: jnp.asarray(v.astype(np.float32))
    outer_fwd = np.concatenate([np.cos(o), -np.sin(o)], axis=0)
    outer_inv = np.concatenate([np.cos(o), -np.sin(o)], axis=1)
    return dict(
        fc=f32(np.cos(ang)), fs=f32(np.sin(ang)),
        tc=f32(np.cos(tw)).reshape(n1_len, 1, DFT_N2), ts=f32(np.sin(tw)).reshape(n1_len, 1, DFT_N2),
        outer_fwd_full=f32(outer_fwd), outer_fwd_half=f32(outer_fwd[:, :n1_len // 2]),
        outer_inv_half=f32(outer_inv[:n1_len // 2]),
    )


def _inner_specs(n1_len):
    sq = pl.BlockSpec((DFT_N2, DFT_N2), lambda k: (0, 0))
    tw = pl.BlockSpec((1, 1, DFT_N2), lambda k: (k, 0, 0))
    return [sq, sq, tw, tw]


def _filter_spectrum(kfull, tabs, n1_len):
    c = kfull.shape[1]
    b = _dft_outer(tabs["outer_fwd_full"], kfull.reshape(1, n1_len, DFT_N2 * c))
    b = b.reshape(1, 2, n1_len, DFT_N2, c)
    return pl.pallas_call(
        _dft_inner_filter_kernel,
        grid=(n1_len,),
        in_specs=_inner_specs(n1_len) + [pl.BlockSpec((1, 2, 1, DFT_N2, c), lambda k: (0, 0, k, 0, 0))],
        out_specs=pl.BlockSpec((1, 2 * DFT_N2, c), lambda k: (k, 0, 0)),
        out_shape=jax.ShapeDtypeStruct((n1_len, 2 * DFT_N2, c), F32),
        compiler_params=_cparams(("arbitrary",)),
        name="dft_inner_filter",
    )(tabs["fc"], tabs["fs"], tabs["tc"], tabs["ts"], b)


def _long_conv(z, khat, scale, tabs, n1_len):
    nb, n, c = z.shape
    bz = _dft_outer(tabs["outer_fwd_half"], z.reshape(nb, n1_len // 2, DFT_N2 * c))
    bz = bz.reshape(nb, 2, n1_len, DFT_N2, c)
    blk = pl.BlockSpec((nb, 2, 1, DFT_N2, c), lambda k: (0, 0, k, 0, 0))
    d = pl.pallas_call(
        _dft_inner_conv_kernel,
        grid=(n1_len,),
        in_specs=_inner_specs(n1_len) + [blk, pl.BlockSpec((1, 2 * DFT_N2, c), lambda k: (k, 0, 0)),
                                         pl.BlockSpec((1, c), lambda k: (0, 0))],
        out_specs=blk,
        out_shape=jax.ShapeDtypeStruct(bz.shape, F32),
        compiler_params=_cparams(("arbitrary",)),
        name="dft_inner_conv",
    )(tabs["fc"], tabs["fs"], tabs["tc"], tabs["ts"], bz, khat, scale)
    y = _dft_outer(tabs["outer_inv_half"], d.reshape(nb, 2 * n1_len, DFT_N2 * c))
    return y.reshape(nb, n, c)


def _hyena_ctx_kernel(ff_ref, fi_ref, k_ref, l1_ref, z_ref, o_ref):
    ff = ff_ref[...]
    n = z_ref.shape[0]
    big = ff.shape[1]
    kh = jnp.dot(ff, k_ref[...], precision=HIGHEST, preferred_element_type=F32)
    x = jnp.dot(ff[:, :n], z_ref[...], precision=HIGHEST, preferred_element_type=F32)
    kr, ki = kh[:big], kh[big:]
    xr, xi = x[:big], x[big:]
    scale = 1.0 / (l1_ref[...] * big)
    y = jnp.concatenate([(xr * kr - xi * ki) * scale, (xr * ki + xi * kr) * scale], axis=0)
    o_ref[...] = jnp.dot(fi_ref[...], y, precision=HIGHEST, preferred_element_type=F32)


def _hyena_ctx(z, kfull, l1, row0_blk, n_batch, n):
    big = 2 * n
    a = np.arange(big)
    ang = 2.0 * np.pi * np.outer(a, a) / big
    ff = jnp.asarray(np.concatenate([np.cos(ang), -np.sin(ang)], axis=0).astype(np.float32))
    fi = jnp.asarray(np.concatenate([np.cos(ang), -np.sin(ang)], axis=1)[:n].astype(np.float32))
    c = z.shape[1]
    const = lambda b: (0, 0)
    return pl.pallas_call(
        _hyena_ctx_kernel,
        grid=(n_batch,),
        in_specs=[pl.BlockSpec(ff.shape, const), pl.BlockSpec(fi.shape, const), pl.BlockSpec(kfull.shape, const),
                  pl.BlockSpec((1, c), const), pl.BlockSpec((n, c), lambda b: (row0_blk + b, 0))],
        out_specs=pl.BlockSpec((n, c), lambda b: (b, 0)),
        out_shape=jax.ShapeDtypeStruct((n_batch * n, c), F32),
        compiler_params=_cparams(("arbitrary",)),
        name="hyena_ctx",
    )(ff, fi, kfull, l1, z)


def _outproj_kernel(h_ref, gt_ref, ya_ref, yconv_ref, z_ref, x0_ref, skip_ref, yc_ref, wa_ref, wb_ref, wc_ref, o_ref):
    yb = ((yconv_ref[...] + z_ref[...] * skip_ref[...]) * x0_ref[...]).astype(BF16)
    mix = (jnp.dot(ya_ref[...], wa_ref[...], preferred_element_type=F32)
           + jnp.dot(yb, wb_ref[...], preferred_element_type=F32)
           + jnp.dot(yc_ref[...], wc_ref[...], preferred_element_type=F32))
    o_ref[...] = h_ref[...] + gt_ref[0] * mix


def _outproj(h, mods, mod_base, ya, yconv, z, x0, skip, yc, wa, wb, wc, n_rows, tiles_per_batch, n_batch):
    d = h.shape[1]
    tm = TOKEN_TILE
    row = lambda i: (i, 0)
    const = lambda i: (0, 0)
    rows = lambda a: pl.BlockSpec((tm, a.shape[1]), row)
    full = lambda a: pl.BlockSpec(a.shape, const)
    return pl.pallas_call(
        _outproj_kernel,
        grid=(n_rows // tm,),
        in_specs=[rows(h)] + _mod_specs(tiles_per_batch, n_batch, mod_base, (5,))
        + [rows(ya), rows(yconv), rows(z), rows(x0), full(skip), rows(yc), full(wa), full(wb), full(wc)],
        out_specs=pl.BlockSpec((tm, d), row),
        out_shape=jax.ShapeDtypeStruct((n_rows, d), F32),
        compiler_params=_cparams(("arbitrary",)),
        name="outproj",
    )(h, mods, ya, yconv, z, x0, skip, yc, wa, wb, wc)


def _rope_tables(seq, n_batch, n_ctx_rows):
    t = jnp.arange(seq, dtype=jnp.int32)
    rows = (t // GRID_W).astype(F32)
    cols = (t % GRID_W).astype(F32)
    nf = HEAD_DIM // 4
    inv = ROPE_THETA ** (-jnp.arange(nf, dtype=F32) / nf)
    ar, ac = rows[:, None] * inv, cols[:, None] * inv
    cos = jnp.concatenate([jnp.cos(ar), jnp.cos(ar), jnp.cos(ac), jnp.cos(ac)], axis=1)
    sin = jnp.concatenate([-jnp.sin(ar), jnp.sin(ar), -jnp.sin(ac), jnp.sin(ac)], axis=1)
    cos = jnp.tile(cos, (n_batch, LANES // HEAD_DIM))
    sin = jnp.tile(sin, (n_batch, LANES // HEAD_DIM))
    cos = jnp.concatenate([cos, jnp.ones((n_ctx_rows, LANES), F32)], axis=0)
    sin = jnp.concatenate([sin, jnp.zeros((n_ctx_rows, LANES), F32)], axis=0)
    return cos, sin


def _head_mean_matrix():
    a = np.arange(LANES)
    return jnp.asarray((a[:, None] // HEAD_DIM == a[None, :] // HEAD_DIM).astype(np.float32) / HEAD_DIM, dtype=BF16)


def _chunk3(w, axis):
    if axis == 1:
        d, hdn = w.shape
        return w.reshape(d, hdn // FFN_CHUNK, FFN_CHUNK).transpose(1, 0, 2).astype(BF16)
    hdn, d = w.shape
    return w.reshape(hdn // FFN_CHUNK, FFN_CHUNK, d).astype(BF16)


def kernel(x, c, ctx, c_ctx, w_ada, b_ada, g_ffn1, w_ffn1_gate, w_ffn1_up, w_ffn1_down, g_mix, w_in, w_out, g_q_attn, g_k_attn, conv_w, conv_b, filt_w1, filt_b1, filt_w2, filt_b2, filt_w3, filt_freq, hyena_skip, g_q_na, g_k_na, na_rpb, g_ffn2, w_ffn2_gate, w_ffn2_up, w_ffn2_down):
    n_batch, seq, d = x.shape
    n_ctx = ctx.shape[1]
    depth = w_ada.shape[0]
    assert d == _D and n_batch + 1 <= 8 and seq % TOKEN_TILE == 0 and seq % (NA_BLOCK_ROWS * GRID_W) == 0
    assert n_ctx == 256 and (n_batch * n_ctx) % TOKEN_TILE == 0
    n_lat = n_batch * seq
    n_all = n_lat + n_batch * n_ctx
    tiles_per_batch = seq // TOKEN_TILE
    hy_w = conv_w.shape[2] // 3
    n1_len = 2 * seq // DFT_N2

    cond8 = jnp.zeros((8, d), F32).at[:n_batch].set(c).at[n_batch].set(c_ctx)
    mods = _adaln_all(cond8, w_ada, b_ada).reshape(depth * 8 * N_MOD, 1, d)
    h = jnp.concatenate([x.reshape(n_lat, d), ctx.reshape(n_batch * n_ctx, d)], axis=0)

    cos_t, sin_t = _rope_tables(seq, n_batch, n_batch * n_ctx)
    headmean = _head_mean_matrix()
    tabs = _dft_tables(n1_len)
    deltas = jnp.linspace(math.log(DECAY_TARGET) / SLOW_DECAY_PCT, math.log(DECAY_TARGET) / FAST_DECAY_PCT,
                          hy_w, dtype=F32).reshape(1, hy_w)

    q_heads = [hd for pair in range(ATTN_HEADS // 2) for hd in (pair, pair + ATTN_HEADS // 2)]
    q_segs = [(hd * HEAD_DIM, (hd + 1) * HEAD_DIM) for hd in q_heads]
    in_segs = q_segs + [(512, 640), (1536, 1792), (1792, 2048), (640, 768), (2048, 2304), (768, 1536)]

    for l in range(depth):
        last = l == depth - 1
        base = l * 8 * N_MOD
        rep = lambda g, k: jnp.tile(g, k)
        gvec = jnp.concatenate([rep(g_q_attn[l], ATTN_HEADS), rep(g_k_attn[l], ATTN_KV_HEADS),
                                rep(g_q_na[l], NA_HEADS), rep(g_k_na[l], NA_HEADS)]).reshape(1, N_NORM)

        h = _ffn(h, mods, base, (0, 1, 2), g_ffn1[l].reshape(1, d), _chunk3(w_ffn1_gate[l], 1),
                 _chunk3(w_ffn1_up[l], 1), _chunk3(w_ffn1_down[l], 0), n_all, tiles_per_batch, n_batch)

        w_in_l = jnp.concatenate([w_in[l][:, a:b] for a, b in in_segs], axis=1).astype(BF16)
        q, k, nq, nk, v, nv, hy = _inproj(h, mods, base, g_mix[l].reshape(1, d), w_in_l,
                                          gvec, headmean, cos_t, sin_t, tiles_per_batch, n_batch)

        out_rows = n_lat if last else n_all
        ya = _gqa(q, k, v, out_rows, n_batch, seq, n_ctx, latent=True)
        bias = _na_bias_tables(na_rpb[l], seq // GRID_W)
        yc = _na_latent(nq, nk, nv, bias, out_rows, n_batch, seq, n_ctx)

        z, x0 = _hconv(hy, conv_w[l], conv_b[l].reshape(1, -1), seq, n_ctx, n_batch)
        fargs = (filt_w1[l], filt_b1[l].reshape(1, -1), filt_w2[l], filt_b2[l].reshape(1, -1), filt_w3[l],
                 filt_freq[l].reshape(1, -1), deltas)
        kfull, l1 = _hfilter(seq, *fargs)
        khat = _filter_spectrum(kfull, tabs, n1_len)
        yconv = _long_conv(z[:n_lat].reshape(n_batch, seq, hy_w), khat, 1.0 / (l1 * (2 * seq)), tabs, n1_len)
        yconv = yconv.reshape(n_lat, hy_w)

        if not last:
            ya_c = _gqa(q, k, v, n_all, n_batch, seq, n_ctx, latent=False)
            yc_c = _pair_ctx_attention(nq, nk, nv, n_all, n_batch, seq, n_ctx)
            ya = lax.dynamic_update_slice(ya, ya_c[n_lat:], (n_lat, 0))
            yc = lax.dynamic_update_slice(yc, yc_c[n_lat:], (n_lat, 0))
            kfull_c, l1_c = _hfilter(n_ctx, *fargs)
            yconv_c = _hyena_ctx(z, kfull_c, l1_c, n_lat // n_ctx, n_batch, n_ctx)
            yconv = jnp.concatenate([yconv, yconv_c], axis=0)

        w_o = w_out[l]
        wa = jnp.concatenate([w_o[a:b] for a, b in q_segs], axis=0).astype(BF16)
        wb = w_o[512:512 + hy_w].astype(BF16)
        wc = w_o[512 + hy_w:].astype(BF16)
        h = _outproj(h, mods, base, ya, yconv, z, x0, hyena_skip[l].reshape(1, hy_w), yc, wa, wb, wc,
                     out_rows, tiles_per_batch, n_batch)
        h = _ffn(h, mods, base, (6, 7, 8), g_ffn2[l].reshape(1, d), _chunk3(w_ffn2_gate[l], 1),
                 _chunk3(w_ffn2_up[l], 1), _chunk3(w_ffn2_down[l], 0), out_rows, tiles_per_batch, n_batch)

    return h[:n_lat].reshape(n_batch, seq, d)
```

```python
import functools
import math

import numpy as np
import jax
import jax.numpy as jnp
from jax import lax
from jax.experimental import pallas as pl
from jax.experimental.pallas import tpu as pltpu

F32 = jnp.float32
BF16 = jnp.bfloat16
HIGHEST = lax.Precision.HIGHEST

HEAD_DIM = 64
GRID_W = 64
ATTN_HEADS = 8
ATTN_KV_HEADS = 2
NA_HEADS = 4
NA_ROWS = 8
NA_COLS = 16
FILTER_BANDS = 16
DECAY_TARGET = 1e-2
FAST_DECAY_PCT = 0.3
SLOW_DECAY_PCT = 1.5
ROPE_THETA = 10000.0
EPS = 1e-6
N_MOD = 9

LANES = 128
TOKEN_TILE = 512
FFN_CHUNK = 256
ATTN_TQ = 512
ATTN_TK = 2048
NA_BLOCK_ROWS = 8
NA_KEY_ROWS = 16
DFT_N2 = 128
OUTER_COLS = 8
INNER_K1 = 8
NEG_BIG = -1e30
VMEM_LIMIT = 56 * 1024 * 1024


def _cparams(sem):
    return pltpu.CompilerParams(dimension_semantics=sem, vmem_limit_bytes=VMEM_LIMIT)


def _ada_kernel(c_ref, w_ref, b_ref, o_ref):
    c = c_ref[...]
    s = c * jax.nn.sigmoid(c)
    o_ref[0] = jnp.dot(s, w_ref[0], precision=HIGHEST, preferred_element_type=F32) + b_ref[0]


def _adaln_all(cond8, w_ada, b_ada):
    depth, d, nd = w_ada.shape
    tn = 1024
    return pl.pallas_call(
        _ada_kernel,
        grid=(depth, nd // tn),
        in_specs=[
            pl.BlockSpec((8, d), lambda l, j: (0, 0)),
            pl.BlockSpec((1, d, tn), lambda l, j: (l, 0, j)),
            pl.BlockSpec((1, 1, tn), lambda l, j: (l, 0, j)),
        ],
        out_specs=pl.BlockSpec((1, 8, tn), lambda l, j: (l, 0, j)),
        out_shape=jax.ShapeDtypeStruct((depth, 8, nd), F32),
        compiler_params=_cparams(("arbitrary", "arbitrary")),
        name="adaln",
    )(cond8, w_ada, b_ada.reshape(depth, 1, nd))


def _norm_mod(x, g, sh, sc):
    ms = jnp.mean(x * x, axis=-1, keepdims=True)
    y = x * lax.rsqrt(ms + EPS) * g
    return y * (1.0 + sc) + sh


def _mod_specs(tiles_per_batch, n_batch, base, idxs):
    def spec(which):
        return pl.BlockSpec(
            (1, 1, _D),
            lambda i, w=which: (base + jnp.minimum(i // tiles_per_batch, n_batch) * N_MOD + w, 0, 0))
    return [spec(w) for w in idxs]


_D = 1024


def _ffn_kernel(h_ref, sh_ref, sc_ref, gt_ref, g_ref, wg_ref, wu_ref, wd_ref, o_ref, xb_ref, acc_ref):
    x = h_ref[...]
    xb_ref[...] = _norm_mod(x, g_ref[...], sh_ref[0], sc_ref[0]).astype(BF16)

    def chunk(k):
        xb = xb_ref[...]
        a = jnp.dot(xb, wg_ref[k], preferred_element_type=F32)
        u = jnp.dot(xb, wu_ref[k], preferred_element_type=F32)
        act = (a * jax.nn.sigmoid(a) * u).astype(BF16)
        return jnp.dot(act, wd_ref[k], preferred_element_type=F32)

    def body(k, carry):
        acc_ref[...] += chunk(k)
        return carry

    acc_ref[...] = chunk(0)
    lax.fori_loop(1, wg_ref.shape[0], body, 0, unroll=True)
    o_ref[...] = x + 0.5 * gt_ref[0] * acc_ref[...]


def _ffn(h, mods, mod_base, which, g, wg3, wu3, wd3, n_rows, tiles_per_batch, n_batch):
    d = h.shape[1]
    tm = TOKEN_TILE
    const3 = lambda i: (0, 0, 0)
    return pl.pallas_call(
        _ffn_kernel,
        grid=(n_rows // tm,),
        in_specs=[pl.BlockSpec((tm, d), lambda i: (i, 0))]
        + _mod_specs(tiles_per_batch, n_batch, mod_base, which)
        + [
            pl.BlockSpec((1, d), lambda i: (0, 0)),
            pl.BlockSpec(wg3.shape, const3, pipeline_mode=pl.Buffered(1)),
            pl.BlockSpec(wu3.shape, const3, pipeline_mode=pl.Buffered(1)),
            pl.BlockSpec(wd3.shape, const3, pipeline_mode=pl.Buffered(1)),
        ],
        out_specs=pl.BlockSpec((tm, d), lambda i: (i, 0)),
        out_shape=jax.ShapeDtypeStruct((n_rows, d), F32),
        scratch_shapes=[pltpu.VMEM((tm, d), BF16), pltpu.VMEM((tm, d), F32)],
        compiler_params=_cparams(("arbitrary",)),
        name="ffn",
    )(h, mods, mods, mods, g, wg3, wu3, wd3)


N_NORM = 1152
N_ROPE = 640
ATTN_Q_CHUNKS = (0, 1, 2, 3)
NA_Q_CHUNKS = (5, 6)
LOG2E = math.log2(math.e)


def _inproj_kernel(h_ref, sh_ref, sc_ref, g_ref, w_ref, gv_ref, hm_ref, cos_ref, sin_ref,
                   q_ref, k_ref, nq_ref, nk_ref, v_ref, nv_ref, hy_ref):
    xb = _norm_mod(h_ref[...], g_ref[...], sh_ref[0], sc_ref[0]).astype(BF16)
    hm = hm_ref[...]
    cosv = cos_ref[...]
    sinv = sin_ref[...]
    lane = lax.broadcasted_iota(jnp.int32, cosv.shape, 1)
    first_half = (lane % 32) < 16
    outs = []
    p_norm = jnp.dot(xb, w_ref[:, 0:N_NORM], preferred_element_type=F32)
    for c in range(N_NORM // LANES):
        p = p_norm[:, c * LANES:(c + 1) * LANES]
        ms = jnp.dot((p * p).astype(BF16), hm, preferred_element_type=F32)
        y = p * lax.rsqrt(ms + EPS) * gv_ref[:, c * LANES:(c + 1) * LANES]
        if c < N_ROPE // LANES:
            swapped = jnp.where(first_half, pltpu.roll(y, LANES - 16, 1), pltpu.roll(y, 16, 1))
            y = y * cosv + swapped * sinv
        if c in ATTN_Q_CHUNKS:
            y = y * (HEAD_DIM ** -0.5 * LOG2E)
        elif c in NA_Q_CHUNKS:
            y = y * (HEAD_DIM ** -0.5)
        outs.append(y.astype(BF16))
    for c in range(4):
        q_ref[:, c * LANES:(c + 1) * LANES] = outs[c]
    k_ref[...] = outs[4]
    nq_ref[:, 0:LANES] = outs[5]
    nq_ref[:, LANES:2 * LANES] = outs[6]
    nk_ref[:, 0:LANES] = outs[7]
    nk_ref[:, LANES:2 * LANES] = outs[8]
    p_rest = jnp.dot(xb, w_ref[:, N_NORM:], preferred_element_type=F32)
    v = p_rest[:, 0:LANES]
    v_ref[:, 0:LANES] = jnp.where(lane < HEAD_DIM, v, 1.0).astype(BF16)
    v_ref[:, LANES:2 * LANES] = jnp.where(lane < HEAD_DIM, 1.0, v).astype(BF16)
    nv_ref[...] = p_rest[:, LANES:3 * LANES].astype(BF16)
    hy_ref[...] = p_rest[:, 3 * LANES:]


def _inproj(h, mods, mod_base, g, w, gvec, headmean, cos_t, sin_t, tiles_per_batch, n_batch):
    n_rows, d = h.shape
    tm = TOKEN_TILE
    row = lambda i: (i, 0)
    const = lambda i: (0, 0)
    widths = (512, 128, 256, 256, 256, 256)
    out_shape = [jax.ShapeDtypeStruct((n_rows, wd), BF16) for wd in widths]
    out_shape.append(jax.ShapeDtypeStruct((n_rows, 768), F32))
    out_specs = [pl.BlockSpec((tm, wd), row) for wd in widths] + [pl.BlockSpec((tm, 768), row)]
    return pl.pallas_call(
        _inproj_kernel,
        grid=(n_rows // tm,),
        in_specs=[pl.BlockSpec((tm, d), row)]
        + _mod_specs(tiles_per_batch, n_batch, mod_base, (3, 4))
        + [
            pl.BlockSpec((1, d), const),
            pl.BlockSpec(w.shape, const, pipeline_mode=pl.Buffered(1)),
            pl.BlockSpec(gvec.shape, const),
            pl.BlockSpec(headmean.shape, const),
            pl.BlockSpec((tm, LANES), row),
            pl.BlockSpec((tm, LANES), row),
        ],
        out_specs=out_specs,
        out_shape=out_shape,
        compiler_params=_cparams(("arbitrary",)),
        name="inproj",
    )(h, mods, mods, g, w, gvec, headmean, cos_t, sin_t)


def _stack_heads(q):
    lane = lax.broadcasted_iota(jnp.int32, q.shape, 1)
    zero = jnp.zeros_like(q)
    return jnp.concatenate([jnp.where(lane < HEAD_DIM, q, zero), jnp.where(lane >= HEAD_DIM, q, zero)], axis=0)


def _unstack_heads(o2, tq):
    lane = lax.broadcasted_iota(jnp.int32, (tq, LANES), 1)
    return jnp.where(lane < HEAD_DIM, o2[:tq], o2[tq:])


def _qk(q2, k):
    return lax.dot_general(q2, k, (((1,), (1,)), ((), ())), preferred_element_type=F32)


def _gqa_kernel(q_ref, kc_ref, vc_ref, *rest, n_latent_chunks, tk):
    if n_latent_chunks:
        kl_ref, vl_ref, o_ref = rest
    else:
        (o_ref,) = rest
    tq = q_ref.shape[0]
    q2 = _stack_heads(q_ref[...])

    def pv(p, v_ref, rows):
        return jnp.concatenate(
            [jnp.dot(p[:tq], v_ref[rows, 0:LANES], preferred_element_type=F32),
             jnp.dot(p[tq:], v_ref[rows, LANES:2 * LANES], preferred_element_type=F32)], axis=0)

    s = _qk(q2, kc_ref[...])
    m = jnp.max(s, axis=-1, keepdims=True)
    acc = pv(jnp.exp2((s - m).astype(BF16)), vc_ref, slice(None))

    if n_latent_chunks:
        def body(j, carry):
            m, acc = carry
            rows = pl.ds(pl.multiple_of(j * tk, tk), tk)
            s = _qk(q2, kl_ref[rows, :])
            m_new = jnp.maximum(m, jnp.max(s, axis=-1, keepdims=True))
            p = jnp.exp2((s - m_new).astype(BF16))
            return m_new, jnp.exp2(m - m_new) * acc + pv(p, vl_ref, rows)

        m, acc = lax.fori_loop(0, n_latent_chunks, body, (m, acc), unroll=True)
    lane = lax.broadcasted_iota(jnp.int32, (tq, LANES), 1)
    num = jnp.where(lane < HEAD_DIM, acc[:tq], acc[tq:])
    den = pltpu.roll(jnp.where(lane < HEAD_DIM, acc[tq:], acc[:tq]), HEAD_DIM, 1)
    o_ref[...] = (num / den).astype(o_ref.dtype)


def _pair_ctx_kernel(q_ref, k_ref, v_ref, o_ref):
    tq = q_ref.shape[0]
    q2 = _stack_heads(q_ref[...])
    s = _qk(q2, k_ref[...])
    p = jnp.exp(s - jnp.max(s, axis=-1, keepdims=True))
    l = jnp.sum(p, axis=-1, keepdims=True)
    o = jnp.dot(p.astype(BF16), v_ref[...], preferred_element_type=F32)
    o_ref[...] = _unstack_heads(o / l, tq).astype(o_ref.dtype)


def _gqa(q, k, v2, out_rows, n_batch, seq, n_ctx, latent):
    chunks = q.shape[1] // LANES
    ctx_blk0 = n_batch * seq // n_ctx
    tk = min(ATTN_TK, seq)
    if latent:
        tq = ATTN_TQ
        grid = (n_batch, chunks, seq // tq)
        qmap = lambda b, j, i: (b * (seq // tq) + i, j)
        kv_specs = [pl.BlockSpec((seq, LANES), lambda b, j, i: (b, 0)),
                    pl.BlockSpec((seq, 2 * LANES), lambda b, j, i: (b, 0))]
        kv_args = (k, v2)
    else:
        tq = n_ctx
        grid = (n_batch, chunks, 1)
        qmap = lambda b, j, i: (ctx_blk0 + b, j)
        kv_specs, kv_args = [], ()
    return pl.pallas_call(
        functools.partial(_gqa_kernel, n_latent_chunks=seq // tk if latent else 0, tk=tk),
        grid=grid,
        in_specs=[pl.BlockSpec((tq, LANES), qmap),
                  pl.BlockSpec((n_ctx, LANES), lambda b, j, i: (ctx_blk0 + b, 0)),
                  pl.BlockSpec((n_ctx, 2 * LANES), lambda b, j, i: (ctx_blk0 + b, 0))] + kv_specs,
        out_specs=pl.BlockSpec((tq, LANES), qmap),
        out_shape=jax.ShapeDtypeStruct((out_rows, q.shape[1]), BF16),
        compiler_params=_cparams(("arbitrary", "arbitrary", "arbitrary")),
        name="attn_latent" if latent else "attn_ctx",
    )(q, k, v2, *kv_args)


def _pair_ctx_attention(q, k, v, out_rows, n_batch, seq, n_ctx):
    chunks = q.shape[1] // LANES
    ctx_blk0 = n_batch * seq // n_ctx
    spec = pl.BlockSpec((n_ctx, LANES), lambda b, j: (ctx_blk0 + b, j))
    return pl.pallas_call(
        _pair_ctx_kernel,
        grid=(n_batch, chunks),
        in_specs=[spec, spec, spec],
        out_specs=spec,
        out_shape=jax.ShapeDtypeStruct((out_rows, q.shape[1]), BF16),
        compiler_params=_cparams(("arbitrary", "arbitrary")),
        name="na_ctx",
    )(q, k, v)


def _na_kernel(q_ref, kc_ref, vc_ref, kl_ref, vl_ref, bias_ref, o_ref, *, grid_rows):
    i = pl.program_id(2)
    tq = q_ref.shape[0]
    nk = NA_KEY_ROWS * GRID_W
    ks = jnp.clip(i * NA_BLOCK_ROWS - NA_ROWS // 2, 0, grid_rows - NA_KEY_ROWS)
    start = pl.multiple_of(ks * GRID_W, GRID_W)
    q2 = _stack_heads(q_ref[...])
    s_nb = _qk(q2, kl_ref[pl.ds(start, nk), :]) + bias_ref[0].reshape(2 * tq, nk)
    s_cx = _qk(q2, kc_ref[...])
    m = jnp.maximum(jnp.max(s_nb, axis=-1, keepdims=True), jnp.max(s_cx, axis=-1, keepdims=True))
    p_nb = jnp.exp(s_nb - m)
    p_cx = jnp.exp(s_cx - m)
    l = jnp.sum(p_nb, axis=-1, keepdims=True) + jnp.sum(p_cx, axis=-1, keepdims=True)
    o = (jnp.dot(p_nb.astype(BF16), vl_ref[pl.ds(start, nk), :], preferred_element_type=F32)
         + jnp.dot(p_cx.astype(BF16), vc_ref[...], preferred_element_type=F32))
    o_ref[...] = _unstack_heads(o / l, tq).astype(o_ref.dtype)


def _na_block_start(blk, grid_rows):
    return int(np.clip(blk * NA_BLOCK_ROWS - NA_ROWS // 2, 0, grid_rows - NA_KEY_ROWS))


def _na_bias_tables(rpb, grid_rows):
    nblk = grid_rows // NA_BLOCK_ROWS
    n_heads = rpb.shape[0]
    qc = np.arange(GRID_W)[:, None]
    kc = np.arange(GRID_W)[None, :]
    cs = np.clip(qc - NA_COLS // 2, 0, GRID_W - NA_COLS)
    col_ok = (kc >= cs) & (kc < cs + NA_COLS)
    col_hot = ((kc - qc + NA_COLS - 1)[:, :, None] == np.arange(2 * NA_COLS - 1)) & col_ok[:, :, None]
    by_col = jnp.einsum("abe,hde->hdab", jnp.asarray(col_hot, F32), rpb, precision=HIGHEST)
    tabs = []
    for blk in (0, min(1, nblk - 1), nblk - 1):
        qr = blk * NA_BLOCK_ROWS + np.arange(NA_BLOCK_ROWS)[:, None]
        kr = _na_block_start(blk, grid_rows) + np.arange(NA_KEY_ROWS)[None, :]
        rs = np.clip(qr - NA_ROWS // 2, 0, grid_rows - NA_ROWS)
        row_ok = (kr >= rs) & (kr < rs + NA_ROWS)
        row_hot = ((kr - qr + NA_ROWS - 1)[:, :, None] == np.arange(2 * NA_ROWS - 1)) & row_ok[:, :, None]
        b = jnp.einsum("qkd,hdab->hqakb", jnp.asarray(row_hot, F32), by_col, precision=HIGHEST)
        valid = row_ok[:, None, :, None] & col_ok[None, :, None, :]
        b = jnp.where(jnp.asarray(valid)[None], b, NEG_BIG)
        tabs.append(b.reshape(n_heads, NA_BLOCK_ROWS * GRID_W, NA_KEY_ROWS * GRID_W))
    return jnp.stack(tabs)


def _na_latent(nq, nk, nv, bias, n_rows_out, n_batch, seq, n_ctx):
    chunks = nq.shape[1] // LANES
    tq = NA_BLOCK_ROWS * GRID_W
    nblk = seq // tq
    ctx_blk0 = n_batch * seq // n_ctx
    qmap = lambda b, j, i: (b * nblk + i, j)
    ctx_spec = pl.BlockSpec((n_ctx, LANES), lambda b, j, i: (ctx_blk0 + b, j))
    lat_spec = pl.BlockSpec((seq, LANES), lambda b, j, i: (b, j))
    variant = lambda i: jnp.where(i == 0, 0, jnp.where(i == nblk - 1, 2, 1))
    bias_spec = pl.BlockSpec((1, 2, tq, NA_KEY_ROWS * GRID_W), lambda b, j, i: (variant(i), j, 0, 0))
    return pl.pallas_call(
        functools.partial(_na_kernel, grid_rows=seq // GRID_W),
        grid=(n_batch, chunks, nblk),
        in_specs=[pl.BlockSpec((tq, LANES), qmap), ctx_spec, ctx_spec, lat_spec, lat_spec, bias_spec],
        out_specs=pl.BlockSpec((tq, LANES), qmap),
        out_shape=jax.ShapeDtypeStruct((n_rows_out, nq.shape[1]), BF16),
        compiler_params=_cparams(("arbitrary", "arbitrary", "arbitrary")),
        name="na_latent",
    )(nq, nk, nv, nk, nv, bias)


def _hconv_kernel(x_ref, prev_ref, next_ref, w_ref, b_ref, z_ref, x0_ref, *, tiles_per_seq, n_latent_tiles):
    i = pl.program_id(0)
    x = x_ref[...]
    t = x.shape[0]
    row = lax.broadcasted_iota(jnp.int32, x.shape, 0)
    first = jnp.logical_or(i % tiles_per_seq == 0, i >= n_latent_tiles)
    last = jnp.logical_or(i % tiles_per_seq == tiles_per_seq - 1, i >= n_latent_tiles)
    prev_row = jnp.where(first, 0.0, prev_ref[7:8, :])
    next_row = jnp.where(last, 0.0, next_ref[0:1, :])
    up = jnp.where(row == 0, prev_row, pltpu.roll(x, 1, 0))
    dn = jnp.where(row == t - 1, next_row, pltpu.roll(x, t - 1, 0))
    uc = up * w_ref[0:1, :] + x * w_ref[1:2, :] + dn * w_ref[2:3, :] + b_ref[...]
    cw = uc.shape[1] // 3
    x0_ref[...] = uc[:, :cw]
    z_ref[...] = uc[:, 2 * cw:] * uc[:, cw:2 * cw]


def _hconv(hy, conv_w, conv_b, seq, n_ctx, n_batch):
    n_rows, ch = hy.shape
    tt = n_ctx
    n_tiles = n_rows // tt
    cw = ch // 3
    return pl.pallas_call(
        functools.partial(_hconv_kernel, tiles_per_seq=seq // tt, n_latent_tiles=n_batch * seq // tt),
        grid=(n_tiles,),
        in_specs=[
            pl.BlockSpec((tt, ch), lambda i: (i, 0)),
            pl.BlockSpec((8, ch), lambda i: (jnp.maximum(i * (tt // 8) - 1, 0), 0)),
            pl.BlockSpec((8, ch), lambda i: (jnp.minimum((i + 1) * (tt // 8), n_rows // 8 - 1), 0)),
            pl.BlockSpec((3, ch), lambda i: (0, 0)),
            pl.BlockSpec((1, ch), lambda i: (0, 0)),
        ],
        out_specs=[pl.BlockSpec((tt, cw), lambda i: (i, 0))] * 2,
        out_shape=[jax.ShapeDtypeStruct((n_rows, cw), F32)] * 2,
        compiler_params=_cparams(("arbitrary",)),
        name="hyena_conv",
    )(hy, hy, hy, conv_w, conv_b)


def _hfilt_kernel(f_ref, w1_ref, b1_ref, w2_ref, b2_ref, w3_ref, fr_ref, dl_ref, k_ref, l1_ref, *, n):
    i = pl.program_id(0)
    f = f_ref[...]
    fr = fr_ref[...]
    h1 = jnp.sin(fr * (jnp.dot(f, w1_ref[...], precision=HIGHEST, preferred_element_type=F32) + b1_ref[...]))
    h2 = jnp.sin(fr * (jnp.dot(h1, w2_ref[...], precision=HIGHEST, preferred_element_type=F32) + b2_ref[...]))
    o = jnp.dot(h2, w3_ref[...], precision=HIGHEST, preferred_element_type=F32)
    cw = o.shape[1] // 2
    t = f.shape[0]
    row = i * t + lax.broadcasted_iota(jnp.int32, (t, cw), 0)
    decay = jnp.exp(-f[:, 0:1] * jnp.abs(dl_ref[...]))
    k = jnp.where(row < n, o[:, :cw], o[:, cw:]) * decay
    k = jnp.where(row == n, 0.0, k)
    k_ref[...] = k

    @pl.when(i == 0)
    def _():
        l1_ref[...] = jnp.zeros_like(l1_ref)

    l1_ref[...] += jnp.sum(jnp.abs(k), axis=0, keepdims=True)


def _filter_feats(n):
    rows = np.arange(2 * n)
    pos = np.where(rows <= n, rows, 2 * n - rows).astype(np.float32)
    pos = jnp.asarray(np.minimum(pos, n - 1))
    t = pos / max(n - 1, 1)
    bands = jnp.linspace(1e-4, FILTER_BANDS - 1, FILTER_BANDS, dtype=F32)
    ang = (2.0 * math.pi / n) * pos[:, None] * bands[None, :]
    feats = jnp.concatenate([t[:, None], jnp.cos(ang), -jnp.sin(ang)], axis=-1)
    return jnp.pad(feats, ((0, 0), (0, LANES - feats.shape[1])))


def _pad_to(a, shape):
    return jnp.pad(a, [(0, s - d) for s, d in zip(shape, a.shape)])


def _hfilter(n, w1, b1, w2, b2, w3, freq, deltas):
    feats = _filter_feats(n)
    w1, b1, freq = _pad_to(w1, (LANES, LANES)), _pad_to(b1, (1, LANES)), _pad_to(freq, (1, LANES))
    w2, b2, w3 = _pad_to(w2, (LANES, LANES)), _pad_to(b2, (1, LANES)), _pad_to(w3, (LANES, w3.shape[1]))
    rows, emb = feats.shape
    t = min(1024, rows)
    cw = w3.shape[1] // 2
    const = lambda i: (0, 0)
    full = lambda a: pl.BlockSpec(a.shape, const)
    return pl.pallas_call(
        functools.partial(_hfilt_kernel, n=n),
        grid=(rows // t,),
        in_specs=[pl.BlockSpec((t, emb), lambda i: (i, 0)), full(w1), full(b1), full(w2), full(b2), full(w3),
                  full(freq), full(deltas)],
        out_specs=[pl.BlockSpec((t, cw), lambda i: (i, 0)), pl.BlockSpec((1, cw), const)],
        out_shape=[jax.ShapeDtypeStruct((rows, cw), F32), jax.ShapeDtypeStruct((1, cw), F32)],
        compiler_params=_cparams(("arbitrary",)),
        name="hyena_filter",
    )(feats, w1, b1, w2, b2, w3, freq, deltas)


def _split_bf16(a):
    hi = a.astype(BF16)
    return hi, (a - hi.astype(F32)).astype(BF16)


def _dot3(a_split, b):
    ah, al = a_split
    bh, bl = _split_bf16(b)
    return (jnp.dot(ah, bh, preferred_element_type=F32) + jnp.dot(ah, bl, preferred_element_type=F32)
            + jnp.dot(al, bh, preferred_element_type=F32))


def _dft_outer_kernel(fh_ref, fl_ref, x_ref, o_ref):
    c = x_ref.shape[-1]
    o_ref[...] = _dot3((fh_ref[...], fl_ref[...]), x_ref[...].reshape(fh_ref.shape[1], c)).reshape(o_ref.shape)


def _kron_split(fmat):
    return _split_bf16(jnp.kron(fmat, jnp.eye(OUTER_COLS, dtype=F32)))


def _dft_outer_fwd(fmat, x3, nb, n1_len):
    k = fmat.shape[1]
    c = x3.shape[2]
    fh, fl = _kron_split(fmat)
    fspec = pl.BlockSpec(fh.shape, lambda b, j: (0, 0), pipeline_mode=pl.Buffered(1))
    return pl.pallas_call(
        _dft_outer_kernel,
        grid=(nb, DFT_N2 // OUTER_COLS),
        in_specs=[fspec, fspec, pl.BlockSpec((k, OUTER_COLS, c), lambda b, j: (b, j, 0))],
        out_specs=pl.BlockSpec((1, 2, n1_len, OUTER_COLS, c), lambda b, j: (b, 0, 0, j, 0)),
        out_shape=jax.ShapeDtypeStruct((nb, 2, n1_len, DFT_N2, c), F32),
        compiler_params=_cparams(("arbitrary", "arbitrary")),
        name="dft_outer_fwd",
    )(fh, fl, x3)


def _dft_outer_inv(fmat, d5):
    nb, _, n1_len, _, c = d5.shape
    half = fmat.shape[0]
    fh, fl = _kron_split(fmat)
    fspec = pl.BlockSpec(fh.shape, lambda b, j: (0, 0), pipeline_mode=pl.Buffered(1))
    return pl.pallas_call(
        _dft_outer_kernel,
        grid=(nb, DFT_N2 // OUTER_COLS),
        in_specs=[fspec, fspec, pl.BlockSpec((1, 2, n1_len, OUTER_COLS, c), lambda b, j: (b, 0, 0, j, 0))],
        out_specs=pl.BlockSpec((half, OUTER_COLS, c), lambda b, j: (b, j, 0)),
        out_shape=jax.ShapeDtypeStruct((nb * half, DFT_N2, c), F32),
        compiler_params=_cparams(("arbitrary", "arbitrary")),
        name="dft_outer_inv",
    )(fh, fl, d5)


def _inner_matrices(fc, fs, tc, ts):
    gr = fc * tc - fs * ts
    gi = -(fc * ts + fs * tc)
    return gr, gi


def _real_form(gr, gi):
    return jnp.concatenate([jnp.concatenate([gr, -gi], axis=1), jnp.concatenate([gi, gr], axis=1)], axis=0)


def _dft_inner_filter_kernel(fc_ref, fs_ref, tc_ref, ts_ref, b_ref, o_ref):
    fc, fs = fc_ref[...], fs_ref[...]
    n2, c = b_ref.shape[3], b_ref.shape[4]

    def body(kk, carry):
        g = _real_form(*_inner_matrices(fc, fs, tc_ref[kk], ts_ref[kk]))
        o_ref[kk] = _dot3(_split_bf16(g), b_ref[0, :, kk].reshape(2 * n2, c))
        return carry

    lax.fori_loop(0, b_ref.shape[2], body, 0, unroll=2)


def _dft_inner_conv_kernel(fc_ref, fs_ref, tc_ref, ts_ref, b_ref, kh_ref, sc_ref, o_ref):
    fc, fs = fc_ref[...], fs_ref[...]
    nb, n2, c = b_ref.shape[0], b_ref.shape[3], b_ref.shape[4]
    scale = sc_ref[...]

    def body(kk, carry):
        gr, gi = _inner_matrices(fc, fs, tc_ref[kk], ts_ref[kk])
        g = _split_bf16(_real_form(gr, gi))
        gh = _split_bf16(_real_form(gr.T, -gi.T))
        kh = kh_ref[kk]
        kr, ki = kh[:n2], kh[n2:]
        for b in range(nb):
            x = _dot3(g, b_ref[b, :, kk].reshape(2 * n2, c))
            xr, xi = x[:n2], x[n2:]
            y = jnp.concatenate([(xr * kr - xi * ki) * scale, (xr * ki + xi * kr) * scale], axis=0)
            o_ref[b, :, kk] = _dot3(gh, y).reshape(2, n2, c)
        return carry

    lax.fori_loop(0, b_ref.shape[2], body, 0, unroll=2)


def _dft_tables(n1_len):
    n_total = n1_len * DFT_N2
    a = np.arange(DFT_N2)
    ang = 2.0 * np.pi * np.outer(a, a) / DFT_N2
    k1 = np.arange(n1_len)
    tw = 2.0 * np.pi * np.outer(k1, a) / n_total
    o = 2.0 * np.pi * np.outer(k1, k1) / n1_len
    f32 = lambda v: jnp.asarray(v.astype(np.float32))
    outer_fwd = np.concatenate([np.cos(o), -np.sin(o)], axis=0)
    outer_inv = np.concatenate([np.cos(o), -np.sin(o)], axis=1)
    return dict(
        fc=f32(np.cos(ang)), fs=f32(np.sin(ang)),
        tc=f32(np.cos(tw)).reshape(n1_len, 1, DFT_N2), ts=f32(np.sin(tw)).reshape(n1_len, 1, DFT_N2),
        outer_fwd_full=f32(outer_fwd), outer_fwd_half=f32(outer_fwd[:, :n1_len // 2]),
        outer_inv_half=f32(outer_inv[:n1_len // 2]),
    )


def _inner_specs():
    sq = pl.BlockSpec((DFT_N2, DFT_N2), lambda k: (0, 0))
    tw = pl.BlockSpec((INNER_K1, 1, DFT_N2), lambda k: (k, 0, 0))
    return [sq, sq, tw, tw]


def _filter_spectrum(kfull, tabs, n1_len):
    c = kfull.shape[1]
    b5 = _dft_outer_fwd(tabs["outer_fwd_full"], kfull.reshape(n1_len, DFT_N2, c), 1, n1_len)
    return pl.pallas_call(
        _dft_inner_filter_kernel,
        grid=(n1_len // INNER_K1,),
        in_specs=_inner_specs() + [pl.BlockSpec((1, 2, INNER_K1, DFT_N2, c), lambda k: (0, 0, k, 0, 0))],
        out_specs=pl.BlockSpec((INNER_K1, 2 * DFT_N2, c), lambda k: (k, 0, 0)),
        out_shape=jax.ShapeDtypeStruct((n1_len, 2 * DFT_N2, c), F32),
        compiler_params=_cparams(("arbitrary",)),
        name="dft_inner_filter",
    )(tabs["fc"], tabs["fs"], tabs["tc"], tabs["ts"], b5)


def _long_conv(z_all, nb, khat, scale, tabs, n1_len):
    c = z_all.shape[1]
    bz = _dft_outer_fwd(tabs["outer_fwd_half"], z_all.reshape(z_all.shape[0] // DFT_N2, DFT_N2, c), nb, n1_len)
    blk = pl.BlockSpec((nb, 2, INNER_K1, DFT_N2, c), lambda k: (0, 0, k, 0, 0))
    d = pl.pallas_call(
        _dft_inner_conv_kernel,
        grid=(n1_len // INNER_K1,),
        in_specs=_inner_specs() + [blk, pl.BlockSpec((INNER_K1, 2 * DFT_N2, c), lambda k: (k, 0, 0)),
                                   pl.BlockSpec((1, c), lambda k: (0, 0))],
        out_specs=blk,
        out_shape=jax.ShapeDtypeStruct(bz.shape, F32),
        compiler_params=_cparams(("arbitrary",)),
        name="dft_inner_conv",
    )(tabs["fc"], tabs["fs"], tabs["tc"], tabs["ts"], bz, khat, scale)
    y3 = _dft_outer_inv(tabs["outer_inv_half"], d)
    return y3.reshape(y3.shape[0] * DFT_N2, c)


def _hyena_ctx_kernel(ff_ref, fi_ref, k_ref, l1_ref, z_ref, o_ref):
    ff = ff_ref[...]
    n = z_ref.shape[0]
    big = ff.shape[1]
    kh = jnp.dot(ff, k_ref[...], precision=HIGHEST, preferred_element_type=F32)
    x = jnp.dot(ff[:, :n], z_ref[...], precision=HIGHEST, preferred_element_type=F32)
    kr, ki = kh[:big], kh[big:]
    xr, xi = x[:big], x[big:]
    scale = 1.0 / (l1_ref[...] * big)
    y = jnp.concatenate([(xr * kr - xi * ki) * scale, (xr * ki + xi * kr) * scale], axis=0)
    o_ref[...] = jnp.dot(fi_ref[...], y, precision=HIGHEST, preferred_element_type=F32)


def _hyena_ctx(z, kfull, l1, row0_blk, n_batch, n):
    big = 2 * n
    a = np.arange(big)
    ang = 2.0 * np.pi * np.outer(a, a) / big
    ff = jnp.asarray(np.concatenate([np.cos(ang), -np.sin(ang)], axis=0).astype(np.float32))
    fi = jnp.asarray(np.concatenate([np.cos(ang), -np.sin(ang)], axis=1)[:n].astype(np.float32))
    c = z.shape[1]
    const = lambda b: (0, 0)
    return pl.pallas_call(
        _hyena_ctx_kernel,
        grid=(n_batch,),
        in_specs=[pl.BlockSpec(ff.shape, const), pl.BlockSpec(fi.shape, const), pl.BlockSpec(kfull.shape, const),
                  pl.BlockSpec((1, c), const), pl.BlockSpec((n, c), lambda b: (row0_blk + b, 0))],
        out_specs=pl.BlockSpec((n, c), lambda b: (b, 0)),
        out_shape=jax.ShapeDtypeStruct((n_batch * n, c), F32),
        compiler_params=_cparams(("arbitrary",)),
        name="hyena_ctx",
    )(ff, fi, kfull, l1, z)


def _outproj_kernel(h_ref, gt_ref, ya_ref, yconv_ref, z_ref, x0_ref, skip_ref, yc_ref, wa_ref, wb_ref, wc_ref, o_ref):
    yb = ((yconv_ref[...] + z_ref[...] * skip_ref[...]) * x0_ref[...]).astype(BF16)
    mix = (jnp.dot(ya_ref[...], wa_ref[...], preferred_element_type=F32)
           + jnp.dot(yb, wb_ref[...], preferred_element_type=F32)
           + jnp.dot(yc_ref[...], wc_ref[...], preferred_element_type=F32))
    o_ref[...] = h_ref[...] + gt_ref[0] * mix


def _outproj(h, mods, mod_base, ya, yconv, z, x0, skip, yc, wa, wb, wc, n_rows, tiles_per_batch, n_batch):
    d = h.shape[1]
    tm = TOKEN_TILE
    row = lambda i: (i, 0)
    const = lambda i: (0, 0)
    rows = lambda a: pl.BlockSpec((tm, a.shape[1]), row)
    full = lambda a: pl.BlockSpec(a.shape, const)
    return pl.pallas_call(
        _outproj_kernel,
        grid=(n_rows // tm,),
        in_specs=[rows(h)] + _mod_specs(tiles_per_batch, n_batch, mod_base, (5,))
        + [rows(ya), rows(yconv), rows(z), rows(x0), full(skip), rows(yc), full(wa), full(wb), full(wc)],
        out_specs=pl.BlockSpec((tm, d), row),
        out_shape=jax.ShapeDtypeStruct((n_rows, d), F32),
        compiler_params=_cparams(("arbitrary",)),
        name="outproj",
    )(h, mods, ya, yconv, z, x0, skip, yc, wa, wb, wc)


def _rope_tables(seq, n_batch, n_ctx_rows):
    t = jnp.arange(seq, dtype=jnp.int32)
    rows = (t // GRID_W).astype(F32)
    cols = (t % GRID_W).astype(F32)
    nf = HEAD_DIM // 4
    inv = ROPE_THETA ** (-jnp.arange(nf, dtype=F32) / nf)
    ar, ac = rows[:, None] * inv, cols[:, None] * inv
    cos = jnp.concatenate([jnp.cos(ar), jnp.cos(ar), jnp.cos(ac), jnp.cos(ac)], axis=1)
    sin = jnp.concatenate([-jnp.sin(ar), jnp.sin(ar), -jnp.sin(ac), jnp.sin(ac)], axis=1)
    cos = jnp.tile(cos, (n_batch, LANES // HEAD_DIM))
    sin = jnp.tile(sin, (n_batch, LANES // HEAD_DIM))
    cos = jnp.concatenate([cos, jnp.ones((n_ctx_rows, LANES), F32)], axis=0)
    sin = jnp.concatenate([sin, jnp.zeros((n_ctx_rows, LANES), F32)], axis=0)
    return cos, sin


def _head_mean_matrix():
    a = np.arange(LANES)
    return jnp.asarray((a[:, None] // HEAD_DIM == a[None, :] // HEAD_DIM).astype(np.float32) / HEAD_DIM, dtype=BF16)


def _chunk3(w, axis):
    if axis == 1:
        d, hdn = w.shape
        return w.reshape(d, hdn // FFN_CHUNK, FFN_CHUNK).transpose(1, 0, 2).astype(BF16)
    hdn, d = w.shape
    return w.reshape(hdn // FFN_CHUNK, FFN_CHUNK, d).astype(BF16)


def kernel(x, c, ctx, c_ctx, w_ada, b_ada, g_ffn1, w_ffn1_gate, w_ffn1_up, w_ffn1_down, g_mix, w_in, w_out, g_q_attn, g_k_attn, conv_w, conv_b, filt_w1, filt_b1, filt_w2, filt_b2, filt_w3, filt_freq, hyena_skip, g_q_na, g_k_na, na_rpb, g_ffn2, w_ffn2_gate, w_ffn2_up, w_ffn2_down):
    n_batch, seq, d = x.shape
    n_ctx = ctx.shape[1]
    depth = w_ada.shape[0]
    assert d == _D and n_batch + 1 <= 8 and seq % TOKEN_TILE == 0 and seq % (NA_BLOCK_ROWS * GRID_W) == 0
    assert n_ctx == 256 and (n_batch * n_ctx) % TOKEN_TILE == 0
    n_lat = n_batch * seq
    n_all = n_lat + n_batch * n_ctx
    tiles_per_batch = seq // TOKEN_TILE
    hy_w = conv_w.shape[2] // 3
    n1_len = 2 * seq // DFT_N2

    cond8 = jnp.zeros((8, d), F32).at[:n_batch].set(c).at[n_batch].set(c_ctx)
    mods = _adaln_all(cond8, w_ada, b_ada).reshape(depth * 8 * N_MOD, 1, d)
    h = jnp.concatenate([x.reshape(n_lat, d), ctx.reshape(n_batch * n_ctx, d)], axis=0)

    cos_t, sin_t = _rope_tables(seq, n_batch, n_batch * n_ctx)
    headmean = _head_mean_matrix()
    tabs = _dft_tables(n1_len)
    deltas = jnp.linspace(math.log(DECAY_TARGET) / SLOW_DECAY_PCT, math.log(DECAY_TARGET) / FAST_DECAY_PCT,
                          hy_w, dtype=F32).reshape(1, hy_w)

    q_heads = [hd for pair in range(ATTN_HEADS // 2) for hd in (pair, pair + ATTN_HEADS // 2)]
    q_segs = [(hd * HEAD_DIM, (hd + 1) * HEAD_DIM) for hd in q_heads]
    in_segs = q_segs + [(512, 640), (1536, 1792), (1792, 2048), (640, 768), (2048, 2304), (768, 1536)]

    for l in range(depth):
        last = l == depth - 1
        base = l * 8 * N_MOD
        rep = lambda g, k: jnp.tile(g, k)
        gvec = jnp.concatenate([rep(g_q_attn[l], ATTN_HEADS), rep(g_k_attn[l], ATTN_KV_HEADS),
                                rep(g_q_na[l], NA_HEADS), rep(g_k_na[l], NA_HEADS)]).reshape(1, N_NORM)

        h = _ffn(h, mods, base, (0, 1, 2), g_ffn1[l].reshape(1, d), _chunk3(w_ffn1_gate[l], 1),
                 _chunk3(w_ffn1_up[l], 1), _chunk3(w_ffn1_down[l], 0), n_all, tiles_per_batch, n_batch)

        w_in_l = jnp.concatenate([w_in[l][:, a:b] for a, b in in_segs], axis=1).astype(BF16)
        q, k, nq, nk, v, nv, hy = _inproj(h, mods, base, g_mix[l].reshape(1, d), w_in_l,
                                          gvec, headmean, cos_t, sin_t, tiles_per_batch, n_batch)

        out_rows = n_lat if last else n_all
        ya = _gqa(q, k, v, out_rows, n_batch, seq, n_ctx, latent=True)
        bias = _na_bias_tables(na_rpb[l], seq // GRID_W)
        yc = _na_latent(nq, nk, nv, bias, out_rows, n_batch, seq, n_ctx)

        z, x0 = _hconv(hy, conv_w[l], conv_b[l].reshape(1, -1), seq, n_ctx, n_batch)
        fargs = (filt_w1[l], filt_b1[l].reshape(1, -1), filt_w2[l], filt_b2[l].reshape(1, -1), filt_w3[l],
                 filt_freq[l].reshape(1, -1), deltas)
        kfull, l1 = _hfilter(seq, *fargs)
        khat = _filter_spectrum(kfull, tabs, n1_len)
        yconv = _long_conv(z, n_batch, khat, 1.0 / (l1 * (2 * seq)), tabs, n1_len)

        if not last:
            ya_c = _gqa(q, k, v, n_all, n_batch, seq, n_ctx, latent=False)
            yc_c = _pair_ctx_attention(nq, nk, nv, n_all, n_batch, seq, n_ctx)
            ya = lax.dynamic_update_slice(ya, ya_c[n_lat:], (n_lat, 0))
            yc = lax.dynamic_update_slice(yc, yc_c[n_lat:], (n_lat, 0))
            kfull_c, l1_c = _hfilter(n_ctx, *fargs)
            yconv_c = _hyena_ctx(z, kfull_c, l1_c, n_lat // n_ctx, n_batch, n_ctx)
            yconv = jnp.concatenate([yconv, yconv_c], axis=0)

        w_o = w_out[l]
        wa = jnp.concatenate([w_o[a:b] for a, b in q_segs], axis=0).astype(BF16)
        wb = w_o[512:512 + hy_w].astype(BF16)
        wc = w_o[512 + hy_w:].astype(BF16)
        h = _outproj(h, mods, base, ya, yconv, z, x0, hyena_skip[l].reshape(1, hy_w), yc, wa, wb, wc,
                     out_rows, tiles_per_batch, n_batch)
        h = _ffn(h, mods, base, (6, 7, 8), g_ffn2[l].reshape(1, d), _chunk3(w_ffn2_gate[l], 1),
                 _chunk3(w_ffn2_up[l], 1), _chunk3(w_ffn2_down[l], 0), out_rows, tiles_per_batch, n_batch)

    return h[:n_lat].reshape(n_batch, seq, d)
```

```python
import functools
import math

import numpy as np
import jax
import jax.numpy as jnp
from jax import lax
from jax.experimental import pallas as pl
from jax.experimental.pallas import tpu as pltpu

F32 = jnp.float32
BF16 = jnp.bfloat16
HIGHEST = lax.Precision.HIGHEST

HEAD_DIM = 64
GRID_W = 64
ATTN_HEADS = 8
ATTN_KV_HEADS = 2
NA_HEADS = 4
NA_ROWS = 8
NA_COLS = 16
FILTER_BANDS = 16
DECAY_TARGET = 1e-2
FAST_DECAY_PCT = 0.3
SLOW_DECAY_PCT = 1.5
ROPE_THETA = 10000.0
EPS = 1e-6
N_MOD = 9

LANES = 128
TOKEN_TILE = 512
FFN_CHUNK = 256
ATTN_TQ = 512
ATTN_TK = 2048
NA_BLOCK_ROWS = 8
NA_KEY_ROWS = 16
NA_SUB_ROWS = 4
DFT_N2 = 128
OUTER_COLS = 8
INNER_K1 = 8
NEG_BIG = -1e30
VMEM_LIMIT = 56 * 1024 * 1024


def _cparams(sem):
    return pltpu.CompilerParams(dimension_semantics=sem, vmem_limit_bytes=VMEM_LIMIT)


def _ada_kernel(c_ref, w_ref, b_ref, o_ref):
    c = c_ref[...]
    s = c * jax.nn.sigmoid(c)
    o_ref[0] = jnp.dot(s, w_ref[0], precision=HIGHEST, preferred_element_type=F32) + b_ref[0]


def _adaln_all(cond8, w_ada, b_ada):
    depth, d, nd = w_ada.shape
    tn = 1024
    return pl.pallas_call(
        _ada_kernel,
        grid=(depth, nd // tn),
        in_specs=[
            pl.BlockSpec((8, d), lambda l, j: (0, 0)),
            pl.BlockSpec((1, d, tn), lambda l, j: (l, 0, j)),
            pl.BlockSpec((1, 1, tn), lambda l, j: (l, 0, j)),
        ],
        out_specs=pl.BlockSpec((1, 8, tn), lambda l, j: (l, 0, j)),
        out_shape=jax.ShapeDtypeStruct((depth, 8, nd), F32),
        compiler_params=_cparams(("arbitrary", "arbitrary")),
        name="adaln",
    )(cond8, w_ada, b_ada.reshape(depth, 1, nd))


def _norm_mod(x, g, sh, sc):
    ms = jnp.mean(x * x, axis=-1, keepdims=True)
    y = x * lax.rsqrt(ms + EPS) * g
    return y * (1.0 + sc) + sh


def _mod_specs(tiles_per_batch, n_batch, base, idxs):
    def spec(which):
        return pl.BlockSpec(
            (1, 1, _D),
            lambda i, w=which: (base + jnp.minimum(i // tiles_per_batch, n_batch) * N_MOD + w, 0, 0))
    return [spec(w) for w in idxs]


_D = 1024


def _ffn_kernel(h_ref, sh_ref, sc_ref, gt_ref, g_ref, wg_ref, wu_ref, wd_ref, o_ref, xb_ref, acc_ref):
    x = h_ref[...]
    xb_ref[...] = _norm_mod(x, g_ref[...], sh_ref[0], sc_ref[0]).astype(BF16)

    for k in range(wg_ref.shape[1] // FFN_CHUNK):
        cols = slice(k * FFN_CHUNK, (k + 1) * FFN_CHUNK)
        xb = xb_ref[...]
        a = jnp.dot(xb, wg_ref[:, cols], preferred_element_type=F32)
        u = jnp.dot(xb, wu_ref[:, cols], preferred_element_type=F32)
        act = (a * jax.nn.sigmoid(a) * u).astype(BF16)
        part = jnp.dot(act, wd_ref[cols, :], preferred_element_type=F32)
        if k == 0:
            acc_ref[...] = part
        else:
            acc_ref[...] += part
    o_ref[...] = x + 0.5 * gt_ref[0] * acc_ref[...]


def _ffn(h, mods, mod_base, which, g, wg3, wu3, wd3, n_rows, tiles_per_batch, n_batch):
    d = h.shape[1]
    tm = TOKEN_TILE
    const3 = lambda i: (0, 0)
    return pl.pallas_call(
        _ffn_kernel,
        grid=(n_rows // tm,),
        in_specs=[pl.BlockSpec((tm, d), lambda i: (i, 0))]
        + _mod_specs(tiles_per_batch, n_batch, mod_base, which)
        + [
            pl.BlockSpec((1, d), lambda i: (0, 0)),
            pl.BlockSpec(wg3.shape, const3, pipeline_mode=pl.Buffered(1)),
            pl.BlockSpec(wu3.shape, const3, pipeline_mode=pl.Buffered(1)),
            pl.BlockSpec(wd3.shape, const3, pipeline_mode=pl.Buffered(1)),
        ],
        out_specs=pl.BlockSpec((tm, d), lambda i: (i, 0)),
        out_shape=jax.ShapeDtypeStruct((n_rows, d), F32),
        scratch_shapes=[pltpu.VMEM((tm, d), BF16), pltpu.VMEM((tm, d), F32)],
        compiler_params=_cparams(("arbitrary",)),
        name="ffn",
    )(h, mods, mods, mods, g, wg3, wu3, wd3)


N_NORM = 1152
N_ROPE = 640
ATTN_Q_CHUNKS = (0, 1, 2, 3)
NA_Q_CHUNKS = (5, 6)
LOG2E = math.log2(math.e)


def _inproj_kernel(h_ref, sh_ref, sc_ref, g_ref, w_ref, gv_ref, hm_ref, cos_ref, sin_ref,
                   q_ref, k_ref, nq_ref, nk_ref, v_ref, nv_ref, hy_ref):
    xb = _norm_mod(h_ref[...], g_ref[...], sh_ref[0], sc_ref[0]).astype(BF16)
    hm = hm_ref[...]
    cosv = cos_ref[...]
    sinv = sin_ref[...]
    lane = lax.broadcasted_iota(jnp.int32, cosv.shape, 1)
    first_half = (lane % 32) < 16
    outs = []
    p_norm = jnp.dot(xb, w_ref[:, 0:N_NORM], preferred_element_type=F32)
    for c in range(N_NORM // LANES):
        p = p_norm[:, c * LANES:(c + 1) * LANES]
        ms = jnp.dot((p * p).astype(BF16), hm, preferred_element_type=F32)
        y = p * lax.rsqrt(ms + EPS) * gv_ref[:, c * LANES:(c + 1) * LANES]
        if c < N_ROPE // LANES:
            swapped = jnp.where(first_half, pltpu.roll(y, LANES - 16, 1), pltpu.roll(y, 16, 1))
            y = y * cosv + swapped * sinv
        if c in ATTN_Q_CHUNKS or c in NA_Q_CHUNKS:
            y = y * (HEAD_DIM ** -0.5 * LOG2E)
        outs.append(y.astype(BF16))
    for c in range(4):
        q_ref[:, c * LANES:(c + 1) * LANES] = outs[c]
    k_ref[...] = outs[4]
    nq_ref[:, 0:LANES] = outs[5]
    nq_ref[:, LANES:2 * LANES] = outs[6]
    nk_ref[:, 0:LANES] = outs[7]
    nk_ref[:, LANES:2 * LANES] = outs[8]
    p_rest = jnp.dot(xb, w_ref[:, N_NORM:], preferred_element_type=F32)
    v = p_rest[:, 0:LANES]
    v_ref[:, 0:LANES] = jnp.where(lane < HEAD_DIM, v, 1.0).astype(BF16)
    v_ref[:, LANES:2 * LANES] = jnp.where(lane < HEAD_DIM, 1.0, v).astype(BF16)
    for c in range(NA_HEADS // 2):
        nv = p_rest[:, (1 + c) * LANES:(2 + c) * LANES]
        nv_ref[:, 2 * c * LANES:(2 * c + 1) * LANES] = jnp.where(lane < HEAD_DIM, nv, 1.0).astype(BF16)
        nv_ref[:, (2 * c + 1) * LANES:(2 * c + 2) * LANES] = jnp.where(lane < HEAD_DIM, 1.0, nv).astype(BF16)
    hy_ref[...] = p_rest[:, 3 * LANES:]


def _inproj(h, mods, mod_base, g, w, gvec, headmean, cos_t, sin_t, tiles_per_batch, n_batch):
    n_rows, d = h.shape
    tm = TOKEN_TILE
    row = lambda i: (i, 0)
    const = lambda i: (0, 0)
    widths = (512, 128, 256, 256, 256, 512)
    out_shape = [jax.ShapeDtypeStruct((n_rows, wd), BF16) for wd in widths]
    out_shape.append(jax.ShapeDtypeStruct((n_rows, 768), F32))
    out_specs = [pl.BlockSpec((tm, wd), row) for wd in widths] + [pl.BlockSpec((tm, 768), row)]
    return pl.pallas_call(
        _inproj_kernel,
        grid=(n_rows // tm,),
        in_specs=[pl.BlockSpec((tm, d), row)]
        + _mod_specs(tiles_per_batch, n_batch, mod_base, (3, 4))
        + [
            pl.BlockSpec((1, d), const),
            pl.BlockSpec(w.shape, const, pipeline_mode=pl.Buffered(1)),
            pl.BlockSpec(gvec.shape, const),
            pl.BlockSpec(headmean.shape, const),
            pl.BlockSpec((tm, LANES), row),
            pl.BlockSpec((tm, LANES), row),
        ],
        out_specs=out_specs,
        out_shape=out_shape,
        compiler_params=_cparams(("arbitrary",)),
        name="inproj",
    )(h, mods, mods, g, w, gvec, headmean, cos_t, sin_t)


def _stack_heads(q):
    lane = lax.broadcasted_iota(jnp.int32, q.shape, 1)
    zero = jnp.zeros_like(q)
    return jnp.concatenate([jnp.where(lane < HEAD_DIM, q, zero), jnp.where(lane >= HEAD_DIM, q, zero)], axis=0)


def _qk(q2, k):
    return lax.dot_general(q2, k, (((1,), (1,)), ((), ())), preferred_element_type=F32)


def _gqa_kernel(q_ref, kc_ref, vc_ref, *rest, n_latent_chunks, tk):
    if n_latent_chunks:
        kl_ref, vl_ref, o_ref = rest
    else:
        (o_ref,) = rest
    tq = q_ref.shape[0]
    q2 = _stack_heads(q_ref[...])

    def pv(p, v_ref, rows):
        return jnp.concatenate(
            [jnp.dot(p[:tq], v_ref[rows, 0:LANES], preferred_element_type=F32),
             jnp.dot(p[tq:], v_ref[rows, LANES:2 * LANES], preferred_element_type=F32)], axis=0)

    s = _qk(q2, kc_ref[...])
    m = jnp.max(s, axis=-1, keepdims=True)
    acc = pv(jnp.exp2((s - m).astype(BF16)), vc_ref, slice(None))

    if n_latent_chunks:
        def body(j, carry):
            m, acc = carry
            rows = pl.ds(pl.multiple_of(j * tk, tk), tk)
            s = _qk(q2, kl_ref[rows, :])
            m_new = jnp.maximum(m, jnp.max(s, axis=-1, keepdims=True))
            p = jnp.exp2((s - m_new).astype(BF16))
            return m_new, jnp.exp2(m - m_new) * acc + pv(p, vl_ref, rows)

        m, acc = lax.fori_loop(0, n_latent_chunks, body, (m, acc), unroll=True)
    lane = lax.broadcasted_iota(jnp.int32, (tq, LANES), 1)
    num = jnp.where(lane < HEAD_DIM, acc[:tq], acc[tq:])
    den = pltpu.roll(jnp.where(lane < HEAD_DIM, acc[tq:], acc[:tq]), HEAD_DIM, 1)
    o_ref[...] = (num / den).astype(o_ref.dtype)


def _gqa(q, k, v2, out_rows, n_batch, seq, n_ctx, latent, kv_chunked=False):
    chunks = q.shape[1] // LANES
    ctx_blk0 = n_batch * seq // n_ctx
    tk = min(ATTN_TK, seq)
    kj = (lambda j: j) if kv_chunked else (lambda j: 0)
    if latent:
        tq = ATTN_TQ
        grid = (n_batch, chunks, seq // tq)
        qmap = lambda b, j, i: (b * (seq // tq) + i, j)
        kv_specs = [pl.BlockSpec((seq, LANES), lambda b, j, i: (b, kj(j))),
                    pl.BlockSpec((seq, 2 * LANES), lambda b, j, i: (b, kj(j)))]
        kv_args = (k, v2)
    else:
        tq = n_ctx
        grid = (n_batch, chunks, 1)
        qmap = lambda b, j, i: (ctx_blk0 + b, j)
        kv_specs, kv_args = [], ()
    return pl.pallas_call(
        functools.partial(_gqa_kernel, n_latent_chunks=seq // tk if latent else 0, tk=tk),
        grid=grid,
        in_specs=[pl.BlockSpec((tq, LANES), qmap),
                  pl.BlockSpec((n_ctx, LANES), lambda b, j, i: (ctx_blk0 + b, kj(j))),
                  pl.BlockSpec((n_ctx, 2 * LANES), lambda b, j, i: (ctx_blk0 + b, kj(j)))] + kv_specs,
        out_specs=pl.BlockSpec((tq, LANES), qmap),
        out_shape=jax.ShapeDtypeStruct((out_rows, q.shape[1]), BF16),
        compiler_params=_cparams(("arbitrary", "arbitrary", "arbitrary")),
        name="attn_latent" if latent else "attn_ctx",
    )(q, k, v2, *kv_args)


def _na_kernel(q_ref, kc_ref, vc_ref, kl_ref, vl_ref, tab_ref, mask_ref, o_ref, *, grid_rows):
    i = pl.program_id(2)
    nk = NA_KEY_ROWS * GRID_W
    ks = jnp.clip(i * NA_BLOCK_ROWS - NA_ROWS // 2, 0, grid_rows - NA_KEY_ROWS)
    keys = pl.ds(pl.multiple_of(ks * GRID_W, GRID_W), nk)
    k = kl_ref[keys, :]
    kc = kc_ref[...]
    sub_rows = NA_SUB_ROWS * GRID_W
    lane = lax.broadcasted_iota(jnp.int32, (sub_rows, LANES), 1)
    low = lane < HEAD_DIM
    for sub in range(NA_BLOCK_ROWS // NA_SUB_ROWS):
        rows = slice(sub * sub_rows, (sub + 1) * sub_rows)
        q = q_ref[rows, :]
        acc = []
        for hh in range(2):
            qh = jnp.where(low if hh == 0 else jnp.logical_not(low), q, jnp.zeros_like(q))
            slabs = []
            for qr in range(sub * NA_SUB_ROWS, (sub + 1) * NA_SUB_ROWS):
                par = qr % 2
                lane0 = (NA_BLOCK_ROWS - qr - par) * GRID_W
                slabs.append(tab_ref[0, hh, par, :, lane0:lane0 + nk] + mask_ref[0, qr:qr + 1, :])
            s_nb = _qk(qh, k) + jnp.concatenate(slabs, axis=0)
            s_cx = _qk(qh, kc)
            m = jnp.maximum(jnp.max(s_nb, axis=-1, keepdims=True), jnp.max(s_cx, axis=-1, keepdims=True))
            vcols = slice(hh * LANES, (hh + 1) * LANES)
            acc.append(jnp.dot(jnp.exp2((s_nb - m).astype(BF16)), vl_ref[keys, vcols], preferred_element_type=F32)
                       + jnp.dot(jnp.exp2((s_cx - m).astype(BF16)), vc_ref[:, vcols], preferred_element_type=F32))
        num = jnp.where(low, acc[0], acc[1])
        den = pltpu.roll(jnp.where(low, acc[1], acc[0]), HEAD_DIM, 1)
        o_ref[rows, :] = (num / den).astype(o_ref.dtype)


def _na_block_start(blk, grid_rows):
    return int(np.clip(blk * NA_BLOCK_ROWS - NA_ROWS // 2, 0, grid_rows - NA_KEY_ROWS))


def _na_variant_blocks(grid_rows):
    nblk = grid_rows // NA_BLOCK_ROWS
    return (0, min(1, nblk - 1), nblk - 1)


def _na_bias_tables(rpb, grid_rows):
    n_heads = rpb.shape[0]
    qc = np.arange(GRID_W)[:, None]
    kc = np.arange(GRID_W)[None, :]
    cs = np.clip(qc - NA_COLS // 2, 0, GRID_W - NA_COLS)
    col_ok = (kc >= cs) & (kc < cs + NA_COLS)
    col_hot = ((kc - qc + NA_COLS - 1)[:, :, None] == np.arange(2 * NA_COLS - 1)) & col_ok[:, :, None]
    by_col = jnp.einsum("abe,hde->hadb", jnp.asarray(col_hot, F32), rpb, precision=HIGHEST)
    by_col = jnp.where(jnp.asarray(col_ok)[None, :, None, :], by_col * LOG2E, NEG_BIG)
    n_off = 2 * NA_ROWS - 1
    slots = NA_BLOCK_ROWS + NA_KEY_ROWS
    tabs = []
    for blk in _na_variant_blocks(grid_rows):
        off = _na_block_start(blk, grid_rows) - blk * NA_BLOCK_ROWS + NA_ROWS - 1
        front = NA_BLOCK_ROWS - off
        t = jnp.pad(by_col, ((0, 0), (0, 0), (front, slots - n_off - front), (0, 0)))
        t = t.reshape(n_heads, GRID_W, slots * GRID_W)
        shifted = jnp.pad(t[:, :, GRID_W:], ((0, 0), (0, 0), (0, GRID_W)))
        tabs.append(jnp.stack([t, shifted], axis=1))
    return jnp.stack(tabs)


def _na_row_masks(grid_rows):
    masks = []
    for blk in _na_variant_blocks(grid_rows):
        qr = blk * NA_BLOCK_ROWS + np.arange(NA_BLOCK_ROWS)[:, None]
        kr = _na_block_start(blk, grid_rows) + np.arange(NA_KEY_ROWS)[None, :]
        rs = np.clip(qr - NA_ROWS // 2, 0, grid_rows - NA_ROWS)
        row_ok = (kr >= rs) & (kr < rs + NA_ROWS)
        masks.append(np.repeat(np.where(row_ok, 0.0, NEG_BIG), GRID_W, axis=1))
    return jnp.asarray(np.stack(masks).astype(np.float32))


def _na_latent(nq, nk, nv2, tab, masks, n_rows_out, n_batch, seq, n_ctx):
    chunks = nq.shape[1] // LANES
    tq = NA_BLOCK_ROWS * GRID_W
    nblk = seq // tq
    ctx_blk0 = n_batch * seq // n_ctx
    qmap = lambda b, j, i: (b * nblk + i, j)
    variant = lambda i: jnp.where(i == 0, 0, jnp.where(i == nblk - 1, 2, 1))
    tab_spec = pl.BlockSpec((1, 2) + tab.shape[2:], lambda b, j, i: (variant(i), j, 0, 0, 0))
    mask_spec = pl.BlockSpec((1,) + masks.shape[1:], lambda b, j, i: (variant(i), 0, 0))
    return pl.pallas_call(
        functools.partial(_na_kernel, grid_rows=seq // GRID_W),
        grid=(n_batch, chunks, nblk),
        in_specs=[pl.BlockSpec((tq, LANES), qmap),
                  pl.BlockSpec((n_ctx, LANES), lambda b, j, i: (ctx_blk0 + b, j)),
                  pl.BlockSpec((n_ctx, 2 * LANES), lambda b, j, i: (ctx_blk0 + b, j)),
                  pl.BlockSpec((seq, LANES), lambda b, j, i: (b, j)),
                  pl.BlockSpec((seq, 2 * LANES), lambda b, j, i: (b, j)),
                  tab_spec, mask_spec],
        out_specs=pl.BlockSpec((tq, LANES), qmap),
        out_shape=jax.ShapeDtypeStruct((n_rows_out, nq.shape[1]), BF16),
        compiler_params=_cparams(("arbitrary", "arbitrary", "arbitrary")),
        name="na_latent",
    )(nq, nk, nv2, nk, nv2, tab, masks)


def _hconv_kernel(x_ref, prev_ref, next_ref, w_ref, b_ref, z_ref, x0_ref, *, tiles_per_seq, n_latent_tiles):
    i = pl.program_id(0)
    x = x_ref[...]
    t = x.shape[0]
    row = lax.broadcasted_iota(jnp.int32, x.shape, 0)
    first = jnp.logical_or(i % tiles_per_seq == 0, i >= n_latent_tiles)
    last = jnp.logical_or(i % tiles_per_seq == tiles_per_seq - 1, i >= n_latent_tiles)
    prev_row = jnp.where(first, 0.0, prev_ref[7:8, :])
    next_row = jnp.where(last, 0.0, next_ref[0:1, :])
    up = jnp.where(row == 0, prev_row, pltpu.roll(x, 1, 0))
    dn = jnp.where(row == t - 1, next_row, pltpu.roll(x, t - 1, 0))
    uc = up * w_ref[0:1, :] + x * w_ref[1:2, :] + dn * w_ref[2:3, :] + b_ref[...]
    cw = uc.shape[1] // 3
    x0_ref[...] = uc[:, :cw]
    z_ref[...] = uc[:, 2 * cw:] * uc[:, cw:2 * cw]


def _hconv(hy, conv_w, conv_b, seq, n_ctx, n_batch):
    n_rows, ch = hy.shape
    tt = n_ctx
    n_tiles = n_rows // tt
    cw = ch // 3
    return pl.pallas_call(
        functools.partial(_hconv_kernel, tiles_per_seq=seq // tt, n_latent_tiles=n_batch * seq // tt),
        grid=(n_tiles,),
        in_specs=[
            pl.BlockSpec((tt, ch), lambda i: (i, 0)),
            pl.BlockSpec((8, ch), lambda i: (jnp.maximum(i * (tt // 8) - 1, 0), 0)),
            pl.BlockSpec((8, ch), lambda i: (jnp.minimum((i + 1) * (tt // 8), n_rows // 8 - 1), 0)),
            pl.BlockSpec((3, ch), lambda i: (0, 0)),
            pl.BlockSpec((1, ch), lambda i: (0, 0)),
        ],
        out_specs=[pl.BlockSpec((tt, cw), lambda i: (i, 0))] * 2,
        out_shape=[jax.ShapeDtypeStruct((n_rows, cw), F32)] * 2,
        compiler_params=_cparams(("arbitrary",)),
        name="hyena_conv",
    )(hy, hy, hy, conv_w, conv_b)


def _hfilt_kernel(f_ref, w1_ref, b1_ref, w2_ref, b2_ref, w3_ref, fr_ref, dl_ref, k_ref, l1_ref, *, n):
    i = pl.program_id(0)
    f = f_ref[...]
    fr = fr_ref[...]
    h1 = jnp.sin(fr * (jnp.dot(f, w1_ref[...], precision=HIGHEST, preferred_element_type=F32) + b1_ref[...]))
    h2 = jnp.sin(fr * (jnp.dot(h1, w2_ref[...], precision=HIGHEST, preferred_element_type=F32) + b2_ref[...]))
    o = jnp.dot(h2, w3_ref[...], precision=HIGHEST, preferred_element_type=F32)
    cw = o.shape[1] // 2
    t = f.shape[0]
    row = i * t + lax.broadcasted_iota(jnp.int32, (t, cw), 0)
    decay = jnp.exp(-f[:, 0:1] * jnp.abs(dl_ref[...]))
    k = jnp.where(row < n, o[:, :cw], o[:, cw:]) * decay
    k = jnp.where(row == n, 0.0, k)
    k_ref[...] = k

    @pl.when(i == 0)
    def _():
        l1_ref[...] = jnp.zeros_like(l1_ref)

    l1_ref[...] += jnp.sum(jnp.abs(k), axis=0, keepdims=True)


def _filter_feats(n):
    rows = np.arange(2 * n)
    pos = np.where(rows <= n, rows, 2 * n - rows).astype(np.float32)
    pos = jnp.asarray(np.minimum(pos, n - 1))
    t = pos / max(n - 1, 1)
    bands = jnp.linspace(1e-4, FILTER_BANDS - 1, FILTER_BANDS, dtype=F32)
    ang = (2.0 * math.pi / n) * pos[:, None] * bands[None, :]
    feats = jnp.concatenate([t[:, None], jnp.cos(ang), -jnp.sin(ang)], axis=-1)
    return jnp.pad(feats, ((0, 0), (0, LANES - feats.shape[1])))


def _pad_to(a, shape):
    return jnp.pad(a, [(0, s - d) for s, d in zip(shape, a.shape)])


def _hfilter(n, w1, b1, w2, b2, w3, freq, deltas):
    feats = _filter_feats(n)
    w1, b1, freq = _pad_to(w1, (LANES, LANES)), _pad_to(b1, (1, LANES)), _pad_to(freq, (1, LANES))
    w2, b2, w3 = _pad_to(w2, (LANES, LANES)), _pad_to(b2, (1, LANES)), _pad_to(w3, (LANES, w3.shape[1]))
    rows, emb = feats.shape
    t = min(1024, rows)
    cw = w3.shape[1] // 2
    const = lambda i: (0, 0)
    full = lambda a: pl.BlockSpec(a.shape, const)
    return pl.pallas_call(
        functools.partial(_hfilt_kernel, n=n),
        grid=(rows // t,),
        in_specs=[pl.BlockSpec((t, emb), lambda i: (i, 0)), full(w1), full(b1), full(w2), full(b2), full(w3),
                  full(freq), full(deltas)],
        out_specs=[pl.BlockSpec((t, cw), lambda i: (i, 0)), pl.BlockSpec((1, cw), const)],
        out_shape=[jax.ShapeDtypeStruct((rows, cw), F32), jax.ShapeDtypeStruct((1, cw), F32)],
        compiler_params=_cparams(("arbitrary",)),
        name="hyena_filter",
    )(feats, w1, b1, w2, b2, w3, freq, deltas)


def _split_bf16(a):
    hi = a.astype(BF16)
    return hi, (a - hi.astype(F32)).astype(BF16)


def _dot3(a_split, b):
    ah, al = a_split
    bh, bl = _split_bf16(b)
    return (jnp.dot(ah, bh, preferred_element_type=F32) + jnp.dot(ah, bl, preferred_element_type=F32)
            + jnp.dot(al, bh, preferred_element_type=F32))


def _dft_outer_kernel(fh_ref, fl_ref, x_ref, o_ref):
    c = x_ref.shape[-1]
    o_ref[...] = _dot3((fh_ref[...], fl_ref[...]), x_ref[...].reshape(fh_ref.shape[1], c)).reshape(o_ref.shape)


def _kron_split(fmat):
    big = np.kron(np.asarray(fmat, np.float32), np.eye(OUTER_COLS, dtype=np.float32))
    hi = big.astype(BF16)
    lo = (big - hi.astype(np.float32)).astype(BF16)
    return jnp.asarray(hi), jnp.asarray(lo)


def _dft_outer_fwd(fmat, x3, nb, n1_len):
    k = fmat.shape[1]
    c = x3.shape[2]
    fh, fl = _kron_split(fmat)
    fspec = pl.BlockSpec(fh.shape, lambda b, j: (0, 0), pipeline_mode=pl.Buffered(1))
    return pl.pallas_call(
        _dft_outer_kernel,
        grid=(nb, DFT_N2 // OUTER_COLS),
        in_specs=[fspec, fspec, pl.BlockSpec((k, OUTER_COLS, c), lambda b, j: (b, j, 0))],
        out_specs=pl.BlockSpec((1, 2, n1_len, OUTER_COLS, c), lambda b, j: (b, 0, 0, j, 0)),
        out_shape=jax.ShapeDtypeStruct((nb, 2, n1_len, DFT_N2, c), F32),
        compiler_params=_cparams(("arbitrary", "arbitrary")),
        name="dft_outer_fwd",
    )(fh, fl, x3)


def _dft_outer_inv(fmat, d5):
    nb, _, n1_len, _, c = d5.shape
    half = fmat.shape[0]
    fh, fl = _kron_split(fmat)
    fspec = pl.BlockSpec(fh.shape, lambda b, j: (0, 0), pipeline_mode=pl.Buffered(1))
    return pl.pallas_call(
        _dft_outer_kernel,
        grid=(nb, DFT_N2 // OUTER_COLS),
        in_specs=[fspec, fspec, pl.BlockSpec((1, 2, n1_len, OUTER_COLS, c), lambda b, j: (b, 0, 0, j, 0))],
        out_specs=pl.BlockSpec((half, OUTER_COLS, c), lambda b, j: (b, j, 0)),
        out_shape=jax.ShapeDtypeStruct((nb * half, DFT_N2, c), F32),
        compiler_params=_cparams(("arbitrary", "arbitrary")),
        name="dft_outer_inv",
    )(fh, fl, d5)


def _inner_matrices(fc, fs, tc, ts):
    gr = fc * tc - fs * ts
    gi = -(fc * ts + fs * tc)
    return gr, gi


def _real_form(gr, gi):
    return jnp.concatenate([jnp.concatenate([gr, -gi], axis=1), jnp.concatenate([gi, gr], axis=1)], axis=0)


def _dft_inner_filter_kernel(fc_ref, fs_ref, tc_ref, ts_ref, b_ref, o_ref):
    fc, fs = fc_ref[...], fs_ref[...]
    n2, c = b_ref.shape[3], b_ref.shape[4]

    def body(kk, carry):
        g = _real_form(*_inner_matrices(fc, fs, tc_ref[kk], ts_ref[kk]))
        o_ref[kk] = _dot3(_split_bf16(g), b_ref[0, :, kk].reshape(2 * n2, c))
        return carry

    lax.fori_loop(0, b_ref.shape[2], body, 0, unroll=2)


def _dft_inner_conv_kernel(fc_ref, fs_ref, tc_ref, ts_ref, b_ref, kh_ref, sc_ref, o_ref):
    fc, fs = fc_ref[...], fs_ref[...]
    nb, n2, c = b_ref.shape[0], b_ref.shape[3], b_ref.shape[4]
    scale = sc_ref[...]

    def body(kk, carry):
        gr, gi = _inner_matrices(fc, fs, tc_ref[kk], ts_ref[kk])
        g = _split_bf16(_real_form(gr, gi))
        gh = _split_bf16(_real_form(gr.T, -gi.T))
        kh = kh_ref[kk]
        kr, ki = kh[:n2], kh[n2:]
        for b in range(nb):
            x = _dot3(g, b_ref[b, :, kk].reshape(2 * n2, c))
            xr, xi = x[:n2], x[n2:]
            y = jnp.concatenate([(xr * kr - xi * ki) * scale, (xr * ki + xi * kr) * scale], axis=0)
            o_ref[b, :, kk] = _dot3(gh, y).reshape(2, n2, c)
        return carry

    lax.fori_loop(0, b_ref.shape[2], body, 0, unroll=2)


def _dft_tables(n1_len):
    n_total = n1_len * DFT_N2
    a = np.arange(DFT_N2)
    ang = 2.0 * np.pi * np.outer(a, a) / DFT_N2
    k1 = np.arange(n1_len)
    tw = 2.0 * np.pi * np.outer(k1, a) / n_total
    o = 2.0 * np.pi * np.outer(k1, k1) / n1_len
    f32 = lambda v: jnp.asarray(v.astype(np.float32))
    outer_fwd = np.concatenate([np.cos(o), -np.sin(o)], axis=0)
    outer_inv = np.concatenate([np.cos(o), -np.sin(o)], axis=1)
    return dict(
        fc=f32(np.cos(ang)), fs=f32(np.sin(ang)),
        tc=f32(np.cos(tw)).reshape(n1_len, 1, DFT_N2), ts=f32(np.sin(tw)).reshape(n1_len, 1, DFT_N2),
        outer_fwd_full=outer_fwd, outer_fwd_half=outer_fwd[:, :n1_len // 2], outer_inv_half=outer_inv[:n1_len // 2],
    )


def _inner_specs():
    sq = pl.BlockSpec((DFT_N2, DFT_N2), lambda k: (0, 0))
    tw = pl.BlockSpec((INNER_K1, 1, DFT_N2), lambda k: (k, 0, 0))
    return [sq, sq, tw, tw]


def _filter_spectrum(kfull, tabs, n1_len):
    c = kfull.shape[1]
    b5 = _dft_outer_fwd(tabs["outer_fwd_full"], kfull.reshape(n1_len, DFT_N2, c), 1, n1_len)
    return pl.pallas_call(
        _dft_inner_filter_kernel,
        grid=(n1_len // INNER_K1,),
        in_specs=_inner_specs() + [pl.BlockSpec((1, 2, INNER_K1, DFT_N2, c), lambda k: (0, 0, k, 0, 0))],
        out_specs=pl.BlockSpec((INNER_K1, 2 * DFT_N2, c), lambda k: (k, 0, 0)),
        out_shape=jax.ShapeDtypeStruct((n1_len, 2 * DFT_N2, c), F32),
        compiler_params=_cparams(("arbitrary",)),
        name="dft_inner_filter",
    )(tabs["fc"], tabs["fs"], tabs["tc"], tabs["ts"], b5)


def _long_conv(z_all, nb, khat, scale, tabs, n1_len):
    c = z_all.shape[1]
    bz = _dft_outer_fwd(tabs["outer_fwd_half"], z_all.reshape(z_all.shape[0] // DFT_N2, DFT_N2, c), nb, n1_len)
    blk = pl.BlockSpec((nb, 2, INNER_K1, DFT_N2, c), lambda k: (0, 0, k, 0, 0))
    d = pl.pallas_call(
        _dft_inner_conv_kernel,
        grid=(n1_len // INNER_K1,),
        in_specs=_inner_specs() + [blk, pl.BlockSpec((INNER_K1, 2 * DFT_N2, c), lambda k: (k, 0, 0)),
                                   pl.BlockSpec((1, c), lambda k: (0, 0))],
        out_specs=blk,
        out_shape=jax.ShapeDtypeStruct(bz.shape, F32),
        compiler_params=_cparams(("arbitrary",)),
        name="dft_inner_conv",
    )(tabs["fc"], tabs["fs"], tabs["tc"], tabs["ts"], bz, khat, scale)
    y3 = _dft_outer_inv(tabs["outer_inv_half"], d)
    return y3.reshape(y3.shape[0] * DFT_N2, c)


def _hyena_ctx_kernel(ff_ref, fi_ref, k_ref, l1_ref, z_ref, o_ref):
    ff = ff_ref[...]
    n = z_ref.shape[0]
    big = ff.shape[1]
    kh = jnp.dot(ff, k_ref[...], precision=HIGHEST, preferred_element_type=F32)
    x = jnp.dot(ff[:, :n], z_ref[...], precision=HIGHEST, preferred_element_type=F32)
    kr, ki = kh[:big], kh[big:]
    xr, xi = x[:big], x[big:]
    scale = 1.0 / (l1_ref[...] * big)
    y = jnp.concatenate([(xr * kr - xi * ki) * scale, (xr * ki + xi * kr) * scale], axis=0)
    o_ref[...] = jnp.dot(fi_ref[...], y, precision=HIGHEST, preferred_element_type=F32)


def _hyena_ctx(z, kfull, l1, row0_blk, n_batch, n):
    big = 2 * n
    a = np.arange(big)
    ang = 2.0 * np.pi * np.outer(a, a) / big
    ff = jnp.asarray(np.concatenate([np.cos(ang), -np.sin(ang)], axis=0).astype(np.float32))
    fi = jnp.asarray(np.concatenate([np.cos(ang), -np.sin(ang)], axis=1)[:n].astype(np.float32))
    c = z.shape[1]
    const = lambda b: (0, 0)
    return pl.pallas_call(
        _hyena_ctx_kernel,
        grid=(n_batch,),
        in_specs=[pl.BlockSpec(ff.shape, const), pl.BlockSpec(fi.shape, const), pl.BlockSpec(kfull.shape, const),
                  pl.BlockSpec((1, c), const), pl.BlockSpec((n, c), lambda b: (row0_blk + b, 0))],
        out_specs=pl.BlockSpec((n, c), lambda b: (b, 0)),
        out_shape=jax.ShapeDtypeStruct((n_batch * n, c), F32),
        compiler_params=_cparams(("arbitrary",)),
        name="hyena_ctx",
    )(ff, fi, kfull, l1, z)


def _outproj_kernel(h_ref, gt_ref, ya_ref, yconv_ref, z_ref, x0_ref, skip_ref, yc_ref, wa_ref, wb_ref, wc_ref, o_ref):
    yb = ((yconv_ref[...] + z_ref[...] * skip_ref[...]) * x0_ref[...]).astype(BF16)
    mix = (jnp.dot(ya_ref[...], wa_ref[...], preferred_element_type=F32)
           + jnp.dot(yb, wb_ref[...], preferred_element_type=F32)
           + jnp.dot(yc_ref[...], wc_ref[...], preferred_element_type=F32))
    o_ref[...] = h_ref[...] + gt_ref[0] * mix


def _outproj(h, mods, mod_base, ya, yconv, z, x0, skip, yc, wa, wb, wc, n_rows, tiles_per_batch, n_batch):
    d = h.shape[1]
    tm = TOKEN_TILE
    row = lambda i: (i, 0)
    const = lambda i: (0, 0)
    rows = lambda a: pl.BlockSpec((tm, a.shape[1]), row)
    full = lambda a: pl.BlockSpec(a.shape, const)
    return pl.pallas_call(
        _outproj_kernel,
        grid=(n_rows // tm,),
        in_specs=[rows(h)] + _mod_specs(tiles_per_batch, n_batch, mod_base, (5,))
        + [rows(ya), rows(yconv), rows(z), rows(x0), full(skip), rows(yc), full(wa), full(wb), full(wc)],
        out_specs=pl.BlockSpec((tm, d), row),
        out_shape=jax.ShapeDtypeStruct((n_rows, d), F32),
        compiler_params=_cparams(("arbitrary",)),
        name="outproj",
    )(h, mods, ya, yconv, z, x0, skip, yc, wa, wb, wc)


def _rope_tables(seq, n_batch, n_ctx_rows):
    t = jnp.arange(seq, dtype=jnp.int32)
    rows = (t // GRID_W).astype(F32)
    cols = (t % GRID_W).astype(F32)
    nf = HEAD_DIM // 4
    inv = ROPE_THETA ** (-jnp.arange(nf, dtype=F32) / nf)
    ar, ac = rows[:, None] * inv, cols[:, None] * inv
    cos = jnp.concatenate([jnp.cos(ar), jnp.cos(ar), jnp.cos(ac), jnp.cos(ac)], axis=1)
    sin = jnp.concatenate([-jnp.sin(ar), jnp.sin(ar), -jnp.sin(ac), jnp.sin(ac)], axis=1)
    cos = jnp.tile(cos, (n_batch, LANES // HEAD_DIM))
    sin = jnp.tile(sin, (n_batch, LANES // HEAD_DIM))
    cos = jnp.concatenate([cos, jnp.ones((n_ctx_rows, LANES), F32)], axis=0)
    sin = jnp.concatenate([sin, jnp.zeros((n_ctx_rows, LANES), F32)], axis=0)
    return cos, sin


def _head_mean_matrix():
    a = np.arange(LANES)
    return jnp.asarray((a[:, None] // HEAD_DIM == a[None, :] // HEAD_DIM).astype(np.float32) / HEAD_DIM, dtype=BF16)


def kernel(x, c, ctx, c_ctx, w_ada, b_ada, g_ffn1, w_ffn1_gate, w_ffn1_up, w_ffn1_down, g_mix, w_in, w_out, g_q_attn, g_k_attn, conv_w, conv_b, filt_w1, filt_b1, filt_w2, filt_b2, filt_w3, filt_freq, hyena_skip, g_q_na, g_k_na, na_rpb, g_ffn2, w_ffn2_gate, w_ffn2_up, w_ffn2_down):
    n_batch, seq, d = x.shape
    n_ctx = ctx.shape[1]
    depth = w_ada.shape[0]
    assert d == _D and n_batch + 1 <= 8 and seq % TOKEN_TILE == 0 and seq % (NA_BLOCK_ROWS * GRID_W) == 0
    assert n_ctx == 256 and (n_batch * n_ctx) % TOKEN_TILE == 0
    n_lat = n_batch * seq
    n_all = n_lat + n_batch * n_ctx
    tiles_per_batch = seq // TOKEN_TILE
    hy_w = conv_w.shape[2] // 3
    n1_len = 2 * seq // DFT_N2

    cond8 = jnp.zeros((8, d), F32).at[:n_batch].set(c).at[n_batch].set(c_ctx)
    mods = _adaln_all(cond8, w_ada, b_ada).reshape(depth * 8 * N_MOD, 1, d)
    h = jnp.concatenate([x.reshape(n_lat, d), ctx.reshape(n_batch * n_ctx, d)], axis=0)

    cos_t, sin_t = _rope_tables(seq, n_batch, n_batch * n_ctx)
    headmean = _head_mean_matrix()
    na_masks = _na_row_masks(seq // GRID_W)
    tabs = _dft_tables(n1_len)
    deltas = jnp.linspace(math.log(DECAY_TARGET) / SLOW_DECAY_PCT, math.log(DECAY_TARGET) / FAST_DECAY_PCT,
                          hy_w, dtype=F32).reshape(1, hy_w)

    q_heads = [hd for pair in range(ATTN_HEADS // 2) for hd in (pair, pair + ATTN_HEADS // 2)]
    q_segs = [(hd * HEAD_DIM, (hd + 1) * HEAD_DIM) for hd in q_heads]
    in_segs = q_segs + [(512, 640), (1536, 1792), (1792, 2048), (640, 768), (2048, 2304), (768, 1536)]

    for l in range(depth):
        last = l == depth - 1
        base = l * 8 * N_MOD
        rep = lambda g, k: jnp.tile(g, k)
        gvec = jnp.concatenate([rep(g_q_attn[l], ATTN_HEADS), rep(g_k_attn[l], ATTN_KV_HEADS),
                                rep(g_q_na[l], NA_HEADS), rep(g_k_na[l], NA_HEADS)]).reshape(1, N_NORM)

        h = _ffn(h, mods, base, (0, 1, 2), g_ffn1[l].reshape(1, d), w_ffn1_gate[l].astype(BF16),
                 w_ffn1_up[l].astype(BF16), w_ffn1_down[l].astype(BF16), n_all, tiles_per_batch, n_batch)

        w_in_l = jnp.concatenate([w_in[l][:, a:b] for a, b in in_segs], axis=1).astype(BF16)
        q, k, nq, nk, v, nv, hy = _inproj(h, mods, base, g_mix[l].reshape(1, d), w_in_l,
                                          gvec, headmean, cos_t, sin_t, tiles_per_batch, n_batch)

        out_rows = n_lat if last else n_all
        ya = _gqa(q, k, v, out_rows, n_batch, seq, n_ctx, latent=True)
        yc = _na_latent(nq, nk, nv, _na_bias_tables(na_rpb[l], seq // GRID_W), na_masks, out_rows, n_batch, seq, n_ctx)

        z, x0 = _hconv(hy, conv_w[l], conv_b[l].reshape(1, -1), seq, n_ctx, n_batch)
        fargs = (filt_w1[l], filt_b1[l].reshape(1, -1), filt_w2[l], filt_b2[l].reshape(1, -1), filt_w3[l],
                 filt_freq[l].reshape(1, -1), deltas)
        kfull, l1 = _hfilter(seq, *fargs)
        khat = _filter_spectrum(kfull, tabs, n1_len)
        yconv = _long_conv(z, n_batch, khat, 1.0 / (l1 * (2 * seq)), tabs, n1_len)

        if not last:
            ya_c = _gqa(q, k, v, n_all, n_batch, seq, n_ctx, latent=False)
            yc_c = _gqa(nq, nk, nv, n_all, n_batch, seq, n_ctx, latent=False, kv_chunked=True)
            ya = lax.dynamic_update_slice(ya, ya_c[n_lat:], (n_lat, 0))
            yc = lax.dynamic_update_slice(yc, yc_c[n_lat:], (n_lat, 0))
            kfull_c, l1_c = _hfilter(n_ctx, *fargs)
            yconv_c = _hyena_ctx(z, kfull_c, l1_c, n_lat // n_ctx, n_batch, n_ctx)
            yconv = jnp.concatenate([yconv, yconv_c], axis=0)

        w_o = w_out[l]
        wa = jnp.concatenate([w_o[a:b] for a, b in q_segs], axis=0).astype(BF16)
        wb = w_o[512:512 + hy_w].astype(BF16)
        wc = w_o[512 + hy_w:].astype(BF16)
        h = _outproj(h, mods, base, ya, yconv, z, x0, hyena_skip[l].reshape(1, hy_w), yc, wa, wb, wc,
                     out_rows, tiles_per_batch, n_batch)
        h = _ffn(h, mods, base, (6, 7, 8), g_ffn2[l].reshape(1, d), w_ffn2_gate[l].astype(BF16),
                 w_ffn2_up[l].astype(BF16), w_ffn2_down[l].astype(BF16), out_rows, tiles_per_batch, n_batch)

    return h[:n_lat].reshape(n_batch, seq, d)
```

```python
import functools
import math

import numpy as np
import jax
import jax.numpy as jnp
from jax import lax
from jax.experimental import pallas as pl
from jax.experimental.pallas import tpu as pltpu

F32 = jnp.float32
BF16 = jnp.bfloat16
HIGHEST = lax.Precision.HIGHEST

HEAD_DIM = 64
GRID_W = 64
ATTN_HEADS = 8
ATTN_KV_HEADS = 2
NA_HEADS = 4
NA_ROWS = 8
NA_COLS = 16
FILTER_BANDS = 16
DECAY_TARGET = 1e-2
FAST_DECAY_PCT = 0.3
SLOW_DECAY_PCT = 1.5
ROPE_THETA = 10000.0
EPS = 1e-6
N_MOD = 9

LANES = 128
TOKEN_TILE = 512
FFN_CHUNK = 256
ATTN_TQ = 512
ATTN_TK = 2048
NA_BLOCK_ROWS = 8
NA_KEY_ROWS = 16
NA_SUB_ROWS = 4
DFT_N2 = 128
OUTER_COLS = 8
INNER_K1 = 8
NEG_BIG = -1e30
VMEM_LIMIT = 56 * 1024 * 1024


def _cparams(sem):
    return pltpu.CompilerParams(dimension_semantics=sem, vmem_limit_bytes=VMEM_LIMIT)


def _ada_kernel(c_ref, w_ref, b_ref, o_ref):
    c = c_ref[...]
    s = c * jax.nn.sigmoid(c)
    o_ref[0] = jnp.dot(s, w_ref[0], precision=HIGHEST, preferred_element_type=F32) + b_ref[0]


def _adaln_all(cond8, w_ada, b_ada):
    depth, d, nd = w_ada.shape
    tn = 1024
    return pl.pallas_call(
        _ada_kernel,
        grid=(depth, nd // tn),
        in_specs=[
            pl.BlockSpec((8, d), lambda l, j: (0, 0)),
            pl.BlockSpec((1, d, tn), lambda l, j: (l, 0, j)),
            pl.BlockSpec((1, 1, tn), lambda l, j: (l, 0, j)),
        ],
        out_specs=pl.BlockSpec((1, 8, tn), lambda l, j: (l, 0, j)),
        out_shape=jax.ShapeDtypeStruct((depth, 8, nd), F32),
        compiler_params=_cparams(("arbitrary", "arbitrary")),
        name="adaln",
    )(cond8, w_ada, b_ada.reshape(depth, 1, nd))


def _norm_mod(x, g, sh, sc):
    ms = jnp.mean(x * x, axis=-1, keepdims=True)
    y = x * lax.rsqrt(ms + EPS) * g
    return y * (1.0 + sc) + sh


def _mod_specs(tiles_per_batch, n_batch, base, idxs):
    def spec(which):
        return pl.BlockSpec(
            (1, 1, _D),
            lambda i, w=which: (base + jnp.minimum(i // tiles_per_batch, n_batch) * N_MOD + w, 0, 0))
    return [spec(w) for w in idxs]


_D = 1024


def _ffn_kernel(h_ref, sh_ref, sc_ref, gt_ref, g_ref, wg_ref, wu_ref, wd_ref, o_ref, xb_ref, acc_ref):
    x = h_ref[...]
    xb_ref[...] = _norm_mod(x, g_ref[...], sh_ref[0], sc_ref[0]).astype(BF16)

    for k in range(wg_ref.shape[1] // FFN_CHUNK):
        cols = slice(k * FFN_CHUNK, (k + 1) * FFN_CHUNK)
        xb = xb_ref[...]
        a = jnp.dot(xb, wg_ref[:, cols], preferred_element_type=F32)
        u = jnp.dot(xb, wu_ref[:, cols], preferred_element_type=F32)
        act = (a * jax.nn.sigmoid(a) * u).astype(BF16)
        part = jnp.dot(act, wd_ref[cols, :], preferred_element_type=F32)
        if k == 0:
            acc_ref[...] = part
        else:
            acc_ref[...] += part
    o_ref[...] = x + 0.5 * gt_ref[0] * acc_ref[...]


def _ffn(h, mods, mod_base, which, g, wg3, wu3, wd3, n_rows, tiles_per_batch, n_batch):
    d = h.shape[1]
    tm = TOKEN_TILE
    const3 = lambda i: (0, 0)
    return pl.pallas_call(
        _ffn_kernel,
        grid=(n_rows // tm,),
        in_specs=[pl.BlockSpec((tm, d), lambda i: (i, 0))]
        + _mod_specs(tiles_per_batch, n_batch, mod_base, which)
        + [
            pl.BlockSpec((1, d), lambda i: (0, 0)),
            pl.BlockSpec(wg3.shape, const3, pipeline_mode=pl.Buffered(1)),
            pl.BlockSpec(wu3.shape, const3, pipeline_mode=pl.Buffered(1)),
            pl.BlockSpec(wd3.shape, const3, pipeline_mode=pl.Buffered(1)),
        ],
        out_specs=pl.BlockSpec((tm, d), lambda i: (i, 0)),
        out_shape=jax.ShapeDtypeStruct((n_rows, d), F32),
        scratch_shapes=[pltpu.VMEM((tm, d), BF16), pltpu.VMEM((tm, d), F32)],
        compiler_params=_cparams(("arbitrary",)),
        name="ffn",
    )(h, mods, mods, mods, g, wg3, wu3, wd3)


N_NORM = 1152
N_ROPE = 640
ATTN_Q_CHUNKS = (0, 1, 2, 3)
NA_Q_CHUNKS = (5, 6)
LOG2E = math.log2(math.e)


def _inproj_kernel(h_ref, prev_ref, next_ref, sh_ref, sc_ref, g_ref, w_ref, gv_ref, hm_ref, cos_ref, sin_ref,
                   cw_ref, cb_ref, q_ref, k_ref, nq_ref, nk_ref, v_ref, nv_ref, z_ref, x0_ref, *, seq, n_ctx, n_lat):
    xb = _norm_mod(h_ref[...], g_ref[...], sh_ref[0], sc_ref[0]).astype(BF16)
    hm = hm_ref[...]
    cosv = cos_ref[...]
    sinv = sin_ref[...]
    lane = lax.broadcasted_iota(jnp.int32, cosv.shape, 1)
    first_half = (lane % 32) < 16
    outs = []
    p_norm = jnp.dot(xb, w_ref[:, 0:N_NORM], preferred_element_type=F32)
    for c in range(N_NORM // LANES):
        p = p_norm[:, c * LANES:(c + 1) * LANES]
        ms = jnp.dot((p * p).astype(BF16), hm, preferred_element_type=F32)
        y = p * lax.rsqrt(ms + EPS) * gv_ref[:, c * LANES:(c + 1) * LANES]
        if c < N_ROPE // LANES:
            swapped = jnp.where(first_half, pltpu.roll(y, LANES - 16, 1), pltpu.roll(y, 16, 1))
            y = y * cosv + swapped * sinv
        if c in ATTN_Q_CHUNKS or c in NA_Q_CHUNKS:
            y = y * (HEAD_DIM ** -0.5 * LOG2E)
        outs.append(y.astype(BF16))
    for c in range(4):
        q_ref[:, c * LANES:(c + 1) * LANES] = outs[c]
    k_ref[...] = outs[4]
    nq_ref[:, 0:LANES] = outs[5]
    nq_ref[:, LANES:2 * LANES] = outs[6]
    nk_ref[:, 0:LANES] = outs[7]
    nk_ref[:, LANES:2 * LANES] = outs[8]
    p_rest = jnp.dot(xb, w_ref[:, N_NORM:], preferred_element_type=F32)
    v = p_rest[:, 0:LANES]
    v_ref[:, 0:LANES] = jnp.where(lane < HEAD_DIM, v, 1.0).astype(BF16)
    v_ref[:, LANES:2 * LANES] = jnp.where(lane < HEAD_DIM, 1.0, v).astype(BF16)
    for c in range(NA_HEADS // 2):
        nv = p_rest[:, (1 + c) * LANES:(2 + c) * LANES]
        nv_ref[:, 2 * c * LANES:(2 * c + 1) * LANES] = jnp.where(lane < HEAD_DIM, nv, 1.0).astype(BF16)
        nv_ref[:, (2 * c + 1) * LANES:(2 * c + 2) * LANES] = jnp.where(lane < HEAD_DIM, 1.0, nv).astype(BF16)

    hy = p_rest[:, 3 * LANES:]
    tm = hy.shape[0]
    halo = jnp.concatenate([prev_ref[...], next_ref[...]], axis=0)
    halo_b = _norm_mod(halo, g_ref[...], sh_ref[0], sc_ref[0]).astype(BF16)
    hy_halo = jnp.dot(halo_b, w_ref[:, N_NORM + 3 * LANES:], preferred_element_type=F32)
    row0 = pl.program_id(0) * tm
    is_ctx = row0 >= n_lat
    period = jnp.where(is_ctx, n_ctx, seq)
    local = lax.broadcasted_iota(jnp.int32, (tm, 1), 0)
    pos = jnp.bitwise_and(row0 + local - jnp.where(is_ctx, n_lat, 0), period - 1)
    up = jnp.where(local == 0, hy_halo[7:8, :], pltpu.roll(hy, 1, 0))
    up = jnp.where(pos == 0, 0.0, up)
    dn = jnp.where(local == tm - 1, hy_halo[8:9, :], pltpu.roll(hy, tm - 1, 0))
    dn = jnp.where(pos == period - 1, 0.0, dn)
    uc = up * cw_ref[0:1, :] + hy * cw_ref[1:2, :] + dn * cw_ref[2:3, :] + cb_ref[...]
    cw = uc.shape[1] // 3
    x0_ref[...] = uc[:, :cw]
    z_ref[...] = uc[:, 2 * cw:] * uc[:, cw:2 * cw]


def _inproj(h, mods, mod_base, g, w, gvec, headmean, cos_t, sin_t, conv_w, conv_b, tiles_per_batch, n_batch, seq,
            n_ctx):
    n_rows, d = h.shape
    tm = TOKEN_TILE
    assert seq & (seq - 1) == 0 and n_ctx & (n_ctx - 1) == 0
    row = lambda i: (i, 0)
    const = lambda i: (0, 0)
    widths = (512, 128, 256, 256, 256, 512)
    hy_w = conv_w.shape[1] // 3
    out_shape = [jax.ShapeDtypeStruct((n_rows, wd), BF16) for wd in widths]
    out_shape += [jax.ShapeDtypeStruct((n_rows, hy_w), F32)] * 2
    out_specs = [pl.BlockSpec((tm, wd), row) for wd in widths] + [pl.BlockSpec((tm, hy_w), row)] * 2
    return pl.pallas_call(
        functools.partial(_inproj_kernel, seq=seq, n_ctx=n_ctx, n_lat=n_batch * seq),
        grid=(n_rows // tm,),
        in_specs=[pl.BlockSpec((tm, d), row),
                  pl.BlockSpec((8, d), lambda i: (jnp.maximum(i * (tm // 8) - 1, 0), 0)),
                  pl.BlockSpec((8, d), lambda i: (jnp.minimum((i + 1) * (tm // 8), n_rows // 8 - 1), 0))]
        + _mod_specs(tiles_per_batch, n_batch, mod_base, (3, 4))
        + [
            pl.BlockSpec((1, d), const),
            pl.BlockSpec(w.shape, const, pipeline_mode=pl.Buffered(1)),
            pl.BlockSpec(gvec.shape, const),
            pl.BlockSpec(headmean.shape, const),
            pl.BlockSpec((tm, LANES), row),
            pl.BlockSpec((tm, LANES), row),
            pl.BlockSpec(conv_w.shape, const),
            pl.BlockSpec(conv_b.shape, const),
        ],
        out_specs=out_specs,
        out_shape=out_shape,
        compiler_params=_cparams(("arbitrary",)),
        name="inproj",
    )(h, h, h, mods, mods, g, w, gvec, headmean, cos_t, sin_t, conv_w, conv_b)


def _stack_heads(q):
    lane = lax.broadcasted_iota(jnp.int32, q.shape, 1)
    zero = jnp.zeros_like(q)
    return jnp.concatenate([jnp.where(lane < HEAD_DIM, q, zero), jnp.where(lane >= HEAD_DIM, q, zero)], axis=0)


def _qk(q2, k):
    return lax.dot_general(q2, k, (((1,), (1,)), ((), ())), preferred_element_type=F32)


def _gqa_kernel(q_ref, kc_ref, vc_ref, *rest, n_latent_chunks, tk):
    if n_latent_chunks:
        kl_ref, vl_ref, o_ref = rest
    else:
        (o_ref,) = rest
    tq = q_ref.shape[0]
    q2 = _stack_heads(q_ref[...])

    def pv(p, v_ref, rows):
        return jnp.concatenate(
            [jnp.dot(p[:tq], v_ref[rows, 0:LANES], preferred_element_type=F32),
             jnp.dot(p[tq:], v_ref[rows, LANES:2 * LANES], preferred_element_type=F32)], axis=0)

    s = _qk(q2, kc_ref[...])
    m = jnp.max(s, axis=-1, keepdims=True)
    acc = pv(jnp.exp2((s - m).astype(BF16)), vc_ref, slice(None))

    if n_latent_chunks:
        def body(j, carry):
            m, acc = carry
            rows = pl.ds(pl.multiple_of(j * tk, tk), tk)
            s = _qk(q2, kl_ref[rows, :])
            m_new = jnp.maximum(m, jnp.max(s, axis=-1, keepdims=True))
            p = jnp.exp2((s - m_new).astype(BF16))
            return m_new, jnp.exp2(m - m_new) * acc + pv(p, vl_ref, rows)

        m, acc = lax.fori_loop(0, n_latent_chunks, body, (m, acc), unroll=True)
    lane = lax.broadcasted_iota(jnp.int32, (tq, LANES), 1)
    num = jnp.where(lane < HEAD_DIM, acc[:tq], acc[tq:])
    den = pltpu.roll(jnp.where(lane < HEAD_DIM, acc[tq:], acc[:tq]), HEAD_DIM, 1)
    o_ref[...] = (num / den).astype(o_ref.dtype)


def _gqa(q, k, v2, out_rows, n_batch, seq, n_ctx, latent, kv_chunked=False):
    chunks = q.shape[1] // LANES
    ctx_blk0 = n_batch * seq // n_ctx
    tk = min(ATTN_TK, seq)
    kj = (lambda j: j) if kv_chunked else (lambda j: 0)
    if latent:
        tq = ATTN_TQ
        grid = (n_batch, chunks, seq // tq)
        qmap = lambda b, j, i: (b * (seq // tq) + i, j)
        kv_specs = [pl.BlockSpec((seq, LANES), lambda b, j, i: (b, kj(j))),
                    pl.BlockSpec((seq, 2 * LANES), lambda b, j, i: (b, kj(j)))]
        kv_args = (k, v2)
    else:
        tq = n_ctx
        grid = (n_batch, chunks, 1)
        qmap = lambda b, j, i: (ctx_blk0 + b, j)
        kv_specs, kv_args = [], ()
    return pl.pallas_call(
        functools.partial(_gqa_kernel, n_latent_chunks=seq // tk if latent else 0, tk=tk),
        grid=grid,
        in_specs=[pl.BlockSpec((tq, LANES), qmap),
                  pl.BlockSpec((n_ctx, LANES), lambda b, j, i: (ctx_blk0 + b, kj(j))),
                  pl.BlockSpec((n_ctx, 2 * LANES), lambda b, j, i: (ctx_blk0 + b, kj(j)))] + kv_specs,
        out_specs=pl.BlockSpec((tq, LANES), qmap),
        out_shape=jax.ShapeDtypeStruct((out_rows, q.shape[1]), BF16),
        compiler_params=_cparams(("arbitrary", "arbitrary", "arbitrary")),
        name="attn_latent" if latent else "attn_ctx",
    )(q, k, v2, *kv_args)


def _na_kernel(q_ref, kc_ref, vc_ref, kl_ref, vl_ref, tab_ref, mask_ref, o_ref, *, grid_rows):
    i = pl.program_id(2)
    nk = NA_KEY_ROWS * GRID_W
    ks = jnp.clip(i * NA_BLOCK_ROWS - NA_ROWS // 2, 0, grid_rows - NA_KEY_ROWS)
    keys = pl.ds(pl.multiple_of(ks * GRID_W, GRID_W), nk)
    k = kl_ref[keys, :]
    kc = kc_ref[...]
    sub_rows = NA_SUB_ROWS * GRID_W
    lane = lax.broadcasted_iota(jnp.int32, (sub_rows, LANES), 1)
    low = lane < HEAD_DIM
    for sub in range(NA_BLOCK_ROWS // NA_SUB_ROWS):
        rows = slice(sub * sub_rows, (sub + 1) * sub_rows)
        q = q_ref[rows, :]
        acc = []
        for hh in range(2):
            qh = jnp.where(low if hh == 0 else jnp.logical_not(low), q, jnp.zeros_like(q))
            slabs = []
            for qr in range(sub * NA_SUB_ROWS, (sub + 1) * NA_SUB_ROWS):
                par = qr % 2
                lane0 = (NA_BLOCK_ROWS - qr - par) * GRID_W
                slabs.append(tab_ref[0, hh, par, :, lane0:lane0 + nk] + mask_ref[0, qr:qr + 1, :])
            s_nb = _qk(qh, k) + jnp.concatenate(slabs, axis=0)
            s_cx = _qk(qh, kc)
            m = jnp.maximum(jnp.max(s_nb, axis=-1, keepdims=True), jnp.max(s_cx, axis=-1, keepdims=True))
            vcols = slice(hh * LANES, (hh + 1) * LANES)
            acc.append(jnp.dot(jnp.exp2((s_nb - m).astype(BF16)), vl_ref[keys, vcols], preferred_element_type=F32)
                       + jnp.dot(jnp.exp2((s_cx - m).astype(BF16)), vc_ref[:, vcols], preferred_element_type=F32))
        num = jnp.where(low, acc[0], acc[1])
        den = pltpu.roll(jnp.where(low, acc[1], acc[0]), HEAD_DIM, 1)
        o_ref[rows, :] = (num / den).astype(o_ref.dtype)


def _na_block_start(blk, grid_rows):
    return int(np.clip(blk * NA_BLOCK_ROWS - NA_ROWS // 2, 0, grid_rows - NA_KEY_ROWS))


def _na_variant_blocks(grid_rows):
    nblk = grid_rows // NA_BLOCK_ROWS
    return (0, min(1, nblk - 1), nblk - 1)


def _na_bias_tables(rpb, grid_rows):
    n_heads = rpb.shape[0]
    qc = np.arange(GRID_W)[:, None]
    kc = np.arange(GRID_W)[None, :]
    cs = np.clip(qc - NA_COLS // 2, 0, GRID_W - NA_COLS)
    col_ok = (kc >= cs) & (kc < cs + NA_COLS)
    col_hot = ((kc - qc + NA_COLS - 1)[:, :, None] == np.arange(2 * NA_COLS - 1)) & col_ok[:, :, None]
    by_col = jnp.einsum("abe,hde->hadb", jnp.asarray(col_hot, F32), rpb, precision=HIGHEST)
    by_col = jnp.where(jnp.asarray(col_ok)[None, :, None, :], by_col * LOG2E, NEG_BIG)
    n_off = 2 * NA_ROWS - 1
    slots = NA_BLOCK_ROWS + NA_KEY_ROWS
    tabs = []
    for blk in _na_variant_blocks(grid_rows):
        off = _na_block_start(blk, grid_rows) - blk * NA_BLOCK_ROWS + NA_ROWS - 1
        front = NA_BLOCK_ROWS - off
        t = jnp.pad(by_col, ((0, 0), (0, 0), (front, slots - n_off - front), (0, 0)))
        t = t.reshape(n_heads, GRID_W, slots * GRID_W)
        shifted = jnp.pad(t[:, :, GRID_W:], ((0, 0), (0, 0), (0, GRID_W)))
        tabs.append(jnp.stack([t, shifted], axis=1))
    return jnp.stack(tabs)


def _na_row_masks(grid_rows):
    masks = []
    for blk in _na_variant_blocks(grid_rows):
        qr = blk * NA_BLOCK_ROWS + np.arange(NA_BLOCK_ROWS)[:, None]
        kr = _na_block_start(blk, grid_rows) + np.arange(NA_KEY_ROWS)[None, :]
        rs = np.clip(qr - NA_ROWS // 2, 0, grid_rows - NA_ROWS)
        row_ok = (kr >= rs) & (kr < rs + NA_ROWS)
        masks.append(np.repeat(np.where(row_ok, 0.0, NEG_BIG), GRID_W, axis=1))
    return jnp.asarray(np.stack(masks).astype(np.float32))


def _na_latent(nq, nk, nv2, tab, masks, n_rows_out, n_batch, seq, n_ctx):
    chunks = nq.shape[1] // LANES
    tq = NA_BLOCK_ROWS * GRID_W
    nblk = seq // tq
    ctx_blk0 = n_batch * seq // n_ctx
    qmap = lambda b, j, i: (b * nblk + i, j)
    variant = lambda i: jnp.where(i == 0, 0, jnp.where(i == nblk - 1, 2, 1))
    tab_spec = pl.BlockSpec((1, 2) + tab.shape[2:], lambda b, j, i: (variant(i), j, 0, 0, 0))
    mask_spec = pl.BlockSpec((1,) + masks.shape[1:], lambda b, j, i: (variant(i), 0, 0))
    return pl.pallas_call(
        functools.partial(_na_kernel, grid_rows=seq // GRID_W),
        grid=(n_batch, chunks, nblk),
        in_specs=[pl.BlockSpec((tq, LANES), qmap),
                  pl.BlockSpec((n_ctx, LANES), lambda b, j, i: (ctx_blk0 + b, j)),
                  pl.BlockSpec((n_ctx, 2 * LANES), lambda b, j, i: (ctx_blk0 + b, j)),
                  pl.BlockSpec((seq, LANES), lambda b, j, i: (b, j)),
                  pl.BlockSpec((seq, 2 * LANES), lambda b, j, i: (b, j)),
                  tab_spec, mask_spec],
        out_specs=pl.BlockSpec((tq, LANES), qmap),
        out_shape=jax.ShapeDtypeStruct((n_rows_out, nq.shape[1]), BF16),
        compiler_params=_cparams(("arbitrary", "arbitrary", "arbitrary")),
        name="na_latent",
    )(nq, nk, nv2, nk, nv2, tab, masks)


def _hfilt_kernel(f_ref, w1_ref, b1_ref, w2_ref, b2_ref, w3_ref, fr_ref, dl_ref, k_ref, l1_ref, *, n):
    i = pl.program_id(0)
    f = f_ref[...]
    fr = fr_ref[...]
    h1 = jnp.sin(fr * (jnp.dot(f, w1_ref[...], precision=HIGHEST, preferred_element_type=F32) + b1_ref[...]))
    h2 = jnp.sin(fr * (jnp.dot(h1, w2_ref[...], precision=HIGHEST, preferred_element_type=F32) + b2_ref[...]))
    o = jnp.dot(h2, w3_ref[...], precision=HIGHEST, preferred_element_type=F32)
    cw = o.shape[1] // 2
    t = f.shape[0]
    row = i * t + lax.broadcasted_iota(jnp.int32, (t, cw), 0)
    decay = jnp.exp(-f[:, 0:1] * jnp.abs(dl_ref[...]))
    k = jnp.where(row < n, o[:, :cw], o[:, cw:]) * decay
    k = jnp.where(row == n, 0.0, k)
    k_ref[...] = k

    @pl.when(i == 0)
    def _():
        l1_ref[...] = jnp.zeros_like(l1_ref)

    l1_ref[...] += jnp.sum(jnp.abs(k), axis=0, keepdims=True)


def _filter_feats(n):
    rows = np.arange(2 * n)
    pos = np.where(rows <= n, rows, 2 * n - rows).astype(np.float32)
    pos = jnp.asarray(np.minimum(pos, n - 1))
    t = pos / max(n - 1, 1)
    bands = jnp.linspace(1e-4, FILTER_BANDS - 1, FILTER_BANDS, dtype=F32)
    ang = (2.0 * math.pi / n) * pos[:, None] * bands[None, :]
    feats = jnp.concatenate([t[:, None], jnp.cos(ang), -jnp.sin(ang)], axis=-1)
    return jnp.pad(feats, ((0, 0), (0, LANES - feats.shape[1])))


def _pad_to(a, shape):
    return jnp.pad(a, [(0, s - d) for s, d in zip(shape, a.shape)])


def _hfilter(n, w1, b1, w2, b2, w3, freq, deltas):
    feats = _filter_feats(n)
    w1, b1, freq = _pad_to(w1, (LANES, LANES)), _pad_to(b1, (1, LANES)), _pad_to(freq, (1, LANES))
    w2, b2, w3 = _pad_to(w2, (LANES, LANES)), _pad_to(b2, (1, LANES)), _pad_to(w3, (LANES, w3.shape[1]))
    rows, emb = feats.shape
    t = min(1024, rows)
    cw = w3.shape[1] // 2
    const = lambda i: (0, 0)
    full = lambda a: pl.BlockSpec(a.shape, const)
    return pl.pallas_call(
        functools.partial(_hfilt_kernel, n=n),
        grid=(rows // t,),
        in_specs=[pl.BlockSpec((t, emb), lambda i: (i, 0)), full(w1), full(b1), full(w2), full(b2), full(w3),
                  full(freq), full(deltas)],
        out_specs=[pl.BlockSpec((t, cw), lambda i: (i, 0)), pl.BlockSpec((1, cw), const)],
        out_shape=[jax.ShapeDtypeStruct((rows, cw), F32), jax.ShapeDtypeStruct((1, cw), F32)],
        compiler_params=_cparams(("arbitrary",)),
        name="hyena_filter",
    )(feats, w1, b1, w2, b2, w3, freq, deltas)


def _split_bf16(a):
    hi = a.astype(BF16)
    return hi, (a - hi.astype(F32)).astype(BF16)


def _dot3(a_split, b):
    ah, al = a_split
    bh, bl = _split_bf16(b)
    return (jnp.dot(ah, bh, preferred_element_type=F32) + jnp.dot(ah, bl, preferred_element_type=F32)
            + jnp.dot(al, bh, preferred_element_type=F32))


def _dft_dot(a_split, b, wide):
    if wide:
        return _dot3(a_split, b)
    return jnp.dot(a_split[0], b.astype(BF16), preferred_element_type=F32)


def _dft_outer_kernel(fh_ref, fl_ref, x_ref, o_ref, *, wide):
    c = x_ref.shape[-1]
    x = x_ref[...].reshape(fh_ref.shape[1], c)
    o_ref[...] = _dft_dot((fh_ref[...], fl_ref[...]), x, wide).reshape(o_ref.shape)


def _kron_split(fmat):
    big = np.kron(np.asarray(fmat, np.float32), np.eye(OUTER_COLS, dtype=np.float32))
    hi = big.astype(BF16)
    lo = (big - hi.astype(np.float32)).astype(BF16)
    return jnp.asarray(hi), jnp.asarray(lo)


def _dft_outer_fwd(fmat, x3, nb, n1_len, wide):
    k = fmat.shape[1]
    c = x3.shape[2]
    fh, fl = _kron_split(fmat)
    fspec = pl.BlockSpec(fh.shape, lambda b, j: (0, 0), pipeline_mode=pl.Buffered(1))
    return pl.pallas_call(
        functools.partial(_dft_outer_kernel, wide=wide),
        grid=(nb, DFT_N2 // OUTER_COLS),
        in_specs=[fspec, fspec, pl.BlockSpec((k, OUTER_COLS, c), lambda b, j: (b, j, 0))],
        out_specs=pl.BlockSpec((1, 2, n1_len, OUTER_COLS, c), lambda b, j: (b, 0, 0, j, 0)),
        out_shape=jax.ShapeDtypeStruct((nb, 2, n1_len, DFT_N2, c), F32),
        compiler_params=_cparams(("arbitrary", "arbitrary")),
        name="dft_outer_fwd",
    )(fh, fl, x3)


def _dft_outer_inv(fmat, d5):
    nb, _, n1_len, _, c = d5.shape
    half = fmat.shape[0]
    fh, fl = _kron_split(fmat)
    fspec = pl.BlockSpec(fh.shape, lambda b, j: (0, 0), pipeline_mode=pl.Buffered(1))
    return pl.pallas_call(
        functools.partial(_dft_outer_kernel, wide=False),
        grid=(nb, DFT_N2 // OUTER_COLS),
        in_specs=[fspec, fspec, pl.BlockSpec((1, 2, n1_len, OUTER_COLS, c), lambda b, j: (b, 0, 0, j, 0))],
        out_specs=pl.BlockSpec((half, OUTER_COLS, c), lambda b, j: (b, j, 0)),
        out_shape=jax.ShapeDtypeStruct((nb * half, DFT_N2, c), F32),
        compiler_params=_cparams(("arbitrary", "arbitrary")),
        name="dft_outer_inv",
    )(fh, fl, d5)


def _inner_matrices(fc, fs, tc, ts):
    gr = fc * tc - fs * ts
    gi = -(fc * ts + fs * tc)
    return gr, gi


def _real_form(gr, gi):
    return jnp.concatenate([jnp.concatenate([gr, -gi], axis=1), jnp.concatenate([gi, gr], axis=1)], axis=0)


def _dft_inner_filter_kernel(fc_ref, fs_ref, tc_ref, ts_ref, b_ref, o_ref):
    fc, fs = fc_ref[...], fs_ref[...]
    n2, c = b_ref.shape[3], b_ref.shape[4]

    def body(kk, carry):
        g = _real_form(*_inner_matrices(fc, fs, tc_ref[kk], ts_ref[kk]))
        o_ref[kk] = _dot3(_split_bf16(g), b_ref[0, :, kk].reshape(2 * n2, c))
        return carry

    lax.fori_loop(0, b_ref.shape[2], body, 0, unroll=2)


def _dft_inner_conv_kernel(fc_ref, fs_ref, tc_ref, ts_ref, b_ref, kh_ref, sc_ref, o_ref):
    fc, fs = fc_ref[...], fs_ref[...]
    nb, n2, c = b_ref.shape[0], b_ref.shape[3], b_ref.shape[4]
    scale = sc_ref[...]

    def body(kk, carry):
        gr, gi = _inner_matrices(fc, fs, tc_ref[kk], ts_ref[kk])
        g = _real_form(gr, gi).astype(BF16)
        gh = _real_form(gr.T, -gi.T).astype(BF16)
        kh = kh_ref[kk]
        kr, ki = kh[:n2], kh[n2:]
        for b in range(nb):
            x = jnp.dot(g, b_ref[b, :, kk].reshape(2 * n2, c).astype(BF16), preferred_element_type=F32)
            xr, xi = x[:n2], x[n2:]
            y = jnp.concatenate([(xr * kr - xi * ki) * scale, (xr * ki + xi * kr) * scale], axis=0)
            o_ref[b, :, kk] = jnp.dot(gh, y.astype(BF16), preferred_element_type=F32).reshape(2, n2, c)
        return carry

    lax.fori_loop(0, b_ref.shape[2], body, 0, unroll=2)


def _dft_tables(n1_len):
    n_total = n1_len * DFT_N2
    a = np.arange(DFT_N2)
    ang = 2.0 * np.pi * np.outer(a, a) / DFT_N2
    k1 = np.arange(n1_len)
    tw = 2.0 * np.pi * np.outer(k1, a) / n_total
    o = 2.0 * np.pi * np.outer(k1, k1) / n1_len
    f32 = lambda v: jnp.asarray(v.astype(np.float32))
    outer_fwd = np.concatenate([np.cos(o), -np.sin(o)], axis=0)
    outer_inv = np.concatenate([np.cos(o), -np.sin(o)], axis=1)
    return dict(
        fc=f32(np.cos(ang)), fs=f32(np.sin(ang)),
        tc=f32(np.cos(tw)).reshape(n1_len, 1, DFT_N2), ts=f32(np.sin(tw)).reshape(n1_len, 1, DFT_N2),
        outer_fwd_full=outer_fwd, outer_fwd_half=outer_fwd[:, :n1_len // 2], outer_inv_half=outer_inv[:n1_len // 2],
    )


def _inner_specs():
    sq = pl.BlockSpec((DFT_N2, DFT_N2), lambda k: (0, 0))
    tw = pl.BlockSpec((INNER_K1, 1, DFT_N2), lambda k: (k, 0, 0))
    return [sq, sq, tw, tw]


def _filter_spectrum(kfull, tabs, n1_len):
    c = kfull.shape[1]
    b5 = _dft_outer_fwd(tabs["outer_fwd_full"], kfull.reshape(n1_len, DFT_N2, c), 1, n1_len, wide=True)
    return pl.pallas_call(
        _dft_inner_filter_kernel,
        grid=(n1_len // INNER_K1,),
        in_specs=_inner_specs() + [pl.BlockSpec((1, 2, INNER_K1, DFT_N2, c), lambda k: (0, 0, k, 0, 0))],
        out_specs=pl.BlockSpec((INNER_K1, 2 * DFT_N2, c), lambda k: (k, 0, 0)),
        out_shape=jax.ShapeDtypeStruct((n1_len, 2 * DFT_N2, c), F32),
        compiler_params=_cparams(("arbitrary",)),
        name="dft_inner_filter",
    )(tabs["fc"], tabs["fs"], tabs["tc"], tabs["ts"], b5)


def _long_conv(z_all, nb, khat, scale, tabs, n1_len):
    c = z_all.shape[1]
    bz = _dft_outer_fwd(tabs["outer_fwd_half"], z_all.reshape(z_all.shape[0] // DFT_N2, DFT_N2, c), nb, n1_len,
                        wide=False)
    blk = pl.BlockSpec((nb, 2, INNER_K1, DFT_N2, c), lambda k: (0, 0, k, 0, 0))
    d = pl.pallas_call(
        _dft_inner_conv_kernel,
        grid=(n1_len // INNER_K1,),
        in_specs=_inner_specs() + [blk, pl.BlockSpec((INNER_K1, 2 * DFT_N2, c), lambda k: (k, 0, 0)),
                                   pl.BlockSpec((1, c), lambda k: (0, 0))],
        out_specs=blk,
        out_shape=jax.ShapeDtypeStruct(bz.shape, F32),
        compiler_params=_cparams(("arbitrary",)),
        name="dft_inner_conv",
    )(tabs["fc"], tabs["fs"], tabs["tc"], tabs["ts"], bz, khat, scale)
    y3 = _dft_outer_inv(tabs["outer_inv_half"], d)
    return y3.reshape(y3.shape[0] * DFT_N2, c)


def _hyena_ctx_kernel(ff_ref, fi_ref, k_ref, l1_ref, z_ref, o_ref):
    ff = ff_ref[...]
    n = z_ref.shape[0]
    big = ff.shape[1]
    kh = jnp.dot(ff, k_ref[...], precision=HIGHEST, preferred_element_type=F32)
    x = jnp.dot(ff[:, :n], z_ref[...], precision=HIGHEST, preferred_element_type=F32)
    kr, ki = kh[:big], kh[big:]
    xr, xi = x[:big], x[big:]
    scale = 1.0 / (l1_ref[...] * big)
    y = jnp.concatenate([(xr * kr - xi * ki) * scale, (xr * ki + xi * kr) * scale], axis=0)
    o_ref[...] = jnp.dot(fi_ref[...], y, precision=HIGHEST, preferred_element_type=F32)


def _hyena_ctx(z, kfull, l1, row0_blk, n_batch, n):
    big = 2 * n
    a = np.arange(big)
    ang = 2.0 * np.pi * np.outer(a, a) / big
    ff = jnp.asarray(np.concatenate([np.cos(ang), -np.sin(ang)], axis=0).astype(np.float32))
    fi = jnp.asarray(np.concatenate([np.cos(ang), -np.sin(ang)], axis=1)[:n].astype(np.float32))
    c = z.shape[1]
    const = lambda b: (0, 0)
    return pl.pallas_call(
        _hyena_ctx_kernel,
        grid=(n_batch,),
        in_specs=[pl.BlockSpec(ff.shape, const), pl.BlockSpec(fi.shape, const), pl.BlockSpec(kfull.shape, const),
                  pl.BlockSpec((1, c), const), pl.BlockSpec((n, c), lambda b: (row0_blk + b, 0))],
        out_specs=pl.BlockSpec((n, c), lambda b: (b, 0)),
        out_shape=jax.ShapeDtypeStruct((n_batch * n, c), F32),
        compiler_params=_cparams(("arbitrary",)),
        name="hyena_ctx",
    )(ff, fi, kfull, l1, z)


def _outproj_kernel(h_ref, gt_ref, ya_ref, yconv_ref, z_ref, x0_ref, skip_ref, yc_ref, wa_ref, wb_ref, wc_ref, o_ref):
    yb = ((yconv_ref[...] + z_ref[...] * skip_ref[...]) * x0_ref[...]).astype(BF16)
    mix = (jnp.dot(ya_ref[...], wa_ref[...], preferred_element_type=F32)
           + jnp.dot(yb, wb_ref[...], preferred_element_type=F32)
           + jnp.dot(yc_ref[...], wc_ref[...], preferred_element_type=F32))
    o_ref[...] = h_ref[...] + gt_ref[0] * mix


def _outproj(h, mods, mod_base, ya, yconv, z, x0, skip, yc, wa, wb, wc, n_rows, tiles_per_batch, n_batch):
    d = h.shape[1]
    tm = TOKEN_TILE
    row = lambda i: (i, 0)
    const = lambda i: (0, 0)
    rows = lambda a: pl.BlockSpec((tm, a.shape[1]), row)
    full = lambda a: pl.BlockSpec(a.shape, const)
    return pl.pallas_call(
        _outproj_kernel,
        grid=(n_rows // tm,),
        in_specs=[rows(h)] + _mod_specs(tiles_per_batch, n_batch, mod_base, (5,))
        + [rows(ya), rows(yconv), rows(z), rows(x0), full(skip), rows(yc), full(wa), full(wb), full(wc)],
        out_specs=pl.BlockSpec((tm, d), row),
        out_shape=jax.ShapeDtypeStruct((n_rows, d), F32),
        compiler_params=_cparams(("arbitrary",)),
        name="outproj",
    )(h, mods, ya, yconv, z, x0, skip, yc, wa, wb, wc)


def _rope_tables(seq, n_batch, n_ctx_rows):
    t = jnp.arange(seq, dtype=jnp.int32)
    rows = (t // GRID_W).astype(F32)
    cols = (t % GRID_W).astype(F32)
    nf = HEAD_DIM // 4
    inv = ROPE_THETA ** (-jnp.arange(nf, dtype=F32) / nf)
    ar, ac = rows[:, None] * inv, cols[:, None] * inv
    cos = jnp.concatenate([jnp.cos(ar), jnp.cos(ar), jnp.cos(ac), jnp.cos(ac)], axis=1)
    sin = jnp.concatenate([-jnp.sin(ar), jnp.sin(ar), -jnp.sin(ac), jnp.sin(ac)], axis=1)
    cos = jnp.tile(cos, (n_batch, LANES // HEAD_DIM))
    sin = jnp.tile(sin, (n_batch, LANES // HEAD_DIM))
    cos = jnp.concatenate([cos, jnp.ones((n_ctx_rows, LANES), F32)], axis=0)
    sin = jnp.concatenate([sin, jnp.zeros((n_ctx_rows, LANES), F32)], axis=0)
    return cos, sin


def _head_mean_matrix():
    a = np.arange(LANES)
    return jnp.asarray((a[:, None] // HEAD_DIM == a[None, :] // HEAD_DIM).astype(np.float32) / HEAD_DIM, dtype=BF16)


def kernel(x, c, ctx, c_ctx, w_ada, b_ada, g_ffn1, w_ffn1_gate, w_ffn1_up, w_ffn1_down, g_mix, w_in, w_out, g_q_attn, g_k_attn, conv_w, conv_b, filt_w1, filt_b1, filt_w2, filt_b2, filt_w3, filt_freq, hyena_skip, g_q_na, g_k_na, na_rpb, g_ffn2, w_ffn2_gate, w_ffn2_up, w_ffn2_down):
    n_batch, seq, d = x.shape
    n_ctx = ctx.shape[1]
    depth = w_ada.shape[0]
    assert d == _D and n_batch + 1 <= 8 and seq % TOKEN_TILE == 0 and seq % (NA_BLOCK_ROWS * GRID_W) == 0
    assert n_ctx == 256 and (n_batch * n_ctx) % TOKEN_TILE == 0
    n_lat = n_batch * seq
    n_all = n_lat + n_batch * n_ctx
    tiles_per_batch = seq // TOKEN_TILE
    hy_w = conv_w.shape[2] // 3
    n1_len = 2 * seq // DFT_N2

    cond8 = jnp.zeros((8, d), F32).at[:n_batch].set(c).at[n_batch].set(c_ctx)
    mods = _adaln_all(cond8, w_ada, b_ada).reshape(depth * 8 * N_MOD, 1, d)
    h = jnp.concatenate([x.reshape(n_lat, d), ctx.reshape(n_batch * n_ctx, d)], axis=0)

    cos_t, sin_t = _rope_tables(seq, n_batch, n_batch * n_ctx)
    headmean = _head_mean_matrix()
    na_masks = _na_row_masks(seq // GRID_W)
    tabs = _dft_tables(n1_len)
    deltas = jnp.linspace(math.log(DECAY_TARGET) / SLOW_DECAY_PCT, math.log(DECAY_TARGET) / FAST_DECAY_PCT,
                          hy_w, dtype=F32).reshape(1, hy_w)

    q_heads = [hd for pair in range(ATTN_HEADS // 2) for hd in (pair, pair + ATTN_HEADS // 2)]
    q_segs = [(hd * HEAD_DIM, (hd + 1) * HEAD_DIM) for hd in q_heads]
    in_segs = q_segs + [(512, 640), (1536, 1792), (1792, 2048), (640, 768), (2048, 2304), (768, 1536)]

    for l in range(depth):
        last = l == depth - 1
        base = l * 8 * N_MOD
        rep = lambda g, k: jnp.tile(g, k)
        gvec = jnp.concatenate([rep(g_q_attn[l], ATTN_HEADS), rep(g_k_attn[l], ATTN_KV_HEADS),
                                rep(g_q_na[l], NA_HEADS), rep(g_k_na[l], NA_HEADS)]).reshape(1, N_NORM)

        h = _ffn(h, mods, base, (0, 1, 2), g_ffn1[l].reshape(1, d), w_ffn1_gate[l].astype(BF16),
                 w_ffn1_up[l].astype(BF16), w_ffn1_down[l].astype(BF16), n_all, tiles_per_batch, n_batch)

        w_in_l = jnp.concatenate([w_in[l][:, a:b] for a, b in in_segs], axis=1).astype(BF16)
        q, k, nq, nk, v, nv, z, x0 = _inproj(h, mods, base, g_mix[l].reshape(1, d), w_in_l, gvec, headmean, cos_t, sin_t,
                                             conv_w[l], conv_b[l].reshape(1, -1), tiles_per_batch, n_batch, seq, n_ctx)

        out_rows = n_lat if last else n_all
        ya = _gqa(q, k, v, out_rows, n_batch, seq, n_ctx, latent=True)
        yc = _na_latent(nq, nk, nv, _na_bias_tables(na_rpb[l], seq // GRID_W), na_masks, out_rows, n_batch, seq, n_ctx)

        fargs = (filt_w1[l], filt_b1[l].reshape(1, -1), filt_w2[l], filt_b2[l].reshape(1, -1), filt_w3[l],
                 filt_freq[l].reshape(1, -1), deltas)
        kfull, l1 = _hfilter(seq, *fargs)
        khat = _filter_spectrum(kfull, tabs, n1_len)
        yconv = _long_conv(z, n_batch, khat, 1.0 / (l1 * (2 * seq)), tabs, n1_len)

        if not last:
            ya_c = _gqa(q, k, v, n_all, n_batch, seq, n_ctx, latent=False)
            yc_c = _gqa(nq, nk, nv, n_all, n_batch, seq, n_ctx, latent=False, kv_chunked=True)
            ya = lax.dynamic_update_slice(ya, ya_c[n_lat:], (n_lat, 0))
            yc = lax.dynamic_update_slice(yc, yc_c[n_lat:], (n_lat, 0))
            kfull_c, l1_c = _hfilter(n_ctx, *fargs)
            yconv_c = _hyena_ctx(z, kfull_c, l1_c, n_lat // n_ctx, n_batch, n_ctx)
            yconv = jnp.concatenate([yconv, yconv_c], axis=0)

        w_o = w_out[l]
        wa = jnp.concatenate([w_o[a:b] for a, b in q_segs], axis=0).astype(BF16)
        wb = w_o[512:512 + hy_w].astype(BF16)
        wc = w_o[512 + hy_w:].astype(BF16)
        h = _outproj(h, mods, base, ya, yconv, z, x0, hyena_skip[l].reshape(1, hy_w), yc, wa, wb, wc,
                     out_rows, tiles_per_batch, n_batch)
        h = _ffn(h, mods, base, (6, 7, 8), g_ffn2[l].reshape(1, d), w_ffn2_gate[l].astype(BF16),
                 w_ffn2_up[l].astype(BF16), w_ffn2_down[l].astype(BF16), out_rows, tiles_per_batch, n_batch)

    return h[:n_lat].reshape(n_batch, seq, d)
```

```python
import functools
import math

import numpy as np
import jax
import jax.numpy as jnp
from jax import lax
from jax.experimental import pallas as pl
from jax.experimental.pallas import tpu as pltpu

F32 = jnp.float32
BF16 = jnp.bfloat16
HIGHEST = lax.Precision.HIGHEST

HEAD_DIM = 64
GRID_W = 64
ATTN_HEADS = 8
ATTN_KV_HEADS = 2
NA_HEADS = 4
NA_ROWS = 8
NA_COLS = 16
FILTER_BANDS = 16
DECAY_TARGET = 1e-2
FAST_DECAY_PCT = 0.3
SLOW_DECAY_PCT = 1.5
ROPE_THETA = 10000.0
EPS = 1e-6
N_MOD = 9

LANES = 128
TOKEN_TILE = 512
FFN_CHUNK = 256
ATTN_TQ = 512
ATTN_TK = 2048
NA_BLOCK_ROWS = 8
NA_KEY_ROWS = 16
NA_SUB_ROWS = 4
DFT_N2 = 128
OUTER_COLS = 8
INNER_K1 = 8
NEG_BIG = -1e30
VMEM_LIMIT = 56 * 1024 * 1024


def _cparams(sem):
    return pltpu.CompilerParams(dimension_semantics=sem, vmem_limit_bytes=VMEM_LIMIT)


def _ada_kernel(c_ref, w_ref, b_ref, o_ref):
    c = c_ref[...]
    s = c * jax.nn.sigmoid(c)
    o_ref[0] = jnp.dot(s, w_ref[0], precision=HIGHEST, preferred_element_type=F32) + b_ref[0]


def _adaln_all(cond8, w_ada, b_ada):
    depth, d, nd = w_ada.shape
    tn = 1024
    return pl.pallas_call(
        _ada_kernel,
        grid=(depth, nd // tn),
        in_specs=[
            pl.BlockSpec((8, d), lambda l, j: (0, 0)),
            pl.BlockSpec((1, d, tn), lambda l, j: (l, 0, j)),
            pl.BlockSpec((1, 1, tn), lambda l, j: (l, 0, j)),
        ],
        out_specs=pl.BlockSpec((1, 8, tn), lambda l, j: (l, 0, j)),
        out_shape=jax.ShapeDtypeStruct((depth, 8, nd), F32),
        compiler_params=_cparams(("arbitrary", "arbitrary")),
        name="adaln",
    )(cond8, w_ada, b_ada.reshape(depth, 1, nd))


def _norm_mod(x, g, sh, sc):
    ms = jnp.mean(x * x, axis=-1, keepdims=True)
    y = x * lax.rsqrt(ms + EPS) * g
    return y * (1.0 + sc) + sh


def _mod_specs(tiles_per_batch, n_batch, base, idxs):
    def spec(which):
        return pl.BlockSpec(
            (1, 1, _D),
            lambda i, w=which: (base + jnp.minimum(i // tiles_per_batch, n_batch) * N_MOD + w, 0, 0))
    return [spec(w) for w in idxs]


_D = 1024


def _ffn_kernel(h_ref, *refs, with_mixer):
    x = h_ref[...]
    if with_mixer:
        gt2_ref, ya_ref, yconv_ref, z_ref, x0_ref, skip_ref, yc_ref, wa_ref, wb_ref, wc_ref = refs[:10]
        refs = refs[10:]
        yb = ((yconv_ref[...] + z_ref[...] * skip_ref[...]) * x0_ref[...]).astype(BF16)
        x = x + gt2_ref[0] * (jnp.dot(ya_ref[...], wa_ref[...], preferred_element_type=F32)
                              + jnp.dot(yb, wb_ref[...], preferred_element_type=F32)
                              + jnp.dot(yc_ref[...], wc_ref[...], preferred_element_type=F32))
    sh_ref, sc_ref, gt_ref, g_ref, wg_ref, wu_ref, wd_ref, o_ref, xb_ref, acc_ref = refs
    if with_mixer:
        o_ref[...] = x
    xb_ref[...] = _norm_mod(x, g_ref[...], sh_ref[0], sc_ref[0]).astype(BF16)

    for k in range(wg_ref.shape[1] // FFN_CHUNK):
        cols = slice(k * FFN_CHUNK, (k + 1) * FFN_CHUNK)
        xb = xb_ref[...]
        a = jnp.dot(xb, wg_ref[:, cols], preferred_element_type=F32)
        u = jnp.dot(xb, wu_ref[:, cols], preferred_element_type=F32)
        act = (a * jax.nn.sigmoid(a) * u).astype(BF16)
        part = jnp.dot(act, wd_ref[cols, :], preferred_element_type=F32)
        if k == 0:
            acc_ref[...] = part
        else:
            acc_ref[...] += part
    resid = o_ref[...] if with_mixer else h_ref[...]
    o_ref[...] = resid + 0.5 * gt_ref[0] * acc_ref[...]


def _ffn(h, mods, mod_base, which, g, wg, wu, wd, n_rows, tiles_per_batch, n_batch, mixer=None):
    d = h.shape[1]
    tm = TOKEN_TILE
    row = lambda i: (i, 0)
    const = lambda i: (0, 0)
    rows = lambda a: pl.BlockSpec((tm, a.shape[1]), row)
    full = lambda a: pl.BlockSpec(a.shape, const)
    once = lambda a: pl.BlockSpec(a.shape, const, pipeline_mode=pl.Buffered(1))
    mix_specs, mix_args = [], ()
    if mixer is not None:
        ya, yconv, z, x0, skip, yc, wa, wb, wc = mixer
        mix_specs = _mod_specs(tiles_per_batch, n_batch, mod_base, (5,)) + [
            rows(ya), rows(yconv), rows(z), rows(x0), full(skip), rows(yc), once(wa), once(wb), once(wc)]
        mix_args = (mods,) + tuple(mixer)
    return pl.pallas_call(
        functools.partial(_ffn_kernel, with_mixer=mixer is not None),
        grid=(n_rows // tm,),
        in_specs=[rows(h)] + mix_specs + _mod_specs(tiles_per_batch, n_batch, mod_base, which)
        + [full(g), once(wg), once(wu), once(wd)],
        out_specs=pl.BlockSpec((tm, d), row),
        out_shape=jax.ShapeDtypeStruct((n_rows, d), F32),
        scratch_shapes=[pltpu.VMEM((tm, d), BF16), pltpu.VMEM((tm, d), F32)],
        compiler_params=_cparams(("arbitrary",)),
        name="ffn_mix" if mixer is not None else "ffn",
    )(h, *mix_args, mods, mods, mods, g, wg, wu, wd)


N_NORM = 1152
N_ROPE = 640
ATTN_Q_CHUNKS = (0, 1, 2, 3)
NA_Q_CHUNKS = (5, 6)
LOG2E = math.log2(math.e)


def _inproj_kernel(h_ref, prev_ref, next_ref, sh_ref, sc_ref, g_ref, w_ref, gv_ref, hm_ref, cos_ref, sin_ref,
                   cw_ref, cb_ref, q_ref, k_ref, nq_ref, nk_ref, v_ref, nv_ref, z_ref, x0_ref, *, seq, n_ctx, n_lat):
    xb = _norm_mod(h_ref[...], g_ref[...], sh_ref[0], sc_ref[0]).astype(BF16)
    hm = hm_ref[...]
    cosv = cos_ref[...]
    sinv = sin_ref[...]
    lane = lax.broadcasted_iota(jnp.int32, cosv.shape, 1)
    first_half = (lane % 32) < 16
    outs = []
    p_norm = jnp.dot(xb, w_ref[:, 0:N_NORM], preferred_element_type=F32)
    for c in range(N_NORM // LANES):
        p = p_norm[:, c * LANES:(c + 1) * LANES]
        ms = jnp.dot((p * p).astype(BF16), hm, preferred_element_type=F32)
        y = p * lax.rsqrt(ms + EPS) * gv_ref[:, c * LANES:(c + 1) * LANES]
        if c < N_ROPE // LANES:
            swapped = jnp.where(first_half, pltpu.roll(y, LANES - 16, 1), pltpu.roll(y, 16, 1))
            y = y * cosv + swapped * sinv
        if c in ATTN_Q_CHUNKS or c in NA_Q_CHUNKS:
            y = y * (HEAD_DIM ** -0.5 * LOG2E)
        outs.append(y.astype(BF16))
    for c in range(4):
        q_ref[:, c * LANES:(c + 1) * LANES] = outs[c]
    k_ref[...] = outs[4]
    nq_ref[:, 0:LANES] = outs[5]
    nq_ref[:, LANES:2 * LANES] = outs[6]
    nk_ref[:, 0:LANES] = outs[7]
    nk_ref[:, LANES:2 * LANES] = outs[8]
    p_rest = jnp.dot(xb, w_ref[:, N_NORM:], preferred_element_type=F32)
    v = p_rest[:, 0:LANES]
    v_ref[:, 0:LANES] = jnp.where(lane < HEAD_DIM, v, 1.0).astype(BF16)
    v_ref[:, LANES:2 * LANES] = jnp.where(lane < HEAD_DIM, 1.0, v).astype(BF16)
    for c in range(NA_HEADS // 2):
        nv = p_rest[:, (1 + c) * LANES:(2 + c) * LANES]
        nv_ref[:, 2 * c * LANES:(2 * c + 1) * LANES] = jnp.where(lane < HEAD_DIM, nv, 1.0).astype(BF16)
        nv_ref[:, (2 * c + 1) * LANES:(2 * c + 2) * LANES] = jnp.where(lane < HEAD_DIM, 1.0, nv).astype(BF16)

    hy = p_rest[:, 3 * LANES:]
    tm = hy.shape[0]
    halo = jnp.concatenate([prev_ref[...], next_ref[...]], axis=0)
    halo_b = _norm_mod(halo, g_ref[...], sh_ref[0], sc_ref[0]).astype(BF16)
    hy_halo = jnp.dot(halo_b, w_ref[:, N_NORM + 3 * LANES:], preferred_element_type=F32)
    row0 = pl.program_id(0) * tm
    is_ctx = row0 >= n_lat
    period = jnp.where(is_ctx, n_ctx, seq)
    local = lax.broadcasted_iota(jnp.int32, (tm, 1), 0)
    pos = jnp.bitwise_and(row0 + local - jnp.where(is_ctx, n_lat, 0), period - 1)
    up = jnp.where(local == 0, hy_halo[7:8, :], pltpu.roll(hy, 1, 0))
    up = jnp.where(pos == 0, 0.0, up)
    dn = jnp.where(local == tm - 1, hy_halo[8:9, :], pltpu.roll(hy, tm - 1, 0))
    dn = jnp.where(pos == period - 1, 0.0, dn)
    uc = up * cw_ref[0:1, :] + hy * cw_ref[1:2, :] + dn * cw_ref[2:3, :] + cb_ref[...]
    cw = uc.shape[1] // 3
    x0_ref[...] = uc[:, :cw]
    z_ref[...] = uc[:, 2 * cw:] * uc[:, cw:2 * cw]


def _inproj(h, mods, mod_base, g, w, gvec, headmean, cos_t, sin_t, conv_w, conv_b, tiles_per_batch, n_batch, seq,
            n_ctx):
    n_rows, d = h.shape
    tm = TOKEN_TILE
    assert seq & (seq - 1) == 0 and n_ctx & (n_ctx - 1) == 0
    row = lambda i: (i, 0)
    const = lambda i: (0, 0)
    widths = (512, 128, 256, 256, 256, 512)
    hy_w = conv_w.shape[1] // 3
    out_shape = [jax.ShapeDtypeStruct((n_rows, wd), BF16) for wd in widths]
    out_shape += [jax.ShapeDtypeStruct((n_rows, hy_w), F32)] * 2
    out_specs = [pl.BlockSpec((tm, wd), row) for wd in widths] + [pl.BlockSpec((tm, hy_w), row)] * 2
    return pl.pallas_call(
        functools.partial(_inproj_kernel, seq=seq, n_ctx=n_ctx, n_lat=n_batch * seq),
        grid=(n_rows // tm,),
        in_specs=[pl.BlockSpec((tm, d), row),
                  pl.BlockSpec((8, d), lambda i: (jnp.maximum(i * (tm // 8) - 1, 0), 0)),
                  pl.BlockSpec((8, d), lambda i: (jnp.minimum((i + 1) * (tm // 8), n_rows // 8 - 1), 0))]
        + _mod_specs(tiles_per_batch, n_batch, mod_base, (3, 4))
        + [
            pl.BlockSpec((1, d), const),
            pl.BlockSpec(w.shape, const, pipeline_mode=pl.Buffered(1)),
            pl.BlockSpec(gvec.shape, const),
            pl.BlockSpec(headmean.shape, const),
            pl.BlockSpec((tm, LANES), row),
            pl.BlockSpec((tm, LANES), row),
            pl.BlockSpec(conv_w.shape, const),
            pl.BlockSpec(conv_b.shape, const),
        ],
        out_specs=out_specs,
        out_shape=out_shape,
        compiler_params=_cparams(("arbitrary",)),
        name="inproj",
    )(h, h, h, mods, mods, g, w, gvec, headmean, cos_t, sin_t, conv_w, conv_b)


def _stack_heads(q):
    lane = lax.broadcasted_iota(jnp.int32, q.shape, 1)
    zero = jnp.zeros_like(q)
    return jnp.concatenate([jnp.where(lane < HEAD_DIM, q, zero), jnp.where(lane >= HEAD_DIM, q, zero)], axis=0)


def _qk(q2, k):
    return lax.dot_general(q2, k, (((1,), (1,)), ((), ())), preferred_element_type=F32)


def _gqa_kernel(q_ref, kc_ref, vc_ref, *rest, n_latent_chunks, tk):
    if n_latent_chunks:
        kl_ref, vl_ref, o_ref = rest
    else:
        (o_ref,) = rest
    tq = q_ref.shape[0]
    q2 = _stack_heads(q_ref[...])

    def pv(p, v_ref, rows):
        return jnp.concatenate(
            [jnp.dot(p[:tq], v_ref[rows, 0:LANES], preferred_element_type=F32),
             jnp.dot(p[tq:], v_ref[rows, LANES:2 * LANES], preferred_element_type=F32)], axis=0)

    s = _qk(q2, kc_ref[...])
    m = jnp.max(s, axis=-1, keepdims=True)
    acc = pv(jnp.exp2((s - m).astype(BF16)), vc_ref, slice(None))

    if n_latent_chunks:
        def body(j, carry):
            m, acc = carry
            rows = pl.ds(pl.multiple_of(j * tk, tk), tk)
            s = _qk(q2, kl_ref[rows, :])
            m_new = jnp.maximum(m, jnp.max(s, axis=-1, keepdims=True))
            p = jnp.exp2((s - m_new).astype(BF16))
            return m_new, jnp.exp2(m - m_new) * acc + pv(p, vl_ref, rows)

        m, acc = lax.fori_loop(0, n_latent_chunks, body, (m, acc), unroll=True)
    lane = lax.broadcasted_iota(jnp.int32, (tq, LANES), 1)
    num = jnp.where(lane < HEAD_DIM, acc[:tq], acc[tq:])
    den = pltpu.roll(jnp.where(lane < HEAD_DIM, acc[tq:], acc[:tq]), HEAD_DIM, 1)
    o_ref[...] = (num / den).astype(o_ref.dtype)


def _gqa(q, k, v2, out_rows, n_batch, seq, n_ctx, latent, kv_chunked=False):
    chunks = q.shape[1] // LANES
    ctx_blk0 = n_batch * seq // n_ctx
    tk = min(ATTN_TK, seq)
    kj = (lambda j: j) if kv_chunked else (lambda j: 0)
    if latent:
        tq = ATTN_TQ
        grid = (n_batch, chunks, seq // tq)
        qmap = lambda b, j, i: (b * (seq // tq) + i, j)
        kv_specs = [pl.BlockSpec((seq, LANES), lambda b, j, i: (b, kj(j))),
                    pl.BlockSpec((seq, 2 * LANES), lambda b, j, i: (b, kj(j)))]
        kv_args = (k, v2)
    else:
        tq = n_ctx
        grid = (n_batch, chunks, 1)
        qmap = lambda b, j, i: (ctx_blk0 + b, j)
        kv_specs, kv_args = [], ()
    return pl.pallas_call(
        functools.partial(_gqa_kernel, n_latent_chunks=seq // tk if latent else 0, tk=tk),
        grid=grid,
        in_specs=[pl.BlockSpec((tq, LANES), qmap),
                  pl.BlockSpec((n_ctx, LANES), lambda b, j, i: (ctx_blk0 + b, kj(j))),
                  pl.BlockSpec((n_ctx, 2 * LANES), lambda b, j, i: (ctx_blk0 + b, kj(j)))] + kv_specs,
        out_specs=pl.BlockSpec((tq, LANES), qmap),
        out_shape=jax.ShapeDtypeStruct((out_rows, q.shape[1]), BF16),
        compiler_params=_cparams(("arbitrary", "arbitrary", "arbitrary")),
        name="attn_latent" if latent else "attn_ctx",
    )(q, k, v2, *kv_args)


def _na_kernel(q_ref, kc_ref, vc_ref, kl_ref, vl_ref, tab_ref, mask_ref, o_ref, *, grid_rows):
    i = pl.program_id(2)
    nk = NA_KEY_ROWS * GRID_W
    ks = jnp.clip(i * NA_BLOCK_ROWS - NA_ROWS // 2, 0, grid_rows - NA_KEY_ROWS)
    keys = pl.ds(pl.multiple_of(ks * GRID_W, GRID_W), nk)
    k = kl_ref[keys, :]
    kc = kc_ref[...]
    sub_rows = NA_SUB_ROWS * GRID_W
    lane = lax.broadcasted_iota(jnp.int32, (sub_rows, LANES), 1)
    low = lane < HEAD_DIM
    for sub in range(NA_BLOCK_ROWS // NA_SUB_ROWS):
        rows = slice(sub * sub_rows, (sub + 1) * sub_rows)
        q = q_ref[rows, :]
        acc = []
        for hh in range(2):
            qh = jnp.where(low if hh == 0 else jnp.logical_not(low), q, jnp.zeros_like(q))
            slabs = []
            for qr in range(sub * NA_SUB_ROWS, (sub + 1) * NA_SUB_ROWS):
                par = qr % 2
                lane0 = (NA_BLOCK_ROWS - qr - par) * GRID_W
                slabs.append(tab_ref[0, hh, par, :, lane0:lane0 + nk] + mask_ref[0, qr:qr + 1, :])
            s_nb = _qk(qh, k) + jnp.concatenate(slabs, axis=0)
            s_cx = _qk(qh, kc)
            m = jnp.maximum(jnp.max(s_nb, axis=-1, keepdims=True), jnp.max(s_cx, axis=-1, keepdims=True))
            vcols = slice(hh * LANES, (hh + 1) * LANES)
            acc.append(jnp.dot(jnp.exp2((s_nb - m).astype(BF16)), vl_ref[keys, vcols], preferred_element_type=F32)
                       + jnp.dot(jnp.exp2((s_cx - m).astype(BF16)), vc_ref[:, vcols], preferred_element_type=F32))
        num = jnp.where(low, acc[0], acc[1])
        den = pltpu.roll(jnp.where(low, acc[1], acc[0]), HEAD_DIM, 1)
        o_ref[rows, :] = (num / den).astype(o_ref.dtype)


def _na_block_start(blk, grid_rows):
    return int(np.clip(blk * NA_BLOCK_ROWS - NA_ROWS // 2, 0, grid_rows - NA_KEY_ROWS))


def _na_variant_blocks(grid_rows):
    nblk = grid_rows // NA_BLOCK_ROWS
    return (0, min(1, nblk - 1), nblk - 1)


def _na_bias_tables(rpb, grid_rows):
    n_heads = rpb.shape[0]
    qc = np.arange(GRID_W)[:, None]
    kc = np.arange(GRID_W)[None, :]
    cs = np.clip(qc - NA_COLS // 2, 0, GRID_W - NA_COLS)
    col_ok = (kc >= cs) & (kc < cs + NA_COLS)
    col_hot = ((kc - qc + NA_COLS - 1)[:, :, None] == np.arange(2 * NA_COLS - 1)) & col_ok[:, :, None]
    by_col = jnp.einsum("abe,hde->hadb", jnp.asarray(col_hot, F32), rpb, precision=HIGHEST)
    by_col = jnp.where(jnp.asarray(col_ok)[None, :, None, :], by_col * LOG2E, NEG_BIG)
    n_off = 2 * NA_ROWS - 1
    slots = NA_BLOCK_ROWS + NA_KEY_ROWS
    tabs = []
    for blk in _na_variant_blocks(grid_rows):
        off = _na_block_start(blk, grid_rows) - blk * NA_BLOCK_ROWS + NA_ROWS - 1
        front = NA_BLOCK_ROWS - off
        t = jnp.pad(by_col, ((0, 0), (0, 0), (front, slots - n_off - front), (0, 0)))
        t = t.reshape(n_heads, GRID_W, slots * GRID_W)
        shifted = jnp.pad(t[:, :, GRID_W:], ((0, 0), (0, 0), (0, GRID_W)))
        tabs.append(jnp.stack([t, shifted], axis=1))
    return jnp.stack(tabs)


def _na_row_masks(grid_rows):
    masks = []
    for blk in _na_variant_blocks(grid_rows):
        qr = blk * NA_BLOCK_ROWS + np.arange(NA_BLOCK_ROWS)[:, None]
        kr = _na_block_start(blk, grid_rows) + np.arange(NA_KEY_ROWS)[None, :]
        rs = np.clip(qr - NA_ROWS // 2, 0, grid_rows - NA_ROWS)
        row_ok = (kr >= rs) & (kr < rs + NA_ROWS)
        masks.append(np.repeat(np.where(row_ok, 0.0, NEG_BIG), GRID_W, axis=1))
    return jnp.asarray(np.stack(masks).astype(np.float32))


def _na_latent(nq, nk, nv2, tab, masks, n_rows_out, n_batch, seq, n_ctx):
    chunks = nq.shape[1] // LANES
    tq = NA_BLOCK_ROWS * GRID_W
    nblk = seq // tq
    ctx_blk0 = n_batch * seq // n_ctx
    qmap = lambda b, j, i: (b * nblk + i, j)
    variant = lambda i: jnp.where(i == 0, 0, jnp.where(i == nblk - 1, 2, 1))
    tab_spec = pl.BlockSpec((1, 2) + tab.shape[2:], lambda b, j, i: (variant(i), j, 0, 0, 0))
    mask_spec = pl.BlockSpec((1,) + masks.shape[1:], lambda b, j, i: (variant(i), 0, 0))
    return pl.pallas_call(
        functools.partial(_na_kernel, grid_rows=seq // GRID_W),
        grid=(n_batch, chunks, nblk),
        in_specs=[pl.BlockSpec((tq, LANES), qmap),
                  pl.BlockSpec((n_ctx, LANES), lambda b, j, i: (ctx_blk0 + b, j)),
                  pl.BlockSpec((n_ctx, 2 * LANES), lambda b, j, i: (ctx_blk0 + b, j)),
                  pl.BlockSpec((seq, LANES), lambda b, j, i: (b, j)),
                  pl.BlockSpec((seq, 2 * LANES), lambda b, j, i: (b, j)),
                  tab_spec, mask_spec],
        out_specs=pl.BlockSpec((tq, LANES), qmap),
        out_shape=jax.ShapeDtypeStruct((n_rows_out, nq.shape[1]), BF16),
        compiler_params=_cparams(("arbitrary", "arbitrary", "arbitrary")),
        name="na_latent",
    )(nq, nk, nv2, nk, nv2, tab, masks)


def _hfilt_kernel(f_ref, w1_ref, b1_ref, w2_ref, b2_ref, w3_ref, fr_ref, dl_ref, k_ref, l1_ref, *, n):
    i = pl.program_id(0)
    f = f_ref[...]
    fr = fr_ref[...]
    h1 = jnp.sin(fr * (jnp.dot(f, w1_ref[...], precision=HIGHEST, preferred_element_type=F32) + b1_ref[...]))
    h2 = jnp.sin(fr * (jnp.dot(h1, w2_ref[...], precision=HIGHEST, preferred_element_type=F32) + b2_ref[...]))
    o = jnp.dot(h2, w3_ref[...], precision=HIGHEST, preferred_element_type=F32)
    cw = o.shape[1] // 2
    t = f.shape[0]
    row = i * t + lax.broadcasted_iota(jnp.int32, (t, cw), 0)
    decay = jnp.exp(-f[:, 0:1] * jnp.abs(dl_ref[...]))
    k = jnp.where(row < n, o[:, :cw], o[:, cw:]) * decay
    k = jnp.where(row == n, 0.0, k)
    k_ref[...] = k

    @pl.when(i == 0)
    def _():
        l1_ref[...] = jnp.zeros_like(l1_ref)

    l1_ref[...] += jnp.sum(jnp.abs(k), axis=0, keepdims=True)


def _filter_feats(n):
    rows = np.arange(2 * n)
    pos = np.where(rows <= n, rows, 2 * n - rows).astype(np.float32)
    pos = jnp.asarray(np.minimum(pos, n - 1))
    t = pos / max(n - 1, 1)
    bands = jnp.linspace(1e-4, FILTER_BANDS - 1, FILTER_BANDS, dtype=F32)
    ang = (2.0 * math.pi / n) * pos[:, None] * bands[None, :]
    feats = jnp.concatenate([t[:, None], jnp.cos(ang), -jnp.sin(ang)], axis=-1)
    return jnp.pad(feats, ((0, 0), (0, LANES - feats.shape[1])))


def _pad_to(a, shape):
    return jnp.pad(a, [(0, s - d) for s, d in zip(shape, a.shape)])


def _hfilter(n, w1, b1, w2, b2, w3, freq, deltas):
    feats = _filter_feats(n)
    w1, b1, freq = _pad_to(w1, (LANES, LANES)), _pad_to(b1, (1, LANES)), _pad_to(freq, (1, LANES))
    w2, b2, w3 = _pad_to(w2, (LANES, LANES)), _pad_to(b2, (1, LANES)), _pad_to(w3, (LANES, w3.shape[1]))
    rows, emb = feats.shape
    t = min(1024, rows)
    cw = w3.shape[1] // 2
    const = lambda i: (0, 0)
    full = lambda a: pl.BlockSpec(a.shape, const)
    return pl.pallas_call(
        functools.partial(_hfilt_kernel, n=n),
        grid=(rows // t,),
        in_specs=[pl.BlockSpec((t, emb), lambda i: (i, 0)), full(w1), full(b1), full(w2), full(b2), full(w3),
                  full(freq), full(deltas)],
        out_specs=[pl.BlockSpec((t, cw), lambda i: (i, 0)), pl.BlockSpec((1, cw), const)],
        out_shape=[jax.ShapeDtypeStruct((rows, cw), F32), jax.ShapeDtypeStruct((1, cw), F32)],
        compiler_params=_cparams(("arbitrary",)),
        name="hyena_filter",
    )(feats, w1, b1, w2, b2, w3, freq, deltas)


def _dft_outer_kernel(f_ref, x_ref, o_ref):
    c = x_ref.shape[-1]
    x = x_ref[...].reshape(f_ref.shape[1], c).astype(BF16)
    o_ref[...] = jnp.dot(f_ref[...], x, preferred_element_type=F32).reshape(o_ref.shape)


def _kron_table(fmat):
    return jnp.asarray(np.kron(np.asarray(fmat, np.float32), np.eye(OUTER_COLS, dtype=np.float32)).astype(BF16))


def _dft_outer_fwd(fmat, x3, nb, n1_len):
    k = fmat.shape[1]
    c = x3.shape[2]
    f = _kron_table(fmat)
    return pl.pallas_call(
        _dft_outer_kernel,
        grid=(nb, DFT_N2 // OUTER_COLS),
        in_specs=[pl.BlockSpec(f.shape, lambda b, j: (0, 0), pipeline_mode=pl.Buffered(1)),
                  pl.BlockSpec((k, OUTER_COLS, c), lambda b, j: (b, j, 0))],
        out_specs=pl.BlockSpec((1, 2, n1_len, OUTER_COLS, c), lambda b, j: (b, 0, 0, j, 0)),
        out_shape=jax.ShapeDtypeStruct((nb, 2, n1_len, DFT_N2, c), F32),
        compiler_params=_cparams(("arbitrary", "arbitrary")),
        name="dft_outer_fwd",
    )(f, x3)


def _dft_outer_inv(fmat, d5):
    nb, _, n1_len, _, c = d5.shape
    half = fmat.shape[0]
    f = _kron_table(fmat)
    return pl.pallas_call(
        _dft_outer_kernel,
        grid=(nb, DFT_N2 // OUTER_COLS),
        in_specs=[pl.BlockSpec(f.shape, lambda b, j: (0, 0), pipeline_mode=pl.Buffered(1)),
                  pl.BlockSpec((1, 2, n1_len, OUTER_COLS, c), lambda b, j: (b, 0, 0, j, 0))],
        out_specs=pl.BlockSpec((half, OUTER_COLS, c), lambda b, j: (b, j, 0)),
        out_shape=jax.ShapeDtypeStruct((nb * half, DFT_N2, c), F32),
        compiler_params=_cparams(("arbitrary", "arbitrary")),
        name="dft_outer_inv",
    )(f, d5)


def _inner_matrices(fc, fs, tc, ts):
    gr = fc * tc - fs * ts
    gi = -(fc * ts + fs * tc)
    return gr, gi


def _real_form(gr, gi):
    return jnp.concatenate([jnp.concatenate([gr, -gi], axis=1), jnp.concatenate([gi, gr], axis=1)], axis=0)


def _dft_inner_filter_kernel(fc_ref, fs_ref, tc_ref, ts_ref, b_ref, o_ref):
    fc, fs = fc_ref[...], fs_ref[...]
    n2, c = b_ref.shape[3], b_ref.shape[4]

    def body(kk, carry):
        g = _real_form(*_inner_matrices(fc, fs, tc_ref[kk], ts_ref[kk]))
        o_ref[kk] = jnp.dot(g.astype(BF16), b_ref[0, :, kk].reshape(2 * n2, c).astype(BF16),
                            preferred_element_type=F32)
        return carry

    lax.fori_loop(0, b_ref.shape[2], body, 0, unroll=2)


def _dft_inner_conv_kernel(fc_ref, fs_ref, tc_ref, ts_ref, b_ref, kh_ref, sc_ref, o_ref):
    fc, fs = fc_ref[...], fs_ref[...]
    nb, n2, c = b_ref.shape[0], b_ref.shape[3], b_ref.shape[4]
    scale = sc_ref[...]

    def body(kk, carry):
        gr, gi = _inner_matrices(fc, fs, tc_ref[kk], ts_ref[kk])
        g = _real_form(gr, gi).astype(BF16)
        gh = _real_form(gr.T, -gi.T).astype(BF16)
        kh = kh_ref[kk]
        kr, ki = kh[:n2], kh[n2:]
        for b in range(nb):
            x = jnp.dot(g, b_ref[b, :, kk].reshape(2 * n2, c).astype(BF16), preferred_element_type=F32)
            xr, xi = x[:n2], x[n2:]
            y = jnp.concatenate([(xr * kr - xi * ki) * scale, (xr * ki + xi * kr) * scale], axis=0)
            o_ref[b, :, kk] = jnp.dot(gh, y.astype(BF16), preferred_element_type=F32).reshape(2, n2, c)
        return carry

    lax.fori_loop(0, b_ref.shape[2], body, 0, unroll=2)


def _dft_tables(n1_len):
    n_total = n1_len * DFT_N2
    a = np.arange(DFT_N2)
    ang = 2.0 * np.pi * np.outer(a, a) / DFT_N2
    k1 = np.arange(n1_len)
    tw = 2.0 * np.pi * np.outer(k1, a) / n_total
    o = 2.0 * np.pi * np.outer(k1, k1) / n1_len
    f32 = lambda v: jnp.asarray(v.astype(np.float32))
    outer_fwd = np.concatenate([np.cos(o), -np.sin(o)], axis=0)
    outer_inv = np.concatenate([np.cos(o), -np.sin(o)], axis=1)
    return dict(
        fc=f32(np.cos(ang)), fs=f32(np.sin(ang)),
        tc=f32(np.cos(tw)).reshape(n1_len, 1, DFT_N2), ts=f32(np.sin(tw)).reshape(n1_len, 1, DFT_N2),
        outer_fwd_full=outer_fwd, outer_fwd_half=outer_fwd[:, :n1_len // 2], outer_inv_half=outer_inv[:n1_len // 2],
    )


def _inner_specs():
    sq = pl.BlockSpec((DFT_N2, DFT_N2), lambda k: (0, 0))
    tw = pl.BlockSpec((INNER_K1, 1, DFT_N2), lambda k: (k, 0, 0))
    return [sq, sq, tw, tw]


def _filter_spectrum(kfull, tabs, n1_len):
    c = kfull.shape[1]
    b5 = _dft_outer_fwd(tabs["outer_fwd_full"], kfull.reshape(n1_len, DFT_N2, c), 1, n1_len)
    return pl.pallas_call(
        _dft_inner_filter_kernel,
        grid=(n1_len // INNER_K1,),
        in_specs=_inner_specs() + [pl.BlockSpec((1, 2, INNER_K1, DFT_N2, c), lambda k: (0, 0, k, 0, 0))],
        out_specs=pl.BlockSpec((INNER_K1, 2 * DFT_N2, c), lambda k: (k, 0, 0)),
        out_shape=jax.ShapeDtypeStruct((n1_len, 2 * DFT_N2, c), F32),
        compiler_params=_cparams(("arbitrary",)),
        name="dft_inner_filter",
    )(tabs["fc"], tabs["fs"], tabs["tc"], tabs["ts"], b5)


def _long_conv(z_all, nb, khat, scale, tabs, n1_len):
    c = z_all.shape[1]
    bz = _dft_outer_fwd(tabs["outer_fwd_half"], z_all.reshape(z_all.shape[0] // DFT_N2, DFT_N2, c), nb, n1_len)
    blk = pl.BlockSpec((nb, 2, INNER_K1, DFT_N2, c), lambda k: (0, 0, k, 0, 0))
    d = pl.pallas_call(
        _dft_inner_conv_kernel,
        grid=(n1_len // INNER_K1,),
        in_specs=_inner_specs() + [blk, pl.BlockSpec((INNER_K1, 2 * DFT_N2, c), lambda k: (k, 0, 0)),
                                   pl.BlockSpec((1, c), lambda k: (0, 0))],
        out_specs=blk,
        out_shape=jax.ShapeDtypeStruct(bz.shape, F32),
        compiler_params=_cparams(("arbitrary",)),
        name="dft_inner_conv",
    )(tabs["fc"], tabs["fs"], tabs["tc"], tabs["ts"], bz, khat, scale)
    y3 = _dft_outer_inv(tabs["outer_inv_half"], d)
    return y3.reshape(y3.shape[0] * DFT_N2, c)


def _hyena_ctx_kernel(ff_ref, fi_ref, k_ref, l1_ref, z_ref, o_ref):
    ff = ff_ref[...]
    n = z_ref.shape[0]
    big = ff.shape[1]
    kh = jnp.dot(ff, k_ref[...], precision=HIGHEST, preferred_element_type=F32)
    x = jnp.dot(ff[:, :n], z_ref[...], precision=HIGHEST, preferred_element_type=F32)
    kr, ki = kh[:big], kh[big:]
    xr, xi = x[:big], x[big:]
    scale = 1.0 / (l1_ref[...] * big)
    y = jnp.concatenate([(xr * kr - xi * ki) * scale, (xr * ki + xi * kr) * scale], axis=0)
    o_ref[...] = jnp.dot(fi_ref[...], y, precision=HIGHEST, preferred_element_type=F32)


def _hyena_ctx(z, kfull, l1, row0_blk, n_batch, n):
    big = 2 * n
    a = np.arange(big)
    ang = 2.0 * np.pi * np.outer(a, a) / big
    ff = jnp.asarray(np.concatenate([np.cos(ang), -np.sin(ang)], axis=0).astype(np.float32))
    fi = jnp.asarray(np.concatenate([np.cos(ang), -np.sin(ang)], axis=1)[:n].astype(np.float32))
    c = z.shape[1]
    const = lambda b: (0, 0)
    return pl.pallas_call(
        _hyena_ctx_kernel,
        grid=(n_batch,),
        in_specs=[pl.BlockSpec(ff.shape, const), pl.BlockSpec(fi.shape, const), pl.BlockSpec(kfull.shape, const),
                  pl.BlockSpec((1, c), const), pl.BlockSpec((n, c), lambda b: (row0_blk + b, 0))],
        out_specs=pl.BlockSpec((n, c), lambda b: (b, 0)),
        out_shape=jax.ShapeDtypeStruct((n_batch * n, c), F32),
        compiler_params=_cparams(("arbitrary",)),
        name="hyena_ctx",
    )(ff, fi, kfull, l1, z)


def _rope_tables(seq, n_batch, n_ctx_rows):
    t = jnp.arange(seq, dtype=jnp.int32)
    rows = (t // GRID_W).astype(F32)
    cols = (t % GRID_W).astype(F32)
    nf = HEAD_DIM // 4
    inv = ROPE_THETA ** (-jnp.arange(nf, dtype=F32) / nf)
    ar, ac = rows[:, None] * inv, cols[:, None] * inv
    cos = jnp.concatenate([jnp.cos(ar), jnp.cos(ar), jnp.cos(ac), jnp.cos(ac)], axis=1)
    sin = jnp.concatenate([-jnp.sin(ar), jnp.sin(ar), -jnp.sin(ac), jnp.sin(ac)], axis=1)
    cos = jnp.tile(cos, (n_batch, LANES // HEAD_DIM))
    sin = jnp.tile(sin, (n_batch, LANES // HEAD_DIM))
    cos = jnp.concatenate([cos, jnp.ones((n_ctx_rows, LANES), F32)], axis=0)
    sin = jnp.concatenate([sin, jnp.zeros((n_ctx_rows, LANES), F32)], axis=0)
    return cos, sin


def _head_mean_matrix():
    a = np.arange(LANES)
    return jnp.asarray((a[:, None] // HEAD_DIM == a[None, :] // HEAD_DIM).astype(np.float32) / HEAD_DIM, dtype=BF16)


def kernel(x, c, ctx, c_ctx, w_ada, b_ada, g_ffn1, w_ffn1_gate, w_ffn1_up, w_ffn1_down, g_mix, w_in, w_out, g_q_attn, g_k_attn, conv_w, conv_b, filt_w1, filt_b1, filt_w2, filt_b2, filt_w3, filt_freq, hyena_skip, g_q_na, g_k_na, na_rpb, g_ffn2, w_ffn2_gate, w_ffn2_up, w_ffn2_down):
    n_batch, seq, d = x.shape
    n_ctx = ctx.shape[1]
    depth = w_ada.shape[0]
    assert d == _D and n_batch + 1 <= 8 and seq % TOKEN_TILE == 0 and seq % (NA_BLOCK_ROWS * GRID_W) == 0
    assert n_ctx == 256 and (n_batch * n_ctx) % TOKEN_TILE == 0
    n_lat = n_batch * seq
    n_all = n_lat + n_batch * n_ctx
    tiles_per_batch = seq // TOKEN_TILE
    hy_w = conv_w.shape[2] // 3
    n1_len = 2 * seq // DFT_N2

    cond8 = jnp.zeros((8, d), F32).at[:n_batch].set(c).at[n_batch].set(c_ctx)
    mods = _adaln_all(cond8, w_ada, b_ada).reshape(depth * 8 * N_MOD, 1, d)
    h = jnp.concatenate([x.reshape(n_lat, d), ctx.reshape(n_batch * n_ctx, d)], axis=0)

    cos_t, sin_t = _rope_tables(seq, n_batch, n_batch * n_ctx)
    headmean = _head_mean_matrix()
    na_masks = _na_row_masks(seq // GRID_W)
    tabs = _dft_tables(n1_len)
    deltas = jnp.linspace(math.log(DECAY_TARGET) / SLOW_DECAY_PCT, math.log(DECAY_TARGET) / FAST_DECAY_PCT,
                          hy_w, dtype=F32).reshape(1, hy_w)

    q_heads = [hd for pair in range(ATTN_HEADS // 2) for hd in (pair, pair + ATTN_HEADS // 2)]
    q_segs = [(hd * HEAD_DIM, (hd + 1) * HEAD_DIM) for hd in q_heads]
    in_segs = q_segs + [(512, 640), (1536, 1792), (1792, 2048), (640, 768), (2048, 2304), (768, 1536)]

    for l in range(depth):
        last = l == depth - 1
        base = l * 8 * N_MOD
        rep = lambda g, k: jnp.tile(g, k)
        gvec = jnp.concatenate([rep(g_q_attn[l], ATTN_HEADS), rep(g_k_attn[l], ATTN_KV_HEADS),
                                rep(g_q_na[l], NA_HEADS), rep(g_k_na[l], NA_HEADS)]).reshape(1, N_NORM)

        h = _ffn(h, mods, base, (0, 1, 2), g_ffn1[l].reshape(1, d), w_ffn1_gate[l].astype(BF16),
                 w_ffn1_up[l].astype(BF16), w_ffn1_down[l].astype(BF16), n_all, tiles_per_batch, n_batch)

        w_in_l = jnp.concatenate([w_in[l][:, a:b] for a, b in in_segs], axis=1).astype(BF16)
        q, k, nq, nk, v, nv, z, x0 = _inproj(h, mods, base, g_mix[l].reshape(1, d), w_in_l, gvec, headmean, cos_t, sin_t,
                                             conv_w[l], conv_b[l].reshape(1, -1), tiles_per_batch, n_batch, seq, n_ctx)

        out_rows = n_lat if last else n_all
        ya = _gqa(q, k, v, out_rows, n_batch, seq, n_ctx, latent=True)
        yc = _na_latent(nq, nk, nv, _na_bias_tables(na_rpb[l], seq // GRID_W), na_masks, out_rows, n_batch, seq, n_ctx)

        fargs = (filt_w1[l], filt_b1[l].reshape(1, -1), filt_w2[l], filt_b2[l].reshape(1, -1), filt_w3[l],
                 filt_freq[l].reshape(1, -1), deltas)
        kfull, l1 = _hfilter(seq, *fargs)
        khat = _filter_spectrum(kfull, tabs, n1_len)
        yconv = _long_conv(z, n_batch, khat, 1.0 / (l1 * (2 * seq)), tabs, n1_len)

        if not last:
            ya_c = _gqa(q, k, v, n_all, n_batch, seq, n_ctx, latent=False)
            yc_c = _gqa(nq, nk, nv, n_all, n_batch, seq, n_ctx, latent=False, kv_chunked=True)
            ya = lax.dynamic_update_slice(ya, ya_c[n_lat:], (n_lat, 0))
            yc = lax.dynamic_update_slice(yc, yc_c[n_lat:], (n_lat, 0))
            kfull_c, l1_c = _hfilter(n_ctx, *fargs)
            yconv_c = _hyena_ctx(z, kfull_c, l1_c, n_lat // n_ctx, n_batch, n_ctx)
            yconv = jnp.concatenate([yconv, yconv_c], axis=0)

        w_o = w_out[l]
        wa = jnp.concatenate([w_o[a:b] for a, b in q_segs], axis=0).astype(BF16)
        wb = w_o[512:512 + hy_w].astype(BF16)
        wc = w_o[512 + hy_w:].astype(BF16)
        h = _ffn(h, mods, base, (6, 7, 8), g_ffn2[l].reshape(1, d), w_ffn2_gate[l].astype(BF16),
                 w_ffn2_up[l].astype(BF16), w_ffn2_down[l].astype(BF16), out_rows, tiles_per_batch, n_batch,
                 mixer=(ya, yconv, z, x0, hyena_skip[l].reshape(1, hy_w), yc, wa, wb, wc))

    return h[:n_lat].reshape(n_batch, seq, d)
```

```python
import functools
import math

import numpy as np
import jax
import jax.numpy as jnp
from jax import lax
from jax.experimental import pallas as pl
from jax.experimental.pallas import tpu as pltpu

F32 = jnp.float32
BF16 = jnp.bfloat16
HIGHEST = lax.Precision.HIGHEST

HEAD_DIM = 64
GRID_W = 64
ATTN_HEADS = 8
ATTN_KV_HEADS = 2
NA_HEADS = 4
NA_ROWS = 8
NA_COLS = 16
FILTER_BANDS = 16
DECAY_TARGET = 1e-2
FAST_DECAY_PCT = 0.3
SLOW_DECAY_PCT = 1.5
ROPE_THETA = 10000.0
EPS = 1e-6
N_MOD = 9

LANES = 128
TOKEN_TILE = 512
FFN_CHUNK = 256
ATTN_TQ = 1024
ATTN_TK = 2048
NA_BLOCK_ROWS = 8
NA_KEY_ROWS = 16
NA_SUB_ROWS = 4
DFT_N2 = 128
OUTER_COLS = 8
INNER_K1 = 8
NEG_BIG = -1e30
VMEM_LIMIT = 56 * 1024 * 1024


def _cparams(sem):
    return pltpu.CompilerParams(dimension_semantics=sem, vmem_limit_bytes=VMEM_LIMIT)


def _ada_kernel(c_ref, w_ref, b_ref, o_ref):
    c = c_ref[...]
    s = c * jax.nn.sigmoid(c)
    o_ref[0] = jnp.dot(s, w_ref[0], precision=HIGHEST, preferred_element_type=F32) + b_ref[0]


def _adaln_all(cond8, w_ada, b_ada):
    depth, d, nd = w_ada.shape
    tn = 1024
    return pl.pallas_call(
        _ada_kernel,
        grid=(depth, nd // tn),
        in_specs=[
            pl.BlockSpec((8, d), lambda l, j: (0, 0)),
            pl.BlockSpec((1, d, tn), lambda l, j: (l, 0, j)),
            pl.BlockSpec((1, 1, tn), lambda l, j: (l, 0, j)),
        ],
        out_specs=pl.BlockSpec((1, 8, tn), lambda l, j: (l, 0, j)),
        out_shape=jax.ShapeDtypeStruct((depth, 8, nd), F32),
        compiler_params=_cparams(("arbitrary", "arbitrary")),
        name="adaln",
    )(cond8, w_ada, b_ada.reshape(depth, 1, nd))


def _norm_mod(x, g, sh, sc):
    ms = jnp.mean(x * x, axis=-1, keepdims=True)
    y = x * lax.rsqrt(ms + EPS) * g
    return y * (1.0 + sc) + sh


def _mod_specs(tiles_per_batch, n_batch, base, idxs):
    def spec(which):
        return pl.BlockSpec(
            (1, 1, _D),
            lambda i, w=which: (base + jnp.minimum(i // tiles_per_batch, n_batch) * N_MOD + w, 0, 0))
    return [spec(w) for w in idxs]


_D = 1024


def _ffn_kernel(h_ref, *refs, with_mixer):
    x = h_ref[...]
    if with_mixer:
        gt2_ref, ya_ref, yconv_ref, z_ref, x0_ref, skip_ref, yc_ref, wa_ref, wb_ref, wc_ref = refs[:10]
        refs = refs[10:]
        yb = ((yconv_ref[...] + z_ref[...] * skip_ref[...]) * x0_ref[...]).astype(BF16)
        x = x + gt2_ref[0] * (jnp.dot(ya_ref[...], wa_ref[...], preferred_element_type=F32)
                              + jnp.dot(yb, wb_ref[...], preferred_element_type=F32)
                              + jnp.dot(yc_ref[...], wc_ref[...], preferred_element_type=F32))
    sh_ref, sc_ref, gt_ref, g_ref, wg_ref, wu_ref, wd_ref, o_ref, xb_ref, acc_ref = refs
    if with_mixer:
        o_ref[...] = x
    xb_ref[...] = _norm_mod(x, g_ref[...], sh_ref[0], sc_ref[0]).astype(BF16)

    for k in range(wg_ref.shape[1] // FFN_CHUNK):
        cols = slice(k * FFN_CHUNK, (k + 1) * FFN_CHUNK)
        xb = xb_ref[...]
        a = jnp.dot(xb, wg_ref[:, cols], preferred_element_type=F32)
        u = jnp.dot(xb, wu_ref[:, cols], preferred_element_type=F32)
        act = (a * jax.nn.sigmoid(a) * u).astype(BF16)
        part = jnp.dot(act, wd_ref[cols, :], preferred_element_type=F32)
        if k == 0:
            acc_ref[...] = part
        else:
            acc_ref[...] += part
    resid = o_ref[...] if with_mixer else h_ref[...]
    o_ref[...] = resid + 0.5 * gt_ref[0] * acc_ref[...]


def _ffn(h, mods, mod_base, which, g, wg, wu, wd, n_rows, tiles_per_batch, n_batch, mixer=None):
    d = h.shape[1]
    tm = TOKEN_TILE
    row = lambda i: (i, 0)
    const = lambda i: (0, 0)
    rows = lambda a: pl.BlockSpec((tm, a.shape[1]), row)
    full = lambda a: pl.BlockSpec(a.shape, const)
    once = lambda a: pl.BlockSpec(a.shape, const, pipeline_mode=pl.Buffered(1))
    mix_specs, mix_args = [], ()
    if mixer is not None:
        ya, yconv, z, x0, skip, yc, wa, wb, wc = mixer
        mix_specs = _mod_specs(tiles_per_batch, n_batch, mod_base, (5,)) + [
            rows(ya), rows(yconv), rows(z), rows(x0), full(skip), rows(yc), once(wa), once(wb), once(wc)]
        mix_args = (mods,) + tuple(mixer)
    return pl.pallas_call(
        functools.partial(_ffn_kernel, with_mixer=mixer is not None),
        grid=(n_rows // tm,),
        in_specs=[rows(h)] + mix_specs + _mod_specs(tiles_per_batch, n_batch, mod_base, which)
        + [full(g), once(wg), once(wu), once(wd)],
        out_specs=pl.BlockSpec((tm, d), row),
        out_shape=jax.ShapeDtypeStruct((n_rows, d), F32),
        scratch_shapes=[pltpu.VMEM((tm, d), BF16), pltpu.VMEM((tm, d), F32)],
        compiler_params=_cparams(("arbitrary",)),
        name="ffn_mix" if mixer is not None else "ffn",
    )(h, *mix_args, mods, mods, mods, g, wg, wu, wd)


N_NORM = 1152
N_ROPE = 640
ATTN_Q_CHUNKS = (0, 1, 2, 3)
NA_Q_CHUNKS = (5, 6)
LOG2E = math.log2(math.e)


def _inproj_kernel(h_ref, prev_ref, next_ref, sh_ref, sc_ref, g_ref, w_ref, gv_ref, hm_ref, cos_ref, sin_ref,
                   cw_ref, cb_ref, q_ref, k_ref, nq_ref, nk_ref, v_ref, nv_ref, z_ref, x0_ref, *, seq, n_ctx, n_lat):
    xb = _norm_mod(h_ref[...], g_ref[...], sh_ref[0], sc_ref[0]).astype(BF16)
    hm = hm_ref[...]
    cosv = cos_ref[...]
    sinv = sin_ref[...]
    lane = lax.broadcasted_iota(jnp.int32, cosv.shape, 1)
    first_half = (lane % 32) < 16
    outs = []
    p_norm = jnp.dot(xb, w_ref[:, 0:N_NORM], preferred_element_type=F32)
    for c in range(N_NORM // LANES):
        p = p_norm[:, c * LANES:(c + 1) * LANES]
        ms = jnp.dot((p * p).astype(BF16), hm, preferred_element_type=F32)
        y = p * lax.rsqrt(ms + EPS) * gv_ref[:, c * LANES:(c + 1) * LANES]
        if c < N_ROPE // LANES:
            swapped = jnp.where(first_half, pltpu.roll(y, LANES - 16, 1), pltpu.roll(y, 16, 1))
            y = y * cosv + swapped * sinv
        if c in ATTN_Q_CHUNKS or c in NA_Q_CHUNKS:
            y = y * (HEAD_DIM ** -0.5 * LOG2E)
        outs.append(y.astype(BF16))
    for c in range(4):
        q_ref[:, c * LANES:(c + 1) * LANES] = outs[c]
    k_ref[...] = outs[4]
    nq_ref[:, 0:LANES] = outs[5]
    nq_ref[:, LANES:2 * LANES] = outs[6]
    nk_ref[:, 0:LANES] = outs[7]
    nk_ref[:, LANES:2 * LANES] = outs[8]
    p_rest = jnp.dot(xb, w_ref[:, N_NORM:], preferred_element_type=F32)
    v = p_rest[:, 0:LANES]
    v_ref[:, 0:LANES] = jnp.where(lane < HEAD_DIM, v, 1.0).astype(BF16)
    v_ref[:, LANES:2 * LANES] = jnp.where(lane < HEAD_DIM, 1.0, v).astype(BF16)
    for c in range(NA_HEADS // 2):
        nv = p_rest[:, (1 + c) * LANES:(2 + c) * LANES]
        nv_ref[:, 2 * c * LANES:(2 * c + 1) * LANES] = jnp.where(lane < HEAD_DIM, nv, 1.0).astype(BF16)
        nv_ref[:, (2 * c + 1) * LANES:(2 * c + 2) * LANES] = jnp.where(lane < HEAD_DIM, 1.0, nv).astype(BF16)

    hy = p_rest[:, 3 * LANES:]
    tm = hy.shape[0]
    halo = jnp.concatenate([prev_ref[...], next_ref[...]], axis=0)
    halo_b = _norm_mod(halo, g_ref[...], sh_ref[0], sc_ref[0]).astype(BF16)
    hy_halo = jnp.dot(halo_b, w_ref[:, N_NORM + 3 * LANES:], preferred_element_type=F32)
    row0 = pl.program_id(0) * tm
    is_ctx = row0 >= n_lat
    period = jnp.where(is_ctx, n_ctx, seq)
    local = lax.broadcasted_iota(jnp.int32, (tm, 1), 0)
    pos = jnp.bitwise_and(row0 + local - jnp.where(is_ctx, n_lat, 0), period - 1)
    up = jnp.where(local == 0, hy_halo[7:8, :], pltpu.roll(hy, 1, 0))
    up = jnp.where(pos == 0, 0.0, up)
    dn = jnp.where(local == tm - 1, hy_halo[8:9, :], pltpu.roll(hy, tm - 1, 0))
    dn = jnp.where(pos == period - 1, 0.0, dn)
    uc = up * cw_ref[0:1, :] + hy * cw_ref[1:2, :] + dn * cw_ref[2:3, :] + cb_ref[...]
    cw = uc.shape[1] // 3
    x0_ref[...] = uc[:, :cw]
    z_ref[...] = uc[:, 2 * cw:] * uc[:, cw:2 * cw]


def _inproj(h, mods, mod_base, g, w, gvec, headmean, cos_t, sin_t, conv_w, conv_b, tiles_per_batch, n_batch, seq,
            n_ctx):
    n_rows, d = h.shape
    tm = TOKEN_TILE
    assert seq & (seq - 1) == 0 and n_ctx & (n_ctx - 1) == 0
    row = lambda i: (i, 0)
    const = lambda i: (0, 0)
    widths = (512, 128, 256, 256, 256, 512)
    hy_w = conv_w.shape[1] // 3
    out_shape = [jax.ShapeDtypeStruct((n_rows, wd), BF16) for wd in widths]
    out_shape += [jax.ShapeDtypeStruct((n_rows, hy_w), F32)] * 2
    out_specs = [pl.BlockSpec((tm, wd), row) for wd in widths] + [pl.BlockSpec((tm, hy_w), row)] * 2
    return pl.pallas_call(
        functools.partial(_inproj_kernel, seq=seq, n_ctx=n_ctx, n_lat=n_batch * seq),
        grid=(n_rows // tm,),
        in_specs=[pl.BlockSpec((tm, d), row),
                  pl.BlockSpec((8, d), lambda i: (jnp.maximum(i * (tm // 8) - 1, 0), 0)),
                  pl.BlockSpec((8, d), lambda i: (jnp.minimum((i + 1) * (tm // 8), n_rows // 8 - 1), 0))]
        + _mod_specs(tiles_per_batch, n_batch, mod_base, (3, 4))
        + [
            pl.BlockSpec((1, d), const),
            pl.BlockSpec(w.shape, const, pipeline_mode=pl.Buffered(1)),
            pl.BlockSpec(gvec.shape, const),
            pl.BlockSpec(headmean.shape, const),
            pl.BlockSpec((tm, LANES), row),
            pl.BlockSpec((tm, LANES), row),
            pl.BlockSpec(conv_w.shape, const),
            pl.BlockSpec(conv_b.shape, const),
        ],
        out_specs=out_specs,
        out_shape=out_shape,
        compiler_params=_cparams(("arbitrary",)),
        name="inproj",
    )(h, h, h, mods, mods, g, w, gvec, headmean, cos_t, sin_t, conv_w, conv_b)


def _stack_heads(q):
    lane = lax.broadcasted_iota(jnp.int32, q.shape, 1)
    zero = jnp.zeros_like(q)
    return jnp.concatenate([jnp.where(lane < HEAD_DIM, q, zero), jnp.where(lane >= HEAD_DIM, q, zero)], axis=0)


def _qk(q2, k):
    return lax.dot_general(q2, k, (((1,), (1,)), ((), ())), preferred_element_type=F32)


def _gqa_kernel(q_ref, kc_ref, vc_ref, *rest, n_latent_chunks, tk):
    if n_latent_chunks:
        kl_ref, vl_ref, o_ref = rest
    else:
        (o_ref,) = rest
    tq = q_ref.shape[0]
    q2 = _stack_heads(q_ref[...])

    def pv(p, v_ref, rows):
        return jnp.concatenate(
            [jnp.dot(p[:tq], v_ref[rows, 0:LANES], preferred_element_type=F32),
             jnp.dot(p[tq:], v_ref[rows, LANES:2 * LANES], preferred_element_type=F32)], axis=0)

    s = _qk(q2, kc_ref[...])
    m = jnp.max(s, axis=-1, keepdims=True)
    acc = pv(jnp.exp2((s - m).astype(BF16)), vc_ref, slice(None))

    if n_latent_chunks:
        def body(j, carry):
            m, acc = carry
            rows = pl.ds(pl.multiple_of(j * tk, tk), tk)
            s = _qk(q2, kl_ref[rows, :])
            m_new = jnp.maximum(m, jnp.max(s, axis=-1, keepdims=True))
            p = jnp.exp2((s - m_new).astype(BF16))
            return m_new, jnp.exp2(m - m_new) * acc + pv(p, vl_ref, rows)

        m, acc = lax.fori_loop(0, n_latent_chunks, body, (m, acc), unroll=True)
    lane = lax.broadcasted_iota(jnp.int32, (tq, LANES), 1)
    num = jnp.where(lane < HEAD_DIM, acc[:tq], acc[tq:])
    den = pltpu.roll(jnp.where(lane < HEAD_DIM, acc[tq:], acc[:tq]), HEAD_DIM, 1)
    o_ref[...] = (num / den).astype(o_ref.dtype)


def _gqa(q, k, v2, out_rows, n_batch, seq, n_ctx, latent, kv_chunked=False):
    chunks = q.shape[1] // LANES
    ctx_blk0 = n_batch * seq // n_ctx
    tk = min(ATTN_TK, seq)
    kj = (lambda j: j) if kv_chunked else (lambda j: 0)
    if latent:
        tq = ATTN_TQ
        grid = (n_batch, chunks, seq // tq)
        qmap = lambda b, j, i: (b * (seq // tq) + i, j)
        kv_specs = [pl.BlockSpec((seq, LANES), lambda b, j, i: (b, kj(j))),
                    pl.BlockSpec((seq, 2 * LANES), lambda b, j, i: (b, kj(j)))]
        kv_args = (k, v2)
    else:
        tq = n_ctx
        grid = (n_batch, chunks, 1)
        qmap = lambda b, j, i: (ctx_blk0 + b, j)
        kv_specs, kv_args = [], ()
    return pl.pallas_call(
        functools.partial(_gqa_kernel, n_latent_chunks=seq // tk if latent else 0, tk=tk),
        grid=grid,
        in_specs=[pl.BlockSpec((tq, LANES), qmap),
                  pl.BlockSpec((n_ctx, LANES), lambda b, j, i: (ctx_blk0 + b, kj(j))),
                  pl.BlockSpec((n_ctx, 2 * LANES), lambda b, j, i: (ctx_blk0 + b, kj(j)))] + kv_specs,
        out_specs=pl.BlockSpec((tq, LANES), qmap),
        out_shape=jax.ShapeDtypeStruct((out_rows, q.shape[1]), BF16),
        compiler_params=_cparams(("arbitrary", "arbitrary", "arbitrary")),
        name="attn_latent" if latent else "attn_ctx",
    )(q, k, v2, *kv_args)


def _na_kernel(q_ref, kc_ref, vc_ref, kl_ref, vl_ref, tab_ref, mask_ref, o_ref, *, grid_rows):
    i = pl.program_id(2)
    nk = NA_KEY_ROWS * GRID_W
    ks = jnp.clip(i * NA_BLOCK_ROWS - NA_ROWS // 2, 0, grid_rows - NA_KEY_ROWS)
    keys = pl.ds(pl.multiple_of(ks * GRID_W, GRID_W), nk)
    k = kl_ref[keys, :]
    kc = kc_ref[...]
    sub_rows = NA_SUB_ROWS * GRID_W
    lane = lax.broadcasted_iota(jnp.int32, (sub_rows, LANES), 1)
    low = lane < HEAD_DIM
    for sub in range(NA_BLOCK_ROWS // NA_SUB_ROWS):
        rows = slice(sub * sub_rows, (sub + 1) * sub_rows)
        q = q_ref[rows, :]
        acc = []
        for hh in range(2):
            qh = jnp.where(low if hh == 0 else jnp.logical_not(low), q, jnp.zeros_like(q))
            slabs = []
            for qr in range(sub * NA_SUB_ROWS, (sub + 1) * NA_SUB_ROWS):
                par = qr % 2
                lane0 = (NA_BLOCK_ROWS - qr - par) * GRID_W
                slabs.append(tab_ref[0, hh, par, :, lane0:lane0 + nk] + mask_ref[0, qr:qr + 1, :])
            s_nb = _qk(qh, k) + jnp.concatenate(slabs, axis=0)
            s_cx = _qk(qh, kc)
            m = jnp.maximum(jnp.max(s_nb, axis=-1, keepdims=True), jnp.max(s_cx, axis=-1, keepdims=True))
            vcols = slice(hh * LANES, (hh + 1) * LANES)
            acc.append(jnp.dot(jnp.exp2((s_nb - m).astype(BF16)), vl_ref[keys, vcols], preferred_element_type=F32)
                       + jnp.dot(jnp.exp2((s_cx - m).astype(BF16)), vc_ref[:, vcols], preferred_element_type=F32))
        num = jnp.where(low, acc[0], acc[1])
        den = pltpu.roll(jnp.where(low, acc[1], acc[0]), HEAD_DIM, 1)
        o_ref[rows, :] = (num / den).astype(o_ref.dtype)


def _na_block_start(blk, grid_rows):
    return int(np.clip(blk * NA_BLOCK_ROWS - NA_ROWS // 2, 0, grid_rows - NA_KEY_ROWS))


def _na_variant_blocks(grid_rows):
    nblk = grid_rows // NA_BLOCK_ROWS
    return (0, min(1, nblk - 1), nblk - 1)


def _na_bias_tables(rpb, grid_rows):
    n_heads = rpb.shape[0]
    qc = np.arange(GRID_W)[:, None]
    kc = np.arange(GRID_W)[None, :]
    cs = np.clip(qc - NA_COLS // 2, 0, GRID_W - NA_COLS)
    col_ok = (kc >= cs) & (kc < cs + NA_COLS)
    col_hot = ((kc - qc + NA_COLS - 1)[:, :, None] == np.arange(2 * NA_COLS - 1)) & col_ok[:, :, None]
    by_col = jnp.einsum("abe,hde->hadb", jnp.asarray(col_hot, F32), rpb, precision=HIGHEST)
    by_col = jnp.where(jnp.asarray(col_ok)[None, :, None, :], by_col * LOG2E, NEG_BIG)
    n_off = 2 * NA_ROWS - 1
    slots = NA_BLOCK_ROWS + NA_KEY_ROWS
    tabs = []
    for blk in _na_variant_blocks(grid_rows):
        off = _na_block_start(blk, grid_rows) - blk * NA_BLOCK_ROWS + NA_ROWS - 1
        front = NA_BLOCK_ROWS - off
        t = jnp.pad(by_col, ((0, 0), (0, 0), (front, slots - n_off - front), (0, 0)))
        t = t.reshape(n_heads, GRID_W, slots * GRID_W)
        shifted = jnp.pad(t[:, :, GRID_W:], ((0, 0), (0, 0), (0, GRID_W)))
        tabs.append(jnp.stack([t, shifted], axis=1))
    return jnp.stack(tabs)


def _na_row_masks(grid_rows):
    masks = []
    for blk in _na_variant_blocks(grid_rows):
        qr = blk * NA_BLOCK_ROWS + np.arange(NA_BLOCK_ROWS)[:, None]
        kr = _na_block_start(blk, grid_rows) + np.arange(NA_KEY_ROWS)[None, :]
        rs = np.clip(qr - NA_ROWS // 2, 0, grid_rows - NA_ROWS)
        row_ok = (kr >= rs) & (kr < rs + NA_ROWS)
        masks.append(np.repeat(np.where(row_ok, 0.0, NEG_BIG), GRID_W, axis=1))
    return jnp.asarray(np.stack(masks).astype(np.float32))


def _na_latent(nq, nk, nv2, tab, masks, n_rows_out, n_batch, seq, n_ctx):
    chunks = nq.shape[1] // LANES
    tq = NA_BLOCK_ROWS * GRID_W
    nblk = seq // tq
    ctx_blk0 = n_batch * seq // n_ctx
    qmap = lambda b, j, i: (b * nblk + i, j)
    variant = lambda i: jnp.where(i == 0, 0, jnp.where(i == nblk - 1, 2, 1))
    tab_spec = pl.BlockSpec((1, 2) + tab.shape[2:], lambda b, j, i: (variant(i), j, 0, 0, 0))
    mask_spec = pl.BlockSpec((1,) + masks.shape[1:], lambda b, j, i: (variant(i), 0, 0))
    return pl.pallas_call(
        functools.partial(_na_kernel, grid_rows=seq // GRID_W),
        grid=(n_batch, chunks, nblk),
        in_specs=[pl.BlockSpec((tq, LANES), qmap),
                  pl.BlockSpec((n_ctx, LANES), lambda b, j, i: (ctx_blk0 + b, j)),
                  pl.BlockSpec((n_ctx, 2 * LANES), lambda b, j, i: (ctx_blk0 + b, j)),
                  pl.BlockSpec((seq, LANES), lambda b, j, i: (b, j)),
                  pl.BlockSpec((seq, 2 * LANES), lambda b, j, i: (b, j)),
                  tab_spec, mask_spec],
        out_specs=pl.BlockSpec((tq, LANES), qmap),
        out_shape=jax.ShapeDtypeStruct((n_rows_out, nq.shape[1]), BF16),
        compiler_params=_cparams(("arbitrary", "arbitrary", "arbitrary")),
        name="na_latent",
    )(nq, nk, nv2, nk, nv2, tab, masks)


def _hfilt_kernel(ft_ref, tc_ref, w1t_ref, b1_ref, w2t_ref, b2_ref, w3_ref, fr_ref, d_ref, k_ref, l1_ref, *, n):
    i = pl.program_id(0)
    fr = fr_ref[...]
    h1 = jnp.sin(fr * (jnp.dot(w1t_ref[...], ft_ref[...], precision=HIGHEST, preferred_element_type=F32)
                       + b1_ref[...]))
    h2 = jnp.sin(fr * (jnp.dot(w2t_ref[...], h1, precision=HIGHEST, preferred_element_type=F32) + b2_ref[...]))
    contract0 = (((0,), (0,)), ((), ()))
    o = lax.dot_general(h2, w3_ref[...], contract0, precision=HIGHEST, preferred_element_type=F32)
    cw = o.shape[1] // 2
    t = o.shape[0]
    row = i * t + lax.broadcasted_iota(jnp.int32, (t, cw), 0)
    k = jnp.where(row < n, o[:, :cw], o[:, cw:]) * jnp.exp(-tc_ref[...] * d_ref[...])
    k = jnp.where(row == n, 0.0, k)
    k_ref[...] = k

    @pl.when(i == 0)
    def _():
        l1_ref[...] = jnp.zeros_like(l1_ref)

    l1_ref[...] += jnp.sum(jnp.abs(k), axis=0, keepdims=True)


def _filter_feats(n):
    rows = np.arange(2 * n)
    pos = np.where(rows <= n, rows, 2 * n - rows).astype(np.float32)
    pos = jnp.asarray(np.minimum(pos, n - 1))
    t = pos / max(n - 1, 1)
    bands = jnp.linspace(1e-4, FILTER_BANDS - 1, FILTER_BANDS, dtype=F32)
    ang = (2.0 * math.pi / n) * pos[None, :] * bands[:, None]
    feats = jnp.concatenate([t[None, :], jnp.cos(ang), -jnp.sin(ang)], axis=0)
    return jnp.pad(feats, ((0, LANES - feats.shape[0]), (0, 0))), t[:, None]


def _hfilter(n, w1, b1, w2, b2, w3, freq, deltas):
    feats_t, t_col = _filter_feats(n)
    w1t = jnp.pad(w1.T, ((0, 0), (0, LANES - w1.shape[0])))
    col = lambda a: a.reshape(-1, 1)
    rows = feats_t.shape[1]
    t = min(1024, rows)
    cw = w3.shape[1] // 2
    const = lambda i: (0, 0)
    full = lambda a: pl.BlockSpec(a.shape, const)
    args = (w1t, col(b1), w2.T, col(b2), w3, col(freq), jnp.abs(deltas))
    return pl.pallas_call(
        functools.partial(_hfilt_kernel, n=n),
        grid=(rows // t,),
        in_specs=[pl.BlockSpec((LANES, t), lambda i: (0, i)), pl.BlockSpec((t, 1), lambda i: (i, 0))]
        + [full(a) for a in args],
        out_specs=[pl.BlockSpec((t, cw), lambda i: (i, 0)), pl.BlockSpec((1, cw), const)],
        out_shape=[jax.ShapeDtypeStruct((rows, cw), F32), jax.ShapeDtypeStruct((1, cw), F32)],
        compiler_params=_cparams(("arbitrary",)),
        name="hyena_filter",
    )(feats_t, t_col, *args)


def _dft_outer_kernel(f_ref, x_ref, o_ref):
    c = x_ref.shape[-1]
    x = x_ref[...].reshape(f_ref.shape[1], c).astype(BF16)
    o_ref[...] = jnp.dot(f_ref[...], x, preferred_element_type=F32).reshape(o_ref.shape)


def _kron_table(fmat):
    return jnp.asarray(np.kron(np.asarray(fmat, np.float32), np.eye(OUTER_COLS, dtype=np.float32)).astype(BF16))


def _dft_outer_fwd(fmat, x3, nb, n1_len):
    k = fmat.shape[1]
    c = x3.shape[2]
    f = _kron_table(fmat)
    return pl.pallas_call(
        _dft_outer_kernel,
        grid=(nb, DFT_N2 // OUTER_COLS),
        in_specs=[pl.BlockSpec(f.shape, lambda b, j: (0, 0), pipeline_mode=pl.Buffered(1)),
                  pl.BlockSpec((k, OUTER_COLS, c), lambda b, j: (b, j, 0))],
        out_specs=pl.BlockSpec((1, 2, n1_len, OUTER_COLS, c), lambda b, j: (b, 0, 0, j, 0)),
        out_shape=jax.ShapeDtypeStruct((nb, 2, n1_len, DFT_N2, c), F32),
        compiler_params=_cparams(("arbitrary", "arbitrary")),
        name="dft_outer_fwd",
    )(f, x3)


def _dft_outer_inv(fmat, d5):
    nb, _, n1_len, _, c = d5.shape
    half = fmat.shape[0]
    f = _kron_table(fmat)
    return pl.pallas_call(
        _dft_outer_kernel,
        grid=(nb, DFT_N2 // OUTER_COLS),
        in_specs=[pl.BlockSpec(f.shape, lambda b, j: (0, 0), pipeline_mode=pl.Buffered(1)),
                  pl.BlockSpec((1, 2, n1_len, OUTER_COLS, c), lambda b, j: (b, 0, 0, j, 0))],
        out_specs=pl.BlockSpec((half, OUTER_COLS, c), lambda b, j: (b, j, 0)),
        out_shape=jax.ShapeDtypeStruct((nb * half, DFT_N2, c), F32),
        compiler_params=_cparams(("arbitrary", "arbitrary")),
        name="dft_outer_inv",
    )(f, d5)


def _inner_matrices(fc, fs, tc, ts):
    gr = fc * tc - fs * ts
    gi = -(fc * ts + fs * tc)
    return gr, gi


def _real_form(gr, gi):
    return jnp.concatenate([jnp.concatenate([gr, -gi], axis=1), jnp.concatenate([gi, gr], axis=1)], axis=0)


def _dft_inner_filter_kernel(fc_ref, fs_ref, tc_ref, ts_ref, b_ref, o_ref):
    fc, fs = fc_ref[...], fs_ref[...]
    n2, c = b_ref.shape[3], b_ref.shape[4]

    def body(kk, carry):
        g = _real_form(*_inner_matrices(fc, fs, tc_ref[kk], ts_ref[kk]))
        o_ref[kk] = jnp.dot(g.astype(BF16), b_ref[0, :, kk].reshape(2 * n2, c).astype(BF16),
                            preferred_element_type=F32)
        return carry

    lax.fori_loop(0, b_ref.shape[2], body, 0, unroll=2)


def _dft_inner_conv_kernel(fc_ref, fs_ref, tc_ref, ts_ref, b_ref, kh_ref, sc_ref, o_ref):
    fc, fs = fc_ref[...], fs_ref[...]
    nb, n2, c = b_ref.shape[0], b_ref.shape[3], b_ref.shape[4]
    scale = sc_ref[...]

    def body(kk, carry):
        gr, gi = _inner_matrices(fc, fs, tc_ref[kk], ts_ref[kk])
        g = _real_form(gr, gi).astype(BF16)
        gh = _real_form(gr.T, -gi.T).astype(BF16)
        kh = kh_ref[kk]
        kr, ki = kh[:n2], kh[n2:]
        for b in range(nb):
            x = jnp.dot(g, b_ref[b, :, kk].reshape(2 * n2, c).astype(BF16), preferred_element_type=F32)
            xr, xi = x[:n2], x[n2:]
            y = jnp.concatenate([(xr * kr - xi * ki) * scale, (xr * ki + xi * kr) * scale], axis=0)
            o_ref[b, :, kk] = jnp.dot(gh, y.astype(BF16), preferred_element_type=F32).reshape(2, n2, c)
        return carry

    lax.fori_loop(0, b_ref.shape[2], body, 0, unroll=2)


def _dft_tables(n1_len):
    n_total = n1_len * DFT_N2
    a = np.arange(DFT_N2)
    ang = 2.0 * np.pi * np.outer(a, a) / DFT_N2
    k1 = np.arange(n1_len)
    tw = 2.0 * np.pi * np.outer(k1, a) / n_total
    o = 2.0 * np.pi * np.outer(k1, k1) / n1_len
    f32 = lambda v: jnp.asarray(v.astype(np.float32))
    outer_fwd = np.concatenate([np.cos(o), -np.sin(o)], axis=0)
    outer_inv = np.concatenate([np.cos(o), -np.sin(o)], axis=1)
    return dict(
        fc=f32(np.cos(ang)), fs=f32(np.sin(ang)),
        tc=f32(np.cos(tw)).reshape(n1_len, 1, DFT_N2), ts=f32(np.sin(tw)).reshape(n1_len, 1, DFT_N2),
        outer_fwd_full=outer_fwd, outer_fwd_half=outer_fwd[:, :n1_len // 2], outer_inv_half=outer_inv[:n1_len // 2],
    )


def _inner_specs():
    sq = pl.BlockSpec((DFT_N2, DFT_N2), lambda k: (0, 0))
    tw = pl.BlockSpec((INNER_K1, 1, DFT_N2), lambda k: (k, 0, 0))
    return [sq, sq, tw, tw]


def _filter_spectrum(kfull, tabs, n1_len):
    c = kfull.shape[1]
    b5 = _dft_outer_fwd(tabs["outer_fwd_full"], kfull.reshape(n1_len, DFT_N2, c), 1, n1_len)
    return pl.pallas_call(
        _dft_inner_filter_kernel,
        grid=(n1_len // INNER_K1,),
        in_specs=_inner_specs() + [pl.BlockSpec((1, 2, INNER_K1, DFT_N2, c), lambda k: (0, 0, k, 0, 0))],
        out_specs=pl.BlockSpec((INNER_K1, 2 * DFT_N2, c), lambda k: (k, 0, 0)),
        out_shape=jax.ShapeDtypeStruct((n1_len, 2 * DFT_N2, c), F32),
        compiler_params=_cparams(("arbitrary",)),
        name="dft_inner_filter",
    )(tabs["fc"], tabs["fs"], tabs["tc"], tabs["ts"], b5)


def _long_conv(z_all, nb, khat, scale, tabs, n1_len):
    c = z_all.shape[1]
    bz = _dft_outer_fwd(tabs["outer_fwd_half"], z_all.reshape(z_all.shape[0] // DFT_N2, DFT_N2, c), nb, n1_len)
    blk = pl.BlockSpec((nb, 2, INNER_K1, DFT_N2, c), lambda k: (0, 0, k, 0, 0))
    d = pl.pallas_call(
        _dft_inner_conv_kernel,
        grid=(n1_len // INNER_K1,),
        in_specs=_inner_specs() + [blk, pl.BlockSpec((INNER_K1, 2 * DFT_N2, c), lambda k: (k, 0, 0)),
                                   pl.BlockSpec((1, c), lambda k: (0, 0))],
        out_specs=blk,
        out_shape=jax.ShapeDtypeStruct(bz.shape, F32),
        compiler_params=_cparams(("arbitrary",)),
        name="dft_inner_conv",
    )(tabs["fc"], tabs["fs"], tabs["tc"], tabs["ts"], bz, khat, scale)
    y3 = _dft_outer_inv(tabs["outer_inv_half"], d)
    return y3.reshape(y3.shape[0] * DFT_N2, c)


def _hyena_ctx_kernel(ff_ref, fi_ref, k_ref, l1_ref, z_ref, o_ref):
    ff = ff_ref[...]
    n = z_ref.shape[0]
    big = ff.shape[1]
    kh = jnp.dot(ff, k_ref[...], precision=HIGHEST, preferred_element_type=F32)
    x = jnp.dot(ff[:, :n], z_ref[...], precision=HIGHEST, preferred_element_type=F32)
    kr, ki = kh[:big], kh[big:]
    xr, xi = x[:big], x[big:]
    scale = 1.0 / (l1_ref[...] * big)
    y = jnp.concatenate([(xr * kr - xi * ki) * scale, (xr * ki + xi * kr) * scale], axis=0)
    o_ref[...] = jnp.dot(fi_ref[...], y, precision=HIGHEST, preferred_element_type=F32)


def _hyena_ctx(z, kfull, l1, row0_blk, n_batch, n):
    big = 2 * n
    a = np.arange(big)
    ang = 2.0 * np.pi * np.outer(a, a) / big
    ff = jnp.asarray(np.concatenate([np.cos(ang), -np.sin(ang)], axis=0).astype(np.float32))
    fi = jnp.asarray(np.concatenate([np.cos(ang), -np.sin(ang)], axis=1)[:n].astype(np.float32))
    c = z.shape[1]
    const = lambda b: (0, 0)
    return pl.pallas_call(
        _hyena_ctx_kernel,
        grid=(n_batch,),
        in_specs=[pl.BlockSpec(ff.shape, const), pl.BlockSpec(fi.shape, const), pl.BlockSpec(kfull.shape, const),
                  pl.BlockSpec((1, c), const), pl.BlockSpec((n, c), lambda b: (row0_blk + b, 0))],
        out_specs=pl.BlockSpec((n, c), lambda b: (b, 0)),
        out_shape=jax.ShapeDtypeStruct((n_batch * n, c), F32),
        compiler_params=_cparams(("arbitrary",)),
        name="hyena_ctx",
    )(ff, fi, kfull, l1, z)


def _rope_tables(seq, n_batch, n_ctx_rows):
    t = jnp.arange(seq, dtype=jnp.int32)
    rows = (t // GRID_W).astype(F32)
    cols = (t % GRID_W).astype(F32)
    nf = HEAD_DIM // 4
    inv = ROPE_THETA ** (-jnp.arange(nf, dtype=F32) / nf)
    ar, ac = rows[:, None] * inv, cols[:, None] * inv
    cos = jnp.concatenate([jnp.cos(ar), jnp.cos(ar), jnp.cos(ac), jnp.cos(ac)], axis=1)
    sin = jnp.concatenate([-jnp.sin(ar), jnp.sin(ar), -jnp.sin(ac), jnp.sin(ac)], axis=1)
    cos = jnp.tile(cos, (n_batch, LANES // HEAD_DIM))
    sin = jnp.tile(sin, (n_batch, LANES // HEAD_DIM))
    cos = jnp.concatenate([cos, jnp.ones((n_ctx_rows, LANES), F32)], axis=0)
    sin = jnp.concatenate([sin, jnp.zeros((n_ctx_rows, LANES), F32)], axis=0)
    return cos, sin


def _head_mean_matrix():
    a = np.arange(LANES)
    return jnp.asarray((a[:, None] // HEAD_DIM == a[None, :] // HEAD_DIM).astype(np.float32) / HEAD_DIM, dtype=BF16)


def kernel(x, c, ctx, c_ctx, w_ada, b_ada, g_ffn1, w_ffn1_gate, w_ffn1_up, w_ffn1_down, g_mix, w_in, w_out, g_q_attn, g_k_attn, conv_w, conv_b, filt_w1, filt_b1, filt_w2, filt_b2, filt_w3, filt_freq, hyena_skip, g_q_na, g_k_na, na_rpb, g_ffn2, w_ffn2_gate, w_ffn2_up, w_ffn2_down):
    n_batch, seq, d = x.shape
    n_ctx = ctx.shape[1]
    depth = w_ada.shape[0]
    assert d == _D and n_batch + 1 <= 8 and seq % TOKEN_TILE == 0 and seq % (NA_BLOCK_ROWS * GRID_W) == 0
    assert n_ctx == 256 and (n_batch * n_ctx) % TOKEN_TILE == 0
    n_lat = n_batch * seq
    n_all = n_lat + n_batch * n_ctx
    tiles_per_batch = seq // TOKEN_TILE
    hy_w = conv_w.shape[2] // 3
    n1_len = 2 * seq // DFT_N2

    cond8 = jnp.zeros((8, d), F32).at[:n_batch].set(c).at[n_batch].set(c_ctx)
    mods = _adaln_all(cond8, w_ada, b_ada).reshape(depth * 8 * N_MOD, 1, d)
    h = jnp.concatenate([x.reshape(n_lat, d), ctx.reshape(n_batch * n_ctx, d)], axis=0)

    cos_t, sin_t = _rope_tables(seq, n_batch, n_batch * n_ctx)
    headmean = _head_mean_matrix()
    na_masks = _na_row_masks(seq // GRID_W)
    tabs = _dft_tables(n1_len)
    deltas = jnp.linspace(math.log(DECAY_TARGET) / SLOW_DECAY_PCT, math.log(DECAY_TARGET) / FAST_DECAY_PCT,
                          hy_w, dtype=F32).reshape(1, hy_w)

    q_heads = [hd for pair in range(ATTN_HEADS // 2) for hd in (pair, pair + ATTN_HEADS // 2)]
    q_segs = [(hd * HEAD_DIM, (hd + 1) * HEAD_DIM) for hd in q_heads]
    in_segs = q_segs + [(512, 640), (1536, 1792), (1792, 2048), (640, 768), (2048, 2304), (768, 1536)]

    for l in range(depth):
        last = l == depth - 1
        base = l * 8 * N_MOD
        rep = lambda g, k: jnp.tile(g, k)
        gvec = jnp.concatenate([rep(g_q_attn[l], ATTN_HEADS), rep(g_k_attn[l], ATTN_KV_HEADS),
                                rep(g_q_na[l], NA_HEADS), rep(g_k_na[l], NA_HEADS)]).reshape(1, N_NORM)

        h = _ffn(h, mods, base, (0, 1, 2), g_ffn1[l].reshape(1, d), w_ffn1_gate[l].astype(BF16),
                 w_ffn1_up[l].astype(BF16), w_ffn1_down[l].astype(BF16), n_all, tiles_per_batch, n_batch)

        w_in_l = jnp.concatenate([w_in[l][:, a:b] for a, b in in_segs], axis=1).astype(BF16)
        q, k, nq, nk, v, nv, z, x0 = _inproj(h, mods, base, g_mix[l].reshape(1, d), w_in_l, gvec, headmean, cos_t, sin_t,
                                             conv_w[l], conv_b[l].reshape(1, -1), tiles_per_batch, n_batch, seq, n_ctx)

        out_rows = n_lat if last else n_all
        ya = _gqa(q, k, v, out_rows, n_batch, seq, n_ctx, latent=True)
        yc = _na_latent(nq, nk, nv, _na_bias_tables(na_rpb[l], seq // GRID_W), na_masks, out_rows, n_batch, seq, n_ctx)

        fargs = (filt_w1[l], filt_b1[l].reshape(1, -1), filt_w2[l], filt_b2[l].reshape(1, -1), filt_w3[l],
                 filt_freq[l].reshape(1, -1), deltas)
        kfull, l1 = _hfilter(seq, *fargs)
        khat = _filter_spectrum(kfull, tabs, n1_len)
        yconv = _long_conv(z, n_batch, khat, 1.0 / (l1 * (2 * seq)), tabs, n1_len)

        if not last:
            ya_c = _gqa(q, k, v, n_all, n_batch, seq, n_ctx, latent=False)
            yc_c = _gqa(nq, nk, nv, n_all, n_batch, seq, n_ctx, latent=False, kv_chunked=True)
            ya = lax.dynamic_update_slice(ya, ya_c[n_lat:], (n_lat, 0))
            yc = lax.dynamic_update_slice(yc, yc_c[n_lat:], (n_lat, 0))
            kfull_c, l1_c = _hfilter(n_ctx, *fargs)
            yconv_c = _hyena_ctx(z, kfull_c, l1_c, n_lat // n_ctx, n_batch, n_ctx)
            yconv = jnp.concatenate([yconv, yconv_c], axis=0)

        w_o = w_out[l]
        wa = jnp.concatenate([w_o[a:b] for a, b in q_segs], axis=0).astype(BF16)
        wb = w_o[512:512 + hy_w].astype(BF16)
        wc = w_o[512 + hy_w:].astype(BF16)
        h = _ffn(h, mods, base, (6, 7, 8), g_ffn2[l].reshape(1, d), w_ffn2_gate[l].astype(BF16),
                 w_ffn2_up[l].astype(BF16), w_ffn2_down[l].astype(BF16), out_rows, tiles_per_batch, n_batch,
                 mixer=(ya, yconv, z, x0, hyena_skip[l].reshape(1, hy_w), yc, wa, wb, wc))

    return h[:n_lat].reshape(n_batch, seq, d)
```

```python
import functools
import math

import numpy as np
import jax
import jax.numpy as jnp
from jax import lax
from jax.experimental import pallas as pl
from jax.experimental.pallas import tpu as pltpu

F32 = jnp.float32
BF16 = jnp.bfloat16
HIGHEST = lax.Precision.HIGHEST

HEAD_DIM = 64
GRID_W = 64
ATTN_HEADS = 8
ATTN_KV_HEADS = 2
NA_HEADS = 4
NA_ROWS = 8
NA_COLS = 16
FILTER_BANDS = 16
DECAY_TARGET = 1e-2
FAST_DECAY_PCT = 0.3
SLOW_DECAY_PCT = 1.5
ROPE_THETA = 10000.0
EPS = 1e-6
N_MOD = 9

LANES = 128
TOKEN_TILE = 512
FFN_CHUNK = 256
ATTN_TQ = 1024
ATTN_TK = 2048
NA_BLOCK_ROWS = 8
NA_KEY_ROWS = 16
NA_SUB_ROWS = 4
DFT_N2 = 128
OUTER_COLS = 8
INNER_K1 = 8
NEG_BIG = -1e30
VMEM_LIMIT = 56 * 1024 * 1024


def _cparams(sem):
    return pltpu.CompilerParams(dimension_semantics=sem, vmem_limit_bytes=VMEM_LIMIT)


def _ada_kernel(c_ref, w_ref, b_ref, o_ref):
    c = c_ref[...]
    s = c * jax.nn.sigmoid(c)
    o_ref[0] = jnp.dot(s, w_ref[0], preferred_element_type=F32) + b_ref[0]


def _adaln_all(cond8, w_ada, b_ada):
    depth, d, nd = w_ada.shape
    tn = 2304
    return pl.pallas_call(
        _ada_kernel,
        grid=(depth, nd // tn),
        in_specs=[
            pl.BlockSpec((8, d), lambda l, j: (0, 0)),
            pl.BlockSpec((1, d, tn), lambda l, j: (l, 0, j)),
            pl.BlockSpec((1, 1, tn), lambda l, j: (l, 0, j)),
        ],
        out_specs=pl.BlockSpec((1, 8, tn), lambda l, j: (l, 0, j)),
        out_shape=jax.ShapeDtypeStruct((depth, 8, nd), F32),
        compiler_params=_cparams(("arbitrary", "arbitrary")),
        name="adaln",
    )(cond8, w_ada, b_ada.reshape(depth, 1, nd))


def _norm_mod(x, g, sh, sc):
    ms = jnp.mean(x * x, axis=-1, keepdims=True)
    y = x * lax.rsqrt(ms + EPS) * g
    return y * (1.0 + sc) + sh


def _mod_specs(tiles_per_batch, n_batch, base, idxs):
    def spec(which):
        return pl.BlockSpec(
            (1, 1, _D),
            lambda i, w=which: (base + jnp.minimum(i // tiles_per_batch, n_batch) * N_MOD + w, 0, 0))
    return [spec(w) for w in idxs]


_D = 1024


def _ffn_kernel(h_ref, *refs, with_mixer):
    x = h_ref[...]
    if with_mixer:
        gt2_ref, ya_ref, yconv_ref, z_ref, x0_ref, skip_ref, yc_ref, wa_ref, wb_ref, wc_ref = refs[:10]
        refs = refs[10:]
        yb = ((yconv_ref[...] + z_ref[...] * skip_ref[...]) * x0_ref[...]).astype(BF16)
        x = x + gt2_ref[0] * (jnp.dot(ya_ref[...], wa_ref[...], preferred_element_type=F32)
                              + jnp.dot(yb, wb_ref[...], preferred_element_type=F32)
                              + jnp.dot(yc_ref[...], wc_ref[...], preferred_element_type=F32))
    sh_ref, sc_ref, gt_ref, g_ref, wg_ref, wu_ref, wd_ref, o_ref, xb_ref, acc_ref = refs
    if with_mixer:
        o_ref[...] = x
    xb_ref[...] = _norm_mod(x, g_ref[...], sh_ref[0], sc_ref[0]).astype(BF16)

    for k in range(wg_ref.shape[1] // FFN_CHUNK):
        cols = slice(k * FFN_CHUNK, (k + 1) * FFN_CHUNK)
        xb = xb_ref[...]
        a = jnp.dot(xb, wg_ref[:, cols], preferred_element_type=F32)
        u = jnp.dot(xb, wu_ref[:, cols], preferred_element_type=F32)
        act = (a * jax.nn.sigmoid(a) * u).astype(BF16)
        part = jnp.dot(act, wd_ref[cols, :], preferred_element_type=F32)
        if k == 0:
            acc_ref[...] = part
        else:
            acc_ref[...] += part
    resid = o_ref[...] if with_mixer else h_ref[...]
    o_ref[...] = resid + 0.5 * gt_ref[0] * acc_ref[...]


def _ffn(h, mods, mod_base, which, g, wg, wu, wd, n_rows, tiles_per_batch, n_batch, mixer=None):
    d = h.shape[1]
    tm = TOKEN_TILE
    row = lambda i: (i, 0)
    const = lambda i: (0, 0)
    rows = lambda a: pl.BlockSpec((tm, a.shape[1]), row)
    full = lambda a: pl.BlockSpec(a.shape, const)
    once = lambda a: pl.BlockSpec(a.shape, const, pipeline_mode=pl.Buffered(1))
    mix_specs, mix_args = [], ()
    if mixer is not None:
        ya, yconv, z, x0, skip, yc, wa, wb, wc = mixer
        mix_specs = _mod_specs(tiles_per_batch, n_batch, mod_base, (5,)) + [
            rows(ya), rows(yconv), rows(z), rows(x0), full(skip), rows(yc), once(wa), once(wb), once(wc)]
        mix_args = (mods,) + tuple(mixer)
    return pl.pallas_call(
        functools.partial(_ffn_kernel, with_mixer=mixer is not None),
        grid=(n_rows // tm,),
        in_specs=[rows(h)] + mix_specs + _mod_specs(tiles_per_batch, n_batch, mod_base, which)
        + [full(g), once(wg), once(wu), once(wd)],
        out_specs=pl.BlockSpec((tm, d), row),
        out_shape=jax.ShapeDtypeStruct((n_rows, d), F32),
        scratch_shapes=[pltpu.VMEM((tm, d), BF16), pltpu.VMEM((tm, d), F32)],
        compiler_params=_cparams(("arbitrary",)),
        name="ffn_mix" if mixer is not None else "ffn",
    )(h, *mix_args, mods, mods, mods, g, wg, wu, wd)


N_NORM = 1152
N_ROPE = 640
ATTN_Q_CHUNKS = (0, 1, 2, 3)
NA_Q_CHUNKS = (5, 6)
LOG2E = math.log2(math.e)


def _inproj_kernel(h_ref, prev_ref, next_ref, sh_ref, sc_ref, g_ref, w_ref, gv_ref, hm_ref, cos_ref, sin_ref,
                   cw_ref, cb_ref, q_ref, k_ref, nq_ref, nk_ref, v_ref, nv_ref, z_ref, x0_ref, *, seq, n_ctx, n_lat):
    xb = _norm_mod(h_ref[...], g_ref[...], sh_ref[0], sc_ref[0]).astype(BF16)
    hm = hm_ref[...]
    cosv = cos_ref[...]
    sinv = sin_ref[...]
    lane = lax.broadcasted_iota(jnp.int32, cosv.shape, 1)
    first_half = (lane % 32) < 16
    outs = []
    p_norm = jnp.dot(xb, w_ref[:, 0:N_NORM], preferred_element_type=F32)
    for c in range(N_NORM // LANES):
        p = p_norm[:, c * LANES:(c + 1) * LANES]
        ms = jnp.dot((p * p).astype(BF16), hm, preferred_element_type=F32)
        y = p * lax.rsqrt(ms + EPS) * gv_ref[:, c * LANES:(c + 1) * LANES]
        if c < N_ROPE // LANES:
            swapped = jnp.where(first_half, pltpu.roll(y, LANES - 16, 1), pltpu.roll(y, 16, 1))
            y = y * cosv + swapped * sinv
        if c in ATTN_Q_CHUNKS or c in NA_Q_CHUNKS:
            y = y * (HEAD_DIM ** -0.5 * LOG2E)
        outs.append(y.astype(BF16))
    for c in range(4):
        q_ref[:, c * LANES:(c + 1) * LANES] = outs[c]
    k_ref[...] = outs[4]
    nq_ref[:, 0:LANES] = outs[5]
    nq_ref[:, LANES:2 * LANES] = outs[6]
    nk_ref[:, 0:LANES] = outs[7]
    nk_ref[:, LANES:2 * LANES] = outs[8]
    p_rest = jnp.dot(xb, w_ref[:, N_NORM:], preferred_element_type=F32)
    v = p_rest[:, 0:LANES]
    v_ref[:, 0:LANES] = jnp.where(lane < HEAD_DIM, v, 1.0).astype(BF16)
    v_ref[:, LANES:2 * LANES] = jnp.where(lane < HEAD_DIM, 1.0, v).astype(BF16)
    for c in range(NA_HEADS // 2):
        nv = p_rest[:, (1 + c) * LANES:(2 + c) * LANES]
        nv_ref[:, 2 * c * LANES:(2 * c + 1) * LANES] = jnp.where(lane < HEAD_DIM, nv, 1.0).astype(BF16)
        nv_ref[:, (2 * c + 1) * LANES:(2 * c + 2) * LANES] = jnp.where(lane < HEAD_DIM, 1.0, nv).astype(BF16)

    hy = p_rest[:, 3 * LANES:]
    tm = hy.shape[0]
    halo = jnp.concatenate([prev_ref[...], next_ref[...]], axis=0)
    halo_b = _norm_mod(halo, g_ref[...], sh_ref[0], sc_ref[0]).astype(BF16)
    hy_halo = jnp.dot(halo_b, w_ref[:, N_NORM + 3 * LANES:], preferred_element_type=F32)
    row0 = pl.program_id(0) * tm
    is_ctx = row0 >= n_lat
    period = jnp.where(is_ctx, n_ctx, seq)
    local = lax.broadcasted_iota(jnp.int32, (tm, 1), 0)
    pos = jnp.bitwise_and(row0 + local - jnp.where(is_ctx, n_lat, 0), period - 1)
    up = jnp.where(local == 0, hy_halo[7:8, :], pltpu.roll(hy, 1, 0))
    up = jnp.where(pos == 0, 0.0, up)
    dn = jnp.where(local == tm - 1, hy_halo[8:9, :], pltpu.roll(hy, tm - 1, 0))
    dn = jnp.where(pos == period - 1, 0.0, dn)
    uc = up * cw_ref[0:1, :] + hy * cw_ref[1:2, :] + dn * cw_ref[2:3, :] + cb_ref[...]
    cw = uc.shape[1] // 3
    x0_ref[...] = uc[:, :cw]
    z_ref[...] = uc[:, 2 * cw:] * uc[:, cw:2 * cw]


def _inproj(h, mods, mod_base, g, w, gvec, headmean, cos_t, sin_t, conv_w, conv_b, tiles_per_batch, n_batch, seq,
            n_ctx):
    n_rows, d = h.shape
    tm = TOKEN_TILE
    assert seq & (seq - 1) == 0 and n_ctx & (n_ctx - 1) == 0
    row = lambda i: (i, 0)
    const = lambda i: (0, 0)
    widths = (512, 128, 256, 256, 256, 512)
    hy_w = conv_w.shape[1] // 3
    out_shape = [jax.ShapeDtypeStruct((n_rows, wd), BF16) for wd in widths]
    out_shape += [jax.ShapeDtypeStruct((n_rows, hy_w), F32)] * 2
    out_specs = [pl.BlockSpec((tm, wd), row) for wd in widths] + [pl.BlockSpec((tm, hy_w), row)] * 2
    return pl.pallas_call(
        functools.partial(_inproj_kernel, seq=seq, n_ctx=n_ctx, n_lat=n_batch * seq),
        grid=(n_rows // tm,),
        in_specs=[pl.BlockSpec((tm, d), row),
                  pl.BlockSpec((8, d), lambda i: (jnp.maximum(i * (tm // 8) - 1, 0), 0)),
                  pl.BlockSpec((8, d), lambda i: (jnp.minimum((i + 1) * (tm // 8), n_rows // 8 - 1), 0))]
        + _mod_specs(tiles_per_batch, n_batch, mod_base, (3, 4))
        + [
            pl.BlockSpec((1, d), const),
            pl.BlockSpec(w.shape, const, pipeline_mode=pl.Buffered(1)),
            pl.BlockSpec(gvec.shape, const),
            pl.BlockSpec(headmean.shape, const),
            pl.BlockSpec((tm, LANES), row),
            pl.BlockSpec((tm, LANES), row),
            pl.BlockSpec(conv_w.shape, const),
            pl.BlockSpec(conv_b.shape, const),
        ],
        out_specs=out_specs,
        out_shape=out_shape,
        compiler_params=_cparams(("arbitrary",)),
        name="inproj",
    )(h, h, h, mods, mods, g, w, gvec, headmean, cos_t, sin_t, conv_w, conv_b)


def _stack_heads(q):
    lane = lax.broadcasted_iota(jnp.int32, q.shape, 1)
    zero = jnp.zeros_like(q)
    return jnp.concatenate([jnp.where(lane < HEAD_DIM, q, zero), jnp.where(lane >= HEAD_DIM, q, zero)], axis=0)


def _qk(q2, k):
    return lax.dot_general(q2, k, (((1,), (1,)), ((), ())), preferred_element_type=F32)


def _gqa_kernel(q_ref, kc_ref, vc_ref, *rest, n_latent_chunks, tk):
    if n_latent_chunks:
        kl_ref, vl_ref, o_ref = rest
    else:
        (o_ref,) = rest
    tq = q_ref.shape[0]
    q2 = _stack_heads(q_ref[...])

    def pv(p, v_ref, rows):
        return jnp.concatenate(
            [jnp.dot(p[:tq], v_ref[rows, 0:LANES], preferred_element_type=F32),
             jnp.dot(p[tq:], v_ref[rows, LANES:2 * LANES], preferred_element_type=F32)], axis=0)

    s = _qk(q2, kc_ref[...])
    m = jnp.max(s, axis=-1, keepdims=True)
    acc = pv(jnp.exp2((s - m).astype(BF16)), vc_ref, slice(None))

    if n_latent_chunks:
        def body(j, carry):
            m, acc = carry
            rows = pl.ds(pl.multiple_of(j * tk, tk), tk)
            s = _qk(q2, kl_ref[rows, :])
            m_new = jnp.maximum(m, jnp.max(s, axis=-1, keepdims=True))
            p = jnp.exp2((s - m_new).astype(BF16))
            return m_new, jnp.exp2(m - m_new) * acc + pv(p, vl_ref, rows)

        m, acc = lax.fori_loop(0, n_latent_chunks, body, (m, acc), unroll=True)
    lane = lax.broadcasted_iota(jnp.int32, (tq, LANES), 1)
    num = jnp.where(lane < HEAD_DIM, acc[:tq], acc[tq:])
    den = pltpu.roll(jnp.where(lane < HEAD_DIM, acc[tq:], acc[:tq]), HEAD_DIM, 1)
    o_ref[...] = (num / den).astype(o_ref.dtype)


def _gqa(q, k, v2, out_rows, n_batch, seq, n_ctx, latent, kv_chunked=False):
    chunks = q.shape[1] // LANES
    ctx_blk0 = n_batch * seq // n_ctx
    tk = min(ATTN_TK, seq)
    kj = (lambda j: j) if kv_chunked else (lambda j: 0)
    if latent:
        tq = ATTN_TQ
        grid = (n_batch, chunks, seq // tq)
        qmap = lambda b, j, i: (b * (seq // tq) + i, j)
        kv_specs = [pl.BlockSpec((seq, LANES), lambda b, j, i: (b, kj(j))),
                    pl.BlockSpec((seq, 2 * LANES), lambda b, j, i: (b, kj(j)))]
        kv_args = (k, v2)
    else:
        tq = n_ctx
        grid = (n_batch, chunks, 1)
        qmap = lambda b, j, i: (ctx_blk0 + b, j)
        kv_specs, kv_args = [], ()
    return pl.pallas_call(
        functools.partial(_gqa_kernel, n_latent_chunks=seq // tk if latent else 0, tk=tk),
        grid=grid,
        in_specs=[pl.BlockSpec((tq, LANES), qmap),
                  pl.BlockSpec((n_ctx, LANES), lambda b, j, i: (ctx_blk0 + b, kj(j))),
                  pl.BlockSpec((n_ctx, 2 * LANES), lambda b, j, i: (ctx_blk0 + b, kj(j)))] + kv_specs,
        out_specs=pl.BlockSpec((tq, LANES), qmap),
        out_shape=jax.ShapeDtypeStruct((out_rows, q.shape[1]), BF16),
        compiler_params=_cparams(("arbitrary", "arbitrary", "arbitrary")),
        name="attn_latent" if latent else "attn_ctx",
    )(q, k, v2, *kv_args)


def _na_kernel(q_ref, kc_ref, vc_ref, kl_ref, vl_ref, tab_ref, mask_ref, o_ref, *, grid_rows):
    i = pl.program_id(2)
    nk = NA_KEY_ROWS * GRID_W
    ks = jnp.clip(i * NA_BLOCK_ROWS - NA_ROWS // 2, 0, grid_rows - NA_KEY_ROWS)
    keys = pl.ds(pl.multiple_of(ks * GRID_W, GRID_W), nk)
    k = kl_ref[keys, :]
    kc = kc_ref[...]
    sub_rows = NA_SUB_ROWS * GRID_W
    lane = lax.broadcasted_iota(jnp.int32, (sub_rows, LANES), 1)
    low = lane < HEAD_DIM
    for sub in range(NA_BLOCK_ROWS // NA_SUB_ROWS):
        rows = slice(sub * sub_rows, (sub + 1) * sub_rows)
        q = q_ref[rows, :]
        acc = []
        for hh in range(2):
            qh = jnp.where(low if hh == 0 else jnp.logical_not(low), q, jnp.zeros_like(q))
            slabs = []
            for qr in range(sub * NA_SUB_ROWS, (sub + 1) * NA_SUB_ROWS):
                par = qr % 2
                lane0 = (NA_BLOCK_ROWS - qr - par) * GRID_W
                slabs.append(tab_ref[0, hh, par, :, lane0:lane0 + nk] + mask_ref[0, qr:qr + 1, :])
            s_nb = _qk(qh, k) + jnp.concatenate(slabs, axis=0)
            s_cx = _qk(qh, kc)
            m = jnp.maximum(jnp.max(s_nb, axis=-1, keepdims=True), jnp.max(s_cx, axis=-1, keepdims=True))
            vcols = slice(hh * LANES, (hh + 1) * LANES)
            acc.append(jnp.dot(jnp.exp2((s_nb - m).astype(BF16)), vl_ref[keys, vcols], preferred_element_type=F32)
                       + jnp.dot(jnp.exp2((s_cx - m).astype(BF16)), vc_ref[:, vcols], preferred_element_type=F32))
        num = jnp.where(low, acc[0], acc[1])
        den = pltpu.roll(jnp.where(low, acc[1], acc[0]), HEAD_DIM, 1)
        o_ref[rows, :] = (num / den).astype(o_ref.dtype)


def _na_block_start(blk, grid_rows):
    return int(np.clip(blk * NA_BLOCK_ROWS - NA_ROWS // 2, 0, grid_rows - NA_KEY_ROWS))


def _na_variant_blocks(grid_rows):
    nblk = grid_rows // NA_BLOCK_ROWS
    return (0, min(1, nblk - 1), nblk - 1)


def _na_bias_tables(rpb, grid_rows):
    n_heads = rpb.shape[0]
    qc = np.arange(GRID_W)[:, None]
    kc = np.arange(GRID_W)[None, :]
    cs = np.clip(qc - NA_COLS // 2, 0, GRID_W - NA_COLS)
    col_ok = (kc >= cs) & (kc < cs + NA_COLS)
    col_hot = ((kc - qc + NA_COLS - 1)[:, :, None] == np.arange(2 * NA_COLS - 1)) & col_ok[:, :, None]
    by_col = jnp.einsum("abe,hde->hadb", jnp.asarray(col_hot, F32), rpb, precision=HIGHEST)
    by_col = jnp.where(jnp.asarray(col_ok)[None, :, None, :], by_col * LOG2E, NEG_BIG)
    n_off = 2 * NA_ROWS - 1
    slots = NA_BLOCK_ROWS + NA_KEY_ROWS
    tabs = []
    for blk in _na_variant_blocks(grid_rows):
        off = _na_block_start(blk, grid_rows) - blk * NA_BLOCK_ROWS + NA_ROWS - 1
        front = NA_BLOCK_ROWS - off
        t = jnp.pad(by_col, ((0, 0), (0, 0), (front, slots - n_off - front), (0, 0)))
        t = t.reshape(n_heads, GRID_W, slots * GRID_W)
        shifted = jnp.pad(t[:, :, GRID_W:], ((0, 0), (0, 0), (0, GRID_W)))
        tabs.append(jnp.stack([t, shifted], axis=1))
    return jnp.stack(tabs)


def _na_row_masks(grid_rows):
    masks = []
    for blk in _na_variant_blocks(grid_rows):
        qr = blk * NA_BLOCK_ROWS + np.arange(NA_BLOCK_ROWS)[:, None]
        kr = _na_block_start(blk, grid_rows) + np.arange(NA_KEY_ROWS)[None, :]
        rs = np.clip(qr - NA_ROWS // 2, 0, grid_rows - NA_ROWS)
        row_ok = (kr >= rs) & (kr < rs + NA_ROWS)
        masks.append(np.repeat(np.where(row_ok, 0.0, NEG_BIG), GRID_W, axis=1))
    return jnp.asarray(np.stack(masks).astype(np.float32))


def _na_latent(nq, nk, nv2, tab, tab0, masks, n_rows_out, n_batch, seq, n_ctx):
    chunks = nq.shape[1] // LANES
    tq = NA_BLOCK_ROWS * GRID_W
    nblk = seq // tq
    ctx_blk0 = n_batch * seq // n_ctx
    qmap = lambda b, j, i: (b * nblk + i, j)
    variant = lambda i: jnp.where(i == 0, 0, jnp.where(i == nblk - 1, 2, 1))
    tab_spec = pl.BlockSpec((1, 2) + tab.shape[2:], lambda b, j, i: (tab0 + variant(i), j, 0, 0, 0))
    mask_spec = pl.BlockSpec((1,) + masks.shape[1:], lambda b, j, i: (variant(i), 0, 0))
    return pl.pallas_call(
        functools.partial(_na_kernel, grid_rows=seq // GRID_W),
        grid=(n_batch, chunks, nblk),
        in_specs=[pl.BlockSpec((tq, LANES), qmap),
                  pl.BlockSpec((n_ctx, LANES), lambda b, j, i: (ctx_blk0 + b, j)),
                  pl.BlockSpec((n_ctx, 2 * LANES), lambda b, j, i: (ctx_blk0 + b, j)),
                  pl.BlockSpec((seq, LANES), lambda b, j, i: (b, j)),
                  pl.BlockSpec((seq, 2 * LANES), lambda b, j, i: (b, j)),
                  tab_spec, mask_spec],
        out_specs=pl.BlockSpec((tq, LANES), qmap),
        out_shape=jax.ShapeDtypeStruct((n_rows_out, nq.shape[1]), BF16),
        compiler_params=_cparams(("arbitrary", "arbitrary", "arbitrary")),
        name="na_latent",
    )(nq, nk, nv2, nk, nv2, tab, masks)


def _hfilt_kernel(ft_ref, tc_ref, w1t_ref, b1_ref, w2t_ref, b2_ref, w3_ref, fr_ref, d_ref, k_ref, l1_ref, *, n):
    i = pl.program_id(0)
    fr = fr_ref[...]
    h1 = jnp.sin(fr * (jnp.dot(w1t_ref[...], ft_ref[...], precision=HIGHEST, preferred_element_type=F32)
                       + b1_ref[...]))
    h2 = jnp.sin(fr * (jnp.dot(w2t_ref[...], h1, precision=HIGHEST, preferred_element_type=F32) + b2_ref[...]))
    contract0 = (((0,), (0,)), ((), ()))
    o = lax.dot_general(h2, w3_ref[...], contract0, precision=HIGHEST, preferred_element_type=F32)
    cw = o.shape[1] // 2
    t = o.shape[0]
    row = i * t + lax.broadcasted_iota(jnp.int32, (t, cw), 0)
    k = jnp.where(row < n, o[:, :cw], o[:, cw:]) * jnp.exp(-tc_ref[...] * d_ref[...])
    k = jnp.where(row == n, 0.0, k)
    k_ref[...] = k

    @pl.when(i == 0)
    def _():
        l1_ref[...] = jnp.zeros_like(l1_ref)

    l1_ref[...] += jnp.sum(jnp.abs(k), axis=0, keepdims=True)


def _filter_feats(n):
    rows = np.arange(2 * n)
    pos = np.where(rows <= n, rows, 2 * n - rows).astype(np.float32)
    pos = jnp.asarray(np.minimum(pos, n - 1))
    t = pos / max(n - 1, 1)
    bands = jnp.linspace(1e-4, FILTER_BANDS - 1, FILTER_BANDS, dtype=F32)
    ang = (2.0 * math.pi / n) * pos[None, :] * bands[:, None]
    feats = jnp.concatenate([t[None, :], jnp.cos(ang), -jnp.sin(ang)], axis=0)
    return jnp.pad(feats, ((0, LANES - feats.shape[0]), (0, 0))), t[:, None]


def _hfilter(n, w1, b1, w2, b2, w3, freq, deltas):
    feats_t, t_col = _filter_feats(n)
    w1t = jnp.pad(w1.T, ((0, 0), (0, LANES - w1.shape[0])))
    col = lambda a: a.reshape(-1, 1)
    rows = feats_t.shape[1]
    t = min(1024, rows)
    cw = w3.shape[1] // 2
    const = lambda i: (0, 0)
    full = lambda a: pl.BlockSpec(a.shape, const)
    args = (w1t, col(b1), w2.T, col(b2), w3, col(freq), jnp.abs(deltas))
    return pl.pallas_call(
        functools.partial(_hfilt_kernel, n=n),
        grid=(rows // t,),
        in_specs=[pl.BlockSpec((LANES, t), lambda i: (0, i)), pl.BlockSpec((t, 1), lambda i: (i, 0))]
        + [full(a) for a in args],
        out_specs=[pl.BlockSpec((t, cw), lambda i: (i, 0)), pl.BlockSpec((1, cw), const)],
        out_shape=[jax.ShapeDtypeStruct((rows, cw), F32), jax.ShapeDtypeStruct((1, cw), F32)],
        compiler_params=_cparams(("arbitrary",)),
        name="hyena_filter",
    )(feats_t, t_col, *args)


def _dft_outer_kernel(f_ref, x_ref, o_ref):
    c = x_ref.shape[-1]
    x = x_ref[...].reshape(f_ref.shape[1], c).astype(BF16)
    o_ref[...] = jnp.dot(f_ref[...], x, preferred_element_type=F32).astype(o_ref.dtype).reshape(o_ref.shape)


def _kron_table(fmat, cols):
    return jnp.asarray(np.kron(np.asarray(fmat, np.float32), np.eye(cols, dtype=np.float32)).astype(BF16))


def _outer_cols(dtype):
    return OUTER_COLS * 4 // jnp.dtype(dtype).itemsize


def _dft_outer_fwd(fmat, x3, nb, n1_len, out_dtype):
    k = fmat.shape[1]
    c = x3.shape[2]
    cols = _outer_cols(out_dtype)
    f = _kron_table(fmat, cols)
    return pl.pallas_call(
        _dft_outer_kernel,
        grid=(nb, DFT_N2 // cols),
        in_specs=[pl.BlockSpec(f.shape, lambda b, j: (0, 0), pipeline_mode=pl.Buffered(1)),
                  pl.BlockSpec((k, cols, c), lambda b, j: (b, j, 0))],
        out_specs=pl.BlockSpec((1, 2, n1_len, cols, c), lambda b, j: (b, 0, 0, j, 0)),
        out_shape=jax.ShapeDtypeStruct((nb, 2, n1_len, DFT_N2, c), out_dtype),
        compiler_params=_cparams(("arbitrary", "arbitrary")),
        name="dft_outer_fwd",
    )(f, x3)


def _dft_outer_inv(fmat, d5):
    nb, _, n1_len, _, c = d5.shape
    half = fmat.shape[0]
    cols = _outer_cols(d5.dtype)
    f = _kron_table(fmat, cols)
    return pl.pallas_call(
        _dft_outer_kernel,
        grid=(nb, DFT_N2 // cols),
        in_specs=[pl.BlockSpec(f.shape, lambda b, j: (0, 0), pipeline_mode=pl.Buffered(1)),
                  pl.BlockSpec((1, 2, n1_len, cols, c), lambda b, j: (b, 0, 0, j, 0))],
        out_specs=pl.BlockSpec((half, cols, c), lambda b, j: (b, j, 0)),
        out_shape=jax.ShapeDtypeStruct((nb * half, DFT_N2, c), F32),
        compiler_params=_cparams(("arbitrary", "arbitrary")),
        name="dft_outer_inv",
    )(f, d5)


def _inner_matrices(fc, fs, tc, ts):
    gr = fc * tc - fs * ts
    gi = -(fc * ts + fs * tc)
    return gr, gi


def _real_form(gr, gi):
    return jnp.concatenate([jnp.concatenate([gr, -gi], axis=1), jnp.concatenate([gi, gr], axis=1)], axis=0)


def _dft_inner_filter_kernel(fc_ref, fs_ref, tc_ref, ts_ref, b_ref, o_ref):
    fc, fs = fc_ref[...], fs_ref[...]
    n2, c = b_ref.shape[3], b_ref.shape[4]

    def body(kk, carry):
        g = _real_form(*_inner_matrices(fc, fs, tc_ref[kk], ts_ref[kk]))
        o_ref[kk] = jnp.dot(g.astype(BF16), b_ref[0, :, kk].reshape(2 * n2, c).astype(BF16),
                            preferred_element_type=F32)
        return carry

    lax.fori_loop(0, b_ref.shape[2], body, 0, unroll=2)


def _dft_inner_conv_kernel(fc_ref, fs_ref, tc_ref, ts_ref, b_ref, kh_ref, sc_ref, o_ref):
    fc, fs = fc_ref[...], fs_ref[...]
    nb, n2, c = b_ref.shape[0], b_ref.shape[3], b_ref.shape[4]
    scale = sc_ref[...]

    def body(kk, carry):
        gr, gi = _inner_matrices(fc, fs, tc_ref[kk], ts_ref[kk])
        g = _real_form(gr, gi).astype(BF16)
        gh = _real_form(gr.T, -gi.T).astype(BF16)
        kh = kh_ref[kk]
        kr, ki = kh[:n2], kh[n2:]
        for b in range(nb):
            x = jnp.dot(g, b_ref[b, :, kk].reshape(2 * n2, c).astype(BF16), preferred_element_type=F32)
            xr, xi = x[:n2], x[n2:]
            y = jnp.concatenate([(xr * kr - xi * ki) * scale, (xr * ki + xi * kr) * scale], axis=0)
            d = jnp.dot(gh, y.astype(BF16), preferred_element_type=F32)
            o_ref[b, :, kk] = d.astype(o_ref.dtype).reshape(2, n2, c)
        return carry

    lax.fori_loop(0, b_ref.shape[2], body, 0, unroll=2)


def _dft_tables(n1_len):
    n_total = n1_len * DFT_N2
    a = np.arange(DFT_N2)
    ang = 2.0 * np.pi * np.outer(a, a) / DFT_N2
    k1 = np.arange(n1_len)
    tw = 2.0 * np.pi * np.outer(k1, a) / n_total
    o = 2.0 * np.pi * np.outer(k1, k1) / n1_len
    f32 = lambda v: jnp.asarray(v.astype(np.float32))
    outer_fwd = np.concatenate([np.cos(o), -np.sin(o)], axis=0)
    outer_inv = np.concatenate([np.cos(o), -np.sin(o)], axis=1)
    return dict(
        fc=f32(np.cos(ang)), fs=f32(np.sin(ang)),
        tc=f32(np.cos(tw)).reshape(n1_len, 1, DFT_N2), ts=f32(np.sin(tw)).reshape(n1_len, 1, DFT_N2),
        outer_fwd_full=outer_fwd, outer_fwd_half=outer_fwd[:, :n1_len // 2], outer_inv_half=outer_inv[:n1_len // 2],
    )


def _inner_specs():
    sq = pl.BlockSpec((DFT_N2, DFT_N2), lambda k: (0, 0))
    tw = pl.BlockSpec((INNER_K1, 1, DFT_N2), lambda k: (k, 0, 0))
    return [sq, sq, tw, tw]


def _filter_spectrum(kfull, tabs, n1_len):
    c = kfull.shape[1]
    b5 = _dft_outer_fwd(tabs["outer_fwd_full"], kfull.reshape(n1_len, DFT_N2, c), 1, n1_len, F32)
    return pl.pallas_call(
        _dft_inner_filter_kernel,
        grid=(n1_len // INNER_K1,),
        in_specs=_inner_specs() + [pl.BlockSpec((1, 2, INNER_K1, DFT_N2, c), lambda k: (0, 0, k, 0, 0))],
        out_specs=pl.BlockSpec((INNER_K1, 2 * DFT_N2, c), lambda k: (k, 0, 0)),
        out_shape=jax.ShapeDtypeStruct((n1_len, 2 * DFT_N2, c), F32),
        compiler_params=_cparams(("arbitrary",)),
        name="dft_inner_filter",
    )(tabs["fc"], tabs["fs"], tabs["tc"], tabs["ts"], b5)


def _long_conv(z_all, nb, khat, scale, tabs, n1_len):
    c = z_all.shape[1]
    bz = _dft_outer_fwd(tabs["outer_fwd_half"], z_all.reshape(z_all.shape[0] // DFT_N2, DFT_N2, c), nb, n1_len, BF16)
    blk = pl.BlockSpec((nb, 2, INNER_K1, DFT_N2, c), lambda k: (0, 0, k, 0, 0))
    d = pl.pallas_call(
        _dft_inner_conv_kernel,
        grid=(n1_len // INNER_K1,),
        in_specs=_inner_specs() + [blk, pl.BlockSpec((INNER_K1, 2 * DFT_N2, c), lambda k: (k, 0, 0)),
                                   pl.BlockSpec((1, c), lambda k: (0, 0))],
        out_specs=blk,
        out_shape=jax.ShapeDtypeStruct(bz.shape, BF16),
        compiler_params=_cparams(("arbitrary",)),
        name="dft_inner_conv",
    )(tabs["fc"], tabs["fs"], tabs["tc"], tabs["ts"], bz, khat, scale)
    y3 = _dft_outer_inv(tabs["outer_inv_half"], d)
    return y3.reshape(y3.shape[0] * DFT_N2, c)


def _hyena_ctx_kernel(ff_ref, fi_ref, k_ref, l1_ref, z_ref, o_ref):
    ff = ff_ref[...]
    n = z_ref.shape[0]
    big = ff.shape[1]
    kh = jnp.dot(ff, k_ref[...], precision=HIGHEST, preferred_element_type=F32)
    x = jnp.dot(ff[:, :n], z_ref[...], precision=HIGHEST, preferred_element_type=F32)
    kr, ki = kh[:big], kh[big:]
    xr, xi = x[:big], x[big:]
    scale = 1.0 / (l1_ref[...] * big)
    y = jnp.concatenate([(xr * kr - xi * ki) * scale, (xr * ki + xi * kr) * scale], axis=0)
    o_ref[...] = jnp.dot(fi_ref[...], y, precision=HIGHEST, preferred_element_type=F32)


def _hyena_ctx(z, kfull, l1, row0_blk, n_batch, n):
    big = 2 * n
    a = np.arange(big)
    ang = 2.0 * np.pi * np.outer(a, a) / big
    ff = jnp.asarray(np.concatenate([np.cos(ang), -np.sin(ang)], axis=0).astype(np.float32))
    fi = jnp.asarray(np.concatenate([np.cos(ang), -np.sin(ang)], axis=1)[:n].astype(np.float32))
    c = z.shape[1]
    const = lambda b: (0, 0)
    return pl.pallas_call(
        _hyena_ctx_kernel,
        grid=(n_batch,),
        in_specs=[pl.BlockSpec(ff.shape, const), pl.BlockSpec(fi.shape, const), pl.BlockSpec(kfull.shape, const),
                  pl.BlockSpec((1, c), const), pl.BlockSpec((n, c), lambda b: (row0_blk + b, 0))],
        out_specs=pl.BlockSpec((n, c), lambda b: (b, 0)),
        out_shape=jax.ShapeDtypeStruct((n_batch * n, c), F32),
        compiler_params=_cparams(("arbitrary",)),
        name="hyena_ctx",
    )(ff, fi, kfull, l1, z)


def _rope_tables(seq, n_batch, n_ctx_rows):
    t = jnp.arange(seq, dtype=jnp.int32)
    rows = (t // GRID_W).astype(F32)
    cols = (t % GRID_W).astype(F32)
    nf = HEAD_DIM // 4
    inv = ROPE_THETA ** (-jnp.arange(nf, dtype=F32) / nf)
    ar, ac = rows[:, None] * inv, cols[:, None] * inv
    cos = jnp.concatenate([jnp.cos(ar), jnp.cos(ar), jnp.cos(ac), jnp.cos(ac)], axis=1)
    sin = jnp.concatenate([-jnp.sin(ar), jnp.sin(ar), -jnp.sin(ac), jnp.sin(ac)], axis=1)
    cos = jnp.tile(cos, (n_batch, LANES // HEAD_DIM))
    sin = jnp.tile(sin, (n_batch, LANES // HEAD_DIM))
    cos = jnp.concatenate([cos, jnp.ones((n_ctx_rows, LANES), F32)], axis=0)
    sin = jnp.concatenate([sin, jnp.zeros((n_ctx_rows, LANES), F32)], axis=0)
    return cos, sin


def _head_mean_matrix():
    a = np.arange(LANES)
    return jnp.asarray((a[:, None] // HEAD_DIM == a[None, :] // HEAD_DIM).astype(np.float32) / HEAD_DIM, dtype=BF16)


def kernel(x, c, ctx, c_ctx, w_ada, b_ada, g_ffn1, w_ffn1_gate, w_ffn1_up, w_ffn1_down, g_mix, w_in, w_out, g_q_attn, g_k_attn, conv_w, conv_b, filt_w1, filt_b1, filt_w2, filt_b2, filt_w3, filt_freq, hyena_skip, g_q_na, g_k_na, na_rpb, g_ffn2, w_ffn2_gate, w_ffn2_up, w_ffn2_down):
    n_batch, seq, d = x.shape
    n_ctx = ctx.shape[1]
    depth = w_ada.shape[0]
    assert d == _D and n_batch + 1 <= 8 and seq % TOKEN_TILE == 0 and seq % (NA_BLOCK_ROWS * GRID_W) == 0
    assert n_ctx == 256 and (n_batch * n_ctx) % TOKEN_TILE == 0
    n_lat = n_batch * seq
    n_all = n_lat + n_batch * n_ctx
    tiles_per_batch = seq // TOKEN_TILE
    hy_w = conv_w.shape[2] // 3
    n1_len = 2 * seq // DFT_N2

    cond8 = jnp.zeros((8, d), F32).at[:n_batch].set(c).at[n_batch].set(c_ctx)
    mods = _adaln_all(cond8, w_ada, b_ada).reshape(depth * 8 * N_MOD, 1, d)
    h = jnp.concatenate([x.reshape(n_lat, d), ctx.reshape(n_batch * n_ctx, d)], axis=0)

    cos_t, sin_t = _rope_tables(seq, n_batch, n_batch * n_ctx)
    headmean = _head_mean_matrix()
    na_masks = _na_row_masks(seq // GRID_W)
    na_tabs = jax.vmap(lambda r: _na_bias_tables(r, seq // GRID_W))(na_rpb)
    na_tabs = na_tabs.reshape((depth * 3,) + na_tabs.shape[2:])
    tabs = _dft_tables(n1_len)
    deltas = jnp.linspace(math.log(DECAY_TARGET) / SLOW_DECAY_PCT, math.log(DECAY_TARGET) / FAST_DECAY_PCT,
                          hy_w, dtype=F32).reshape(1, hy_w)

    q_heads = [hd for pair in range(ATTN_HEADS // 2) for hd in (pair, pair + ATTN_HEADS // 2)]
    q_segs = [(hd * HEAD_DIM, (hd + 1) * HEAD_DIM) for hd in q_heads]
    in_segs = q_segs + [(512, 640), (1536, 1792), (1792, 2048), (640, 768), (2048, 2304), (768, 1536)]

    for l in range(depth):
        last = l == depth - 1
        base = l * 8 * N_MOD
        rep = lambda g, k: jnp.tile(g, k)
        gvec = jnp.concatenate([rep(g_q_attn[l], ATTN_HEADS), rep(g_k_attn[l], ATTN_KV_HEADS),
                                rep(g_q_na[l], NA_HEADS), rep(g_k_na[l], NA_HEADS)]).reshape(1, N_NORM)

        h = _ffn(h, mods, base, (0, 1, 2), g_ffn1[l].reshape(1, d), w_ffn1_gate[l].astype(BF16),
                 w_ffn1_up[l].astype(BF16), w_ffn1_down[l].astype(BF16), n_all, tiles_per_batch, n_batch)

        w_in_l = jnp.concatenate([w_in[l][:, a:b] for a, b in in_segs], axis=1).astype(BF16)
        q, k, nq, nk, v, nv, z, x0 = _inproj(h, mods, base, g_mix[l].reshape(1, d), w_in_l, gvec, headmean, cos_t, sin_t,
                                             conv_w[l], conv_b[l].reshape(1, -1), tiles_per_batch, n_batch, seq, n_ctx)

        out_rows = n_lat if last else n_all
        ya = _gqa(q, k, v, out_rows, n_batch, seq, n_ctx, latent=True)
        yc = _na_latent(nq, nk, nv, na_tabs, 3 * l, na_masks, out_rows, n_batch, seq, n_ctx)

        fargs = (filt_w1[l], filt_b1[l].reshape(1, -1), filt_w2[l], filt_b2[l].reshape(1, -1), filt_w3[l],
                 filt_freq[l].reshape(1, -1), deltas)
        kfull, l1 = _hfilter(seq, *fargs)
        khat = _filter_spectrum(kfull, tabs, n1_len)
        yconv = _long_conv(z, n_batch, khat, 1.0 / (l1 * (2 * seq)), tabs, n1_len)

        if not last:
            ya_c = _gqa(q, k, v, n_all, n_batch, seq, n_ctx, latent=False)
            yc_c = _gqa(nq, nk, nv, n_all, n_batch, seq, n_ctx, latent=False, kv_chunked=True)
            ya = lax.dynamic_update_slice(ya, ya_c[n_lat:], (n_lat, 0))
            yc = lax.dynamic_update_slice(yc, yc_c[n_lat:], (n_lat, 0))
            kfull_c, l1_c = _hfilter(n_ctx, *fargs)
            yconv_c = _hyena_ctx(z, kfull_c, l1_c, n_lat // n_ctx, n_batch, n_ctx)
            yconv = jnp.concatenate([yconv, yconv_c], axis=0)

        w_o = w_out[l]
        wa = jnp.concatenate([w_o[a:b] for a, b in q_segs], axis=0).astype(BF16)
        wb = w_o[512:512 + hy_w].astype(BF16)
        wc = w_o[512 + hy_w:].astype(BF16)
        h = _ffn(h, mods, base, (6, 7, 8), g_ffn2[l].reshape(1, d), w_ffn2_gate[l].astype(BF16),
                 w_ffn2_up[l].astype(BF16), w_ffn2_down[l].astype(BF16), out_rows, tiles_per_batch, n_batch,
                 mixer=(ya, yconv, z, x0, hyena_skip[l].reshape(1, hy_w), yc, wa, wb, wc))

    return h[:n_lat].reshape(n_batch, seq, d)
```

```python
import functools
import math

import numpy as np
import jax
import jax.numpy as jnp
from jax import lax
from jax.experimental import pallas as pl
from jax.experimental.pallas import tpu as pltpu

F32 = jnp.float32
BF16 = jnp.bfloat16
HIGHEST = lax.Precision.HIGHEST

HEAD_DIM = 64
GRID_W = 64
ATTN_HEADS = 8
ATTN_KV_HEADS = 2
NA_HEADS = 4
NA_ROWS = 8
NA_COLS = 16
FILTER_BANDS = 16
DECAY_TARGET = 1e-2
FAST_DECAY_PCT = 0.3
SLOW_DECAY_PCT = 1.5
ROPE_THETA = 10000.0
EPS = 1e-6
N_MOD = 9

LANES = 128
TOKEN_TILE = 512
FFN_CHUNK = 256
ATTN_TQ = 1024
ATTN_TK = 2048
NA_BLOCK_ROWS = 8
NA_KEY_ROWS = 16
NA_SUB_ROWS = 4
DFT_N2 = 128
OUTER_COLS = 8
INNER_K1 = 8
NEG_BIG = -1e30
VMEM_LIMIT = 56 * 1024 * 1024


def _cparams(sem):
    return pltpu.CompilerParams(dimension_semantics=sem, vmem_limit_bytes=VMEM_LIMIT)


def _ada_kernel(c_ref, w_ref, b_ref, o_ref):
    c = c_ref[...]
    s = c * jax.nn.sigmoid(c)
    o_ref[0] = jnp.dot(s, w_ref[0], preferred_element_type=F32) + b_ref[0]


def _adaln_all(cond8, w_ada, b_ada):
    depth, d, nd = w_ada.shape
    tn = 2304
    return pl.pallas_call(
        _ada_kernel,
        grid=(depth, nd // tn),
        in_specs=[
            pl.BlockSpec((8, d), lambda l, j: (0, 0)),
            pl.BlockSpec((1, d, tn), lambda l, j: (l, 0, j)),
            pl.BlockSpec((1, 1, tn), lambda l, j: (l, 0, j)),
        ],
        out_specs=pl.BlockSpec((1, 8, tn), lambda l, j: (l, 0, j)),
        out_shape=jax.ShapeDtypeStruct((depth, 8, nd), F32),
        compiler_params=_cparams(("arbitrary", "arbitrary")),
        name="adaln",
    )(cond8, w_ada, b_ada.reshape(depth, 1, nd))


def _norm_mod(x, g, sh, sc):
    ms = jnp.mean(x * x, axis=-1, keepdims=True)
    y = x * lax.rsqrt(ms + EPS) * g
    return y * (1.0 + sc) + sh


def _mod_specs(tiles_per_batch, n_batch, base, idxs):
    def spec(which):
        return pl.BlockSpec(
            (1, 1, _D),
            lambda i, w=which: (base + jnp.minimum(i // tiles_per_batch, n_batch) * N_MOD + w, 0, 0))
    return [spec(w) for w in idxs]


_D = 1024


def _ffn_kernel(h_ref, *refs, with_mixer):
    x = h_ref[...]
    if with_mixer:
        gt2_ref, ya_ref, yconv_ref, z_ref, x0_ref, skip_ref, yc_ref, wa_ref, wb_ref, wc_ref = refs[:10]
        refs = refs[10:]
        yb = ((yconv_ref[...] + z_ref[...] * skip_ref[...]) * x0_ref[...]).astype(BF16)
        x = x + gt2_ref[0] * (jnp.dot(ya_ref[...], wa_ref[...], preferred_element_type=F32)
                              + jnp.dot(yb, wb_ref[...], preferred_element_type=F32)
                              + jnp.dot(yc_ref[...], wc_ref[...], preferred_element_type=F32))
    sh_ref, sc_ref, gt_ref, g_ref, wg_ref, wu_ref, wd_ref, o_ref, xb_ref, acc_ref = refs
    if with_mixer:
        o_ref[...] = x
    xb_ref[...] = _norm_mod(x, g_ref[...], sh_ref[0], sc_ref[0]).astype(BF16)

    for k in range(wg_ref.shape[2] // FFN_CHUNK):
        cols = slice(k * FFN_CHUNK, (k + 1) * FFN_CHUNK)
        xb = xb_ref[...]
        a = jnp.dot(xb, wg_ref[0, :, cols], preferred_element_type=F32)
        u = jnp.dot(xb, wu_ref[0, :, cols], preferred_element_type=F32)
        act = (a * jax.nn.sigmoid(a) * u).astype(BF16)
        part = jnp.dot(act, wd_ref[0, cols, :], preferred_element_type=F32)
        if k == 0:
            acc_ref[...] = part
        else:
            acc_ref[...] += part
    resid = o_ref[...] if with_mixer else h_ref[...]
    o_ref[...] = resid + 0.5 * gt_ref[0] * acc_ref[...]


def _ffn(h, mods, mod_base, which, g, layer, wg, wu, wd, n_rows, tiles_per_batch, n_batch, mixer=None):
    d = h.shape[1]
    of_layer = lambda a: pl.BlockSpec((1,) + a.shape[1:], lambda i: (layer, 0, 0), pipeline_mode=pl.Buffered(1))
    tm = TOKEN_TILE
    row = lambda i: (i, 0)
    const = lambda i: (0, 0)
    rows = lambda a: pl.BlockSpec((tm, a.shape[1]), row)
    full = lambda a: pl.BlockSpec(a.shape, const)
    once = lambda a: pl.BlockSpec(a.shape, const, pipeline_mode=pl.Buffered(1))
    mix_specs, mix_args = [], ()
    if mixer is not None:
        ya, yconv, z, x0, skip, yc, wa, wb, wc = mixer
        mix_specs = _mod_specs(tiles_per_batch, n_batch, mod_base, (5,)) + [
            rows(ya), rows(yconv), rows(z), rows(x0), full(skip), rows(yc), once(wa), once(wb), once(wc)]
        mix_args = (mods,) + tuple(mixer)
    return pl.pallas_call(
        functools.partial(_ffn_kernel, with_mixer=mixer is not None),
        grid=(n_rows // tm,),
        in_specs=[rows(h)] + mix_specs + _mod_specs(tiles_per_batch, n_batch, mod_base, which)
        + [full(g), of_layer(wg), of_layer(wu), of_layer(wd)],
        out_specs=pl.BlockSpec((tm, d), row),
        out_shape=jax.ShapeDtypeStruct((n_rows, d), F32),
        scratch_shapes=[pltpu.VMEM((tm, d), BF16), pltpu.VMEM((tm, d), F32)],
        compiler_params=_cparams(("arbitrary",)),
        name="ffn_mix" if mixer is not None else "ffn",
    )(h, *mix_args, mods, mods, mods, g, wg, wu, wd)


N_NORM = 1152
N_ROPE = 640
ATTN_Q_CHUNKS = (0, 1, 2, 3)
NA_Q_CHUNKS = (5, 6)
LOG2E = math.log2(math.e)


def _inproj_kernel(h_ref, prev_ref, next_ref, sh_ref, sc_ref, g_ref, w_ref, gv_ref, hm_ref, cos_ref, sin_ref,
                   cw_ref, cb_ref, q_ref, k_ref, nq_ref, nk_ref, v_ref, nv_ref, z_ref, x0_ref, *, seq, n_ctx, n_lat):
    xb = _norm_mod(h_ref[...], g_ref[...], sh_ref[0], sc_ref[0]).astype(BF16)
    hm = hm_ref[...]
    cosv = cos_ref[...]
    sinv = sin_ref[...]
    lane = lax.broadcasted_iota(jnp.int32, cosv.shape, 1)
    first_half = (lane % 32) < 16
    outs = []
    p_norm = jnp.dot(xb, w_ref[:, 0:N_NORM], preferred_element_type=F32)
    for c in range(N_NORM // LANES):
        p = p_norm[:, c * LANES:(c + 1) * LANES]
        ms = jnp.dot((p * p).astype(BF16), hm, preferred_element_type=F32)
        y = p * lax.rsqrt(ms + EPS) * gv_ref[:, c * LANES:(c + 1) * LANES]
        if c < N_ROPE // LANES:
            swapped = jnp.where(first_half, pltpu.roll(y, LANES - 16, 1), pltpu.roll(y, 16, 1))
            y = y * cosv + swapped * sinv
        if c in ATTN_Q_CHUNKS or c in NA_Q_CHUNKS:
            y = y * (HEAD_DIM ** -0.5 * LOG2E)
        outs.append(y.astype(BF16))
    for c in range(4):
        q_ref[:, c * LANES:(c + 1) * LANES] = outs[c]
    k_ref[...] = outs[4]
    nq_ref[:, 0:LANES] = outs[5]
    nq_ref[:, LANES:2 * LANES] = outs[6]
    nk_ref[:, 0:LANES] = outs[7]
    nk_ref[:, LANES:2 * LANES] = outs[8]
    p_rest = jnp.dot(xb, w_ref[:, N_NORM:], preferred_element_type=F32)
    v = p_rest[:, 0:LANES]
    v_ref[:, 0:LANES] = jnp.where(lane < HEAD_DIM, v, 1.0).astype(BF16)
    v_ref[:, LANES:2 * LANES] = jnp.where(lane < HEAD_DIM, 1.0, v).astype(BF16)
    for c in range(NA_HEADS // 2):
        nv = p_rest[:, (1 + c) * LANES:(2 + c) * LANES]
        nv_ref[:, 2 * c * LANES:(2 * c + 1) * LANES] = jnp.where(lane < HEAD_DIM, nv, 1.0).astype(BF16)
        nv_ref[:, (2 * c + 1) * LANES:(2 * c + 2) * LANES] = jnp.where(lane < HEAD_DIM, 1.0, nv).astype(BF16)

    hy = p_rest[:, 3 * LANES:]
    tm = hy.shape[0]
    halo = jnp.concatenate([prev_ref[...], next_ref[...]], axis=0)
    halo_b = _norm_mod(halo, g_ref[...], sh_ref[0], sc_ref[0]).astype(BF16)
    hy_halo = jnp.dot(halo_b, w_ref[:, N_NORM + 3 * LANES:], preferred_element_type=F32)
    row0 = pl.program_id(0) * tm
    is_ctx = row0 >= n_lat
    period = jnp.where(is_ctx, n_ctx, seq)
    local = lax.broadcasted_iota(jnp.int32, (tm, 1), 0)
    pos = jnp.bitwise_and(row0 + local - jnp.where(is_ctx, n_lat, 0), period - 1)
    up = jnp.where(local == 0, hy_halo[7:8, :], pltpu.roll(hy, 1, 0))
    up = jnp.where(pos == 0, 0.0, up)
    dn = jnp.where(local == tm - 1, hy_halo[8:9, :], pltpu.roll(hy, tm - 1, 0))
    dn = jnp.where(pos == period - 1, 0.0, dn)
    uc = up * cw_ref[0:1, :] + hy * cw_ref[1:2, :] + dn * cw_ref[2:3, :] + cb_ref[...]
    cw = uc.shape[1] // 3
    x0_ref[...] = uc[:, :cw]
    z_ref[...] = uc[:, 2 * cw:] * uc[:, cw:2 * cw]


def _inproj(h, mods, mod_base, g, w, gvec, headmean, cos_t, sin_t, conv_w, conv_b, tiles_per_batch, n_batch, seq,
            n_ctx):
    n_rows, d = h.shape
    tm = TOKEN_TILE
    assert seq & (seq - 1) == 0 and n_ctx & (n_ctx - 1) == 0
    row = lambda i: (i, 0)
    const = lambda i: (0, 0)
    widths = (512, 128, 256, 256, 256, 512)
    hy_w = conv_w.shape[1] // 3
    out_shape = [jax.ShapeDtypeStruct((n_rows, wd), BF16) for wd in widths]
    out_shape += [jax.ShapeDtypeStruct((n_rows, hy_w), F32)] * 2
    out_specs = [pl.BlockSpec((tm, wd), row) for wd in widths] + [pl.BlockSpec((tm, hy_w), row)] * 2
    return pl.pallas_call(
        functools.partial(_inproj_kernel, seq=seq, n_ctx=n_ctx, n_lat=n_batch * seq),
        grid=(n_rows // tm,),
        in_specs=[pl.BlockSpec((tm, d), row),
                  pl.BlockSpec((8, d), lambda i: (jnp.maximum(i * (tm // 8) - 1, 0), 0)),
                  pl.BlockSpec((8, d), lambda i: (jnp.minimum((i + 1) * (tm // 8), n_rows // 8 - 1), 0))]
        + _mod_specs(tiles_per_batch, n_batch, mod_base, (3, 4))
        + [
            pl.BlockSpec((1, d), const),
            pl.BlockSpec(w.shape, const, pipeline_mode=pl.Buffered(1)),
            pl.BlockSpec(gvec.shape, const),
            pl.BlockSpec(headmean.shape, const),
            pl.BlockSpec((tm, LANES), row),
            pl.BlockSpec((tm, LANES), row),
            pl.BlockSpec(conv_w.shape, const),
            pl.BlockSpec(conv_b.shape, const),
        ],
        out_specs=out_specs,
        out_shape=out_shape,
        compiler_params=_cparams(("arbitrary",)),
        name="inproj",
    )(h, h, h, mods, mods, g, w, gvec, headmean, cos_t, sin_t, conv_w, conv_b)


def _stack_heads(q):
    lane = lax.broadcasted_iota(jnp.int32, q.shape, 1)
    zero = jnp.zeros_like(q)
    return jnp.concatenate([jnp.where(lane < HEAD_DIM, q, zero), jnp.where(lane >= HEAD_DIM, q, zero)], axis=0)


def _qk(q2, k):
    return lax.dot_general(q2, k, (((1,), (1,)), ((), ())), preferred_element_type=F32)


def _gqa_kernel(q_ref, kc_ref, vc_ref, *rest, n_latent_chunks, tk):
    if n_latent_chunks:
        kl_ref, vl_ref, o_ref = rest
    else:
        (o_ref,) = rest
    tq = q_ref.shape[0]
    q2 = _stack_heads(q_ref[...])

    def pv(p, v_ref, rows):
        return jnp.concatenate(
            [jnp.dot(p[:tq], v_ref[rows, 0:LANES], preferred_element_type=F32),
             jnp.dot(p[tq:], v_ref[rows, LANES:2 * LANES], preferred_element_type=F32)], axis=0)

    s = _qk(q2, kc_ref[...])
    m = jnp.max(s, axis=-1, keepdims=True)
    acc = pv(jnp.exp2((s - m).astype(BF16)), vc_ref, slice(None))

    if n_latent_chunks:
        def body(j, carry):
            m, acc = carry
            rows = pl.ds(pl.multiple_of(j * tk, tk), tk)
            s = _qk(q2, kl_ref[rows, :])
            m_new = jnp.maximum(m, jnp.max(s, axis=-1, keepdims=True))
            p = jnp.exp2((s - m_new).astype(BF16))
            return m_new, jnp.exp2(m - m_new) * acc + pv(p, vl_ref, rows)

        m, acc = lax.fori_loop(0, n_latent_chunks, body, (m, acc), unroll=True)
    lane = lax.broadcasted_iota(jnp.int32, (tq, LANES), 1)
    num = jnp.where(lane < HEAD_DIM, acc[:tq], acc[tq:])
    den = pltpu.roll(jnp.where(lane < HEAD_DIM, acc[tq:], acc[:tq]), HEAD_DIM, 1)
    o_ref[...] = (num / den).astype(o_ref.dtype)


def _gqa(q, k, v2, out_rows, n_batch, seq, n_ctx, latent, kv_chunked=False):
    chunks = q.shape[1] // LANES
    ctx_blk0 = n_batch * seq // n_ctx
    tk = min(ATTN_TK, seq)
    kj = (lambda j: j) if kv_chunked else (lambda j: 0)
    if latent:
        tq = ATTN_TQ
        grid = (n_batch, chunks, seq // tq)
        qmap = lambda b, j, i: (b * (seq // tq) + i, j)
        kv_specs = [pl.BlockSpec((seq, LANES), lambda b, j, i: (b, kj(j))),
                    pl.BlockSpec((seq, 2 * LANES), lambda b, j, i: (b, kj(j)))]
        kv_args = (k, v2)
    else:
        tq = n_ctx
        grid = (n_batch, chunks, 1)
        qmap = lambda b, j, i: (ctx_blk0 + b, j)
        kv_specs, kv_args = [], ()
    return pl.pallas_call(
        functools.partial(_gqa_kernel, n_latent_chunks=seq // tk if latent else 0, tk=tk),
        grid=grid,
        in_specs=[pl.BlockSpec((tq, LANES), qmap),
                  pl.BlockSpec((n_ctx, LANES), lambda b, j, i: (ctx_blk0 + b, kj(j))),
                  pl.BlockSpec((n_ctx, 2 * LANES), lambda b, j, i: (ctx_blk0 + b, kj(j)))] + kv_specs,
        out_specs=pl.BlockSpec((tq, LANES), qmap),
        out_shape=jax.ShapeDtypeStruct((out_rows, q.shape[1]), BF16),
        compiler_params=_cparams(("arbitrary", "arbitrary", "arbitrary")),
        name="attn_latent" if latent else "attn_ctx",
    )(q, k, v2, *kv_args)


def _na_kernel(q_ref, kc_ref, vc_ref, kl_ref, vl_ref, tab_ref, mask_ref, o_ref, *, grid_rows):
    i = pl.program_id(2)
    nk = NA_KEY_ROWS * GRID_W
    ks = jnp.clip(i * NA_BLOCK_ROWS - NA_ROWS // 2, 0, grid_rows - NA_KEY_ROWS)
    keys = pl.ds(pl.multiple_of(ks * GRID_W, GRID_W), nk)
    k = kl_ref[keys, :]
    kc = kc_ref[...]
    sub_rows = NA_SUB_ROWS * GRID_W
    lane = lax.broadcasted_iota(jnp.int32, (sub_rows, LANES), 1)
    low = lane < HEAD_DIM
    for sub in range(NA_BLOCK_ROWS // NA_SUB_ROWS):
        rows = slice(sub * sub_rows, (sub + 1) * sub_rows)
        q = q_ref[rows, :]
        acc = []
        for hh in range(2):
            qh = jnp.where(low if hh == 0 else jnp.logical_not(low), q, jnp.zeros_like(q))
            slabs = []
            for qr in range(sub * NA_SUB_ROWS, (sub + 1) * NA_SUB_ROWS):
                pairs = [tab_ref[0, hh, NA_BLOCK_ROWS - qr + 2 * kp] for kp in range(NA_KEY_ROWS // 2)]
                slabs.append(jnp.concatenate(pairs, axis=1) + mask_ref[0, qr:qr + 1, :])
            s_nb = _qk(qh, k) + jnp.concatenate(slabs, axis=0)
            s_cx = _qk(qh, kc)
            m = jnp.maximum(jnp.max(s_nb, axis=-1, keepdims=True), jnp.max(s_cx, axis=-1, keepdims=True))
            vcols = slice(hh * LANES, (hh + 1) * LANES)
            acc.append(jnp.dot(jnp.exp2((s_nb - m).astype(BF16)), vl_ref[keys, vcols], preferred_element_type=F32)
                       + jnp.dot(jnp.exp2((s_cx - m).astype(BF16)), vc_ref[:, vcols], preferred_element_type=F32))
        num = jnp.where(low, acc[0], acc[1])
        den = pltpu.roll(jnp.where(low, acc[1], acc[0]), HEAD_DIM, 1)
        o_ref[rows, :] = (num / den).astype(o_ref.dtype)


def _na_block_start(blk, grid_rows):
    return int(np.clip(blk * NA_BLOCK_ROWS - NA_ROWS // 2, 0, grid_rows - NA_KEY_ROWS))


def _na_variant_blocks(grid_rows):
    nblk = grid_rows // NA_BLOCK_ROWS
    return (0, min(1, nblk - 1), nblk - 1)


def _na_bias_tables(rpb, grid_rows):
    qc = np.arange(GRID_W)[:, None]
    kc = np.arange(GRID_W)[None, :]
    cs = np.clip(qc - NA_COLS // 2, 0, GRID_W - NA_COLS)
    col_ok = (kc >= cs) & (kc < cs + NA_COLS)
    col_hot = ((kc - qc + NA_COLS - 1)[:, :, None] == np.arange(2 * NA_COLS - 1)) & col_ok[:, :, None]
    by_col = jnp.einsum("abe,hde->hdab", jnp.asarray(col_hot, F32), rpb, precision=HIGHEST)
    by_col = jnp.where(jnp.asarray(col_ok)[None, None], by_col * LOG2E, NEG_BIG)
    n_off = 2 * NA_ROWS - 1
    slots = NA_BLOCK_ROWS + NA_KEY_ROWS
    tabs = []
    for blk in _na_variant_blocks(grid_rows):
        off = _na_block_start(blk, grid_rows) - blk * NA_BLOCK_ROWS + NA_ROWS - 1
        front = NA_BLOCK_ROWS - off
        t = jnp.pad(by_col, ((0, 0), (front, slots + 1 - n_off - front), (0, 0), (0, 0)))
        tabs.append(jnp.concatenate([t[:, :-1], t[:, 1:]], axis=-1))
    return jnp.stack(tabs)


def _na_row_masks(grid_rows):
    masks = []
    for blk in _na_variant_blocks(grid_rows):
        qr = blk * NA_BLOCK_ROWS + np.arange(NA_BLOCK_ROWS)[:, None]
        kr = _na_block_start(blk, grid_rows) + np.arange(NA_KEY_ROWS)[None, :]
        rs = np.clip(qr - NA_ROWS // 2, 0, grid_rows - NA_ROWS)
        row_ok = (kr >= rs) & (kr < rs + NA_ROWS)
        masks.append(np.repeat(np.where(row_ok, 0.0, NEG_BIG), GRID_W, axis=1))
    return jnp.asarray(np.stack(masks).astype(np.float32))


def _na_latent(nq, nk, nv2, tab, tab0, masks, n_rows_out, n_batch, seq, n_ctx):
    chunks = nq.shape[1] // LANES
    tq = NA_BLOCK_ROWS * GRID_W
    nblk = seq // tq
    ctx_blk0 = n_batch * seq // n_ctx
    qmap = lambda b, j, i: (b * nblk + i, j)
    variant = lambda i: jnp.where(i == 0, 0, jnp.where(i == nblk - 1, 2, 1))
    tab_spec = pl.BlockSpec((1, 2) + tab.shape[2:], lambda b, j, i: (tab0 + variant(i), j, 0, 0, 0))
    mask_spec = pl.BlockSpec((1,) + masks.shape[1:], lambda b, j, i: (variant(i), 0, 0))
    return pl.pallas_call(
        functools.partial(_na_kernel, grid_rows=seq // GRID_W),
        grid=(n_batch, chunks, nblk),
        in_specs=[pl.BlockSpec((tq, LANES), qmap),
                  pl.BlockSpec((n_ctx, LANES), lambda b, j, i: (ctx_blk0 + b, j)),
                  pl.BlockSpec((n_ctx, 2 * LANES), lambda b, j, i: (ctx_blk0 + b, j)),
                  pl.BlockSpec((seq, LANES), lambda b, j, i: (b, j)),
                  pl.BlockSpec((seq, 2 * LANES), lambda b, j, i: (b, j)),
                  tab_spec, mask_spec],
        out_specs=pl.BlockSpec((tq, LANES), qmap),
        out_shape=jax.ShapeDtypeStruct((n_rows_out, nq.shape[1]), BF16),
        compiler_params=_cparams(("arbitrary", "arbitrary", "arbitrary")),
        name="na_latent",
    )(nq, nk, nv2, nk, nv2, tab, masks)


def _hfilt_kernel(ft_ref, tc_ref, w1t_ref, b1_ref, w2t_ref, b2_ref, w3_ref, fr_ref, d_ref, k_ref, l1_ref, *, n):
    i = pl.program_id(0)
    fr = fr_ref[...]
    h1 = jnp.sin(fr * (jnp.dot(w1t_ref[...], ft_ref[...], precision=HIGHEST, preferred_element_type=F32)
                       + b1_ref[...]))
    h2 = jnp.sin(fr * (jnp.dot(w2t_ref[...], h1, precision=HIGHEST, preferred_element_type=F32) + b2_ref[...]))
    contract0 = (((0,), (0,)), ((), ()))
    o = lax.dot_general(h2, w3_ref[...], contract0, precision=HIGHEST, preferred_element_type=F32)
    cw = o.shape[1] // 2
    t = o.shape[0]
    row = i * t + lax.broadcasted_iota(jnp.int32, (t, cw), 0)
    k = jnp.where(row < n, o[:, :cw], o[:, cw:]) * jnp.exp(-tc_ref[...] * d_ref[...])
    k = jnp.where(row == n, 0.0, k)
    k_ref[...] = k

    @pl.when(i == 0)
    def _():
        l1_ref[...] = jnp.zeros_like(l1_ref)

    l1_ref[...] += jnp.sum(jnp.abs(k), axis=0, keepdims=True)


def _filter_feats(n):
    rows = np.arange(2 * n)
    pos = np.where(rows <= n, rows, 2 * n - rows).astype(np.float32)
    pos = jnp.asarray(np.minimum(pos, n - 1))
    t = pos / max(n - 1, 1)
    bands = jnp.linspace(1e-4, FILTER_BANDS - 1, FILTER_BANDS, dtype=F32)
    ang = (2.0 * math.pi / n) * pos[None, :] * bands[:, None]
    feats = jnp.concatenate([t[None, :], jnp.cos(ang), -jnp.sin(ang)], axis=0)
    return jnp.pad(feats, ((0, LANES - feats.shape[0]), (0, 0))), t[:, None]


def _hfilter(n, w1, b1, w2, b2, w3, freq, deltas):
    feats_t, t_col = _filter_feats(n)
    w1t = jnp.pad(w1.T, ((0, 0), (0, LANES - w1.shape[0])))
    col = lambda a: a.reshape(-1, 1)
    rows = feats_t.shape[1]
    t = min(1024, rows)
    cw = w3.shape[1] // 2
    const = lambda i: (0, 0)
    full = lambda a: pl.BlockSpec(a.shape, const)
    args = (w1t, col(b1), w2.T, col(b2), w3, col(freq), jnp.abs(deltas))
    return pl.pallas_call(
        functools.partial(_hfilt_kernel, n=n),
        grid=(rows // t,),
        in_specs=[pl.BlockSpec((LANES, t), lambda i: (0, i)), pl.BlockSpec((t, 1), lambda i: (i, 0))]
        + [full(a) for a in args],
        out_specs=[pl.BlockSpec((t, cw), lambda i: (i, 0)), pl.BlockSpec((1, cw), const)],
        out_shape=[jax.ShapeDtypeStruct((rows, cw), F32), jax.ShapeDtypeStruct((1, cw), F32)],
        compiler_params=_cparams(("arbitrary",)),
        name="hyena_filter",
    )(feats_t, t_col, *args)


def _dft_outer_kernel(f_ref, x_ref, o_ref):
    c = x_ref.shape[-1]
    x = x_ref[...].reshape(f_ref.shape[1], c).astype(BF16)
    o_ref[...] = jnp.dot(f_ref[...], x, preferred_element_type=F32).astype(o_ref.dtype).reshape(o_ref.shape)


def _kron_table(fmat, cols):
    return jnp.asarray(np.kron(np.asarray(fmat, np.float32), np.eye(cols, dtype=np.float32)).astype(BF16))


def _outer_cols(dtype):
    return OUTER_COLS * 4 // jnp.dtype(dtype).itemsize


def _dft_outer_fwd(fmat, x3, nb, n1_len, out_dtype):
    k = fmat.shape[1]
    c = x3.shape[2]
    cols = _outer_cols(out_dtype)
    f = _kron_table(fmat, cols)
    return pl.pallas_call(
        _dft_outer_kernel,
        grid=(nb, DFT_N2 // cols),
        in_specs=[pl.BlockSpec(f.shape, lambda b, j: (0, 0), pipeline_mode=pl.Buffered(1)),
                  pl.BlockSpec((k, cols, c), lambda b, j: (b, j, 0))],
        out_specs=pl.BlockSpec((1, 2, n1_len, cols, c), lambda b, j: (b, 0, 0, j, 0)),
        out_shape=jax.ShapeDtypeStruct((nb, 2, n1_len, DFT_N2, c), out_dtype),
        compiler_params=_cparams(("arbitrary", "arbitrary")),
        name="dft_outer_fwd",
    )(f, x3)


def _dft_outer_inv(fmat, d5, out_groups):
    nb, _, n1_len, _, c = d5.shape
    half = fmat.shape[0]
    cols = _outer_cols(d5.dtype)
    f = _kron_table(fmat, cols)
    return pl.pallas_call(
        _dft_outer_kernel,
        grid=(nb, DFT_N2 // cols),
        in_specs=[pl.BlockSpec(f.shape, lambda b, j: (0, 0), pipeline_mode=pl.Buffered(1)),
                  pl.BlockSpec((1, 2, n1_len, cols, c), lambda b, j: (b, 0, 0, j, 0))],
        out_specs=pl.BlockSpec((half, cols, c), lambda b, j: (b, j, 0)),
        out_shape=jax.ShapeDtypeStruct((out_groups, DFT_N2, c), F32),
        compiler_params=_cparams(("arbitrary", "arbitrary")),
        name="dft_outer_inv",
    )(f, d5)


def _inner_matrices(fc, fs, tc, ts):
    gr = fc * tc - fs * ts
    gi = -(fc * ts + fs * tc)
    return gr, gi


def _real_form(gr, gi):
    return jnp.concatenate([jnp.concatenate([gr, -gi], axis=1), jnp.concatenate([gi, gr], axis=1)], axis=0)


def _dft_inner_filter_kernel(fc_ref, fs_ref, tc_ref, ts_ref, b_ref, o_ref):
    fc, fs = fc_ref[...], fs_ref[...]
    n2, c = b_ref.shape[3], b_ref.shape[4]

    def body(kk, carry):
        g = _real_form(*_inner_matrices(fc, fs, tc_ref[kk], ts_ref[kk]))
        o_ref[kk] = jnp.dot(g.astype(BF16), b_ref[0, :, kk].reshape(2 * n2, c).astype(BF16),
                            preferred_element_type=F32)
        return carry

    lax.fori_loop(0, b_ref.shape[2], body, 0, unroll=2)


def _dft_inner_conv_kernel(fc_ref, fs_ref, tc_ref, ts_ref, b_ref, kh_ref, sc_ref, o_ref):
    fc, fs = fc_ref[...], fs_ref[...]
    nb, n2, c = b_ref.shape[0], b_ref.shape[3], b_ref.shape[4]
    scale = sc_ref[...]

    def body(kk, carry):
        gr, gi = _inner_matrices(fc, fs, tc_ref[kk], ts_ref[kk])
        g = _real_form(gr, gi).astype(BF16)
        gh = _real_form(gr.T, -gi.T).astype(BF16)
        kh = kh_ref[kk]
        kr, ki = kh[:n2], kh[n2:]
        for b in range(nb):
            x = jnp.dot(g, b_ref[b, :, kk].reshape(2 * n2, c).astype(BF16), preferred_element_type=F32)
            xr, xi = x[:n2], x[n2:]
            y = jnp.concatenate([(xr * kr - xi * ki) * scale, (xr * ki + xi * kr) * scale], axis=0)
            d = jnp.dot(gh, y.astype(BF16), preferred_element_type=F32)
            o_ref[b, :, kk] = d.astype(o_ref.dtype).reshape(2, n2, c)
        return carry

    lax.fori_loop(0, b_ref.shape[2], body, 0, unroll=2)


def _dft_tables(n1_len):
    n_total = n1_len * DFT_N2
    a = np.arange(DFT_N2)
    ang = 2.0 * np.pi * np.outer(a, a) / DFT_N2
    k1 = np.arange(n1_len)
    tw = 2.0 * np.pi * np.outer(k1, a) / n_total
    o = 2.0 * np.pi * np.outer(k1, k1) / n1_len
    f32 = lambda v: jnp.asarray(v.astype(np.float32))
    outer_fwd = np.concatenate([np.cos(o), -np.sin(o)], axis=0)
    outer_inv = np.concatenate([np.cos(o), -np.sin(o)], axis=1)
    return dict(
        fc=f32(np.cos(ang)), fs=f32(np.sin(ang)),
        tc=f32(np.cos(tw)).reshape(n1_len, 1, DFT_N2), ts=f32(np.sin(tw)).reshape(n1_len, 1, DFT_N2),
        outer_fwd_full=outer_fwd, outer_fwd_half=outer_fwd[:, :n1_len // 2], outer_inv_half=outer_inv[:n1_len // 2],
    )


def _inner_specs():
    sq = pl.BlockSpec((DFT_N2, DFT_N2), lambda k: (0, 0))
    tw = pl.BlockSpec((INNER_K1, 1, DFT_N2), lambda k: (k, 0, 0))
    return [sq, sq, tw, tw]


def _filter_spectrum(kfull, tabs, n1_len):
    c = kfull.shape[1]
    b5 = _dft_outer_fwd(tabs["outer_fwd_full"], kfull.reshape(n1_len, DFT_N2, c), 1, n1_len, F32)
    return pl.pallas_call(
        _dft_inner_filter_kernel,
        grid=(n1_len // INNER_K1,),
        in_specs=_inner_specs() + [pl.BlockSpec((1, 2, INNER_K1, DFT_N2, c), lambda k: (0, 0, k, 0, 0))],
        out_specs=pl.BlockSpec((INNER_K1, 2 * DFT_N2, c), lambda k: (k, 0, 0)),
        out_shape=jax.ShapeDtypeStruct((n1_len, 2 * DFT_N2, c), F32),
        compiler_params=_cparams(("arbitrary",)),
        name="dft_inner_filter",
    )(tabs["fc"], tabs["fs"], tabs["tc"], tabs["ts"], b5)


def _long_conv(z_all, nb, khat, scale, tabs, n1_len):
    c = z_all.shape[1]
    bz = _dft_outer_fwd(tabs["outer_fwd_half"], z_all.reshape(z_all.shape[0] // DFT_N2, DFT_N2, c), nb, n1_len, BF16)
    blk = pl.BlockSpec((nb, 2, INNER_K1, DFT_N2, c), lambda k: (0, 0, k, 0, 0))
    d = pl.pallas_call(
        _dft_inner_conv_kernel,
        grid=(n1_len // INNER_K1,),
        in_specs=_inner_specs() + [blk, pl.BlockSpec((INNER_K1, 2 * DFT_N2, c), lambda k: (k, 0, 0)),
                                   pl.BlockSpec((1, c), lambda k: (0, 0))],
        out_specs=blk,
        out_shape=jax.ShapeDtypeStruct(bz.shape, BF16),
        compiler_params=_cparams(("arbitrary",)),
        name="dft_inner_conv",
    )(tabs["fc"], tabs["fs"], tabs["tc"], tabs["ts"], bz, khat, scale)
    y3 = _dft_outer_inv(tabs["outer_inv_half"], d, z_all.shape[0] // DFT_N2)
    return y3.reshape(z_all.shape)


def _hyena_ctx_kernel(ff_ref, fi_ref, k_ref, l1_ref, z_ref, yconv_hbm_ref, o_ref):
    del yconv_hbm_ref
    ff = ff_ref[...]
    n = z_ref.shape[0]
    big = ff.shape[1]
    kh = jnp.dot(ff, k_ref[...], precision=HIGHEST, preferred_element_type=F32)
    x = jnp.dot(ff[:, :n], z_ref[...], precision=HIGHEST, preferred_element_type=F32)
    kr, ki = kh[:big], kh[big:]
    xr, xi = x[:big], x[big:]
    scale = 1.0 / (l1_ref[...] * big)
    y = jnp.concatenate([(xr * kr - xi * ki) * scale, (xr * ki + xi * kr) * scale], axis=0)
    o_ref[...] = jnp.dot(fi_ref[...], y, precision=HIGHEST, preferred_element_type=F32)


def _hyena_ctx(z, kfull, l1, row0_blk, n_batch, n, yconv):
    big = 2 * n
    a = np.arange(big)
    ang = 2.0 * np.pi * np.outer(a, a) / big
    ff = jnp.asarray(np.concatenate([np.cos(ang), -np.sin(ang)], axis=0).astype(np.float32))
    fi = jnp.asarray(np.concatenate([np.cos(ang), -np.sin(ang)], axis=1)[:n].astype(np.float32))
    c = z.shape[1]
    const = lambda b: (0, 0)
    return pl.pallas_call(
        _hyena_ctx_kernel,
        grid=(n_batch,),
        in_specs=[pl.BlockSpec(ff.shape, const), pl.BlockSpec(fi.shape, const), pl.BlockSpec(kfull.shape, const),
                  pl.BlockSpec((1, c), const), pl.BlockSpec((n, c), lambda b: (row0_blk + b, 0)),
                  pl.BlockSpec(memory_space=pl.ANY)],
        out_specs=pl.BlockSpec((n, c), lambda b: (row0_blk + b, 0)),
        out_shape=jax.ShapeDtypeStruct(yconv.shape, F32),
        input_output_aliases={5: 0},
        compiler_params=_cparams(("arbitrary",)),
        name="hyena_ctx",
    )(ff, fi, kfull, l1, z, yconv)


def _rope_tables(seq, n_batch, n_ctx_rows):
    t = jnp.arange(seq, dtype=jnp.int32)
    rows = (t // GRID_W).astype(F32)
    cols = (t % GRID_W).astype(F32)
    nf = HEAD_DIM // 4
    inv = ROPE_THETA ** (-jnp.arange(nf, dtype=F32) / nf)
    ar, ac = rows[:, None] * inv, cols[:, None] * inv
    cos = jnp.concatenate([jnp.cos(ar), jnp.cos(ar), jnp.cos(ac), jnp.cos(ac)], axis=1)
    sin = jnp.concatenate([-jnp.sin(ar), jnp.sin(ar), -jnp.sin(ac), jnp.sin(ac)], axis=1)
    cos = jnp.tile(cos, (n_batch, LANES // HEAD_DIM))
    sin = jnp.tile(sin, (n_batch, LANES // HEAD_DIM))
    cos = jnp.concatenate([cos, jnp.ones((n_ctx_rows, LANES), F32)], axis=0)
    sin = jnp.concatenate([sin, jnp.zeros((n_ctx_rows, LANES), F32)], axis=0)
    return cos, sin


def _head_mean_matrix():
    a = np.arange(LANES)
    return jnp.asarray((a[:, None] // HEAD_DIM == a[None, :] // HEAD_DIM).astype(np.float32) / HEAD_DIM, dtype=BF16)


def kernel(x, c, ctx, c_ctx, w_ada, b_ada, g_ffn1, w_ffn1_gate, w_ffn1_up, w_ffn1_down, g_mix, w_in, w_out, g_q_attn, g_k_attn, conv_w, conv_b, filt_w1, filt_b1, filt_w2, filt_b2, filt_w3, filt_freq, hyena_skip, g_q_na, g_k_na, na_rpb, g_ffn2, w_ffn2_gate, w_ffn2_up, w_ffn2_down):
    n_batch, seq, d = x.shape
    n_ctx = ctx.shape[1]
    depth = w_ada.shape[0]
    assert d == _D and n_batch + 1 <= 8 and seq % TOKEN_TILE == 0 and seq % (NA_BLOCK_ROWS * GRID_W) == 0
    assert n_ctx == 256 and (n_batch * n_ctx) % TOKEN_TILE == 0
    n_lat = n_batch * seq
    n_all = n_lat + n_batch * n_ctx
    tiles_per_batch = seq // TOKEN_TILE
    hy_w = conv_w.shape[2] // 3
    n1_len = 2 * seq // DFT_N2

    cond8 = jnp.zeros((8, d), F32).at[:n_batch].set(c).at[n_batch].set(c_ctx)
    mods = _adaln_all(cond8, w_ada, b_ada).reshape(depth * 8 * N_MOD, 1, d)
    h = jnp.concatenate([x.reshape(n_lat, d), ctx.reshape(n_batch * n_ctx, d)], axis=0)

    cos_t, sin_t = _rope_tables(seq, n_batch, n_batch * n_ctx)
    headmean = _head_mean_matrix()
    ffn1_w = tuple(w.astype(BF16) for w in (w_ffn1_gate, w_ffn1_up, w_ffn1_down))
    ffn2_w = tuple(w.astype(BF16) for w in (w_ffn2_gate, w_ffn2_up, w_ffn2_down))
    na_masks = _na_row_masks(seq // GRID_W)
    na_tabs = jax.vmap(lambda r: _na_bias_tables(r, seq // GRID_W))(na_rpb)
    na_tabs = na_tabs.reshape((depth * 3,) + na_tabs.shape[2:])
    tabs = _dft_tables(n1_len)
    deltas = jnp.linspace(math.log(DECAY_TARGET) / SLOW_DECAY_PCT, math.log(DECAY_TARGET) / FAST_DECAY_PCT,
                          hy_w, dtype=F32).reshape(1, hy_w)

    q_heads = [hd for pair in range(ATTN_HEADS // 2) for hd in (pair, pair + ATTN_HEADS // 2)]
    q_segs = [(hd * HEAD_DIM, (hd + 1) * HEAD_DIM) for hd in q_heads]
    in_segs = q_segs + [(512, 640), (1536, 1792), (1792, 2048), (640, 768), (2048, 2304), (768, 1536)]

    for l in range(depth):
        last = l == depth - 1
        base = l * 8 * N_MOD
        rep = lambda g, k: jnp.tile(g, k)
        gvec = jnp.concatenate([rep(g_q_attn[l], ATTN_HEADS), rep(g_k_attn[l], ATTN_KV_HEADS),
                                rep(g_q_na[l], NA_HEADS), rep(g_k_na[l], NA_HEADS)]).reshape(1, N_NORM)

        h = _ffn(h, mods, base, (0, 1, 2), g_ffn1[l].reshape(1, d), l, *ffn1_w, n_all, tiles_per_batch, n_batch)

        w_in_l = jnp.concatenate([w_in[l][:, a:b] for a, b in in_segs], axis=1).astype(BF16)
        q, k, nq, nk, v, nv, z, x0 = _inproj(h, mods, base, g_mix[l].reshape(1, d), w_in_l, gvec, headmean, cos_t, sin_t,
                                             conv_w[l], conv_b[l].reshape(1, -1), tiles_per_batch, n_batch, seq, n_ctx)

        out_rows = n_lat if last else n_all
        ya = _gqa(q, k, v, out_rows, n_batch, seq, n_ctx, latent=True)
        yc = _na_latent(nq, nk, nv, na_tabs, 3 * l, na_masks, out_rows, n_batch, seq, n_ctx)

        fargs = (filt_w1[l], filt_b1[l].reshape(1, -1), filt_w2[l], filt_b2[l].reshape(1, -1), filt_w3[l],
                 filt_freq[l].reshape(1, -1), deltas)
        kfull, l1 = _hfilter(seq, *fargs)
        khat = _filter_spectrum(kfull, tabs, n1_len)
        yconv = _long_conv(z, n_batch, khat, 1.0 / (l1 * (2 * seq)), tabs, n1_len)

        if not last:
            ya_c = _gqa(q, k, v, n_all, n_batch, seq, n_ctx, latent=False)
            yc_c = _gqa(nq, nk, nv, n_all, n_batch, seq, n_ctx, latent=False, kv_chunked=True)
            ya = lax.dynamic_update_slice(ya, ya_c[n_lat:], (n_lat, 0))
            yc = lax.dynamic_update_slice(yc, yc_c[n_lat:], (n_lat, 0))
            kfull_c, l1_c = _hfilter(n_ctx, *fargs)
            yconv = _hyena_ctx(z, kfull_c, l1_c, n_lat // n_ctx, n_batch, n_ctx, yconv)

        w_o = w_out[l]
        wa = jnp.concatenate([w_o[a:b] for a, b in q_segs], axis=0).astype(BF16)
        wb = w_o[512:512 + hy_w].astype(BF16)
        wc = w_o[512 + hy_w:].astype(BF16)
        h = _ffn(h, mods, base, (6, 7, 8), g_ffn2[l].reshape(1, d), l, *ffn2_w, out_rows, tiles_per_batch, n_batch,
                 mixer=(ya, yconv, z, x0, hyena_skip[l].reshape(1, hy_w), yc, wa, wb, wc))

    return h[:n_lat].reshape(n_batch, seq, d)
```

```python
import functools
import math

import numpy as np
import jax
import jax.numpy as jnp
from jax import lax
from jax.experimental import pallas as pl
from jax.experimental.pallas import tpu as pltpu

F32 = jnp.float32
BF16 = jnp.bfloat16
HIGHEST = lax.Precision.HIGHEST

HEAD_DIM = 64
GRID_W = 64
ATTN_HEADS = 8
ATTN_KV_HEADS = 2
NA_HEADS = 4
NA_ROWS = 8
NA_COLS = 16
FILTER_BANDS = 16
DECAY_TARGET = 1e-2
FAST_DECAY_PCT = 0.3
SLOW_DECAY_PCT = 1.5
ROPE_THETA = 10000.0
EPS = 1e-6
N_MOD = 9

LANES = 128
TOKEN_TILE = 512
FFN_CHUNK = 256
ATTN_TQ = 1024
ATTN_TK = 2048
NA_BLOCK_ROWS = 8
NA_KEY_ROWS = 16
NA_SUB_ROWS = 4
DFT_N2 = 128
OUTER_COLS = 8
INNER_K1 = 8
NEG_BIG = -1e30
VMEM_LIMIT = 56 * 1024 * 1024


def _cparams(sem):
    return pltpu.CompilerParams(dimension_semantics=sem, vmem_limit_bytes=VMEM_LIMIT)


def _ada_kernel(c_ref, w_ref, b_ref, o_ref):
    c = c_ref[...]
    s = c * jax.nn.sigmoid(c)
    o_ref[0] = jnp.dot(s, w_ref[0], preferred_element_type=F32) + b_ref[0]


def _adaln_all(cond8, w_ada, b_ada):
    depth, d, nd = w_ada.shape
    tn = 2304
    return pl.pallas_call(
        _ada_kernel,
        grid=(depth, nd // tn),
        in_specs=[
            pl.BlockSpec((8, d), lambda l, j: (0, 0)),
            pl.BlockSpec((1, d, tn), lambda l, j: (l, 0, j)),
            pl.BlockSpec((1, 1, tn), lambda l, j: (l, 0, j)),
        ],
        out_specs=pl.BlockSpec((1, 8, tn), lambda l, j: (l, 0, j)),
        out_shape=jax.ShapeDtypeStruct((depth, 8, nd), F32),
        compiler_params=_cparams(("arbitrary", "arbitrary")),
        name="adaln",
    )(cond8, w_ada, b_ada.reshape(depth, 1, nd))


def _norm_mod(x, g, sh, sc):
    ms = jnp.mean(x * x, axis=-1, keepdims=True)
    y = x * lax.rsqrt(ms + EPS) * g
    return y * (1.0 + sc) + sh


def _mod_specs(tiles_per_batch, n_batch, base, idxs):
    def spec(which):
        return pl.BlockSpec(
            (1, 1, _D),
            lambda i, w=which: (base + jnp.minimum(i // tiles_per_batch, n_batch) * N_MOD + w, 0, 0))
    return [spec(w) for w in idxs]


_D = 1024


def _ffn_kernel(h_ref, *refs, with_mixer, split_tile):
    x = h_ref[...]
    if split_tile is not None:
        x = jnp.where(pl.program_id(0) < split_tile, x, refs[0][...])
        refs = refs[1:]
    if with_mixer:
        gt2_ref, ya_ref, yconv_ref, z_ref, x0_ref, skip_ref, yc_ref, wa_ref, wb_ref, wc_ref = refs[:10]
        refs = refs[10:]
        yb = ((yconv_ref[...] + z_ref[...] * skip_ref[...]) * x0_ref[...]).astype(BF16)
        x = x + gt2_ref[0] * (jnp.dot(ya_ref[...], wa_ref[...], preferred_element_type=F32)
                              + jnp.dot(yb, wb_ref[...], preferred_element_type=F32)
                              + jnp.dot(yc_ref[...], wc_ref[...], preferred_element_type=F32))
    sh_ref, sc_ref, gt_ref, g_ref, wg_ref, wu_ref, wd_ref, o_ref, xb_ref, acc_ref = refs
    x_is_new = with_mixer or split_tile is not None
    if x_is_new:
        o_ref[...] = x
    xb_ref[...] = _norm_mod(x, g_ref[...], sh_ref[0], sc_ref[0]).astype(BF16)

    for k in range(wg_ref.shape[2] // FFN_CHUNK):
        cols = slice(k * FFN_CHUNK, (k + 1) * FFN_CHUNK)
        xb = xb_ref[...]
        a = jnp.dot(xb, wg_ref[0, :, cols], preferred_element_type=F32)
        u = jnp.dot(xb, wu_ref[0, :, cols], preferred_element_type=F32)
        act = (a * jax.nn.sigmoid(a) * u).astype(BF16)
        part = jnp.dot(act, wd_ref[0, cols, :], preferred_element_type=F32)
        if k == 0:
            acc_ref[...] = part
        else:
            acc_ref[...] += part
    resid = o_ref[...] if x_is_new else h_ref[...]
    o_ref[...] = resid + 0.5 * gt_ref[0] * acc_ref[...]


def _ffn(h, mods, mod_base, which, g, layer, wg, wu, wd, n_rows, tiles_per_batch, n_batch, mixer=None, h_tail=None):
    d = h.shape[1]
    of_layer = lambda a: pl.BlockSpec((1,) + a.shape[1:], lambda i: (layer, 0, 0), pipeline_mode=pl.Buffered(1))
    tm = TOKEN_TILE
    row = lambda i: (i, 0)
    const = lambda i: (0, 0)
    rows = lambda a: pl.BlockSpec((tm, a.shape[1]), row)
    full = lambda a: pl.BlockSpec(a.shape, const)
    once = lambda a: pl.BlockSpec(a.shape, const, pipeline_mode=pl.Buffered(1))
    mix_specs, mix_args = [], ()
    if mixer is not None:
        ya, yconv, z, x0, skip, yc, wa, wb, wc = mixer
        mix_specs = _mod_specs(tiles_per_batch, n_batch, mod_base, (5,)) + [
            rows(ya), rows(yconv), rows(z), rows(x0), full(skip), rows(yc), once(wa), once(wb), once(wc)]
        mix_args = (mods,) + tuple(mixer)
    h_specs, h_args, split = [rows(h)], (h,), None
    if h_tail is not None:
        split = h.shape[0] // tm
        h_specs = [pl.BlockSpec((tm, d), lambda i: (jnp.minimum(i, split - 1), 0)),
                   pl.BlockSpec((tm, d), lambda i: (jnp.maximum(i - split, 0), 0))]
        h_args = (h, h_tail)
    return pl.pallas_call(
        functools.partial(_ffn_kernel, with_mixer=mixer is not None, split_tile=split),
        grid=(n_rows // tm,),
        in_specs=h_specs + mix_specs + _mod_specs(tiles_per_batch, n_batch, mod_base, which)
        + [full(g), of_layer(wg), of_layer(wu), of_layer(wd)],
        out_specs=pl.BlockSpec((tm, d), row),
        out_shape=jax.ShapeDtypeStruct((n_rows, d), F32),
        scratch_shapes=[pltpu.VMEM((tm, d), BF16), pltpu.VMEM((tm, d), F32)],
        compiler_params=_cparams(("arbitrary",)),
        name="ffn_mix" if mixer is not None else "ffn",
    )(*h_args, *mix_args, mods, mods, mods, g, wg, wu, wd)


N_NORM = 1152
N_ROPE = 640
ATTN_Q_CHUNKS = (0, 1, 2, 3)
NA_Q_CHUNKS = (5, 6)
LOG2E = math.log2(math.e)


def _inproj_kernel(h_ref, prev_ref, next_ref, sh_ref, sc_ref, g_ref, w_ref, gv_ref, hm_ref, cos_ref, sin_ref,
                   cw_ref, cb_ref, q_ref, k_ref, nq_ref, nk_ref, v_ref, nv_ref, z_ref, x0_ref, *, seq, n_ctx, n_lat):
    xb = _norm_mod(h_ref[...], g_ref[...], sh_ref[0], sc_ref[0]).astype(BF16)
    hm = hm_ref[...]
    cosv = cos_ref[...]
    sinv = sin_ref[...]
    lane = lax.broadcasted_iota(jnp.int32, cosv.shape, 1)
    first_half = (lane % 32) < 16
    outs = []
    p_norm = jnp.dot(xb, w_ref[:, 0:N_NORM], preferred_element_type=F32)
    for c in range(N_NORM // LANES):
        p = p_norm[:, c * LANES:(c + 1) * LANES]
        ms = jnp.dot((p * p).astype(BF16), hm, preferred_element_type=F32)
        y = p * lax.rsqrt(ms + EPS) * gv_ref[:, c * LANES:(c + 1) * LANES]
        if c < N_ROPE // LANES:
            swapped = jnp.where(first_half, pltpu.roll(y, LANES - 16, 1), pltpu.roll(y, 16, 1))
            y = y * cosv + swapped * sinv
        if c in ATTN_Q_CHUNKS or c in NA_Q_CHUNKS:
            y = y * (HEAD_DIM ** -0.5 * LOG2E)
        outs.append(y.astype(BF16))
    for c in range(4):
        q_ref[:, c * LANES:(c + 1) * LANES] = outs[c]
    k_ref[...] = outs[4]
    nq_ref[:, 0:LANES] = outs[5]
    nq_ref[:, LANES:2 * LANES] = outs[6]
    nk_ref[:, 0:LANES] = outs[7]
    nk_ref[:, LANES:2 * LANES] = outs[8]
    p_rest = jnp.dot(xb, w_ref[:, N_NORM:], preferred_element_type=F32)
    v = p_rest[:, 0:LANES]
    v_ref[:, 0:LANES] = jnp.where(lane < HEAD_DIM, v, 1.0).astype(BF16)
    v_ref[:, LANES:2 * LANES] = jnp.where(lane < HEAD_DIM, 1.0, v).astype(BF16)
    for c in range(NA_HEADS // 2):
        nv = p_rest[:, (1 + c) * LANES:(2 + c) * LANES]
        nv_ref[:, 2 * c * LANES:(2 * c + 1) * LANES] = jnp.where(lane < HEAD_DIM, nv, 1.0).astype(BF16)
        nv_ref[:, (2 * c + 1) * LANES:(2 * c + 2) * LANES] = jnp.where(lane < HEAD_DIM, 1.0, nv).astype(BF16)

    hy = p_rest[:, 3 * LANES:]
    tm = hy.shape[0]
    halo = jnp.concatenate([prev_ref[...], next_ref[...]], axis=0)
    halo_b = _norm_mod(halo, g_ref[...], sh_ref[0], sc_ref[0]).astype(BF16)
    hy_halo = jnp.dot(halo_b, w_ref[:, N_NORM + 3 * LANES:], preferred_element_type=F32)
    row0 = pl.program_id(0) * tm
    is_ctx = row0 >= n_lat
    period = jnp.where(is_ctx, n_ctx, seq)
    local = lax.broadcasted_iota(jnp.int32, (tm, 1), 0)
    pos = jnp.bitwise_and(row0 + local - jnp.where(is_ctx, n_lat, 0), period - 1)
    up = jnp.where(local == 0, hy_halo[7:8, :], pltpu.roll(hy, 1, 0))
    up = jnp.where(pos == 0, 0.0, up)
    dn = jnp.where(local == tm - 1, hy_halo[8:9, :], pltpu.roll(hy, tm - 1, 0))
    dn = jnp.where(pos == period - 1, 0.0, dn)
    uc = up * cw_ref[0:1, :] + hy * cw_ref[1:2, :] + dn * cw_ref[2:3, :] + cb_ref[...]
    cw = uc.shape[1] // 3
    x0_ref[...] = uc[:, :cw]
    z_ref[...] = uc[:, 2 * cw:] * uc[:, cw:2 * cw]


def _inproj(h, mods, mod_base, g, w, gvec, headmean, cos_t, sin_t, conv_w, conv_b, tiles_per_batch, n_batch, seq,
            n_ctx):
    n_rows, d = h.shape
    tm = TOKEN_TILE
    assert seq & (seq - 1) == 0 and n_ctx & (n_ctx - 1) == 0
    row = lambda i: (i, 0)
    const = lambda i: (0, 0)
    widths = (512, 128, 256, 256, 256, 512)
    hy_w = conv_w.shape[1] // 3
    out_shape = [jax.ShapeDtypeStruct((n_rows, wd), BF16) for wd in widths]
    out_shape += [jax.ShapeDtypeStruct((n_rows, hy_w), F32)] * 2
    out_specs = [pl.BlockSpec((tm, wd), row) for wd in widths] + [pl.BlockSpec((tm, hy_w), row)] * 2
    return pl.pallas_call(
        functools.partial(_inproj_kernel, seq=seq, n_ctx=n_ctx, n_lat=n_batch * seq),
        grid=(n_rows // tm,),
        in_specs=[pl.BlockSpec((tm, d), row),
                  pl.BlockSpec((8, d), lambda i: (jnp.maximum(i * (tm // 8) - 1, 0), 0)),
                  pl.BlockSpec((8, d), lambda i: (jnp.minimum((i + 1) * (tm // 8), n_rows // 8 - 1), 0))]
        + _mod_specs(tiles_per_batch, n_batch, mod_base, (3, 4))
        + [
            pl.BlockSpec((1, d), const),
            pl.BlockSpec(w.shape, const, pipeline_mode=pl.Buffered(1)),
            pl.BlockSpec(gvec.shape, const),
            pl.BlockSpec(headmean.shape, const),
            pl.BlockSpec((tm, LANES), row),
            pl.BlockSpec((tm, LANES), row),
            pl.BlockSpec(conv_w.shape, const),
            pl.BlockSpec(conv_b.shape, const),
        ],
        out_specs=out_specs,
        out_shape=out_shape,
        compiler_params=_cparams(("arbitrary",)),
        name="inproj",
    )(h, h, h, mods, mods, g, w, gvec, headmean, cos_t, sin_t, conv_w, conv_b)


def _stack_heads(q):
    lane = lax.broadcasted_iota(jnp.int32, q.shape, 1)
    zero = jnp.zeros_like(q)
    return jnp.concatenate([jnp.where(lane < HEAD_DIM, q, zero), jnp.where(lane >= HEAD_DIM, q, zero)], axis=0)


def _qk(q2, k):
    return lax.dot_general(q2, k, (((1,), (1,)), ((), ())), preferred_element_type=F32)


def _gqa_kernel(q_ref, kc_ref, vc_ref, *rest, n_latent_chunks, tk):
    if n_latent_chunks:
        kl_ref, vl_ref, o_ref = rest
    else:
        (o_ref,) = rest
    tq = q_ref.shape[0]
    q2 = _stack_heads(q_ref[...])

    def pv(p, v_ref, rows):
        return jnp.concatenate(
            [jnp.dot(p[:tq], v_ref[rows, 0:LANES], preferred_element_type=F32),
             jnp.dot(p[tq:], v_ref[rows, LANES:2 * LANES], preferred_element_type=F32)], axis=0)

    s = _qk(q2, kc_ref[...])
    m = jnp.max(s, axis=-1, keepdims=True)
    acc = pv(jnp.exp2((s - m).astype(BF16)), vc_ref, slice(None))

    if n_latent_chunks:
        def body(j, carry):
            m, acc = carry
            rows = pl.ds(pl.multiple_of(j * tk, tk), tk)
            s = _qk(q2, kl_ref[rows, :])
            m_new = jnp.maximum(m, jnp.max(s, axis=-1, keepdims=True))
            p = jnp.exp2((s - m_new).astype(BF16))
            return m_new, jnp.exp2(m - m_new) * acc + pv(p, vl_ref, rows)

        m, acc = lax.fori_loop(0, n_latent_chunks, body, (m, acc), unroll=True)
    lane = lax.broadcasted_iota(jnp.int32, (tq, LANES), 1)
    num = jnp.where(lane < HEAD_DIM, acc[:tq], acc[tq:])
    den = pltpu.roll(jnp.where(lane < HEAD_DIM, acc[tq:], acc[:tq]), HEAD_DIM, 1)
    o_ref[...] = (num / den).astype(o_ref.dtype)


def _gqa(q, k, v2, out_rows, n_batch, seq, n_ctx, latent, kv_chunked=False):
    chunks = q.shape[1] // LANES
    ctx_blk0 = n_batch * seq // n_ctx
    tk = min(ATTN_TK, seq)
    kj = (lambda j: j) if kv_chunked else (lambda j: 0)
    if latent:
        tq = ATTN_TQ
        grid = (n_batch, chunks, seq // tq)
        qmap = lambda b, j, i: (b * (seq // tq) + i, j)
        kv_specs = [pl.BlockSpec((seq, LANES), lambda b, j, i: (b, kj(j))),
                    pl.BlockSpec((seq, 2 * LANES), lambda b, j, i: (b, kj(j)))]
        kv_args = (k, v2)
    else:
        tq = n_ctx
        grid = (n_batch, chunks, 1)
        qmap = lambda b, j, i: (ctx_blk0 + b, j)
        kv_specs, kv_args = [], ()
    return pl.pallas_call(
        functools.partial(_gqa_kernel, n_latent_chunks=seq // tk if latent else 0, tk=tk),
        grid=grid,
        in_specs=[pl.BlockSpec((tq, LANES), qmap),
                  pl.BlockSpec((n_ctx, LANES), lambda b, j, i: (ctx_blk0 + b, kj(j))),
                  pl.BlockSpec((n_ctx, 2 * LANES), lambda b, j, i: (ctx_blk0 + b, kj(j)))] + kv_specs,
        out_specs=pl.BlockSpec((tq, LANES), qmap),
        out_shape=jax.ShapeDtypeStruct((out_rows, q.shape[1]), BF16),
        compiler_params=_cparams(("arbitrary", "arbitrary", "arbitrary")),
        name="attn_latent" if latent else "attn_ctx",
    )(q, k, v2, *kv_args)


def _na_kernel(q_ref, kc_ref, vc_ref, kl_ref, vl_ref, tab_ref, mask_ref, o_ref, *, grid_rows):
    i = pl.program_id(2)
    nk = NA_KEY_ROWS * GRID_W
    ks = jnp.clip(i * NA_BLOCK_ROWS - NA_ROWS // 2, 0, grid_rows - NA_KEY_ROWS)
    keys = pl.ds(pl.multiple_of(ks * GRID_W, GRID_W), nk)
    shift = ks - i * NA_BLOCK_ROWS + NA_ROWS
    k = kl_ref[keys, :]
    kc = kc_ref[...]
    sub_rows = NA_SUB_ROWS * GRID_W
    lane = lax.broadcasted_iota(jnp.int32, (sub_rows, LANES), 1)
    low = lane < HEAD_DIM
    for sub in range(NA_BLOCK_ROWS // NA_SUB_ROWS):
        rows = slice(sub * sub_rows, (sub + 1) * sub_rows)
        q = q_ref[rows, :]
        acc = []
        for hh in range(2):
            qh = jnp.where(low if hh == 0 else jnp.logical_not(low), q, jnp.zeros_like(q))
            slabs = []
            for qr in range(sub * NA_SUB_ROWS, (sub + 1) * NA_SUB_ROWS):
                pairs = [tab_ref[0, hh, shift + (NA_BLOCK_ROWS - qr + 2 * kp)] for kp in range(NA_KEY_ROWS // 2)]
                slabs.append(jnp.concatenate(pairs, axis=1) + mask_ref[0, qr:qr + 1, :])
            s_nb = _qk(qh, k) + jnp.concatenate(slabs, axis=0)
            s_cx = _qk(qh, kc)
            m = jnp.maximum(jnp.max(s_nb, axis=-1, keepdims=True), jnp.max(s_cx, axis=-1, keepdims=True))
            vcols = slice(hh * LANES, (hh + 1) * LANES)
            acc.append(jnp.dot(jnp.exp2((s_nb - m).astype(BF16)), vl_ref[keys, vcols], preferred_element_type=F32)
                       + jnp.dot(jnp.exp2((s_cx - m).astype(BF16)), vc_ref[:, vcols], preferred_element_type=F32))
        num = jnp.where(low, acc[0], acc[1])
        den = pltpu.roll(jnp.where(low, acc[1], acc[0]), HEAD_DIM, 1)
        o_ref[rows, :] = (num / den).astype(o_ref.dtype)


def _na_block_start(blk, grid_rows):
    return int(np.clip(blk * NA_BLOCK_ROWS - NA_ROWS // 2, 0, grid_rows - NA_KEY_ROWS))


def _na_variant_blocks(grid_rows):
    nblk = grid_rows // NA_BLOCK_ROWS
    return (0, min(1, nblk - 1), nblk - 1)


def _na_bias_tables(rpb):
    qc = np.arange(GRID_W)[:, None]
    kc = np.arange(GRID_W)[None, :]
    cs = np.clip(qc - NA_COLS // 2, 0, GRID_W - NA_COLS)
    col_ok = (kc >= cs) & (kc < cs + NA_COLS)
    col_hot = ((kc - qc + NA_COLS - 1)[:, :, None] == np.arange(2 * NA_COLS - 1)) & col_ok[:, :, None]
    by_col = jnp.einsum("abe,hde->hdab", jnp.asarray(col_hot, F32), rpb, precision=HIGHEST)
    by_col = jnp.where(jnp.asarray(col_ok)[None, None], by_col * LOG2E, NEG_BIG)
    front = NA_BLOCK_ROWS + 1
    back = NA_BLOCK_ROWS + NA_KEY_ROWS + NA_ROWS - front - (2 * NA_ROWS - 1)
    t = jnp.pad(by_col, ((0, 0), (front, back), (0, 0), (0, 0)))
    return jnp.concatenate([t[:, :-1], t[:, 1:]], axis=-1)


def _na_row_masks(grid_rows):
    masks = []
    for blk in _na_variant_blocks(grid_rows):
        qr = blk * NA_BLOCK_ROWS + np.arange(NA_BLOCK_ROWS)[:, None]
        kr = _na_block_start(blk, grid_rows) + np.arange(NA_KEY_ROWS)[None, :]
        rs = np.clip(qr - NA_ROWS // 2, 0, grid_rows - NA_ROWS)
        row_ok = (kr >= rs) & (kr < rs + NA_ROWS)
        masks.append(np.repeat(np.where(row_ok, 0.0, NEG_BIG), GRID_W, axis=1))
    return jnp.asarray(np.stack(masks).astype(np.float32))


def _na_latent(nq, nk, nv2, tab, layer, masks, n_rows_out, n_batch, seq, n_ctx):
    chunks = nq.shape[1] // LANES
    tq = NA_BLOCK_ROWS * GRID_W
    nblk = seq // tq
    ctx_blk0 = n_batch * seq // n_ctx
    qmap = lambda b, j, i: (b * nblk + i, j)
    variant = lambda i: jnp.where(i == 0, 0, jnp.where(i == nblk - 1, 2, 1))
    tab_spec = pl.BlockSpec((1, 2) + tab.shape[2:], lambda b, j, i: (layer, j, 0, 0, 0))
    mask_spec = pl.BlockSpec((1,) + masks.shape[1:], lambda b, j, i: (variant(i), 0, 0))
    return pl.pallas_call(
        functools.partial(_na_kernel, grid_rows=seq // GRID_W),
        grid=(n_batch, chunks, nblk),
        in_specs=[pl.BlockSpec((tq, LANES), qmap),
                  pl.BlockSpec((n_ctx, LANES), lambda b, j, i: (ctx_blk0 + b, j)),
                  pl.BlockSpec((n_ctx, 2 * LANES), lambda b, j, i: (ctx_blk0 + b, j)),
                  pl.BlockSpec((seq, LANES), lambda b, j, i: (b, j)),
                  pl.BlockSpec((seq, 2 * LANES), lambda b, j, i: (b, j)),
                  tab_spec, mask_spec],
        out_specs=pl.BlockSpec((tq, LANES), qmap),
        out_shape=jax.ShapeDtypeStruct((n_rows_out, nq.shape[1]), BF16),
        compiler_params=_cparams(("arbitrary", "arbitrary", "arbitrary")),
        name="na_latent",
    )(nq, nk, nv2, nk, nv2, tab, masks)


def _hfilt_kernel(ft_ref, tc_ref, w1t_ref, b1_ref, w2t_ref, b2_ref, w3_ref, fr_ref, d_ref, k_ref, l1_ref, *, n):
    i = pl.program_id(0)
    fr = fr_ref[...]
    h1 = jnp.sin(fr * (jnp.dot(w1t_ref[...], ft_ref[...], precision=HIGHEST, preferred_element_type=F32)
                       + b1_ref[...]))
    h2 = jnp.sin(fr * (jnp.dot(w2t_ref[...], h1, precision=HIGHEST, preferred_element_type=F32) + b2_ref[...]))
    contract0 = (((0,), (0,)), ((), ()))
    o = lax.dot_general(h2, w3_ref[...], contract0, precision=HIGHEST, preferred_element_type=F32)
    cw = o.shape[1] // 2
    t = o.shape[0]
    row = i * t + lax.broadcasted_iota(jnp.int32, (t, cw), 0)
    k = jnp.where(row < n, o[:, :cw], o[:, cw:]) * jnp.exp(-tc_ref[...] * d_ref[...])
    k = jnp.where(row == n, 0.0, k)
    k_ref[...] = k

    @pl.when(i == 0)
    def _():
        l1_ref[...] = jnp.zeros_like(l1_ref)

    l1_ref[...] += jnp.sum(jnp.abs(k), axis=0, keepdims=True)


def _filter_feats(n):
    rows = np.arange(2 * n)
    pos = np.where(rows <= n, rows, 2 * n - rows).astype(np.float32)
    pos = jnp.asarray(np.minimum(pos, n - 1))
    t = pos / max(n - 1, 1)
    bands = jnp.linspace(1e-4, FILTER_BANDS - 1, FILTER_BANDS, dtype=F32)
    ang = (2.0 * math.pi / n) * pos[None, :] * bands[:, None]
    feats = jnp.concatenate([t[None, :], jnp.cos(ang), -jnp.sin(ang)], axis=0)
    return jnp.pad(feats, ((0, LANES - feats.shape[0]), (0, 0))), t[:, None]


def _hfilter(n, w1, b1, w2, b2, w3, freq, deltas):
    feats_t, t_col = _filter_feats(n)
    w1t = jnp.pad(w1.T, ((0, 0), (0, LANES - w1.shape[0])))
    col = lambda a: a.reshape(-1, 1)
    rows = feats_t.shape[1]
    t = min(1024, rows)
    cw = w3.shape[1] // 2
    const = lambda i: (0, 0)
    full = lambda a: pl.BlockSpec(a.shape, const)
    args = (w1t, col(b1), w2.T, col(b2), w3, col(freq), jnp.abs(deltas))
    return pl.pallas_call(
        functools.partial(_hfilt_kernel, n=n),
        grid=(rows // t,),
        in_specs=[pl.BlockSpec((LANES, t), lambda i: (0, i)), pl.BlockSpec((t, 1), lambda i: (i, 0))]
        + [full(a) for a in args],
        out_specs=[pl.BlockSpec((t, cw), lambda i: (i, 0)), pl.BlockSpec((1, cw), const)],
        out_shape=[jax.ShapeDtypeStruct((rows, cw), F32), jax.ShapeDtypeStruct((1, cw), F32)],
        compiler_params=_cparams(("arbitrary",)),
        name="hyena_filter",
    )(feats_t, t_col, *args)


def _dft_outer_kernel(f_ref, x_ref, o_ref):
    c = x_ref.shape[-1]
    x = x_ref[...].reshape(f_ref.shape[1], c).astype(BF16)
    o_ref[...] = jnp.dot(f_ref[...], x, preferred_element_type=F32).astype(o_ref.dtype).reshape(o_ref.shape)


def _kron_table(fmat, cols):
    return jnp.asarray(np.kron(np.asarray(fmat, np.float32), np.eye(cols, dtype=np.float32)).astype(BF16))


def _outer_cols(dtype):
    return OUTER_COLS * 4 // jnp.dtype(dtype).itemsize


def _dft_outer_fwd(fmat, x3, nb, n1_len, out_dtype):
    k = fmat.shape[1]
    c = x3.shape[2]
    cols = _outer_cols(out_dtype)
    f = _kron_table(fmat, cols)
    return pl.pallas_call(
        _dft_outer_kernel,
        grid=(nb, DFT_N2 // cols),
        in_specs=[pl.BlockSpec(f.shape, lambda b, j: (0, 0), pipeline_mode=pl.Buffered(1)),
                  pl.BlockSpec((k, cols, c), lambda b, j: (b, j, 0))],
        out_specs=pl.BlockSpec((1, 2, n1_len, cols, c), lambda b, j: (b, 0, 0, j, 0)),
        out_shape=jax.ShapeDtypeStruct((nb, 2, n1_len, DFT_N2, c), out_dtype),
        compiler_params=_cparams(("arbitrary", "arbitrary")),
        name="dft_outer_fwd",
    )(f, x3)


def _dft_outer_inv(fmat, d5, out_groups):
    nb, _, n1_len, _, c = d5.shape
    half = fmat.shape[0]
    cols = _outer_cols(d5.dtype)
    f = _kron_table(fmat, cols)
    return pl.pallas_call(
        _dft_outer_kernel,
        grid=(nb, DFT_N2 // cols),
        in_specs=[pl.BlockSpec(f.shape, lambda b, j: (0, 0), pipeline_mode=pl.Buffered(1)),
                  pl.BlockSpec((1, 2, n1_len, cols, c), lambda b, j: (b, 0, 0, j, 0))],
        out_specs=pl.BlockSpec((half, cols, c), lambda b, j: (b, j, 0)),
        out_shape=jax.ShapeDtypeStruct((out_groups, DFT_N2, c), F32),
        compiler_params=_cparams(("arbitrary", "arbitrary")),
        name="dft_outer_inv",
    )(f, d5)


def _inner_matrices(fc, fs, tc, ts):
    gr = fc * tc - fs * ts
    gi = -(fc * ts + fs * tc)
    return gr, gi


def _real_form(gr, gi):
    return jnp.concatenate([jnp.concatenate([gr, -gi], axis=1), jnp.concatenate([gi, gr], axis=1)], axis=0)


def _dft_inner_filter_kernel(fc_ref, fs_ref, tc_ref, ts_ref, b_ref, o_ref):
    fc, fs = fc_ref[...], fs_ref[...]
    n2, c = b_ref.shape[3], b_ref.shape[4]

    def body(kk, carry):
        g = _real_form(*_inner_matrices(fc, fs, tc_ref[kk], ts_ref[kk]))
        o_ref[kk] = jnp.dot(g.astype(BF16), b_ref[0, :, kk].reshape(2 * n2, c).astype(BF16),
                            preferred_element_type=F32)
        return carry

    lax.fori_loop(0, b_ref.shape[2], body, 0, unroll=2)


def _dft_inner_conv_kernel(fc_ref, fs_ref, tc_ref, ts_ref, b_ref, kh_ref, sc_ref, o_ref):
    fc, fs = fc_ref[...], fs_ref[...]
    nb, n2, c = b_ref.shape[0], b_ref.shape[3], b_ref.shape[4]
    scale = sc_ref[...]

    def body(kk, carry):
        gr, gi = _inner_matrices(fc, fs, tc_ref[kk], ts_ref[kk])
        g = _real_form(gr, gi).astype(BF16)
        gh = _real_form(gr.T, -gi.T).astype(BF16)
        kh = kh_ref[kk]
        kr, ki = kh[:n2], kh[n2:]
        for b in range(nb):
            x = jnp.dot(g, b_ref[b, :, kk].reshape(2 * n2, c).astype(BF16), preferred_element_type=F32)
            xr, xi = x[:n2], x[n2:]
            y = jnp.concatenate([(xr * kr - xi * ki) * scale, (xr * ki + xi * kr) * scale], axis=0)
            d = jnp.dot(gh, y.astype(BF16), preferred_element_type=F32)
            o_ref[b, :, kk] = d.astype(o_ref.dtype).reshape(2, n2, c)
        return carry

    lax.fori_loop(0, b_ref.shape[2], body, 0, unroll=2)


def _dft_tables(n1_len):
    n_total = n1_len * DFT_N2
    a = np.arange(DFT_N2)
    ang = 2.0 * np.pi * np.outer(a, a) / DFT_N2
    k1 = np.arange(n1_len)
    tw = 2.0 * np.pi * np.outer(k1, a) / n_total
    o = 2.0 * np.pi * np.outer(k1, k1) / n1_len
    f32 = lambda v: jnp.asarray(v.astype(np.float32))
    outer_fwd = np.concatenate([np.cos(o), -np.sin(o)], axis=0)
    outer_inv = np.concatenate([np.cos(o), -np.sin(o)], axis=1)
    return dict(
        fc=f32(np.cos(ang)), fs=f32(np.sin(ang)),
        tc=f32(np.cos(tw)).reshape(n1_len, 1, DFT_N2), ts=f32(np.sin(tw)).reshape(n1_len, 1, DFT_N2),
        outer_fwd_full=outer_fwd, outer_fwd_half=outer_fwd[:, :n1_len // 2], outer_inv_half=outer_inv[:n1_len // 2],
    )


def _inner_specs():
    sq = pl.BlockSpec((DFT_N2, DFT_N2), lambda k: (0, 0))
    tw = pl.BlockSpec((INNER_K1, 1, DFT_N2), lambda k: (k, 0, 0))
    return [sq, sq, tw, tw]


def _filter_spectrum(kfull, tabs, n1_len):
    c = kfull.shape[1]
    b5 = _dft_outer_fwd(tabs["outer_fwd_full"], kfull.reshape(n1_len, DFT_N2, c), 1, n1_len, F32)
    return pl.pallas_call(
        _dft_inner_filter_kernel,
        grid=(n1_len // INNER_K1,),
        in_specs=_inner_specs() + [pl.BlockSpec((1, 2, INNER_K1, DFT_N2, c), lambda k: (0, 0, k, 0, 0))],
        out_specs=pl.BlockSpec((INNER_K1, 2 * DFT_N2, c), lambda k: (k, 0, 0)),
        out_shape=jax.ShapeDtypeStruct((n1_len, 2 * DFT_N2, c), F32),
        compiler_params=_cparams(("arbitrary",)),
        name="dft_inner_filter",
    )(tabs["fc"], tabs["fs"], tabs["tc"], tabs["ts"], b5)


def _long_conv(z_all, nb, khat, scale, tabs, n1_len):
    c = z_all.shape[1]
    bz = _dft_outer_fwd(tabs["outer_fwd_half"], z_all.reshape(z_all.shape[0] // DFT_N2, DFT_N2, c), nb, n1_len, BF16)
    blk = pl.BlockSpec((nb, 2, INNER_K1, DFT_N2, c), lambda k: (0, 0, k, 0, 0))
    d = pl.pallas_call(
        _dft_inner_conv_kernel,
        grid=(n1_len // INNER_K1,),
        in_specs=_inner_specs() + [blk, pl.BlockSpec((INNER_K1, 2 * DFT_N2, c), lambda k: (k, 0, 0)),
                                   pl.BlockSpec((1, c), lambda k: (0, 0))],
        out_specs=blk,
        out_shape=jax.ShapeDtypeStruct(bz.shape, BF16),
        compiler_params=_cparams(("arbitrary",)),
        name="dft_inner_conv",
    )(tabs["fc"], tabs["fs"], tabs["tc"], tabs["ts"], bz, khat, scale)
    y3 = _dft_outer_inv(tabs["outer_inv_half"], d, z_all.shape[0] // DFT_N2)
    return y3.reshape(z_all.shape)


def _hyena_ctx_kernel(ff_ref, fi_ref, k_ref, l1_ref, z_ref, yconv_hbm_ref, o_ref):
    del yconv_hbm_ref
    ff = ff_ref[...]
    n = z_ref.shape[0]
    big = ff.shape[1]
    kh = jnp.dot(ff, k_ref[...], precision=HIGHEST, preferred_element_type=F32)
    x = jnp.dot(ff[:, :n], z_ref[...], precision=HIGHEST, preferred_element_type=F32)
    kr, ki = kh[:big], kh[big:]
    xr, xi = x[:big], x[big:]
    scale = 1.0 / (l1_ref[...] * big)
    y = jnp.concatenate([(xr * kr - xi * ki) * scale, (xr * ki + xi * kr) * scale], axis=0)
    o_ref[...] = jnp.dot(fi_ref[...], y, precision=HIGHEST, preferred_element_type=F32)


def _hyena_ctx(z, kfull, l1, row0_blk, n_batch, n, yconv):
    big = 2 * n
    a = np.arange(big)
    ang = 2.0 * np.pi * np.outer(a, a) / big
    ff = jnp.asarray(np.concatenate([np.cos(ang), -np.sin(ang)], axis=0).astype(np.float32))
    fi = jnp.asarray(np.concatenate([np.cos(ang), -np.sin(ang)], axis=1)[:n].astype(np.float32))
    c = z.shape[1]
    const = lambda b: (0, 0)
    return pl.pallas_call(
        _hyena_ctx_kernel,
        grid=(n_batch,),
        in_specs=[pl.BlockSpec(ff.shape, const), pl.BlockSpec(fi.shape, const), pl.BlockSpec(kfull.shape, const),
                  pl.BlockSpec((1, c), const), pl.BlockSpec((n, c), lambda b: (row0_blk + b, 0)),
                  pl.BlockSpec(memory_space=pl.ANY)],
        out_specs=pl.BlockSpec((n, c), lambda b: (row0_blk + b, 0)),
        out_shape=jax.ShapeDtypeStruct(yconv.shape, F32),
        input_output_aliases={5: 0},
        compiler_params=_cparams(("arbitrary",)),
        name="hyena_ctx",
    )(ff, fi, kfull, l1, z, yconv)


def _rope_tables(seq, n_batch, n_ctx_rows):
    t = jnp.arange(seq, dtype=jnp.int32)
    rows = (t // GRID_W).astype(F32)
    cols = (t % GRID_W).astype(F32)
    nf = HEAD_DIM // 4
    inv = ROPE_THETA ** (-jnp.arange(nf, dtype=F32) / nf)
    ar, ac = rows[:, None] * inv, cols[:, None] * inv
    cos = jnp.concatenate([jnp.cos(ar), jnp.cos(ar), jnp.cos(ac), jnp.cos(ac)], axis=1)
    sin = jnp.concatenate([-jnp.sin(ar), jnp.sin(ar), -jnp.sin(ac), jnp.sin(ac)], axis=1)
    cos = jnp.tile(cos, (n_batch, LANES // HEAD_DIM))
    sin = jnp.tile(sin, (n_batch, LANES // HEAD_DIM))
    cos = jnp.concatenate([cos, jnp.ones((n_ctx_rows, LANES), F32)], axis=0)
    sin = jnp.concatenate([sin, jnp.zeros((n_ctx_rows, LANES), F32)], axis=0)
    return cos, sin


def _head_mean_matrix():
    a = np.arange(LANES)
    return jnp.asarray((a[:, None] // HEAD_DIM == a[None, :] // HEAD_DIM).astype(np.float32) / HEAD_DIM, dtype=BF16)


def kernel(x, c, ctx, c_ctx, w_ada, b_ada, g_ffn1, w_ffn1_gate, w_ffn1_up, w_ffn1_down, g_mix, w_in, w_out, g_q_attn, g_k_attn, conv_w, conv_b, filt_w1, filt_b1, filt_w2, filt_b2, filt_w3, filt_freq, hyena_skip, g_q_na, g_k_na, na_rpb, g_ffn2, w_ffn2_gate, w_ffn2_up, w_ffn2_down):
    n_batch, seq, d = x.shape
    n_ctx = ctx.shape[1]
    depth = w_ada.shape[0]
    assert d == _D and n_batch + 1 <= 8 and seq % TOKEN_TILE == 0 and seq % (NA_BLOCK_ROWS * GRID_W) == 0
    assert n_ctx == 256 and (n_batch * n_ctx) % TOKEN_TILE == 0
    n_lat = n_batch * seq
    n_all = n_lat + n_batch * n_ctx
    tiles_per_batch = seq // TOKEN_TILE
    hy_w = conv_w.shape[2] // 3
    n1_len = 2 * seq // DFT_N2

    cond8 = jnp.zeros((8, d), F32).at[:n_batch].set(c).at[n_batch].set(c_ctx)
    mods = _adaln_all(cond8, w_ada, b_ada).reshape(depth * 8 * N_MOD, 1, d)
    h = x.reshape(n_lat, d)
    h_ctx = ctx.reshape(n_batch * n_ctx, d)

    cos_t, sin_t = _rope_tables(seq, n_batch, n_batch * n_ctx)
    headmean = _head_mean_matrix()
    ffn1_w = tuple(w.astype(BF16) for w in (w_ffn1_gate, w_ffn1_up, w_ffn1_down))
    ffn2_w = tuple(w.astype(BF16) for w in (w_ffn2_gate, w_ffn2_up, w_ffn2_down))
    na_masks = _na_row_masks(seq // GRID_W)
    na_tabs = jax.vmap(_na_bias_tables)(na_rpb)
    tabs = _dft_tables(n1_len)
    deltas = jnp.linspace(math.log(DECAY_TARGET) / SLOW_DECAY_PCT, math.log(DECAY_TARGET) / FAST_DECAY_PCT,
                          hy_w, dtype=F32).reshape(1, hy_w)

    q_heads = [hd for pair in range(ATTN_HEADS // 2) for hd in (pair, pair + ATTN_HEADS // 2)]
    q_segs = [(hd * HEAD_DIM, (hd + 1) * HEAD_DIM) for hd in q_heads]
    in_segs = q_segs + [(512, 640), (1536, 1792), (1792, 2048), (640, 768), (2048, 2304), (768, 1536)]

    for l in range(depth):
        last = l == depth - 1
        base = l * 8 * N_MOD
        rep = lambda g, k: jnp.tile(g, k)
        gvec = jnp.concatenate([rep(g_q_attn[l], ATTN_HEADS), rep(g_k_attn[l], ATTN_KV_HEADS),
                                rep(g_q_na[l], NA_HEADS), rep(g_k_na[l], NA_HEADS)]).reshape(1, N_NORM)

        h = _ffn(h, mods, base, (0, 1, 2), g_ffn1[l].reshape(1, d), l, *ffn1_w, n_all, tiles_per_batch, n_batch,
                 h_tail=h_ctx if l == 0 else None)

        w_in_l = jnp.concatenate([w_in[l][:, a:b] for a, b in in_segs], axis=1).astype(BF16)
        q, k, nq, nk, v, nv, z, x0 = _inproj(h, mods, base, g_mix[l].reshape(1, d), w_in_l, gvec, headmean, cos_t, sin_t,
                                             conv_w[l], conv_b[l].reshape(1, -1), tiles_per_batch, n_batch, seq, n_ctx)

        out_rows = n_lat if last else n_all
        ya = _gqa(q, k, v, out_rows, n_batch, seq, n_ctx, latent=True)
        yc = _na_latent(nq, nk, nv, na_tabs, l, na_masks, out_rows, n_batch, seq, n_ctx)

        fargs = (filt_w1[l], filt_b1[l].reshape(1, -1), filt_w2[l], filt_b2[l].reshape(1, -1), filt_w3[l],
                 filt_freq[l].reshape(1, -1), deltas)
        kfull, l1 = _hfilter(seq, *fargs)
        khat = _filter_spectrum(kfull, tabs, n1_len)
        yconv = _long_conv(z, n_batch, khat, 1.0 / (l1 * (2 * seq)), tabs, n1_len)

        if not last:
            ya_c = _gqa(q, k, v, n_all, n_batch, seq, n_ctx, latent=False)
            yc_c = _gqa(nq, nk, nv, n_all, n_batch, seq, n_ctx, latent=False, kv_chunked=True)
            ya = lax.dynamic_update_slice(ya, ya_c[n_lat:], (n_lat, 0))
            yc = lax.dynamic_update_slice(yc, yc_c[n_lat:], (n_lat, 0))
            kfull_c, l1_c = _hfilter(n_ctx, *fargs)
            yconv = _hyena_ctx(z, kfull_c, l1_c, n_lat // n_ctx, n_batch, n_ctx, yconv)

        w_o = w_out[l]
        wa = jnp.concatenate([w_o[a:b] for a, b in q_segs], axis=0).astype(BF16)
        wb = w_o[512:512 + hy_w].astype(BF16)
        wc = w_o[512 + hy_w:].astype(BF16)
        h = _ffn(h, mods, base, (6, 7, 8), g_ffn2[l].reshape(1, d), l, *ffn2_w, out_rows, tiles_per_batch, n_batch,
                 mixer=(ya, yconv, z, x0, hyena_skip[l].reshape(1, hy_w), yc, wa, wb, wc))

    return h[:n_lat].reshape(n_batch, seq, d)
```

```python
import functools
import math

import numpy as np
import jax
import jax.numpy as jnp
from jax import lax
from jax.experimental import pallas as pl
from jax.experimental.pallas import tpu as pltpu

F32 = jnp.float32
BF16 = jnp.bfloat16
HIGHEST = lax.Precision.HIGHEST

D_MODEL = 1024
HEAD_DIM = 64
GRID_W = 64
ATTN_HEADS = 8
ATTN_KV_HEADS = 2
NA_HEADS = 4
NA_ROWS = 8
NA_COLS = 16
FILTER_BANDS = 16
DECAY_TARGET = 1e-2
FAST_DECAY_PCT = 0.3
SLOW_DECAY_PCT = 1.5
ROPE_THETA = 10000.0
EPS = 1e-6
N_MOD = 9

LANES = 128
TOKEN_TILE = 512
FFN_CHUNK = 256
ATTN_TQ = 1024
ATTN_TK = 2048
NA_BLOCK_ROWS = 8
NA_SUB_ROWS = 4
NA_KEY_ROWS = 12
DFT_N2 = 128
OUTER_COLS = 8
INNER_K1 = 16
NEG_BIG = -1e30
VMEM_LIMIT = 56 * 1024 * 1024


def _cparams(sem):
    return pltpu.CompilerParams(dimension_semantics=sem, vmem_limit_bytes=VMEM_LIMIT)


def _ada_kernel(c_ref, w_ref, b_ref, o_ref):
    c = c_ref[...]
    s = c * jax.nn.sigmoid(c)
    o_ref[0] = jnp.dot(s, w_ref[0], preferred_element_type=F32) + b_ref[0]


def _adaln_all(cond8, w_ada, b_ada):
    depth, d, nd = w_ada.shape
    tn = 2304
    return pl.pallas_call(
        _ada_kernel,
        grid=(depth, nd // tn),
        in_specs=[
            pl.BlockSpec((8, d), lambda l, j: (0, 0)),
            pl.BlockSpec((1, d, tn), lambda l, j: (l, 0, j)),
            pl.BlockSpec((1, 1, tn), lambda l, j: (l, 0, j)),
        ],
        out_specs=pl.BlockSpec((1, 8, tn), lambda l, j: (l, 0, j)),
        out_shape=jax.ShapeDtypeStruct((depth, 8, nd), F32),
        compiler_params=_cparams(("arbitrary", "arbitrary")),
        name="adaln",
    )(cond8, w_ada, b_ada.reshape(depth, 1, nd))


def _norm_mod(x, g, sh, sc):
    ms = jnp.mean(x * x, axis=-1, keepdims=True)
    y = x * lax.rsqrt(ms + EPS) * g
    return y * (1.0 + sc) + sh


def _mod_specs(tiles_per_batch, n_batch, base, idxs):
    def spec(which):
        return pl.BlockSpec(
            (1, 1, D_MODEL),
            lambda i, w=which: (base + jnp.minimum(i // tiles_per_batch, n_batch) * N_MOD + w, 0, 0))
    return [spec(w) for w in idxs]


def _ffn_kernel(h_ref, *refs, with_mixer, split_tile):
    x = h_ref[...]
    if split_tile is not None:
        x = jnp.where(pl.program_id(0) < split_tile, x, refs[0][...])
        refs = refs[1:]
    if with_mixer:
        gt2_ref, ya_ref, yconv_ref, z_ref, x0_ref, skip_ref, yc_ref, wa_ref, wb_ref, wc_ref = refs[:10]
        refs = refs[10:]
        yb = ((yconv_ref[...] + z_ref[...] * skip_ref[...]) * x0_ref[...]).astype(BF16)
        x = x + gt2_ref[0] * (jnp.dot(ya_ref[...], wa_ref[0], preferred_element_type=F32)
                              + jnp.dot(yb, wb_ref[0], preferred_element_type=F32)
                              + jnp.dot(yc_ref[...], wc_ref[0], preferred_element_type=F32))
    sh_ref, sc_ref, gt_ref, g_ref, wg_ref, wu_ref, wd_ref, o_ref, xb_ref, acc_ref = refs
    x_is_new = with_mixer or split_tile is not None
    if x_is_new:
        o_ref[...] = x
    xb_ref[...] = _norm_mod(x, g_ref[...], sh_ref[0], sc_ref[0]).astype(BF16)

    for k in range(wg_ref.shape[2] // FFN_CHUNK):
        cols = slice(k * FFN_CHUNK, (k + 1) * FFN_CHUNK)
        xb = xb_ref[...]
        a = jnp.dot(xb, wg_ref[0, :, cols], preferred_element_type=F32)
        u = jnp.dot(xb, wu_ref[0, :, cols], preferred_element_type=F32)
        act = (a * jax.nn.sigmoid(a) * u).astype(BF16)
        part = jnp.dot(act, wd_ref[0, cols, :], preferred_element_type=F32)
        if k == 0:
            acc_ref[...] = part
        else:
            acc_ref[...] += part
    resid = o_ref[...] if x_is_new else h_ref[...]
    o_ref[...] = resid + 0.5 * gt_ref[0] * acc_ref[...]


def _ffn(h, mods, mod_base, which, g, layer, wg, wu, wd, n_rows, tiles_per_batch, n_batch, mixer=None, h_tail=None):
    d = h.shape[1]
    of_layer = lambda a: pl.BlockSpec((1,) + a.shape[1:], lambda i: (layer, 0, 0), pipeline_mode=pl.Buffered(1))
    tm = TOKEN_TILE
    row = lambda i: (i, 0)
    const = lambda i: (0, 0)
    rows = lambda a: pl.BlockSpec((tm, a.shape[1]), row)
    full = lambda a: pl.BlockSpec(a.shape, const)
    mix_specs, mix_args = [], ()
    if mixer is not None:
        ya, yconv, z, x0, skip, yc, wa, wb, wc = mixer
        mix_specs = _mod_specs(tiles_per_batch, n_batch, mod_base, (5,)) + [
            rows(ya), rows(yconv), rows(z), rows(x0), full(skip), rows(yc), of_layer(wa), of_layer(wb), of_layer(wc)]
        mix_args = (mods,) + tuple(mixer)
    h_specs, h_args, split = [rows(h)], (h,), None
    if h_tail is not None:
        split = h.shape[0] // tm
        h_specs = [pl.BlockSpec((tm, d), lambda i: (jnp.minimum(i, split - 1), 0)),
                   pl.BlockSpec((tm, d), lambda i: (jnp.maximum(i - split, 0), 0))]
        h_args = (h, h_tail)
    return pl.pallas_call(
        functools.partial(_ffn_kernel, with_mixer=mixer is not None, split_tile=split),
        grid=(n_rows // tm,),
        in_specs=h_specs + mix_specs + _mod_specs(tiles_per_batch, n_batch, mod_base, which)
        + [full(g), of_layer(wg), of_layer(wu), of_layer(wd)],
        out_specs=pl.BlockSpec((tm, d), row),
        out_shape=jax.ShapeDtypeStruct((n_rows, d), F32),
        scratch_shapes=[pltpu.VMEM((tm, d), BF16), pltpu.VMEM((tm, d), F32)],
        compiler_params=_cparams(("arbitrary",)),
        name="ffn_mix" if mixer is not None else "ffn",
    )(*h_args, *mix_args, mods, mods, mods, g, wg, wu, wd)


N_NORM = 1152
N_ROPE = 640
ATTN_Q_CHUNKS = (0, 1, 2, 3)
NA_Q_CHUNKS = (5, 6)
LOG2E = math.log2(math.e)


def _inproj_kernel(h_ref, prev_ref, next_ref, sh_ref, sc_ref, g_ref, w_ref, gv_ref, hm_ref, cos_ref, sin_ref,
                   cw_ref, cb_ref, q_ref, k_ref, nq_ref, nk_ref, v_ref, nv_ref, z_ref, x0_ref, *, seq, n_ctx, n_lat):
    xb = _norm_mod(h_ref[...], g_ref[...], sh_ref[0], sc_ref[0]).astype(BF16)
    hm = hm_ref[...]
    cosv = cos_ref[...]
    sinv = sin_ref[...]
    lane = lax.broadcasted_iota(jnp.int32, cosv.shape, 1)
    first_half = (lane % 32) < 16
    outs = []
    p_norm = jnp.dot(xb, w_ref[0, :, 0:N_NORM], preferred_element_type=F32)
    for c in range(N_NORM // LANES):
        p = p_norm[:, c * LANES:(c + 1) * LANES]
        ms = jnp.dot((p * p).astype(BF16), hm, preferred_element_type=F32)
        y = p * lax.rsqrt(ms + EPS) * gv_ref[:, c * LANES:(c + 1) * LANES]
        if c < N_ROPE // LANES:
            swapped = jnp.where(first_half, pltpu.roll(y, LANES - 16, 1), pltpu.roll(y, 16, 1))
            y = y * cosv + swapped * sinv
        if c in ATTN_Q_CHUNKS or c in NA_Q_CHUNKS:
            y = y * (HEAD_DIM ** -0.5 * LOG2E)
        outs.append(y.astype(BF16))
    for c in range(4):
        q_ref[:, c * LANES:(c + 1) * LANES] = outs[c]
    k_ref[...] = outs[4]
    nq_ref[:, 0:LANES] = outs[5]
    nq_ref[:, LANES:2 * LANES] = outs[6]
    nk_ref[:, 0:LANES] = outs[7]
    nk_ref[:, LANES:2 * LANES] = outs[8]
    p_rest = jnp.dot(xb, w_ref[0, :, N_NORM:], preferred_element_type=F32)
    v = p_rest[:, 0:LANES]
    v_ref[:, 0:LANES] = jnp.where(lane < HEAD_DIM, v, 1.0).astype(BF16)
    v_ref[:, LANES:2 * LANES] = jnp.where(lane < HEAD_DIM, 1.0, v).astype(BF16)
    for c in range(NA_HEADS // 2):
        nv = p_rest[:, (1 + c) * LANES:(2 + c) * LANES]
        nv_ref[:, 2 * c * LANES:(2 * c + 1) * LANES] = jnp.where(lane < HEAD_DIM, nv, 1.0).astype(BF16)
        nv_ref[:, (2 * c + 1) * LANES:(2 * c + 2) * LANES] = jnp.where(lane < HEAD_DIM, 1.0, nv).astype(BF16)

    hy = p_rest[:, 3 * LANES:]
    tm = hy.shape[0]
    halo = jnp.concatenate([prev_ref[...], next_ref[...]], axis=0)
    halo_b = _norm_mod(halo, g_ref[...], sh_ref[0], sc_ref[0]).astype(BF16)
    hy_halo = jnp.dot(halo_b, w_ref[0, :, N_NORM + 3 * LANES:], preferred_element_type=F32)
    row0 = pl.program_id(0) * tm
    is_ctx = row0 >= n_lat
    period = jnp.where(is_ctx, n_ctx, seq)
    local = lax.broadcasted_iota(jnp.int32, (tm, 1), 0)
    pos = jnp.bitwise_and(row0 + local - jnp.where(is_ctx, n_lat, 0), period - 1)
    up = jnp.where(local == 0, hy_halo[7:8, :], pltpu.roll(hy, 1, 0))
    up = jnp.where(pos == 0, 0.0, up)
    dn = jnp.where(local == tm - 1, hy_halo[8:9, :], pltpu.roll(hy, tm - 1, 0))
    dn = jnp.where(pos == period - 1, 0.0, dn)
    uc = up * cw_ref[0:1, :] + hy * cw_ref[1:2, :] + dn * cw_ref[2:3, :] + cb_ref[...]
    cw = uc.shape[1] // 3
    x0_ref[...] = uc[:, :cw]
    z_ref[...] = uc[:, 2 * cw:] * uc[:, cw:2 * cw]


def _inproj(h, mods, mod_base, g, layer, w, gvec, headmean, cos_t, sin_t, conv_w, conv_b, tiles_per_batch, n_batch, seq,
            n_ctx):
    n_rows, d = h.shape
    tm = TOKEN_TILE
    assert seq & (seq - 1) == 0 and n_ctx & (n_ctx - 1) == 0
    row = lambda i: (i, 0)
    const = lambda i: (0, 0)
    widths = (512, 128, 256, 256, 256, 512)
    hy_w = conv_w.shape[1] // 3
    out_shape = [jax.ShapeDtypeStruct((n_rows, wd), BF16) for wd in widths]
    out_shape += [jax.ShapeDtypeStruct((n_rows, hy_w), F32)] * 2
    out_specs = [pl.BlockSpec((tm, wd), row) for wd in widths] + [pl.BlockSpec((tm, hy_w), row)] * 2
    return pl.pallas_call(
        functools.partial(_inproj_kernel, seq=seq, n_ctx=n_ctx, n_lat=n_batch * seq),
        grid=(n_rows // tm,),
        in_specs=[pl.BlockSpec((tm, d), row),
                  pl.BlockSpec((8, d), lambda i: (jnp.maximum(i * (tm // 8) - 1, 0), 0)),
                  pl.BlockSpec((8, d), lambda i: (jnp.minimum((i + 1) * (tm // 8), n_rows // 8 - 1), 0))]
        + _mod_specs(tiles_per_batch, n_batch, mod_base, (3, 4))
        + [
            pl.BlockSpec((1, d), const),
            pl.BlockSpec((1,) + w.shape[1:], lambda i: (layer, 0, 0), pipeline_mode=pl.Buffered(1)),
            pl.BlockSpec(gvec.shape, const),
            pl.BlockSpec(headmean.shape, const),
            pl.BlockSpec((tm, LANES), row),
            pl.BlockSpec((tm, LANES), row),
            pl.BlockSpec(conv_w.shape, const),
            pl.BlockSpec(conv_b.shape, const),
        ],
        out_specs=out_specs,
        out_shape=out_shape,
        compiler_params=_cparams(("arbitrary",)),
        name="inproj",
    )(h, h, h, mods, mods, g, w, gvec, headmean, cos_t, sin_t, conv_w, conv_b)


def _stack_heads(q):
    lane = lax.broadcasted_iota(jnp.int32, q.shape, 1)
    zero = jnp.zeros_like(q)
    return jnp.concatenate([jnp.where(lane < HEAD_DIM, q, zero), jnp.where(lane >= HEAD_DIM, q, zero)], axis=0)


def _qk(q2, k):
    return lax.dot_general(q2, k, (((1,), (1,)), ((), ())), preferred_element_type=F32)


def _gqa_kernel(q_ref, kc_ref, vc_ref, *rest, n_latent_chunks, tk):
    if n_latent_chunks:
        kl_ref, vl_ref, o_ref = rest
    else:
        (o_ref,) = rest
    tq = q_ref.shape[0]
    q2 = _stack_heads(q_ref[...])

    def pv(p, v_ref, rows):
        return jnp.concatenate(
            [jnp.dot(p[:tq], v_ref[rows, 0:LANES], preferred_element_type=F32),
             jnp.dot(p[tq:], v_ref[rows, LANES:2 * LANES], preferred_element_type=F32)], axis=0)

    s = _qk(q2, kc_ref[...])
    m = jnp.max(s, axis=-1, keepdims=True)
    if n_latent_chunks:
        rows0 = pl.ds(0, tk)
        s0 = _qk(q2, kl_ref[rows0, :])
        m = jnp.maximum(m, jnp.max(s0, axis=-1, keepdims=True))
        acc = (pv(jnp.exp2((s - m).astype(BF16)), vc_ref, slice(None))
               + pv(jnp.exp2((s0 - m).astype(BF16)), vl_ref, rows0))
    else:
        acc = pv(jnp.exp2((s - m).astype(BF16)), vc_ref, slice(None))

    if n_latent_chunks:
        def body(j, carry):
            m, acc = carry
            rows = pl.ds(pl.multiple_of(j * tk, tk), tk)
            s = _qk(q2, kl_ref[rows, :])
            m_new = jnp.maximum(m, jnp.max(s, axis=-1, keepdims=True))
            p = jnp.exp2((s - m_new).astype(BF16))
            return m_new, jnp.exp2(m - m_new) * acc + pv(p, vl_ref, rows)

        m, acc = lax.fori_loop(1, n_latent_chunks, body, (m, acc), unroll=True)
    lane = lax.broadcasted_iota(jnp.int32, (tq, LANES), 1)
    num = jnp.where(lane < HEAD_DIM, acc[:tq], acc[tq:])
    den = pltpu.roll(jnp.where(lane < HEAD_DIM, acc[tq:], acc[:tq]), HEAD_DIM, 1)
    o_ref[...] = (num / den).astype(o_ref.dtype)


def _gqa(q, k, v2, out_rows, n_batch, seq, n_ctx, latent, kv_chunked=False):
    chunks = q.shape[1] // LANES
    ctx_blk0 = n_batch * seq // n_ctx
    tk = min(ATTN_TK, seq)
    kj = (lambda j: j) if kv_chunked else (lambda j: 0)
    if latent:
        tq = ATTN_TQ
        grid = (n_batch, chunks, seq // tq)
        qmap = lambda b, j, i: (b * (seq // tq) + i, j)
        kv_specs = [pl.BlockSpec((seq, LANES), lambda b, j, i: (b, kj(j))),
                    pl.BlockSpec((seq, 2 * LANES), lambda b, j, i: (b, kj(j)))]
        kv_args = (k, v2)
    else:
        tq = n_ctx
        grid = (n_batch, chunks, 1)
        qmap = lambda b, j, i: (ctx_blk0 + b, j)
        kv_specs, kv_args = [], ()
    return pl.pallas_call(
        functools.partial(_gqa_kernel, n_latent_chunks=seq // tk if latent else 0, tk=tk),
        grid=grid,
        in_specs=[pl.BlockSpec((tq, LANES), qmap),
                  pl.BlockSpec((n_ctx, LANES), lambda b, j, i: (ctx_blk0 + b, kj(j))),
                  pl.BlockSpec((n_ctx, 2 * LANES), lambda b, j, i: (ctx_blk0 + b, kj(j)))] + kv_specs,
        out_specs=pl.BlockSpec((tq, LANES), qmap),
        out_shape=jax.ShapeDtypeStruct((out_rows, q.shape[1]), BF16),
        compiler_params=_cparams(("arbitrary", "arbitrary", "arbitrary")),
        name="attn_latent" if latent else "attn_ctx",
    )(q, k, v2, *kv_args)


def _na_kernel(q_ref, kc_ref, vc_ref, kl_ref, vl_ref, tab_ref, mask_ref, o_ref, *, grid_rows):
    i = pl.program_id(2)
    nk = NA_KEY_ROWS * GRID_W
    n_sub = NA_BLOCK_ROWS // NA_SUB_ROWS
    last_sub = grid_rows // NA_SUB_ROWS - 1
    kc = kc_ref[...]
    sub_rows = NA_SUB_ROWS * GRID_W
    lane = lax.broadcasted_iota(jnp.int32, (sub_rows, LANES), 1)
    low = lane < HEAD_DIM
    for sub in range(n_sub):
        sb = i * n_sub + sub
        r0 = sb * NA_SUB_ROWS
        ks = jnp.clip(r0 - NA_ROWS // 2, 0, grid_rows - NA_KEY_ROWS)
        keys = pl.ds(pl.multiple_of(ks * GRID_W, GRID_W), nk)
        shift = ks - r0 + 2 * NA_ROWS
        variant = jnp.where(sb == 0, 0, jnp.where(sb == last_sub, 2, 1))
        k = kl_ref[keys, :]
        rows = slice(sub * sub_rows, (sub + 1) * sub_rows)
        q = q_ref[rows, :]
        acc = []
        for hh in range(2):
            qh = jnp.where(low if hh == 0 else jnp.logical_not(low), q, jnp.zeros_like(q))
            slabs = []
            for qr in range(NA_SUB_ROWS):
                pairs = [tab_ref[0, hh, shift - qr + 2 * kp] for kp in range(NA_KEY_ROWS // 2)]
                slabs.append(jnp.concatenate(pairs, axis=1) + mask_ref[variant, qr:qr + 1, :])
            s_nb = _qk(qh, k) + jnp.concatenate(slabs, axis=0)
            s_cx = _qk(qh, kc)
            m = jnp.maximum(jnp.max(s_nb, axis=-1, keepdims=True), jnp.max(s_cx, axis=-1, keepdims=True))
            vcols = slice(hh * LANES, (hh + 1) * LANES)
            acc.append(jnp.dot(jnp.exp2((s_nb - m).astype(BF16)), vl_ref[keys, vcols], preferred_element_type=F32)
                       + jnp.dot(jnp.exp2((s_cx - m).astype(BF16)), vc_ref[:, vcols], preferred_element_type=F32))
        num = jnp.where(low, acc[0], acc[1])
        den = pltpu.roll(jnp.where(low, acc[1], acc[0]), HEAD_DIM, 1)
        o_ref[rows, :] = (num / den).astype(o_ref.dtype)


def _na_bias_tables(rpb):
    qc = np.arange(GRID_W)[:, None]
    kc = np.arange(GRID_W)[None, :]
    cs = np.clip(qc - NA_COLS // 2, 0, GRID_W - NA_COLS)
    col_ok = (kc >= cs) & (kc < cs + NA_COLS)
    col_hot = ((kc - qc + NA_COLS - 1)[:, :, None] == np.arange(2 * NA_COLS - 1)) & col_ok[:, :, None]
    by_col = jnp.einsum("abe,hde->hdab", jnp.asarray(col_hot, F32), rpb, precision=HIGHEST)
    by_col = jnp.where(jnp.asarray(col_ok)[None, None], by_col * LOG2E, NEG_BIG)
    t = jnp.pad(by_col, ((0, 0), (NA_ROWS + 1, NA_ROWS), (0, 0), (0, 0)))
    return jnp.concatenate([t[:, :-1], t[:, 1:]], axis=-1)


def _na_row_masks(grid_rows):
    n_sub = grid_rows // NA_SUB_ROWS
    masks = []
    for sb in (0, min(1, n_sub - 1), n_sub - 1):
        qr = sb * NA_SUB_ROWS + np.arange(NA_SUB_ROWS)[:, None]
        ks = int(np.clip(sb * NA_SUB_ROWS - NA_ROWS // 2, 0, grid_rows - NA_KEY_ROWS))
        kr = ks + np.arange(NA_KEY_ROWS)[None, :]
        rs = np.clip(qr - NA_ROWS // 2, 0, grid_rows - NA_ROWS)
        row_ok = (kr >= rs) & (kr < rs + NA_ROWS)
        masks.append(np.repeat(np.where(row_ok, 0.0, NEG_BIG), GRID_W, axis=1))
    return jnp.asarray(np.stack(masks).astype(np.float32))


def _na_latent(nq, nk, nv2, tab, layer, masks, n_rows_out, n_batch, seq, n_ctx):
    chunks = nq.shape[1] // LANES
    tq = NA_BLOCK_ROWS * GRID_W
    nblk = seq // tq
    ctx_blk0 = n_batch * seq // n_ctx
    qmap = lambda b, j, i: (b * nblk + i, j)
    tab_spec = pl.BlockSpec((1, 2) + tab.shape[2:], lambda b, j, i: (layer, j, 0, 0, 0))
    mask_spec = pl.BlockSpec(masks.shape, lambda b, j, i: (0, 0, 0))
    return pl.pallas_call(
        functools.partial(_na_kernel, grid_rows=seq // GRID_W),
        grid=(n_batch, chunks, nblk),
        in_specs=[pl.BlockSpec((tq, LANES), qmap),
                  pl.BlockSpec((n_ctx, LANES), lambda b, j, i: (ctx_blk0 + b, j)),
                  pl.BlockSpec((n_ctx, 2 * LANES), lambda b, j, i: (ctx_blk0 + b, j)),
                  pl.BlockSpec((seq, LANES), lambda b, j, i: (b, j)),
                  pl.BlockSpec((seq, 2 * LANES), lambda b, j, i: (b, j)),
                  tab_spec, mask_spec],
        out_specs=pl.BlockSpec((tq, LANES), qmap),
        out_shape=jax.ShapeDtypeStruct((n_rows_out, nq.shape[1]), BF16),
        compiler_params=_cparams(("arbitrary", "arbitrary", "arbitrary")),
        name="na_latent",
    )(nq, nk, nv2, nk, nv2, tab, masks)


def _hfilt_kernel(ft_ref, tc_ref, w1t_ref, b1_ref, w2t_ref, b2_ref, w3_ref, fr_ref, d_ref, k_ref, l1_ref, *, n):
    i = pl.program_id(0)
    fr = fr_ref[...]
    h1 = jnp.sin(fr * (jnp.dot(w1t_ref[...], ft_ref[...], precision=HIGHEST, preferred_element_type=F32)
                       + b1_ref[...]))
    h2 = jnp.sin(fr * (jnp.dot(w2t_ref[...], h1, precision=HIGHEST, preferred_element_type=F32) + b2_ref[...]))
    contract0 = (((0,), (0,)), ((), ()))
    o = lax.dot_general(h2, w3_ref[...], contract0, precision=HIGHEST, preferred_element_type=F32)
    cw = o.shape[1] // 2
    t = o.shape[0]
    row = i * t + lax.broadcasted_iota(jnp.int32, (t, cw), 0)
    k = jnp.where(row < n, o[:, :cw], o[:, cw:]) * jnp.exp(-tc_ref[...] * d_ref[...])
    k = jnp.where(row == n, 0.0, k)
    k_ref[...] = k

    @pl.when(i == 0)
    def _():
        l1_ref[...] = jnp.zeros_like(l1_ref)

    l1_ref[...] += jnp.sum(jnp.abs(k), axis=0, keepdims=True)


def _filter_feats(n):
    rows = np.arange(2 * n)
    pos = np.where(rows <= n, rows, 2 * n - rows).astype(np.float32)
    pos = jnp.asarray(np.minimum(pos, n - 1))
    t = pos / max(n - 1, 1)
    bands = jnp.linspace(1e-4, FILTER_BANDS - 1, FILTER_BANDS, dtype=F32)
    ang = (2.0 * math.pi / n) * pos[None, :] * bands[:, None]
    feats = jnp.concatenate([t[None, :], jnp.cos(ang), -jnp.sin(ang)], axis=0)
    return jnp.pad(feats, ((0, LANES - feats.shape[0]), (0, 0))), t[:, None]


def _hfilter(n, w1, b1, w2, b2, w3, freq, deltas):
    feats_t, t_col = _filter_feats(n)
    w1t = jnp.pad(w1.T, ((0, 0), (0, LANES - w1.shape[0])))
    col = lambda a: a.reshape(-1, 1)
    rows = feats_t.shape[1]
    t = min(1024, rows)
    cw = w3.shape[1] // 2
    const = lambda i: (0, 0)
    full = lambda a: pl.BlockSpec(a.shape, const)
    args = (w1t, col(b1), w2.T, col(b2), w3, col(freq), jnp.abs(deltas))
    return pl.pallas_call(
        functools.partial(_hfilt_kernel, n=n),
        grid=(rows // t,),
        in_specs=[pl.BlockSpec((LANES, t), lambda i: (0, i)), pl.BlockSpec((t, 1), lambda i: (i, 0))]
        + [full(a) for a in args],
        out_specs=[pl.BlockSpec((t, cw), lambda i: (i, 0)), pl.BlockSpec((1, cw), const)],
        out_shape=[jax.ShapeDtypeStruct((rows, cw), F32), jax.ShapeDtypeStruct((1, cw), F32)],
        compiler_params=_cparams(("arbitrary",)),
        name="hyena_filter",
    )(feats_t, t_col, *args)


def _dft_outer_kernel(f_ref, x_ref, o_ref):
    c = x_ref.shape[-1]
    x = x_ref[...].reshape(f_ref.shape[1], c).astype(BF16)
    o_ref[...] = jnp.dot(f_ref[...], x, preferred_element_type=F32).astype(o_ref.dtype).reshape(o_ref.shape)


def _kron_table(fmat, cols):
    return jnp.asarray(np.kron(np.asarray(fmat, np.float32), np.eye(cols, dtype=np.float32)).astype(BF16))


def _outer_cols(dtype):
    return OUTER_COLS * 4 // jnp.dtype(dtype).itemsize


def _dft_outer_fwd(fmat, x3, nb, n1_len, out_dtype):
    k = fmat.shape[1]
    c = x3.shape[2]
    cols = _outer_cols(out_dtype)
    f = _kron_table(fmat, cols)
    return pl.pallas_call(
        _dft_outer_kernel,
        grid=(nb, DFT_N2 // cols),
        in_specs=[pl.BlockSpec(f.shape, lambda b, j: (0, 0), pipeline_mode=pl.Buffered(1)),
                  pl.BlockSpec((k, cols, c), lambda b, j: (b, j, 0))],
        out_specs=pl.BlockSpec((1, 2, n1_len, cols, c), lambda b, j: (b, 0, 0, j, 0)),
        out_shape=jax.ShapeDtypeStruct((nb, 2, n1_len, DFT_N2, c), out_dtype),
        compiler_params=_cparams(("arbitrary", "arbitrary")),
        name="dft_outer_fwd",
    )(f, x3)


def _dft_outer_inv(fmat, d5, out_groups):
    nb, _, n1_len, _, c = d5.shape
    half = fmat.shape[0]
    cols = _outer_cols(d5.dtype)
    f = _kron_table(fmat, cols)
    return pl.pallas_call(
        _dft_outer_kernel,
        grid=(nb, DFT_N2 // cols),
        in_specs=[pl.BlockSpec(f.shape, lambda b, j: (0, 0), pipeline_mode=pl.Buffered(1)),
                  pl.BlockSpec((1, 2, n1_len, cols, c), lambda b, j: (b, 0, 0, j, 0))],
        out_specs=pl.BlockSpec((half, cols, c), lambda b, j: (b, j, 0)),
        out_shape=jax.ShapeDtypeStruct((out_groups, DFT_N2, c), F32),
        compiler_params=_cparams(("arbitrary", "arbitrary")),
        name="dft_outer_inv",
    )(f, d5)


def _inner_matrices(fc, fs, tc, ts):
    gr = fc * tc - fs * ts
    gi = -(fc * ts + fs * tc)
    return gr, gi


def _real_form(gr, gi):
    return jnp.concatenate([jnp.concatenate([gr, -gi], axis=1), jnp.concatenate([gi, gr], axis=1)], axis=0)


def _dft_inner_filter_kernel(fc_ref, fs_ref, tc_ref, ts_ref, b_ref, o_ref):
    fc, fs = fc_ref[...], fs_ref[...]
    n2, c = b_ref.shape[3], b_ref.shape[4]

    def body(kk, carry):
        g = _real_form(*_inner_matrices(fc, fs, tc_ref[kk], ts_ref[kk]))
        o_ref[kk] = jnp.dot(g.astype(BF16), b_ref[0, :, kk].reshape(2 * n2, c).astype(BF16),
                            preferred_element_type=F32)
        return carry

    lax.fori_loop(0, b_ref.shape[2], body, 0, unroll=2)


def _dft_inner_conv_kernel(fc_ref, fs_ref, tc_ref, ts_ref, b_ref, kh_ref, sc_ref, o_ref):
    fc, fs = fc_ref[...], fs_ref[...]
    nb, n2, c = b_ref.shape[0], b_ref.shape[3], b_ref.shape[4]
    scale = sc_ref[...]

    def body(kk, carry):
        gr, gi = _inner_matrices(fc, fs, tc_ref[kk], ts_ref[kk])
        g = _real_form(gr, gi).astype(BF16)
        gh = _real_form(gr.T, -gi.T).astype(BF16)
        kh = kh_ref[kk]
        kr, ki = kh[:n2], kh[n2:]
        for b in range(nb):
            x = jnp.dot(g, b_ref[b, :, kk].reshape(2 * n2, c).astype(BF16), preferred_element_type=F32)
            xr, xi = x[:n2], x[n2:]
            y = jnp.concatenate([(xr * kr - xi * ki) * scale, (xr * ki + xi * kr) * scale], axis=0)
            d = jnp.dot(gh, y.astype(BF16), preferred_element_type=F32)
            o_ref[b, :, kk] = d.astype(o_ref.dtype).reshape(2, n2, c)
        return carry

    lax.fori_loop(0, b_ref.shape[2], body, 0, unroll=2)


def _dft_tables(n1_len):
    n_total = n1_len * DFT_N2
    a = np.arange(DFT_N2)
    ang = 2.0 * np.pi * np.outer(a, a) / DFT_N2
    k1 = np.arange(n1_len)
    tw = 2.0 * np.pi * np.outer(k1, a) / n_total
    o = 2.0 * np.pi * np.outer(k1, k1) / n1_len
    f32 = lambda v: jnp.asarray(v.astype(np.float32))
    outer_fwd = np.concatenate([np.cos(o), -np.sin(o)], axis=0)
    outer_inv = np.concatenate([np.cos(o), -np.sin(o)], axis=1)
    return dict(
        fc=f32(np.cos(ang)), fs=f32(np.sin(ang)),
        tc=f32(np.cos(tw)).reshape(n1_len, 1, DFT_N2), ts=f32(np.sin(tw)).reshape(n1_len, 1, DFT_N2),
        outer_fwd_full=outer_fwd, outer_fwd_half=outer_fwd[:, :n1_len // 2], outer_inv_half=outer_inv[:n1_len // 2],
    )


def _inner_specs():
    sq = pl.BlockSpec((DFT_N2, DFT_N2), lambda k: (0, 0))
    tw = pl.BlockSpec((INNER_K1, 1, DFT_N2), lambda k: (k, 0, 0))
    return [sq, sq, tw, tw]


def _filter_spectrum(kfull, tabs, n1_len):
    c = kfull.shape[1]
    b5 = _dft_outer_fwd(tabs["outer_fwd_full"], kfull.reshape(n1_len, DFT_N2, c), 1, n1_len, F32)
    return pl.pallas_call(
        _dft_inner_filter_kernel,
        grid=(n1_len // INNER_K1,),
        in_specs=_inner_specs() + [pl.BlockSpec((1, 2, INNER_K1, DFT_N2, c), lambda k: (0, 0, k, 0, 0))],
        out_specs=pl.BlockSpec((INNER_K1, 2 * DFT_N2, c), lambda k: (k, 0, 0)),
        out_shape=jax.ShapeDtypeStruct((n1_len, 2 * DFT_N2, c), F32),
        compiler_params=_cparams(("arbitrary",)),
        name="dft_inner_filter",
    )(tabs["fc"], tabs["fs"], tabs["tc"], tabs["ts"], b5)


def _long_conv(z_all, nb, khat, scale, tabs, n1_len):
    c = z_all.shape[1]
    bz = _dft_outer_fwd(tabs["outer_fwd_half"], z_all.reshape(z_all.shape[0] // DFT_N2, DFT_N2, c), nb, n1_len, BF16)
    blk = pl.BlockSpec((nb, 2, INNER_K1, DFT_N2, c), lambda k: (0, 0, k, 0, 0))
    d = pl.pallas_call(
        _dft_inner_conv_kernel,
        grid=(n1_len // INNER_K1,),
        in_specs=_inner_specs() + [blk, pl.BlockSpec((INNER_K1, 2 * DFT_N2, c), lambda k: (k, 0, 0)),
                                   pl.BlockSpec((1, c), lambda k: (0, 0))],
        out_specs=blk,
        out_shape=jax.ShapeDtypeStruct(bz.shape, BF16),
        compiler_params=_cparams(("arbitrary",)),
        name="dft_inner_conv",
    )(tabs["fc"], tabs["fs"], tabs["tc"], tabs["ts"], bz, khat, scale)
    y3 = _dft_outer_inv(tabs["outer_inv_half"], d, z_all.shape[0] // DFT_N2)
    return y3.reshape(z_all.shape)


def _hyena_ctx_kernel(ff_ref, fi_ref, k_ref, l1_ref, z_ref, yconv_hbm_ref, o_ref):
    del yconv_hbm_ref
    ff = ff_ref[...]
    n = z_ref.shape[0]
    big = ff.shape[1]
    kh = jnp.dot(ff, k_ref[...], precision=HIGHEST, preferred_element_type=F32)
    x = jnp.dot(ff[:, :n], z_ref[...], precision=HIGHEST, preferred_element_type=F32)
    kr, ki = kh[:big], kh[big:]
    xr, xi = x[:big], x[big:]
    scale = 1.0 / (l1_ref[...] * big)
    y = jnp.concatenate([(xr * kr - xi * ki) * scale, (xr * ki + xi * kr) * scale], axis=0)
    o_ref[...] = jnp.dot(fi_ref[...], y, precision=HIGHEST, preferred_element_type=F32)


def _hyena_ctx(z, kfull, l1, row0_blk, n_batch, n, yconv):
    big = 2 * n
    a = np.arange(big)
    ang = 2.0 * np.pi * np.outer(a, a) / big
    ff = jnp.asarray(np.concatenate([np.cos(ang), -np.sin(ang)], axis=0).astype(np.float32))
    fi = jnp.asarray(np.concatenate([np.cos(ang), -np.sin(ang)], axis=1)[:n].astype(np.float32))
    c = z.shape[1]
    const = lambda b: (0, 0)
    return pl.pallas_call(
        _hyena_ctx_kernel,
        grid=(n_batch,),
        in_specs=[pl.BlockSpec(ff.shape, const), pl.BlockSpec(fi.shape, const), pl.BlockSpec(kfull.shape, const),
                  pl.BlockSpec((1, c), const), pl.BlockSpec((n, c), lambda b: (row0_blk + b, 0)),
                  pl.BlockSpec(memory_space=pl.ANY)],
        out_specs=pl.BlockSpec((n, c), lambda b: (row0_blk + b, 0)),
        out_shape=jax.ShapeDtypeStruct(yconv.shape, F32),
        input_output_aliases={5: 0},
        compiler_params=_cparams(("arbitrary",)),
        name="hyena_ctx",
    )(ff, fi, kfull, l1, z, yconv)


def _rope_tables(seq, n_batch, n_ctx_rows):
    t = jnp.arange(seq, dtype=jnp.int32)
    rows = (t // GRID_W).astype(F32)
    cols = (t % GRID_W).astype(F32)
    nf = HEAD_DIM // 4
    inv = ROPE_THETA ** (-jnp.arange(nf, dtype=F32) / nf)
    ar, ac = rows[:, None] * inv, cols[:, None] * inv
    cos = jnp.concatenate([jnp.cos(ar), jnp.cos(ar), jnp.cos(ac), jnp.cos(ac)], axis=1)
    sin = jnp.concatenate([-jnp.sin(ar), jnp.sin(ar), -jnp.sin(ac), jnp.sin(ac)], axis=1)
    cos = jnp.tile(cos, (n_batch, LANES // HEAD_DIM))
    sin = jnp.tile(sin, (n_batch, LANES // HEAD_DIM))
    cos = jnp.concatenate([cos, jnp.ones((n_ctx_rows, LANES), F32)], axis=0)
    sin = jnp.concatenate([sin, jnp.zeros((n_ctx_rows, LANES), F32)], axis=0)
    return cos, sin


def _head_mean_matrix():
    a = np.arange(LANES)
    return jnp.asarray((a[:, None] // HEAD_DIM == a[None, :] // HEAD_DIM).astype(np.float32) / HEAD_DIM, dtype=BF16)


def kernel(x, c, ctx, c_ctx, w_ada, b_ada, g_ffn1, w_ffn1_gate, w_ffn1_up, w_ffn1_down, g_mix, w_in, w_out, g_q_attn, g_k_attn, conv_w, conv_b, filt_w1, filt_b1, filt_w2, filt_b2, filt_w3, filt_freq, hyena_skip, g_q_na, g_k_na, na_rpb, g_ffn2, w_ffn2_gate, w_ffn2_up, w_ffn2_down):
    n_batch, seq, d = x.shape
    n_ctx = ctx.shape[1]
    depth = w_ada.shape[0]
    assert d == D_MODEL and n_batch + 1 <= 8 and seq % TOKEN_TILE == 0 and seq % (NA_BLOCK_ROWS * GRID_W) == 0
    assert n_ctx == 256 and (n_batch * n_ctx) % TOKEN_TILE == 0
    n_lat = n_batch * seq
    n_all = n_lat + n_batch * n_ctx
    tiles_per_batch = seq // TOKEN_TILE
    hy_w = conv_w.shape[2] // 3
    n1_len = 2 * seq // DFT_N2

    cond8 = jnp.zeros((8, d), F32).at[:n_batch].set(c).at[n_batch].set(c_ctx)
    mods = _adaln_all(cond8, w_ada, b_ada).reshape(depth * 8 * N_MOD, 1, d)
    h = x.reshape(n_lat, d)
    h_ctx = ctx.reshape(n_batch * n_ctx, d)

    cos_t, sin_t = _rope_tables(seq, n_batch, n_batch * n_ctx)
    headmean = _head_mean_matrix()
    na_masks = _na_row_masks(seq // GRID_W)
    na_tabs = jax.vmap(_na_bias_tables)(na_rpb)
    tabs = _dft_tables(n1_len)
    deltas = jnp.linspace(math.log(DECAY_TARGET) / SLOW_DECAY_PCT, math.log(DECAY_TARGET) / FAST_DECAY_PCT,
                          hy_w, dtype=F32).reshape(1, hy_w)

    q_heads = [hd for pair in range(ATTN_HEADS // 2) for hd in (pair, pair + ATTN_HEADS // 2)]
    q_segs = [(hd * HEAD_DIM, (hd + 1) * HEAD_DIM) for hd in q_heads]
    in_segs = q_segs + [(512, 640), (1536, 1792), (1792, 2048), (640, 768), (2048, 2304), (768, 1536)]
    w_in_p = jnp.concatenate([w_in[:, :, a:b] for a, b in in_segs], axis=2).astype(BF16)
    wa = jnp.concatenate([w_out[:, a:b] for a, b in q_segs], axis=1).astype(BF16)
    wb = w_out[:, 512:512 + hy_w].astype(BF16)
    wc = w_out[:, 512 + hy_w:].astype(BF16)
    ffn1_w = tuple(w.astype(BF16) for w in (w_ffn1_gate, w_ffn1_up, w_ffn1_down))
    ffn2_w = tuple(w.astype(BF16) for w in (w_ffn2_gate, w_ffn2_up, w_ffn2_down))

    for l in range(depth):
        last = l == depth - 1
        base = l * 8 * N_MOD
        rep = lambda g, k: jnp.tile(g, k)
        gvec = jnp.concatenate([rep(g_q_attn[l], ATTN_HEADS), rep(g_k_attn[l], ATTN_KV_HEADS),
                                rep(g_q_na[l], NA_HEADS), rep(g_k_na[l], NA_HEADS)]).reshape(1, N_NORM)

        h = _ffn(h, mods, base, (0, 1, 2), g_ffn1[l].reshape(1, d), l, *ffn1_w, n_all, tiles_per_batch, n_batch,
                 h_tail=h_ctx if l == 0 else None)

        q, k, nq, nk, v, nv, z, x0 = _inproj(h, mods, base, g_mix[l].reshape(1, d), l, w_in_p, gvec, headmean, cos_t, sin_t,
                                             conv_w[l], conv_b[l].reshape(1, -1), tiles_per_batch, n_batch, seq, n_ctx)

        out_rows = n_lat if last else n_all
        ya = _gqa(q, k, v, out_rows, n_batch, seq, n_ctx, latent=True)
        yc = _na_latent(nq, nk, nv, na_tabs, l, na_masks, out_rows, n_batch, seq, n_ctx)

        fargs = (filt_w1[l], filt_b1[l].reshape(1, -1), filt_w2[l], filt_b2[l].reshape(1, -1), filt_w3[l],
                 filt_freq[l].reshape(1, -1), deltas)
        kfull, l1 = _hfilter(seq, *fargs)
        khat = _filter_spectrum(kfull, tabs, n1_len)
        yconv = _long_conv(z, n_batch, khat, 1.0 / (l1 * (2 * seq)), tabs, n1_len)

        if not last:
            ya_c = _gqa(q, k, v, n_all, n_batch, seq, n_ctx, latent=False)
            yc_c = _gqa(nq, nk, nv, n_all, n_batch, seq, n_ctx, latent=False, kv_chunked=True)
            ya = lax.dynamic_update_slice(ya, ya_c[n_lat:], (n_lat, 0))
            yc = lax.dynamic_update_slice(yc, yc_c[n_lat:], (n_lat, 0))
            kfull_c, l1_c = _hfilter(n_ctx, *fargs)
            yconv = _hyena_ctx(z, kfull_c, l1_c, n_lat // n_ctx, n_batch, n_ctx, yconv)

        h = _ffn(h, mods, base, (6, 7, 8), g_ffn2[l].reshape(1, d), l, *ffn2_w, out_rows, tiles_per_batch, n_batch,
                 mixer=(ya, yconv, z, x0, hyena_skip[l].reshape(1, hy_w), yc, wa, wb, wc))

    return h[:n_lat].reshape(n_batch, seq, d)
```

```python
import functools
import math

import numpy as np
import jax
import jax.numpy as jnp
from jax import lax
from jax.experimental import pallas as pl
from jax.experimental.pallas import tpu as pltpu

F32 = jnp.float32
BF16 = jnp.bfloat16
HIGHEST = lax.Precision.HIGHEST

D_MODEL = 1024
HEAD_DIM = 64
GRID_W = 64
ATTN_HEADS = 8
ATTN_KV_HEADS = 2
NA_HEADS = 4
NA_ROWS = 8
NA_COLS = 16
FILTER_BANDS = 16
DECAY_TARGET = 1e-2
FAST_DECAY_PCT = 0.3
SLOW_DECAY_PCT = 1.5
ROPE_THETA = 10000.0
EPS = 1e-6
N_MOD = 9

LANES = 128
TOKEN_TILE = 512
FFN_CHUNK = 256
ATTN_TQ = 1024
ATTN_TK = 2048
NA_BLOCK_ROWS = 16
NA_SUB_ROWS = 4
NA_KEY_ROWS = 12
DFT_N2 = 128
OUTER_COLS = 8
INNER_K1 = 16
NEG_BIG = -1e30
VMEM_LIMIT = 56 * 1024 * 1024


def _cparams(sem):
    return pltpu.CompilerParams(dimension_semantics=sem, vmem_limit_bytes=VMEM_LIMIT)


def _ada_kernel(c_ref, w_ref, b_ref, o_ref):
    c = c_ref[...]
    s = c * jax.nn.sigmoid(c)
    o_ref[0] = jnp.dot(s, w_ref[0], preferred_element_type=F32) + b_ref[0]


def _adaln_all(cond8, w_ada, b_ada):
    depth, d, nd = w_ada.shape
    tn = 2304
    return pl.pallas_call(
        _ada_kernel,
        grid=(depth, nd // tn),
        in_specs=[
            pl.BlockSpec((8, d), lambda l, j: (0, 0)),
            pl.BlockSpec((1, d, tn), lambda l, j: (l, 0, j)),
            pl.BlockSpec((1, 1, tn), lambda l, j: (l, 0, j)),
        ],
        out_specs=pl.BlockSpec((1, 8, tn), lambda l, j: (l, 0, j)),
        out_shape=jax.ShapeDtypeStruct((depth, 8, nd), F32),
        compiler_params=_cparams(("arbitrary", "arbitrary")),
        name="adaln",
    )(cond8, w_ada, b_ada.reshape(depth, 1, nd))


def _norm_mod(x, g, sh, sc):
    ms = jnp.mean(x * x, axis=-1, keepdims=True)
    y = x * lax.rsqrt(ms + EPS) * g
    return y * (1.0 + sc) + sh


def _mod_specs(tiles_per_batch, n_batch, base, idxs):
    def spec(which):
        return pl.BlockSpec(
            (1, 1, D_MODEL),
            lambda i, w=which: (base + jnp.minimum(i // tiles_per_batch, n_batch) * N_MOD + w, 0, 0))
    return [spec(w) for w in idxs]


def _ffn_kernel(h_ref, *refs, with_mixer, split_tile):
    x = h_ref[...]
    if split_tile is not None:
        x = jnp.where(pl.program_id(0) < split_tile, x, refs[0][...])
        refs = refs[1:]
    if with_mixer:
        gt2_ref, ya_ref, yconv_ref, z_ref, x0_ref, skip_ref, yc_ref, wa_ref, wb_ref, wc_ref = refs[:10]
        refs = refs[10:]
        yb = ((yconv_ref[...] + z_ref[...] * skip_ref[...]) * x0_ref[...]).astype(BF16)
        x = x + gt2_ref[0] * (jnp.dot(ya_ref[...], wa_ref[0], preferred_element_type=F32)
                              + jnp.dot(yb, wb_ref[0], preferred_element_type=F32)
                              + jnp.dot(yc_ref[...], wc_ref[0], preferred_element_type=F32))
    sh_ref, sc_ref, gt_ref, g_ref, wg_ref, wu_ref, wd_ref, o_ref, xb_ref, acc_ref = refs
    x_is_new = with_mixer or split_tile is not None
    if x_is_new:
        o_ref[...] = x
    xb_ref[...] = _norm_mod(x, g_ref[...], sh_ref[0], sc_ref[0]).astype(BF16)

    for k in range(wg_ref.shape[2] // FFN_CHUNK):
        cols = slice(k * FFN_CHUNK, (k + 1) * FFN_CHUNK)
        xb = xb_ref[...]
        a = jnp.dot(xb, wg_ref[0, :, cols], preferred_element_type=F32)
        u = jnp.dot(xb, wu_ref[0, :, cols], preferred_element_type=F32)
        act = (a * jax.nn.sigmoid(a) * u).astype(BF16)
        part = jnp.dot(act, wd_ref[0, cols, :], preferred_element_type=F32)
        if k == 0:
            acc_ref[...] = part
        else:
            acc_ref[...] += part
    resid = o_ref[...] if x_is_new else h_ref[...]
    o_ref[...] = resid + 0.5 * gt_ref[0] * acc_ref[...]


def _ffn(h, mods, mod_base, which, g, layer, wg, wu, wd, n_rows, tiles_per_batch, n_batch, mixer=None, h_tail=None):
    d = h.shape[1]
    of_layer = lambda a: pl.BlockSpec((1,) + a.shape[1:], lambda i: (layer, 0, 0), pipeline_mode=pl.Buffered(1))
    tm = TOKEN_TILE
    row = lambda i: (i, 0)
    const = lambda i: (0, 0)
    rows = lambda a: pl.BlockSpec((tm, a.shape[1]), row)
    full = lambda a: pl.BlockSpec(a.shape, const)
    mix_specs, mix_args = [], ()
    if mixer is not None:
        ya, yconv, z, x0, skip, yc, wa, wb, wc = mixer
        mix_specs = _mod_specs(tiles_per_batch, n_batch, mod_base, (5,)) + [
            rows(ya), rows(yconv), rows(z), rows(x0), full(skip), rows(yc), of_layer(wa), of_layer(wb), of_layer(wc)]
        mix_args = (mods,) + tuple(mixer)
    h_specs, h_args, split = [rows(h)], (h,), None
    if h_tail is not None:
        split = h.shape[0] // tm
        h_specs = [pl.BlockSpec((tm, d), lambda i: (jnp.minimum(i, split - 1), 0)),
                   pl.BlockSpec((tm, d), lambda i: (jnp.maximum(i - split, 0), 0))]
        h_args = (h, h_tail)
    return pl.pallas_call(
        functools.partial(_ffn_kernel, with_mixer=mixer is not None, split_tile=split),
        grid=(n_rows // tm,),
        in_specs=h_specs + mix_specs + _mod_specs(tiles_per_batch, n_batch, mod_base, which)
        + [full(g), of_layer(wg), of_layer(wu), of_layer(wd)],
        out_specs=pl.BlockSpec((tm, d), row),
        out_shape=jax.ShapeDtypeStruct((n_rows, d), F32),
        scratch_shapes=[pltpu.VMEM((tm, d), BF16), pltpu.VMEM((tm, d), F32)],
        compiler_params=_cparams(("arbitrary",)),
        name="ffn_mix" if mixer is not None else "ffn",
    )(*h_args, *mix_args, mods, mods, mods, g, wg, wu, wd)


N_NORM = 1152
N_ROPE = 640
ATTN_Q_CHUNKS = (0, 1, 2, 3)
NA_Q_CHUNKS = (5, 6)
LOG2E = math.log2(math.e)


def _inproj_kernel(h_ref, prev_ref, next_ref, sh_ref, sc_ref, g_ref, w_ref, gv_ref, hm_ref, cos_ref, sin_ref,
                   cw_ref, cb_ref, q_ref, k_ref, nq_ref, nk_ref, v_ref, nv_ref, z_ref, x0_ref, *, seq, n_ctx, n_lat):
    xb = _norm_mod(h_ref[...], g_ref[...], sh_ref[0], sc_ref[0]).astype(BF16)
    hm = hm_ref[...]
    cosv = cos_ref[...]
    sinv = sin_ref[...]
    lane = lax.broadcasted_iota(jnp.int32, cosv.shape, 1)
    first_half = (lane % 32) < 16
    outs = []
    p_norm = jnp.dot(xb, w_ref[0, :, 0:N_NORM], preferred_element_type=F32)
    for c in range(N_NORM // LANES):
        p = p_norm[:, c * LANES:(c + 1) * LANES]
        ms = jnp.dot((p * p).astype(BF16), hm, preferred_element_type=F32)
        y = p * lax.rsqrt(ms + EPS) * gv_ref[:, c * LANES:(c + 1) * LANES]
        if c < N_ROPE // LANES:
            swapped = jnp.where(first_half, pltpu.roll(y, LANES - 16, 1), pltpu.roll(y, 16, 1))
            y = y * cosv + swapped * sinv
        if c in ATTN_Q_CHUNKS or c in NA_Q_CHUNKS:
            y = y * (HEAD_DIM ** -0.5 * LOG2E)
        outs.append(y.astype(BF16))
    for c in range(4):
        q_ref[:, c * LANES:(c + 1) * LANES] = outs[c]
    k_ref[...] = outs[4]
    nq_ref[:, 0:LANES] = outs[5]
    nq_ref[:, LANES:2 * LANES] = outs[6]
    nk_ref[:, 0:LANES] = outs[7]
    nk_ref[:, LANES:2 * LANES] = outs[8]
    p_rest = jnp.dot(xb, w_ref[0, :, N_NORM:], preferred_element_type=F32)
    v = p_rest[:, 0:LANES]
    v_ref[:, 0:LANES] = jnp.where(lane < HEAD_DIM, v, 1.0).astype(BF16)
    v_ref[:, LANES:2 * LANES] = jnp.where(lane < HEAD_DIM, 1.0, v).astype(BF16)
    for c in range(NA_HEADS // 2):
        nv = p_rest[:, (1 + c) * LANES:(2 + c) * LANES]
        nv_ref[:, 2 * c * LANES:(2 * c + 1) * LANES] = jnp.where(lane < HEAD_DIM, nv, 1.0).astype(BF16)
        nv_ref[:, (2 * c + 1) * LANES:(2 * c + 2) * LANES] = jnp.where(lane < HEAD_DIM, 1.0, nv).astype(BF16)

    hy = p_rest[:, 3 * LANES:]
    tm = hy.shape[0]
    halo = jnp.concatenate([prev_ref[...], next_ref[...]], axis=0)
    halo_b = _norm_mod(halo, g_ref[...], sh_ref[0], sc_ref[0]).astype(BF16)
    hy_halo = jnp.dot(halo_b, w_ref[0, :, N_NORM + 3 * LANES:], preferred_element_type=F32)
    row0 = pl.program_id(0) * tm
    is_ctx = row0 >= n_lat
    period = jnp.where(is_ctx, n_ctx, seq)
    local = lax.broadcasted_iota(jnp.int32, (tm, 1), 0)
    pos = jnp.bitwise_and(row0 + local - jnp.where(is_ctx, n_lat, 0), period - 1)
    up = jnp.where(local == 0, hy_halo[7:8, :], pltpu.roll(hy, 1, 0))
    up = jnp.where(pos == 0, 0.0, up)
    dn = jnp.where(local == tm - 1, hy_halo[8:9, :], pltpu.roll(hy, tm - 1, 0))
    dn = jnp.where(pos == period - 1, 0.0, dn)
    uc = up * cw_ref[0:1, :] + hy * cw_ref[1:2, :] + dn * cw_ref[2:3, :] + cb_ref[...]
    cw = uc.shape[1] // 3
    x0_ref[...] = uc[:, :cw]
    z_ref[...] = uc[:, 2 * cw:] * uc[:, cw:2 * cw]


def _inproj(h, mods, mod_base, g, layer, w, gvec, headmean, cos_t, sin_t, conv_w, conv_b, tiles_per_batch, n_batch, seq,
            n_ctx):
    n_rows, d = h.shape
    tm = TOKEN_TILE
    assert seq & (seq - 1) == 0 and n_ctx & (n_ctx - 1) == 0
    row = lambda i: (i, 0)
    const = lambda i: (0, 0)
    widths = (512, 128, 256, 256, 256, 512)
    hy_w = conv_w.shape[1] // 3
    out_shape = [jax.ShapeDtypeStruct((n_rows, wd), BF16) for wd in widths]
    out_shape += [jax.ShapeDtypeStruct((n_rows, hy_w), F32)] * 2
    out_specs = [pl.BlockSpec((tm, wd), row) for wd in widths] + [pl.BlockSpec((tm, hy_w), row)] * 2
    return pl.pallas_call(
        functools.partial(_inproj_kernel, seq=seq, n_ctx=n_ctx, n_lat=n_batch * seq),
        grid=(n_rows // tm,),
        in_specs=[pl.BlockSpec((tm, d), row),
                  pl.BlockSpec((8, d), lambda i: (jnp.maximum(i * (tm // 8) - 1, 0), 0)),
                  pl.BlockSpec((8, d), lambda i: (jnp.minimum((i + 1) * (tm // 8), n_rows // 8 - 1), 0))]
        + _mod_specs(tiles_per_batch, n_batch, mod_base, (3, 4))
        + [
            pl.BlockSpec((1, d), const),
            pl.BlockSpec((1,) + w.shape[1:], lambda i: (layer, 0, 0), pipeline_mode=pl.Buffered(1)),
            pl.BlockSpec(gvec.shape, const),
            pl.BlockSpec(headmean.shape, const),
            pl.BlockSpec((tm, LANES), row),
            pl.BlockSpec((tm, LANES), row),
            pl.BlockSpec(conv_w.shape, const),
            pl.BlockSpec(conv_b.shape, const),
        ],
        out_specs=out_specs,
        out_shape=out_shape,
        compiler_params=_cparams(("arbitrary",)),
        name="inproj",
    )(h, h, h, mods, mods, g, w, gvec, headmean, cos_t, sin_t, conv_w, conv_b)


def _stack_heads(q):
    lane = lax.broadcasted_iota(jnp.int32, q.shape, 1)
    zero = jnp.zeros_like(q)
    return jnp.concatenate([jnp.where(lane < HEAD_DIM, q, zero), jnp.where(lane >= HEAD_DIM, q, zero)], axis=0)


def _qk(q2, k):
    return lax.dot_general(q2, k, (((1,), (1,)), ((), ())), preferred_element_type=F32)


def _gqa_kernel(q_ref, kc_ref, vc_ref, *rest, n_latent_chunks, tk):
    if n_latent_chunks:
        kl_ref, vl_ref, o_ref = rest
    else:
        (o_ref,) = rest
    tq = q_ref.shape[0]
    q2 = _stack_heads(q_ref[...])

    def pv(p, v_ref, rows):
        return jnp.concatenate(
            [jnp.dot(p[:tq], v_ref[rows, 0:LANES], preferred_element_type=F32),
             jnp.dot(p[tq:], v_ref[rows, LANES:2 * LANES], preferred_element_type=F32)], axis=0)

    s = _qk(q2, kc_ref[...])
    m = jnp.max(s, axis=-1, keepdims=True)
    if n_latent_chunks:
        rows0 = pl.ds(0, tk)
        s0 = _qk(q2, kl_ref[rows0, :])
        m = jnp.maximum(m, jnp.max(s0, axis=-1, keepdims=True))
        acc = (pv(jnp.exp2((s - m).astype(BF16)), vc_ref, slice(None))
               + pv(jnp.exp2((s0 - m).astype(BF16)), vl_ref, rows0))
    else:
        acc = pv(jnp.exp2((s - m).astype(BF16)), vc_ref, slice(None))

    if n_latent_chunks:
        def body(j, carry):
            m, acc = carry
            rows = pl.ds(pl.multiple_of(j * tk, tk), tk)
            s = _qk(q2, kl_ref[rows, :])
            m_new = jnp.maximum(m, jnp.max(s, axis=-1, keepdims=True))
            p = jnp.exp2((s - m_new).astype(BF16))
            return m_new, jnp.exp2(m - m_new) * acc + pv(p, vl_ref, rows)

        m, acc = lax.fori_loop(1, n_latent_chunks, body, (m, acc), unroll=True)
    lane = lax.broadcasted_iota(jnp.int32, (tq, LANES), 1)
    num = jnp.where(lane < HEAD_DIM, acc[:tq], acc[tq:])
    den = pltpu.roll(jnp.where(lane < HEAD_DIM, acc[tq:], acc[:tq]), HEAD_DIM, 1)
    o_ref[...] = (num / den).astype(o_ref.dtype)


def _gqa(q, k, v2, out_rows, n_batch, seq, n_ctx, latent, kv_chunked=False):
    chunks = q.shape[1] // LANES
    ctx_blk0 = n_batch * seq // n_ctx
    tk = min(ATTN_TK, seq)
    kj = (lambda j: j) if kv_chunked else (lambda j: 0)
    if latent:
        tq = ATTN_TQ
        grid = (n_batch, chunks, seq // tq)
        qmap = lambda b, j, i: (b * (seq // tq) + i, j)
        kv_specs = [pl.BlockSpec((seq, LANES), lambda b, j, i: (b, kj(j))),
                    pl.BlockSpec((seq, 2 * LANES), lambda b, j, i: (b, kj(j)))]
        kv_args = (k, v2)
    else:
        tq = n_ctx
        grid = (n_batch, chunks, 1)
        qmap = lambda b, j, i: (ctx_blk0 + b, j)
        kv_specs, kv_args = [], ()
    return pl.pallas_call(
        functools.partial(_gqa_kernel, n_latent_chunks=seq // tk if latent else 0, tk=tk),
        grid=grid,
        in_specs=[pl.BlockSpec((tq, LANES), qmap),
                  pl.BlockSpec((n_ctx, LANES), lambda b, j, i: (ctx_blk0 + b, kj(j))),
                  pl.BlockSpec((n_ctx, 2 * LANES), lambda b, j, i: (ctx_blk0 + b, kj(j)))] + kv_specs,
        out_specs=pl.BlockSpec((tq, LANES), qmap),
        out_shape=jax.ShapeDtypeStruct((out_rows, q.shape[1]), BF16),
        compiler_params=_cparams(("arbitrary", "arbitrary", "arbitrary")),
        name="attn_latent" if latent else "attn_ctx",
    )(q, k, v2, *kv_args)


def _na_kernel(q_ref, kc_ref, vc_ref, kl_ref, vl_ref, tab_ref, mask_ref, o_ref, *, grid_rows):
    i = pl.program_id(2)
    nk = NA_KEY_ROWS * GRID_W
    n_sub = NA_BLOCK_ROWS // NA_SUB_ROWS
    last_sub = grid_rows // NA_SUB_ROWS - 1
    kc = kc_ref[...]
    sub_rows = NA_SUB_ROWS * GRID_W
    lane = lax.broadcasted_iota(jnp.int32, (sub_rows, LANES), 1)
    low = lane < HEAD_DIM
    for sub in range(n_sub):
        sb = i * n_sub + sub
        r0 = sb * NA_SUB_ROWS
        ks = jnp.clip(r0 - NA_ROWS // 2, 0, grid_rows - NA_KEY_ROWS)
        keys = pl.ds(pl.multiple_of(ks * GRID_W, GRID_W), nk)
        shift = ks - r0 + 2 * NA_ROWS
        variant = jnp.where(sb == 0, 0, jnp.where(sb == last_sub, 2, 1))
        k = kl_ref[keys, :]
        rows = slice(sub * sub_rows, (sub + 1) * sub_rows)
        q = q_ref[rows, :]
        acc = []
        for hh in range(2):
            qh = jnp.where(low if hh == 0 else jnp.logical_not(low), q, jnp.zeros_like(q))
            slabs = []
            for qr in range(NA_SUB_ROWS):
                pairs = [tab_ref[0, hh, shift - qr + 2 * kp] for kp in range(NA_KEY_ROWS // 2)]
                slabs.append(jnp.concatenate(pairs, axis=1) + mask_ref[variant, qr:qr + 1, :])
            s_nb = _qk(qh, k) + jnp.concatenate(slabs, axis=0)
            s_cx = _qk(qh, kc)
            m = jnp.maximum(jnp.max(s_nb, axis=-1, keepdims=True), jnp.max(s_cx, axis=-1, keepdims=True))
            vcols = slice(hh * LANES, (hh + 1) * LANES)
            acc.append(jnp.dot(jnp.exp2((s_nb - m).astype(BF16)), vl_ref[keys, vcols], preferred_element_type=F32)
                       + jnp.dot(jnp.exp2((s_cx - m).astype(BF16)), vc_ref[:, vcols], preferred_element_type=F32))
        num = jnp.where(low, acc[0], acc[1])
        den = pltpu.roll(jnp.where(low, acc[1], acc[0]), HEAD_DIM, 1)
        o_ref[rows, :] = (num / den).astype(o_ref.dtype)


def _na_bias_tables(rpb):
    qc = np.arange(GRID_W)[:, None]
    kc = np.arange(GRID_W)[None, :]
    cs = np.clip(qc - NA_COLS // 2, 0, GRID_W - NA_COLS)
    col_ok = (kc >= cs) & (kc < cs + NA_COLS)
    col_hot = ((kc - qc + NA_COLS - 1)[:, :, None] == np.arange(2 * NA_COLS - 1)) & col_ok[:, :, None]
    by_col = jnp.einsum("abe,hde->hdab", jnp.asarray(col_hot, F32), rpb, precision=HIGHEST)
    by_col = jnp.where(jnp.asarray(col_ok)[None, None], by_col * LOG2E, NEG_BIG)
    t = jnp.pad(by_col, ((0, 0), (NA_ROWS + 1, NA_ROWS), (0, 0), (0, 0)))
    return jnp.concatenate([t[:, :-1], t[:, 1:]], axis=-1)


def _na_row_masks(grid_rows):
    n_sub = grid_rows // NA_SUB_ROWS
    masks = []
    for sb in (0, min(1, n_sub - 1), n_sub - 1):
        qr = sb * NA_SUB_ROWS + np.arange(NA_SUB_ROWS)[:, None]
        ks = int(np.clip(sb * NA_SUB_ROWS - NA_ROWS // 2, 0, grid_rows - NA_KEY_ROWS))
        kr = ks + np.arange(NA_KEY_ROWS)[None, :]
        rs = np.clip(qr - NA_ROWS // 2, 0, grid_rows - NA_ROWS)
        row_ok = (kr >= rs) & (kr < rs + NA_ROWS)
        masks.append(np.repeat(np.where(row_ok, 0.0, NEG_BIG), GRID_W, axis=1))
    return jnp.asarray(np.stack(masks).astype(np.float32))


def _na_latent(nq, nk, nv2, tab, layer, masks, n_rows_out, n_batch, seq, n_ctx):
    chunks = nq.shape[1] // LANES
    tq = NA_BLOCK_ROWS * GRID_W
    nblk = seq // tq
    ctx_blk0 = n_batch * seq // n_ctx
    qmap = lambda b, j, i: (b * nblk + i, j)
    tab_spec = pl.BlockSpec((1, 2) + tab.shape[2:], lambda b, j, i: (layer, j, 0, 0, 0))
    mask_spec = pl.BlockSpec(masks.shape, lambda b, j, i: (0, 0, 0))
    return pl.pallas_call(
        functools.partial(_na_kernel, grid_rows=seq // GRID_W),
        grid=(n_batch, chunks, nblk),
        in_specs=[pl.BlockSpec((tq, LANES), qmap),
                  pl.BlockSpec((n_ctx, LANES), lambda b, j, i: (ctx_blk0 + b, j)),
                  pl.BlockSpec((n_ctx, 2 * LANES), lambda b, j, i: (ctx_blk0 + b, j)),
                  pl.BlockSpec((seq, LANES), lambda b, j, i: (b, j)),
                  pl.BlockSpec((seq, 2 * LANES), lambda b, j, i: (b, j)),
                  tab_spec, mask_spec],
        out_specs=pl.BlockSpec((tq, LANES), qmap),
        out_shape=jax.ShapeDtypeStruct((n_rows_out, nq.shape[1]), BF16),
        compiler_params=_cparams(("arbitrary", "arbitrary", "arbitrary")),
        name="na_latent",
    )(nq, nk, nv2, nk, nv2, tab, masks)


def _hfilt_kernel(ft_ref, tc_ref, w1t_ref, b1_ref, w2t_ref, b2_ref, w3_ref, fr_ref, d_ref, k_ref, l1_ref, *, n):
    i = pl.program_id(0)
    fr = fr_ref[...]
    h1 = jnp.sin(fr * (jnp.dot(w1t_ref[...], ft_ref[...], precision=HIGHEST, preferred_element_type=F32)
                       + b1_ref[...]))
    h2 = jnp.sin(fr * (jnp.dot(w2t_ref[...], h1, precision=HIGHEST, preferred_element_type=F32) + b2_ref[...]))
    contract0 = (((0,), (0,)), ((), ()))
    o = lax.dot_general(h2, w3_ref[...], contract0, precision=HIGHEST, preferred_element_type=F32)
    cw = o.shape[1] // 2
    t = o.shape[0]
    row = i * t + lax.broadcasted_iota(jnp.int32, (t, cw), 0)
    k = jnp.where(row < n, o[:, :cw], o[:, cw:]) * jnp.exp(-tc_ref[...] * d_ref[...])
    k = jnp.where(row == n, 0.0, k)
    k_ref[...] = k

    @pl.when(i == 0)
    def _():
        l1_ref[...] = jnp.zeros_like(l1_ref)

    l1_ref[...] += jnp.sum(jnp.abs(k), axis=0, keepdims=True)


def _filter_feats(n):
    rows = np.arange(2 * n)
    pos = np.where(rows <= n, rows, 2 * n - rows).astype(np.float32)
    pos = jnp.asarray(np.minimum(pos, n - 1))
    t = pos / max(n - 1, 1)
    bands = jnp.linspace(1e-4, FILTER_BANDS - 1, FILTER_BANDS, dtype=F32)
    ang = (2.0 * math.pi / n) * pos[None, :] * bands[:, None]
    feats = jnp.concatenate([t[None, :], jnp.cos(ang), -jnp.sin(ang)], axis=0)
    return jnp.pad(feats, ((0, LANES - feats.shape[0]), (0, 0))), t[:, None]


def _hfilter(n, w1, b1, w2, b2, w3, freq, deltas):
    feats_t, t_col = _filter_feats(n)
    w1t = jnp.pad(w1.T, ((0, 0), (0, LANES - w1.shape[0])))
    col = lambda a: a.reshape(-1, 1)
    rows = feats_t.shape[1]
    t = min(1024, rows)
    cw = w3.shape[1] // 2
    const = lambda i: (0, 0)
    full = lambda a: pl.BlockSpec(a.shape, const)
    args = (w1t, col(b1), w2.T, col(b2), w3, col(freq), jnp.abs(deltas))
    return pl.pallas_call(
        functools.partial(_hfilt_kernel, n=n),
        grid=(rows // t,),
        in_specs=[pl.BlockSpec((LANES, t), lambda i: (0, i)), pl.BlockSpec((t, 1), lambda i: (i, 0))]
        + [full(a) for a in args],
        out_specs=[pl.BlockSpec((t, cw), lambda i: (i, 0)), pl.BlockSpec((1, cw), const)],
        out_shape=[jax.ShapeDtypeStruct((rows, cw), F32), jax.ShapeDtypeStruct((1, cw), F32)],
        compiler_params=_cparams(("arbitrary",)),
        name="hyena_filter",
    )(feats_t, t_col, *args)


def _dft_outer_kernel(f_ref, x_ref, o_ref):
    c = x_ref.shape[-1]
    x = x_ref[...].reshape(f_ref.shape[1], c).astype(BF16)
    o_ref[...] = jnp.dot(f_ref[...], x, preferred_element_type=F32).astype(o_ref.dtype).reshape(o_ref.shape)


def _kron_table(fmat, cols):
    return jnp.asarray(np.kron(np.asarray(fmat, np.float32), np.eye(cols, dtype=np.float32)).astype(BF16))


def _outer_cols(dtype):
    return OUTER_COLS * 4 // jnp.dtype(dtype).itemsize


def _dft_outer_fwd(fmat, x3, nb, n1_len, out_dtype):
    k = fmat.shape[1]
    c = x3.shape[2]
    cols = _outer_cols(out_dtype)
    f = _kron_table(fmat, cols)
    return pl.pallas_call(
        _dft_outer_kernel,
        grid=(nb, DFT_N2 // cols),
        in_specs=[pl.BlockSpec(f.shape, lambda b, j: (0, 0), pipeline_mode=pl.Buffered(1)),
                  pl.BlockSpec((k, cols, c), lambda b, j: (b, j, 0))],
        out_specs=pl.BlockSpec((1, 2, n1_len, cols, c), lambda b, j: (b, 0, 0, j, 0)),
        out_shape=jax.ShapeDtypeStruct((nb, 2, n1_len, DFT_N2, c), out_dtype),
        compiler_params=_cparams(("arbitrary", "arbitrary")),
        name="dft_outer_fwd",
    )(f, x3)


def _dft_outer_inv(fmat, d5, out_groups):
    nb, _, n1_len, _, c = d5.shape
    half = fmat.shape[0]
    cols = _outer_cols(d5.dtype)
    f = _kron_table(fmat, cols)
    return pl.pallas_call(
        _dft_outer_kernel,
        grid=(nb, DFT_N2 // cols),
        in_specs=[pl.BlockSpec(f.shape, lambda b, j: (0, 0), pipeline_mode=pl.Buffered(1)),
                  pl.BlockSpec((1, 2, n1_len, cols, c), lambda b, j: (b, 0, 0, j, 0))],
        out_specs=pl.BlockSpec((half, cols, c), lambda b, j: (b, j, 0)),
        out_shape=jax.ShapeDtypeStruct((out_groups, DFT_N2, c), F32),
        compiler_params=_cparams(("arbitrary", "arbitrary")),
        name="dft_outer_inv",
    )(f, d5)


def _inner_matrices(fc, fs, tc, ts):
    gr = fc * tc - fs * ts
    gi = -(fc * ts + fs * tc)
    return gr, gi


def _real_form(gr, gi):
    return jnp.concatenate([jnp.concatenate([gr, -gi], axis=1), jnp.concatenate([gi, gr], axis=1)], axis=0)


def _dft_inner_filter_kernel(fc_ref, fs_ref, tc_ref, ts_ref, b_ref, o_ref):
    fc, fs = fc_ref[...], fs_ref[...]
    n2, c = b_ref.shape[3], b_ref.shape[4]

    def body(kk, carry):
        g = _real_form(*_inner_matrices(fc, fs, tc_ref[kk], ts_ref[kk]))
        o_ref[kk] = jnp.dot(g.astype(BF16), b_ref[0, :, kk].reshape(2 * n2, c).astype(BF16),
                            preferred_element_type=F32)
        return carry

    lax.fori_loop(0, b_ref.shape[2], body, 0, unroll=2)


def _dft_inner_conv_kernel(fc_ref, fs_ref, tc_ref, ts_ref, b_ref, kh_ref, sc_ref, o_ref):
    fc, fs = fc_ref[...], fs_ref[...]
    nb, n2, c = b_ref.shape[0], b_ref.shape[3], b_ref.shape[4]
    scale = sc_ref[...]

    def body(kk, carry):
        gr, gi = _inner_matrices(fc, fs, tc_ref[kk], ts_ref[kk])
        g = _real_form(gr, gi).astype(BF16)
        gh = _real_form(gr.T, -gi.T).astype(BF16)
        kh = kh_ref[kk]
        kr, ki = kh[:n2], kh[n2:]
        for b in range(nb):
            x = jnp.dot(g, b_ref[b, :, kk].reshape(2 * n2, c).astype(BF16), preferred_element_type=F32)
            xr, xi = x[:n2], x[n2:]
            y = jnp.concatenate([(xr * kr - xi * ki) * scale, (xr * ki + xi * kr) * scale], axis=0)
            d = jnp.dot(gh, y.astype(BF16), preferred_element_type=F32)
            o_ref[b, :, kk] = d.astype(o_ref.dtype).reshape(2, n2, c)
        return carry

    lax.fori_loop(0, b_ref.shape[2], body, 0, unroll=2)


def _dft_tables(n1_len):
    n_total = n1_len * DFT_N2
    a = np.arange(DFT_N2)
    ang = 2.0 * np.pi * np.outer(a, a) / DFT_N2
    k1 = np.arange(n1_len)
    tw = 2.0 * np.pi * np.outer(k1, a) / n_total
    o = 2.0 * np.pi * np.outer(k1, k1) / n1_len
    f32 = lambda v: jnp.asarray(v.astype(np.float32))
    outer_fwd = np.concatenate([np.cos(o), -np.sin(o)], axis=0)
    outer_inv = np.concatenate([np.cos(o), -np.sin(o)], axis=1)
    return dict(
        fc=f32(np.cos(ang)), fs=f32(np.sin(ang)),
        tc=f32(np.cos(tw)).reshape(n1_len, 1, DFT_N2), ts=f32(np.sin(tw)).reshape(n1_len, 1, DFT_N2),
        outer_fwd_full=outer_fwd, outer_fwd_half=outer_fwd[:, :n1_len // 2], outer_inv_half=outer_inv[:n1_len // 2],
    )


def _inner_specs():
    sq = pl.BlockSpec((DFT_N2, DFT_N2), lambda k: (0, 0))
    tw = pl.BlockSpec((INNER_K1, 1, DFT_N2), lambda k: (k, 0, 0))
    return [sq, sq, tw, tw]


def _filter_spectrum(kfull, tabs, n1_len):
    c = kfull.shape[1]
    b5 = _dft_outer_fwd(tabs["outer_fwd_full"], kfull.reshape(n1_len, DFT_N2, c), 1, n1_len, F32)
    return pl.pallas_call(
        _dft_inner_filter_kernel,
        grid=(n1_len // INNER_K1,),
        in_specs=_inner_specs() + [pl.BlockSpec((1, 2, INNER_K1, DFT_N2, c), lambda k: (0, 0, k, 0, 0))],
        out_specs=pl.BlockSpec((INNER_K1, 2 * DFT_N2, c), lambda k: (k, 0, 0)),
        out_shape=jax.ShapeDtypeStruct((n1_len, 2 * DFT_N2, c), F32),
        compiler_params=_cparams(("arbitrary",)),
        name="dft_inner_filter",
    )(tabs["fc"], tabs["fs"], tabs["tc"], tabs["ts"], b5)


def _long_conv(z_all, nb, khat, scale, tabs, n1_len):
    c = z_all.shape[1]
    bz = _dft_outer_fwd(tabs["outer_fwd_half"], z_all.reshape(z_all.shape[0] // DFT_N2, DFT_N2, c), nb, n1_len, BF16)
    blk = pl.BlockSpec((nb, 2, INNER_K1, DFT_N2, c), lambda k: (0, 0, k, 0, 0))
    d = pl.pallas_call(
        _dft_inner_conv_kernel,
        grid=(n1_len // INNER_K1,),
        in_specs=_inner_specs() + [blk, pl.BlockSpec((INNER_K1, 2 * DFT_N2, c), lambda k: (k, 0, 0)),
                                   pl.BlockSpec((1, c), lambda k: (0, 0))],
        out_specs=blk,
        out_shape=jax.ShapeDtypeStruct(bz.shape, BF16),
        compiler_params=_cparams(("arbitrary",)),
        name="dft_inner_conv",
    )(tabs["fc"], tabs["fs"], tabs["tc"], tabs["ts"], bz, khat, scale)
    y3 = _dft_outer_inv(tabs["outer_inv_half"], d, z_all.shape[0] // DFT_N2)
    return y3.reshape(z_all.shape)


def _hyena_ctx_kernel(ff_ref, fi_ref, k_ref, l1_ref, z_ref, yconv_hbm_ref, o_ref):
    del yconv_hbm_ref
    ff = ff_ref[...]
    n = z_ref.shape[0]
    big = ff.shape[1]
    kh = jnp.dot(ff, k_ref[...], precision=HIGHEST, preferred_element_type=F32)
    x = jnp.dot(ff[:, :n], z_ref[...], precision=HIGHEST, preferred_element_type=F32)
    kr, ki = kh[:big], kh[big:]
    xr, xi = x[:big], x[big:]
    scale = 1.0 / (l1_ref[...] * big)
    y = jnp.concatenate([(xr * kr - xi * ki) * scale, (xr * ki + xi * kr) * scale], axis=0)
    o_ref[...] = jnp.dot(fi_ref[...], y, precision=HIGHEST, preferred_element_type=F32)


def _hyena_ctx(z, kfull, l1, row0_blk, n_batch, n, yconv):
    big = 2 * n
    a = np.arange(big)
    ang = 2.0 * np.pi * np.outer(a, a) / big
    ff = jnp.asarray(np.concatenate([np.cos(ang), -np.sin(ang)], axis=0).astype(np.float32))
    fi = jnp.asarray(np.concatenate([np.cos(ang), -np.sin(ang)], axis=1)[:n].astype(np.float32))
    c = z.shape[1]
    const = lambda b: (0, 0)
    return pl.pallas_call(
        _hyena_ctx_kernel,
        grid=(n_batch,),
        in_specs=[pl.BlockSpec(ff.shape, const), pl.BlockSpec(fi.shape, const), pl.BlockSpec(kfull.shape, const),
                  pl.BlockSpec((1, c), const), pl.BlockSpec((n, c), lambda b: (row0_blk + b, 0)),
                  pl.BlockSpec(memory_space=pl.ANY)],
        out_specs=pl.BlockSpec((n, c), lambda b: (row0_blk + b, 0)),
        out_shape=jax.ShapeDtypeStruct(yconv.shape, F32),
        input_output_aliases={5: 0},
        compiler_params=_cparams(("arbitrary",)),
        name="hyena_ctx",
    )(ff, fi, kfull, l1, z, yconv)


def _rope_tables(seq, n_batch, n_ctx_rows):
    t = jnp.arange(seq, dtype=jnp.int32)
    rows = (t // GRID_W).astype(F32)
    cols = (t % GRID_W).astype(F32)
    nf = HEAD_DIM // 4
    inv = ROPE_THETA ** (-jnp.arange(nf, dtype=F32) / nf)
    ar, ac = rows[:, None] * inv, cols[:, None] * inv
    cos = jnp.concatenate([jnp.cos(ar), jnp.cos(ar), jnp.cos(ac), jnp.cos(ac)], axis=1)
    sin = jnp.concatenate([-jnp.sin(ar), jnp.sin(ar), -jnp.sin(ac), jnp.sin(ac)], axis=1)
    cos = jnp.tile(cos, (n_batch, LANES // HEAD_DIM))
    sin = jnp.tile(sin, (n_batch, LANES // HEAD_DIM))
    cos = jnp.concatenate([cos, jnp.ones((n_ctx_rows, LANES), F32)], axis=0)
    sin = jnp.concatenate([sin, jnp.zeros((n_ctx_rows, LANES), F32)], axis=0)
    return cos, sin


def _head_mean_matrix():
    a = np.arange(LANES)
    return jnp.asarray((a[:, None] // HEAD_DIM == a[None, :] // HEAD_DIM).astype(np.float32) / HEAD_DIM, dtype=BF16)


def kernel(x, c, ctx, c_ctx, w_ada, b_ada, g_ffn1, w_ffn1_gate, w_ffn1_up, w_ffn1_down, g_mix, w_in, w_out, g_q_attn, g_k_attn, conv_w, conv_b, filt_w1, filt_b1, filt_w2, filt_b2, filt_w3, filt_freq, hyena_skip, g_q_na, g_k_na, na_rpb, g_ffn2, w_ffn2_gate, w_ffn2_up, w_ffn2_down):
    n_batch, seq, d = x.shape
    n_ctx = ctx.shape[1]
    depth = w_ada.shape[0]
    assert d == D_MODEL and n_batch + 1 <= 8 and seq % TOKEN_TILE == 0 and seq % (NA_BLOCK_ROWS * GRID_W) == 0
    assert n_ctx == 256 and (n_batch * n_ctx) % TOKEN_TILE == 0
    n_lat = n_batch * seq
    n_all = n_lat + n_batch * n_ctx
    tiles_per_batch = seq // TOKEN_TILE
    hy_w = conv_w.shape[2] // 3
    n1_len = 2 * seq // DFT_N2

    cond8 = jnp.zeros((8, d), F32).at[:n_batch].set(c).at[n_batch].set(c_ctx)
    mods = _adaln_all(cond8, w_ada, b_ada).reshape(depth * 8 * N_MOD, 1, d)
    h = x.reshape(n_lat, d)
    h_ctx = ctx.reshape(n_batch * n_ctx, d)

    cos_t, sin_t = _rope_tables(seq, n_batch, n_batch * n_ctx)
    headmean = _head_mean_matrix()
    na_masks = _na_row_masks(seq // GRID_W)
    na_tabs = jax.vmap(_na_bias_tables)(na_rpb)
    tabs = _dft_tables(n1_len)
    deltas = jnp.linspace(math.log(DECAY_TARGET) / SLOW_DECAY_PCT, math.log(DECAY_TARGET) / FAST_DECAY_PCT,
                          hy_w, dtype=F32).reshape(1, hy_w)

    q_heads = [hd for pair in range(ATTN_HEADS // 2) for hd in (pair, pair + ATTN_HEADS // 2)]
    q_segs = [(hd * HEAD_DIM, (hd + 1) * HEAD_DIM) for hd in q_heads]
    in_segs = q_segs + [(512, 640), (1536, 1792), (1792, 2048), (640, 768), (2048, 2304), (768, 1536)]
    w_in_p = jnp.concatenate([w_in[:, :, a:b] for a, b in in_segs], axis=2).astype(BF16)
    wa = jnp.concatenate([w_out[:, a:b] for a, b in q_segs], axis=1).astype(BF16)
    wb = w_out[:, 512:512 + hy_w].astype(BF16)
    wc = w_out[:, 512 + hy_w:].astype(BF16)
    ffn1_w = tuple(w.astype(BF16) for w in (w_ffn1_gate, w_ffn1_up, w_ffn1_down))
    ffn2_w = tuple(w.astype(BF16) for w in (w_ffn2_gate, w_ffn2_up, w_ffn2_down))

    for l in range(depth):
        last = l == depth - 1
        base = l * 8 * N_MOD
        rep = lambda g, k: jnp.tile(g, k)
        gvec = jnp.concatenate([rep(g_q_attn[l], ATTN_HEADS), rep(g_k_attn[l], ATTN_KV_HEADS),
                                rep(g_q_na[l], NA_HEADS), rep(g_k_na[l], NA_HEADS)]).reshape(1, N_NORM)

        h = _ffn(h, mods, base, (0, 1, 2), g_ffn1[l].reshape(1, d), l, *ffn1_w, n_all, tiles_per_batch, n_batch,
                 h_tail=h_ctx if l == 0 else None)

        q, k, nq, nk, v, nv, z, x0 = _inproj(h, mods, base, g_mix[l].reshape(1, d), l, w_in_p, gvec, headmean, cos_t, sin_t,
                                             conv_w[l], conv_b[l].reshape(1, -1), tiles_per_batch, n_batch, seq, n_ctx)

        out_rows = n_lat if last else n_all
        ya = _gqa(q, k, v, out_rows, n_batch, seq, n_ctx, latent=True)
        yc = _na_latent(nq, nk, nv, na_tabs, l, na_masks, out_rows, n_batch, seq, n_ctx)

        fargs = (filt_w1[l], filt_b1[l].reshape(1, -1), filt_w2[l], filt_b2[l].reshape(1, -1), filt_w3[l],
                 filt_freq[l].reshape(1, -1), deltas)
        kfull, l1 = _hfilter(seq, *fargs)
        khat = _filter_spectrum(kfull, tabs, n1_len)
        yconv = _long_conv(z, n_batch, khat, 1.0 / (l1 * (2 * seq)), tabs, n1_len)

        if not last:
            ya_c = _gqa(q, k, v, n_all, n_batch, seq, n_ctx, latent=False)
            yc_c = _gqa(nq, nk, nv, n_all, n_batch, seq, n_ctx, latent=False, kv_chunked=True)
            ya = lax.dynamic_update_slice(ya, ya_c[n_lat:], (n_lat, 0))
            yc = lax.dynamic_update_slice(yc, yc_c[n_lat:], (n_lat, 0))
            kfull_c, l1_c = _hfilter(n_ctx, *fargs)
            yconv = _hyena_ctx(z, kfull_c, l1_c, n_lat // n_ctx, n_batch, n_ctx, yconv)

        h = _ffn(h, mods, base, (6, 7, 8), g_ffn2[l].reshape(1, d), l, *ffn2_w, out_rows, tiles_per_batch, n_batch,
                 mixer=(ya, yconv, z, x0, hyena_skip[l].reshape(1, hy_w), yc, wa, wb, wc))

    return h[:n_lat].reshape(n_batch, seq, d)
```

```python
import functools
import math

import numpy as np
import jax
import jax.numpy as jnp
from jax import lax
from jax.experimental import pallas as pl
from jax.experimental.pallas import tpu as pltpu

F32 = jnp.float32
BF16 = jnp.bfloat16
HIGHEST = lax.Precision.HIGHEST

D_MODEL = 1024
HEAD_DIM = 64
GRID_W = 64
ATTN_HEADS = 8
ATTN_KV_HEADS = 2
NA_HEADS = 4
NA_ROWS = 8
NA_COLS = 16
FILTER_BANDS = 16
DECAY_TARGET = 1e-2
FAST_DECAY_PCT = 0.3
SLOW_DECAY_PCT = 1.5
ROPE_THETA = 10000.0
EPS = 1e-6
N_MOD = 9

LANES = 128
TOKEN_TILE = 512
FFN_CHUNK = 256
ATTN_TQ = 1024
ATTN_TK = 2048
NA_BLOCK_ROWS = 32
NA_SUB_ROWS = 4
NA_KEY_ROWS = 12
DFT_N2 = 128
OUTER_COLS = 8
INNER_K1 = 16
NEG_BIG = -1e30
VMEM_LIMIT = 56 * 1024 * 1024


def _cparams(sem):
    return pltpu.CompilerParams(dimension_semantics=sem, vmem_limit_bytes=VMEM_LIMIT)


def _ada_kernel(c_ref, w_ref, b_ref, o_ref):
    c = c_ref[...]
    s = c * jax.nn.sigmoid(c)
    o_ref[0] = jnp.dot(s, w_ref[0], preferred_element_type=F32) + b_ref[0]


def _adaln_all(cond8, w_ada, b_ada):
    depth, d, nd = w_ada.shape
    tn = 2304
    return pl.pallas_call(
        _ada_kernel,
        grid=(depth, nd // tn),
        in_specs=[
            pl.BlockSpec((8, d), lambda l, j: (0, 0)),
            pl.BlockSpec((1, d, tn), lambda l, j: (l, 0, j)),
            pl.BlockSpec((1, 1, tn), lambda l, j: (l, 0, j)),
        ],
        out_specs=pl.BlockSpec((1, 8, tn), lambda l, j: (l, 0, j)),
        out_shape=jax.ShapeDtypeStruct((depth, 8, nd), F32),
        compiler_params=_cparams(("arbitrary", "arbitrary")),
        name="adaln",
    )(cond8, w_ada, b_ada.reshape(depth, 1, nd))


def _norm_mod(x, g, sh, sc):
    ms = jnp.mean(x * x, axis=-1, keepdims=True)
    y = x * lax.rsqrt(ms + EPS) * g
    return y * (1.0 + sc) + sh


def _mod_specs(tiles_per_batch, n_batch, base, idxs):
    def spec(which):
        return pl.BlockSpec(
            (1, 1, D_MODEL),
            lambda i, w=which: (base + jnp.minimum(i // tiles_per_batch, n_batch) * N_MOD + w, 0, 0))
    return [spec(w) for w in idxs]


def _ffn_kernel(h_ref, *refs, with_mixer, split_tile):
    x = h_ref[...]
    if split_tile is not None:
        x = jnp.where(pl.program_id(0) < split_tile, x, refs[0][...])
        refs = refs[1:]
    if with_mixer:
        gt2_ref, ya_ref, yconv_ref, z_ref, x0_ref, skip_ref, yc_ref, wa_ref, wb_ref, wc_ref = refs[:10]
        refs = refs[10:]
        yb = ((yconv_ref[...] + z_ref[...] * skip_ref[...]) * x0_ref[...]).astype(BF16)
        x = x + gt2_ref[0] * (jnp.dot(ya_ref[...], wa_ref[0], preferred_element_type=F32)
                              + jnp.dot(yb, wb_ref[0], preferred_element_type=F32)
                              + jnp.dot(yc_ref[...], wc_ref[0], preferred_element_type=F32))
    sh_ref, sc_ref, gt_ref, g_ref, wg_ref, wu_ref, wd_ref, o_ref, xb_ref, acc_ref = refs
    x_is_new = with_mixer or split_tile is not None
    if x_is_new:
        o_ref[...] = x
    xb_ref[...] = _norm_mod(x, g_ref[...], sh_ref[0], sc_ref[0]).astype(BF16)

    for k in range(wg_ref.shape[2] // FFN_CHUNK):
        cols = slice(k * FFN_CHUNK, (k + 1) * FFN_CHUNK)
        xb = xb_ref[...]
        a = jnp.dot(xb, wg_ref[0, :, cols], preferred_element_type=F32)
        u = jnp.dot(xb, wu_ref[0, :, cols], preferred_element_type=F32)
        act = (a * jax.nn.sigmoid(a) * u).astype(BF16)
        part = jnp.dot(act, wd_ref[0, cols, :], preferred_element_type=F32)
        if k == 0:
            acc_ref[...] = part
        else:
            acc_ref[...] += part
    resid = o_ref[...] if x_is_new else h_ref[...]
    o_ref[...] = resid + 0.5 * gt_ref[0] * acc_ref[...]


def _ffn(h, mods, mod_base, which, g, layer, wg, wu, wd, n_rows, tiles_per_batch, n_batch, mixer=None, h_tail=None):
    d = h.shape[1]
    of_layer = lambda a: pl.BlockSpec((1,) + a.shape[1:], lambda i: (layer, 0, 0), pipeline_mode=pl.Buffered(1))
    tm = TOKEN_TILE
    row = lambda i: (i, 0)
    const = lambda i: (0, 0)
    rows = lambda a: pl.BlockSpec((tm, a.shape[1]), row)
    full = lambda a: pl.BlockSpec(a.shape, const)
    mix_specs, mix_args = [], ()
    if mixer is not None:
        ya, yconv, z, x0, skip, yc, wa, wb, wc = mixer
        mix_specs = _mod_specs(tiles_per_batch, n_batch, mod_base, (5,)) + [
            rows(ya), rows(yconv), rows(z), rows(x0), full(skip), rows(yc), of_layer(wa), of_layer(wb), of_layer(wc)]
        mix_args = (mods,) + tuple(mixer)
    h_specs, h_args, split = [rows(h)], (h,), None
    if h_tail is not None:
        split = h.shape[0] // tm
        h_specs = [pl.BlockSpec((tm, d), lambda i: (jnp.minimum(i, split - 1), 0)),
                   pl.BlockSpec((tm, d), lambda i: (jnp.maximum(i - split, 0), 0))]
        h_args = (h, h_tail)
    return pl.pallas_call(
        functools.partial(_ffn_kernel, with_mixer=mixer is not None, split_tile=split),
        grid=(n_rows // tm,),
        in_specs=h_specs + mix_specs + _mod_specs(tiles_per_batch, n_batch, mod_base, which)
        + [full(g), of_layer(wg), of_layer(wu), of_layer(wd)],
        out_specs=pl.BlockSpec((tm, d), row),
        out_shape=jax.ShapeDtypeStruct((n_rows, d), F32),
        scratch_shapes=[pltpu.VMEM((tm, d), BF16), pltpu.VMEM((tm, d), F32)],
        compiler_params=_cparams(("arbitrary",)),
        name="ffn_mix" if mixer is not None else "ffn",
    )(*h_args, *mix_args, mods, mods, mods, g, wg, wu, wd)


N_NORM = 1152
N_ROPE = 640
ATTN_Q_CHUNKS = (0, 1, 2, 3)
NA_Q_CHUNKS = (5, 6)
LOG2E = math.log2(math.e)


def _inproj_kernel(h_ref, prev_ref, next_ref, sh_ref, sc_ref, g_ref, w_ref, gv_ref, hm_ref, cos_ref, sin_ref,
                   cw_ref, cb_ref, q_ref, k_ref, nq_ref, nk_ref, v_ref, nv_ref, z_ref, x0_ref, *, seq, n_ctx, n_lat):
    xb = _norm_mod(h_ref[...], g_ref[...], sh_ref[0], sc_ref[0]).astype(BF16)
    hm = hm_ref[...]
    cosv = cos_ref[...]
    sinv = sin_ref[...]
    lane = lax.broadcasted_iota(jnp.int32, cosv.shape, 1)
    first_half = (lane % 32) < 16
    outs = []
    p_norm = jnp.dot(xb, w_ref[0, :, 0:N_NORM], preferred_element_type=F32)
    for c in range(N_NORM // LANES):
        p = p_norm[:, c * LANES:(c + 1) * LANES]
        ms = jnp.dot((p * p).astype(BF16), hm, preferred_element_type=F32)
        y = p * lax.rsqrt(ms + EPS) * gv_ref[:, c * LANES:(c + 1) * LANES]
        if c < N_ROPE // LANES:
            swapped = jnp.where(first_half, pltpu.roll(y, LANES - 16, 1), pltpu.roll(y, 16, 1))
            y = y * cosv + swapped * sinv
        if c in ATTN_Q_CHUNKS or c in NA_Q_CHUNKS:
            y = y * (HEAD_DIM ** -0.5 * LOG2E)
        outs.append(y.astype(BF16))
    for c in range(4):
        q_ref[:, c * LANES:(c + 1) * LANES] = outs[c]
    k_ref[...] = outs[4]
    nq_ref[:, 0:LANES] = outs[5]
    nq_ref[:, LANES:2 * LANES] = outs[6]
    nk_ref[:, 0:LANES] = outs[7]
    nk_ref[:, LANES:2 * LANES] = outs[8]
    p_rest = jnp.dot(xb, w_ref[0, :, N_NORM:], preferred_element_type=F32)
    v = p_rest[:, 0:LANES]
    v_ref[:, 0:LANES] = jnp.where(lane < HEAD_DIM, v, 1.0).astype(BF16)
    v_ref[:, LANES:2 * LANES] = jnp.where(lane < HEAD_DIM, 1.0, v).astype(BF16)
    for c in range(NA_HEADS // 2):
        nv = p_rest[:, (1 + c) * LANES:(2 + c) * LANES]
        nv_ref[:, 2 * c * LANES:(2 * c + 1) * LANES] = jnp.where(lane < HEAD_DIM, nv, 1.0).astype(BF16)
        nv_ref[:, (2 * c + 1) * LANES:(2 * c + 2) * LANES] = jnp.where(lane < HEAD_DIM, 1.0, nv).astype(BF16)

    hy = p_rest[:, 3 * LANES:]
    tm = hy.shape[0]
    halo = jnp.concatenate([prev_ref[...], next_ref[...]], axis=0)
    halo_b = _norm_mod(halo, g_ref[...], sh_ref[0], sc_ref[0]).astype(BF16)
    hy_halo = jnp.dot(halo_b, w_ref[0, :, N_NORM + 3 * LANES:], preferred_element_type=F32)
    row0 = pl.program_id(0) * tm
    is_ctx = row0 >= n_lat
    period = jnp.where(is_ctx, n_ctx, seq)
    local = lax.broadcasted_iota(jnp.int32, (tm, 1), 0)
    pos = jnp.bitwise_and(row0 + local - jnp.where(is_ctx, n_lat, 0), period - 1)
    up = jnp.where(local == 0, hy_halo[7:8, :], pltpu.roll(hy, 1, 0))
    up = jnp.where(pos == 0, 0.0, up)
    dn = jnp.where(local == tm - 1, hy_halo[8:9, :], pltpu.roll(hy, tm - 1, 0))
    dn = jnp.where(pos == period - 1, 0.0, dn)
    uc = up * cw_ref[0:1, :] + hy * cw_ref[1:2, :] + dn * cw_ref[2:3, :] + cb_ref[...]
    cw = uc.shape[1] // 3
    x0_ref[...] = uc[:, :cw]
    z_ref[...] = uc[:, 2 * cw:] * uc[:, cw:2 * cw]


def _inproj(h, mods, mod_base, g, layer, w, gvec, headmean, cos_t, sin_t, conv_w, conv_b, tiles_per_batch, n_batch, seq,
            n_ctx):
    n_rows, d = h.shape
    tm = TOKEN_TILE
    assert seq & (seq - 1) == 0 and n_ctx & (n_ctx - 1) == 0
    row = lambda i: (i, 0)
    const = lambda i: (0, 0)
    widths = (512, 128, 256, 256, 256, 512)
    hy_w = conv_w.shape[1] // 3
    out_shape = [jax.ShapeDtypeStruct((n_rows, wd), BF16) for wd in widths]
    out_shape += [jax.ShapeDtypeStruct((n_rows, hy_w), F32)] * 2
    out_specs = [pl.BlockSpec((tm, wd), row) for wd in widths] + [pl.BlockSpec((tm, hy_w), row)] * 2
    return pl.pallas_call(
        functools.partial(_inproj_kernel, seq=seq, n_ctx=n_ctx, n_lat=n_batch * seq),
        grid=(n_rows // tm,),
        in_specs=[pl.BlockSpec((tm, d), row),
                  pl.BlockSpec((8, d), lambda i: (jnp.maximum(i * (tm // 8) - 1, 0), 0)),
                  pl.BlockSpec((8, d), lambda i: (jnp.minimum((i + 1) * (tm // 8), n_rows // 8 - 1), 0))]
        + _mod_specs(tiles_per_batch, n_batch, mod_base, (3, 4))
        + [
            pl.BlockSpec((1, d), const),
            pl.BlockSpec((1,) + w.shape[1:], lambda i: (layer, 0, 0), pipeline_mode=pl.Buffered(1)),
            pl.BlockSpec(gvec.shape, const),
            pl.BlockSpec(headmean.shape, const),
            pl.BlockSpec((tm, LANES), row),
            pl.BlockSpec((tm, LANES), row),
            pl.BlockSpec(conv_w.shape, const),
            pl.BlockSpec(conv_b.shape, const),
        ],
        out_specs=out_specs,
        out_shape=out_shape,
        compiler_params=_cparams(("arbitrary",)),
        name="inproj",
    )(h, h, h, mods, mods, g, w, gvec, headmean, cos_t, sin_t, conv_w, conv_b)


def _stack_heads(q):
    lane = lax.broadcasted_iota(jnp.int32, q.shape, 1)
    zero = jnp.zeros_like(q)
    return jnp.concatenate([jnp.where(lane < HEAD_DIM, q, zero), jnp.where(lane >= HEAD_DIM, q, zero)], axis=0)


def _qk(q2, k):
    return lax.dot_general(q2, k, (((1,), (1,)), ((), ())), preferred_element_type=F32)


def _gqa_kernel(q_ref, kc_ref, vc_ref, *rest, n_latent_chunks, tk):
    if n_latent_chunks:
        kl_ref, vl_ref, o_ref = rest
    else:
        (o_ref,) = rest
    tq = q_ref.shape[0]
    q2 = _stack_heads(q_ref[...])

    def pv(p, v_ref, rows):
        return jnp.concatenate(
            [jnp.dot(p[:tq], v_ref[rows, 0:LANES], preferred_element_type=F32),
             jnp.dot(p[tq:], v_ref[rows, LANES:2 * LANES], preferred_element_type=F32)], axis=0)

    s = _qk(q2, kc_ref[...])
    m = jnp.max(s, axis=-1, keepdims=True)
    if n_latent_chunks:
        rows0 = pl.ds(0, tk)
        s0 = _qk(q2, kl_ref[rows0, :])
        m = jnp.maximum(m, jnp.max(s0, axis=-1, keepdims=True))
        acc = (pv(jnp.exp2((s - m).astype(BF16)), vc_ref, slice(None))
               + pv(jnp.exp2((s0 - m).astype(BF16)), vl_ref, rows0))
    else:
        acc = pv(jnp.exp2((s - m).astype(BF16)), vc_ref, slice(None))

    if n_latent_chunks:
        def body(j, carry):
            m, acc = carry
            rows = pl.ds(pl.multiple_of(j * tk, tk), tk)
            s = _qk(q2, kl_ref[rows, :])
            m_new = jnp.maximum(m, jnp.max(s, axis=-1, keepdims=True))
            p = jnp.exp2((s - m_new).astype(BF16))
            return m_new, jnp.exp2(m - m_new) * acc + pv(p, vl_ref, rows)

        m, acc = lax.fori_loop(1, n_latent_chunks, body, (m, acc), unroll=True)
    lane = lax.broadcasted_iota(jnp.int32, (tq, LANES), 1)
    num = jnp.where(lane < HEAD_DIM, acc[:tq], acc[tq:])
    den = pltpu.roll(jnp.where(lane < HEAD_DIM, acc[tq:], acc[:tq]), HEAD_DIM, 1)
    o_ref[...] = (num / den).astype(o_ref.dtype)


def _gqa(q, k, v2, out_rows, n_batch, seq, n_ctx, latent, kv_chunked=False):
    chunks = q.shape[1] // LANES
    ctx_blk0 = n_batch * seq // n_ctx
    tk = min(ATTN_TK, seq)
    kj = (lambda j: j) if kv_chunked else (lambda j: 0)
    if latent:
        tq = ATTN_TQ
        grid = (n_batch, chunks, seq // tq)
        qmap = lambda b, j, i: (b * (seq // tq) + i, j)
        kv_specs = [pl.BlockSpec((seq, LANES), lambda b, j, i: (b, kj(j))),
                    pl.BlockSpec((seq, 2 * LANES), lambda b, j, i: (b, kj(j)))]
        kv_args = (k, v2)
    else:
        tq = n_ctx
        grid = (n_batch, chunks, 1)
        qmap = lambda b, j, i: (ctx_blk0 + b, j)
        kv_specs, kv_args = [], ()
    return pl.pallas_call(
        functools.partial(_gqa_kernel, n_latent_chunks=seq // tk if latent else 0, tk=tk),
        grid=grid,
        in_specs=[pl.BlockSpec((tq, LANES), qmap),
                  pl.BlockSpec((n_ctx, LANES), lambda b, j, i: (ctx_blk0 + b, kj(j))),
                  pl.BlockSpec((n_ctx, 2 * LANES), lambda b, j, i: (ctx_blk0 + b, kj(j)))] + kv_specs,
        out_specs=pl.BlockSpec((tq, LANES), qmap),
        out_shape=jax.ShapeDtypeStruct((out_rows, q.shape[1]), BF16),
        compiler_params=_cparams(("arbitrary", "arbitrary", "arbitrary")),
        name="attn_latent" if latent else "attn_ctx",
    )(q, k, v2, *kv_args)


def _na_kernel(q_ref, kc_ref, vc_ref, kl_ref, vl_ref, tab_ref, mask_ref, o_ref, *, grid_rows):
    i = pl.program_id(2)
    nk = NA_KEY_ROWS * GRID_W
    n_sub = NA_BLOCK_ROWS // NA_SUB_ROWS
    last_sub = grid_rows // NA_SUB_ROWS - 1
    kc = kc_ref[...]
    sub_rows = NA_SUB_ROWS * GRID_W
    lane = lax.broadcasted_iota(jnp.int32, (sub_rows, LANES), 1)
    low = lane < HEAD_DIM
    for sub in range(n_sub):
        sb = i * n_sub + sub
        r0 = sb * NA_SUB_ROWS
        ks = jnp.clip(r0 - NA_ROWS // 2, 0, grid_rows - NA_KEY_ROWS)
        keys = pl.ds(pl.multiple_of(ks * GRID_W, GRID_W), nk)
        shift = ks - r0 + 2 * NA_ROWS
        variant = jnp.where(sb == 0, 0, jnp.where(sb == last_sub, 2, 1))
        k = kl_ref[keys, :]
        rows = slice(sub * sub_rows, (sub + 1) * sub_rows)
        q = q_ref[rows, :]
        acc = []
        for hh in range(2):
            qh = jnp.where(low if hh == 0 else jnp.logical_not(low), q, jnp.zeros_like(q))
            slabs = []
            for qr in range(NA_SUB_ROWS):
                pairs = [tab_ref[0, hh, shift - qr + 2 * kp] for kp in range(NA_KEY_ROWS // 2)]
                slabs.append(jnp.concatenate(pairs, axis=1) + mask_ref[variant, qr:qr + 1, :])
            s_nb = _qk(qh, k) + jnp.concatenate(slabs, axis=0)
            s_cx = _qk(qh, kc)
            m = jnp.maximum(jnp.max(s_nb, axis=-1, keepdims=True), jnp.max(s_cx, axis=-1, keepdims=True))
            vcols = slice(hh * LANES, (hh + 1) * LANES)
            acc.append(jnp.dot(jnp.exp2((s_nb - m).astype(BF16)), vl_ref[keys, vcols], preferred_element_type=F32)
                       + jnp.dot(jnp.exp2((s_cx - m).astype(BF16)), vc_ref[:, vcols], preferred_element_type=F32))
        num = jnp.where(low, acc[0], acc[1])
        den = pltpu.roll(jnp.where(low, acc[1], acc[0]), HEAD_DIM, 1)
        o_ref[rows, :] = (num / den).astype(o_ref.dtype)


def _na_bias_tables(rpb):
    qc = np.arange(GRID_W)[:, None]
    kc = np.arange(GRID_W)[None, :]
    cs = np.clip(qc - NA_COLS // 2, 0, GRID_W - NA_COLS)
    col_ok = (kc >= cs) & (kc < cs + NA_COLS)
    col_hot = ((kc - qc + NA_COLS - 1)[:, :, None] == np.arange(2 * NA_COLS - 1)) & col_ok[:, :, None]
    by_col = jnp.einsum("abe,hde->hdab", jnp.asarray(col_hot, F32), rpb, precision=HIGHEST)
    by_col = jnp.where(jnp.asarray(col_ok)[None, None], by_col * LOG2E, NEG_BIG)
    t = jnp.pad(by_col, ((0, 0), (NA_ROWS + 1, NA_ROWS), (0, 0), (0, 0)))
    return jnp.concatenate([t[:, :-1], t[:, 1:]], axis=-1)


def _na_row_masks(grid_rows):
    n_sub = grid_rows // NA_SUB_ROWS
    masks = []
    for sb in (0, min(1, n_sub - 1), n_sub - 1):
        qr = sb * NA_SUB_ROWS + np.arange(NA_SUB_ROWS)[:, None]
        ks = int(np.clip(sb * NA_SUB_ROWS - NA_ROWS // 2, 0, grid_rows - NA_KEY_ROWS))
        kr = ks + np.arange(NA_KEY_ROWS)[None, :]
        rs = np.clip(qr - NA_ROWS // 2, 0, grid_rows - NA_ROWS)
        row_ok = (kr >= rs) & (kr < rs + NA_ROWS)
        masks.append(np.repeat(np.where(row_ok, 0.0, NEG_BIG), GRID_W, axis=1))
    return jnp.asarray(np.stack(masks).astype(np.float32))


def _na_latent(nq, nk, nv2, tab, layer, masks, n_rows_out, n_batch, seq, n_ctx):
    chunks = nq.shape[1] // LANES
    tq = NA_BLOCK_ROWS * GRID_W
    nblk = seq // tq
    ctx_blk0 = n_batch * seq // n_ctx
    qmap = lambda b, j, i: (b * nblk + i, j)
    tab_spec = pl.BlockSpec((1, 2) + tab.shape[2:], lambda b, j, i: (layer, j, 0, 0, 0))
    mask_spec = pl.BlockSpec(masks.shape, lambda b, j, i: (0, 0, 0))
    return pl.pallas_call(
        functools.partial(_na_kernel, grid_rows=seq // GRID_W),
        grid=(n_batch, chunks, nblk),
        in_specs=[pl.BlockSpec((tq, LANES), qmap),
                  pl.BlockSpec((n_ctx, LANES), lambda b, j, i: (ctx_blk0 + b, j)),
                  pl.BlockSpec((n_ctx, 2 * LANES), lambda b, j, i: (ctx_blk0 + b, j)),
                  pl.BlockSpec((seq, LANES), lambda b, j, i: (b, j)),
                  pl.BlockSpec((seq, 2 * LANES), lambda b, j, i: (b, j)),
                  tab_spec, mask_spec],
        out_specs=pl.BlockSpec((tq, LANES), qmap),
        out_shape=jax.ShapeDtypeStruct((n_rows_out, nq.shape[1]), BF16),
        compiler_params=_cparams(("arbitrary", "arbitrary", "arbitrary")),
        name="na_latent",
    )(nq, nk, nv2, nk, nv2, tab, masks)


def _hfilt_kernel(ft_ref, tc_ref, w1t_ref, b1_ref, w2t_ref, b2_ref, w3_ref, fr_ref, d_ref, k_ref, l1_ref, *, n):
    i = pl.program_id(0)
    fr = fr_ref[...]
    h1 = jnp.sin(fr * (jnp.dot(w1t_ref[...], ft_ref[...], precision=HIGHEST, preferred_element_type=F32)
                       + b1_ref[...]))
    h2 = jnp.sin(fr * (jnp.dot(w2t_ref[...], h1, precision=HIGHEST, preferred_element_type=F32) + b2_ref[...]))
    contract0 = (((0,), (0,)), ((), ()))
    o = lax.dot_general(h2, w3_ref[...], contract0, precision=HIGHEST, preferred_element_type=F32)
    cw = o.shape[1] // 2
    t = o.shape[0]
    row = i * t + lax.broadcasted_iota(jnp.int32, (t, cw), 0)
    k = jnp.where(row < n, o[:, :cw], o[:, cw:]) * jnp.exp(-tc_ref[...] * d_ref[...])
    k = jnp.where(row == n, 0.0, k)
    k_ref[...] = k

    @pl.when(i == 0)
    def _():
        l1_ref[...] = jnp.zeros_like(l1_ref)

    l1_ref[...] += jnp.sum(jnp.abs(k), axis=0, keepdims=True)


def _filter_feats(n):
    rows = np.arange(2 * n)
    pos = np.where(rows <= n, rows, 2 * n - rows).astype(np.float32)
    pos = jnp.asarray(np.minimum(pos, n - 1))
    t = pos / max(n - 1, 1)
    bands = jnp.linspace(1e-4, FILTER_BANDS - 1, FILTER_BANDS, dtype=F32)
    ang = (2.0 * math.pi / n) * pos[None, :] * bands[:, None]
    feats = jnp.concatenate([t[None, :], jnp.cos(ang), -jnp.sin(ang)], axis=0)
    return jnp.pad(feats, ((0, LANES - feats.shape[0]), (0, 0))), t[:, None]


def _hfilter(n, w1, b1, w2, b2, w3, freq, deltas):
    feats_t, t_col = _filter_feats(n)
    w1t = jnp.pad(w1.T, ((0, 0), (0, LANES - w1.shape[0])))
    col = lambda a: a.reshape(-1, 1)
    rows = feats_t.shape[1]
    t = min(2048, rows)
    cw = w3.shape[1] // 2
    const = lambda i: (0, 0)
    full = lambda a: pl.BlockSpec(a.shape, const)
    args = (w1t, col(b1), w2.T, col(b2), w3, col(freq), jnp.abs(deltas))
    return pl.pallas_call(
        functools.partial(_hfilt_kernel, n=n),
        grid=(rows // t,),
        in_specs=[pl.BlockSpec((LANES, t), lambda i: (0, i)), pl.BlockSpec((t, 1), lambda i: (i, 0))]
        + [full(a) for a in args],
        out_specs=[pl.BlockSpec((t, cw), lambda i: (i, 0)), pl.BlockSpec((1, cw), const)],
        out_shape=[jax.ShapeDtypeStruct((rows, cw), F32), jax.ShapeDtypeStruct((1, cw), F32)],
        compiler_params=_cparams(("arbitrary",)),
        name="hyena_filter",
    )(feats_t, t_col, *args)


def _dft_outer_kernel(f_ref, x_ref, o_ref):
    c = x_ref.shape[-1]
    x = x_ref[...].reshape(f_ref.shape[1], c).astype(BF16)
    o_ref[...] = jnp.dot(f_ref[...], x, preferred_element_type=F32).astype(o_ref.dtype).reshape(o_ref.shape)


def _kron_table(fmat, cols):
    return jnp.asarray(np.kron(np.asarray(fmat, np.float32), np.eye(cols, dtype=np.float32)).astype(BF16))


def _outer_cols(dtype):
    return OUTER_COLS * 4 // jnp.dtype(dtype).itemsize


def _dft_outer_fwd(fmat, x3, nb, n1_len, out_dtype):
    k = fmat.shape[1]
    c = x3.shape[2]
    cols = _outer_cols(out_dtype)
    f = _kron_table(fmat, cols)
    return pl.pallas_call(
        _dft_outer_kernel,
        grid=(nb, DFT_N2 // cols),
        in_specs=[pl.BlockSpec(f.shape, lambda b, j: (0, 0), pipeline_mode=pl.Buffered(1)),
                  pl.BlockSpec((k, cols, c), lambda b, j: (b, j, 0))],
        out_specs=pl.BlockSpec((1, 2, n1_len, cols, c), lambda b, j: (b, 0, 0, j, 0)),
        out_shape=jax.ShapeDtypeStruct((nb, 2, n1_len, DFT_N2, c), out_dtype),
        compiler_params=_cparams(("arbitrary", "arbitrary")),
        name="dft_outer_fwd",
    )(f, x3)


def _dft_outer_inv(fmat, d5, out_groups):
    nb, _, n1_len, _, c = d5.shape
    half = fmat.shape[0]
    cols = _outer_cols(d5.dtype)
    f = _kron_table(fmat, cols)
    return pl.pallas_call(
        _dft_outer_kernel,
        grid=(nb, DFT_N2 // cols),
        in_specs=[pl.BlockSpec(f.shape, lambda b, j: (0, 0), pipeline_mode=pl.Buffered(1)),
                  pl.BlockSpec((1, 2, n1_len, cols, c), lambda b, j: (b, 0, 0, j, 0))],
        out_specs=pl.BlockSpec((half, cols, c), lambda b, j: (b, j, 0)),
        out_shape=jax.ShapeDtypeStruct((out_groups, DFT_N2, c), F32),
        compiler_params=_cparams(("arbitrary", "arbitrary")),
        name="dft_outer_inv",
    )(f, d5)


def _inner_matrices(fc, fs, tc, ts):
    gr = fc * tc - fs * ts
    gi = -(fc * ts + fs * tc)
    return gr, gi


def _real_form(gr, gi):
    return jnp.concatenate([jnp.concatenate([gr, -gi], axis=1), jnp.concatenate([gi, gr], axis=1)], axis=0)


def _dft_inner_filter_kernel(fc_ref, fs_ref, tc_ref, ts_ref, b_ref, o_ref):
    fc, fs = fc_ref[...], fs_ref[...]
    n2, c = b_ref.shape[3], b_ref.shape[4]

    def body(kk, carry):
        g = _real_form(*_inner_matrices(fc, fs, tc_ref[kk], ts_ref[kk]))
        o_ref[kk] = jnp.dot(g.astype(BF16), b_ref[0, :, kk].reshape(2 * n2, c).astype(BF16),
                            preferred_element_type=F32)
        return carry

    lax.fori_loop(0, b_ref.shape[2], body, 0, unroll=2)


def _dft_inner_conv_kernel(fc_ref, fs_ref, tc_ref, ts_ref, b_ref, kh_ref, sc_ref, o_ref):
    fc, fs = fc_ref[...], fs_ref[...]
    nb, n2, c = b_ref.shape[0], b_ref.shape[3], b_ref.shape[4]
    scale = sc_ref[...]

    def body(kk, carry):
        gr, gi = _inner_matrices(fc, fs, tc_ref[kk], ts_ref[kk])
        g = _real_form(gr, gi).astype(BF16)
        gh = _real_form(gr.T, -gi.T).astype(BF16)
        kh = kh_ref[kk]
        kr, ki = kh[:n2], kh[n2:]
        for b in range(nb):
            x = jnp.dot(g, b_ref[b, :, kk].reshape(2 * n2, c).astype(BF16), preferred_element_type=F32)
            xr, xi = x[:n2], x[n2:]
            y = jnp.concatenate([(xr * kr - xi * ki) * scale, (xr * ki + xi * kr) * scale], axis=0)
            d = jnp.dot(gh, y.astype(BF16), preferred_element_type=F32)
            o_ref[b, :, kk] = d.astype(o_ref.dtype).reshape(2, n2, c)
        return carry

    lax.fori_loop(0, b_ref.shape[2], body, 0, unroll=2)


def _dft_tables(n1_len):
    n_total = n1_len * DFT_N2
    a = np.arange(DFT_N2)
    ang = 2.0 * np.pi * np.outer(a, a) / DFT_N2
    k1 = np.arange(n1_len)
    tw = 2.0 * np.pi * np.outer(k1, a) / n_total
    o = 2.0 * np.pi * np.outer(k1, k1) / n1_len
    f32 = lambda v: jnp.asarray(v.astype(np.float32))
    outer_fwd = np.concatenate([np.cos(o), -np.sin(o)], axis=0)
    outer_inv = np.concatenate([np.cos(o), -np.sin(o)], axis=1)
    return dict(
        fc=f32(np.cos(ang)), fs=f32(np.sin(ang)),
        tc=f32(np.cos(tw)).reshape(n1_len, 1, DFT_N2), ts=f32(np.sin(tw)).reshape(n1_len, 1, DFT_N2),
        outer_fwd_full=outer_fwd, outer_fwd_half=outer_fwd[:, :n1_len // 2], outer_inv_half=outer_inv[:n1_len // 2],
    )


def _inner_specs():
    sq = pl.BlockSpec((DFT_N2, DFT_N2), lambda k: (0, 0))
    tw = pl.BlockSpec((INNER_K1, 1, DFT_N2), lambda k: (k, 0, 0))
    return [sq, sq, tw, tw]


def _filter_spectrum(kfull, tabs, n1_len):
    c = kfull.shape[1]
    b5 = _dft_outer_fwd(tabs["outer_fwd_full"], kfull.reshape(n1_len, DFT_N2, c), 1, n1_len, F32)
    return pl.pallas_call(
        _dft_inner_filter_kernel,
        grid=(n1_len // INNER_K1,),
        in_specs=_inner_specs() + [pl.BlockSpec((1, 2, INNER_K1, DFT_N2, c), lambda k: (0, 0, k, 0, 0))],
        out_specs=pl.BlockSpec((INNER_K1, 2 * DFT_N2, c), lambda k: (k, 0, 0)),
        out_shape=jax.ShapeDtypeStruct((n1_len, 2 * DFT_N2, c), F32),
        compiler_params=_cparams(("arbitrary",)),
        name="dft_inner_filter",
    )(tabs["fc"], tabs["fs"], tabs["tc"], tabs["ts"], b5)


def _long_conv(z_all, nb, khat, scale, tabs, n1_len):
    c = z_all.shape[1]
    bz = _dft_outer_fwd(tabs["outer_fwd_half"], z_all.reshape(z_all.shape[0] // DFT_N2, DFT_N2, c), nb, n1_len, BF16)
    blk = pl.BlockSpec((nb, 2, INNER_K1, DFT_N2, c), lambda k: (0, 0, k, 0, 0))
    d = pl.pallas_call(
        _dft_inner_conv_kernel,
        grid=(n1_len // INNER_K1,),
        in_specs=_inner_specs() + [blk, pl.BlockSpec((INNER_K1, 2 * DFT_N2, c), lambda k: (k, 0, 0)),
                                   pl.BlockSpec((1, c), lambda k: (0, 0))],
        out_specs=blk,
        out_shape=jax.ShapeDtypeStruct(bz.shape, BF16),
        compiler_params=_cparams(("arbitrary",)),
        name="dft_inner_conv",
    )(tabs["fc"], tabs["fs"], tabs["tc"], tabs["ts"], bz, khat, scale)
    y3 = _dft_outer_inv(tabs["outer_inv_half"], d, z_all.shape[0] // DFT_N2)
    return y3.reshape(z_all.shape)


def _hyena_ctx_kernel(ff_ref, fi_ref, k_ref, l1_ref, z_ref, yconv_hbm_ref, o_ref):
    del yconv_hbm_ref
    ff = ff_ref[...]
    n = z_ref.shape[0]
    big = ff.shape[1]
    kh = jnp.dot(ff, k_ref[...], precision=HIGHEST, preferred_element_type=F32)
    x = jnp.dot(ff[:, :n], z_ref[...], precision=HIGHEST, preferred_element_type=F32)
    kr, ki = kh[:big], kh[big:]
    xr, xi = x[:big], x[big:]
    scale = 1.0 / (l1_ref[...] * big)
    y = jnp.concatenate([(xr * kr - xi * ki) * scale, (xr * ki + xi * kr) * scale], axis=0)
    o_ref[...] = jnp.dot(fi_ref[...], y, precision=HIGHEST, preferred_element_type=F32)


def _hyena_ctx(z, kfull, l1, row0_blk, n_batch, n, yconv):
    big = 2 * n
    a = np.arange(big)
    ang = 2.0 * np.pi * np.outer(a, a) / big
    ff = jnp.asarray(np.concatenate([np.cos(ang), -np.sin(ang)], axis=0).astype(np.float32))
    fi = jnp.asarray(np.concatenate([np.cos(ang), -np.sin(ang)], axis=1)[:n].astype(np.float32))
    c = z.shape[1]
    const = lambda b: (0, 0)
    return pl.pallas_call(
        _hyena_ctx_kernel,
        grid=(n_batch,),
        in_specs=[pl.BlockSpec(ff.shape, const), pl.BlockSpec(fi.shape, const), pl.BlockSpec(kfull.shape, const),
                  pl.BlockSpec((1, c), const), pl.BlockSpec((n, c), lambda b: (row0_blk + b, 0)),
                  pl.BlockSpec(memory_space=pl.ANY)],
        out_specs=pl.BlockSpec((n, c), lambda b: (row0_blk + b, 0)),
        out_shape=jax.ShapeDtypeStruct(yconv.shape, F32),
        input_output_aliases={5: 0},
        compiler_params=_cparams(("arbitrary",)),
        name="hyena_ctx",
    )(ff, fi, kfull, l1, z, yconv)


def _rope_tables(seq, n_batch, n_ctx_rows):
    t = jnp.arange(seq, dtype=jnp.int32)
    rows = (t // GRID_W).astype(F32)
    cols = (t % GRID_W).astype(F32)
    nf = HEAD_DIM // 4
    inv = ROPE_THETA ** (-jnp.arange(nf, dtype=F32) / nf)
    ar, ac = rows[:, None] * inv, cols[:, None] * inv
    cos = jnp.concatenate([jnp.cos(ar), jnp.cos(ar), jnp.cos(ac), jnp.cos(ac)], axis=1)
    sin = jnp.concatenate([-jnp.sin(ar), jnp.sin(ar), -jnp.sin(ac), jnp.sin(ac)], axis=1)
    cos = jnp.tile(cos, (n_batch, LANES // HEAD_DIM))
    sin = jnp.tile(sin, (n_batch, LANES // HEAD_DIM))
    cos = jnp.concatenate([cos, jnp.ones((n_ctx_rows, LANES), F32)], axis=0)
    sin = jnp.concatenate([sin, jnp.zeros((n_ctx_rows, LANES), F32)], axis=0)
    return cos, sin


def _head_mean_matrix():
    a = np.arange(LANES)
    return jnp.asarray((a[:, None] // HEAD_DIM == a[None, :] // HEAD_DIM).astype(np.float32) / HEAD_DIM, dtype=BF16)


def kernel(x, c, ctx, c_ctx, w_ada, b_ada, g_ffn1, w_ffn1_gate, w_ffn1_up, w_ffn1_down, g_mix, w_in, w_out, g_q_attn, g_k_attn, conv_w, conv_b, filt_w1, filt_b1, filt_w2, filt_b2, filt_w3, filt_freq, hyena_skip, g_q_na, g_k_na, na_rpb, g_ffn2, w_ffn2_gate, w_ffn2_up, w_ffn2_down):
    n_batch, seq, d = x.shape
    n_ctx = ctx.shape[1]
    depth = w_ada.shape[0]
    assert d == D_MODEL and n_batch + 1 <= 8 and seq % TOKEN_TILE == 0 and seq % (NA_BLOCK_ROWS * GRID_W) == 0
    assert n_ctx == 256 and (n_batch * n_ctx) % TOKEN_TILE == 0
    n_lat = n_batch * seq
    n_all = n_lat + n_batch * n_ctx
    tiles_per_batch = seq // TOKEN_TILE
    hy_w = conv_w.shape[2] // 3
    n1_len = 2 * seq // DFT_N2

    cond8 = jnp.zeros((8, d), F32).at[:n_batch].set(c).at[n_batch].set(c_ctx)
    mods = _adaln_all(cond8, w_ada, b_ada).reshape(depth * 8 * N_MOD, 1, d)
    h = x.reshape(n_lat, d)
    h_ctx = ctx.reshape(n_batch * n_ctx, d)

    cos_t, sin_t = _rope_tables(seq, n_batch, n_batch * n_ctx)
    headmean = _head_mean_matrix()
    na_masks = _na_row_masks(seq // GRID_W)
    na_tabs = jax.vmap(_na_bias_tables)(na_rpb)
    tabs = _dft_tables(n1_len)
    deltas = jnp.linspace(math.log(DECAY_TARGET) / SLOW_DECAY_PCT, math.log(DECAY_TARGET) / FAST_DECAY_PCT,
                          hy_w, dtype=F32).reshape(1, hy_w)

    q_heads = [hd for pair in range(ATTN_HEADS // 2) for hd in (pair, pair + ATTN_HEADS // 2)]
    q_segs = [(hd * HEAD_DIM, (hd + 1) * HEAD_DIM) for hd in q_heads]
    in_segs = q_segs + [(512, 640), (1536, 1792), (1792, 2048), (640, 768), (2048, 2304), (768, 1536)]
    w_in_p = jnp.concatenate([w_in[:, :, a:b] for a, b in in_segs], axis=2).astype(BF16)
    wa = jnp.concatenate([w_out[:, a:b] for a, b in q_segs], axis=1).astype(BF16)
    wb = w_out[:, 512:512 + hy_w].astype(BF16)
    wc = w_out[:, 512 + hy_w:].astype(BF16)
    ffn1_w = tuple(w.astype(BF16) for w in (w_ffn1_gate, w_ffn1_up, w_ffn1_down))
    ffn2_w = tuple(w.astype(BF16) for w in (w_ffn2_gate, w_ffn2_up, w_ffn2_down))

    for l in range(depth):
        last = l == depth - 1
        base = l * 8 * N_MOD
        rep = lambda g, k: jnp.tile(g, k)
        gvec = jnp.concatenate([rep(g_q_attn[l], ATTN_HEADS), rep(g_k_attn[l], ATTN_KV_HEADS),
                                rep(g_q_na[l], NA_HEADS), rep(g_k_na[l], NA_HEADS)]).reshape(1, N_NORM)

        h = _ffn(h, mods, base, (0, 1, 2), g_ffn1[l].reshape(1, d), l, *ffn1_w, n_all, tiles_per_batch, n_batch,
                 h_tail=h_ctx if l == 0 else None)

        q, k, nq, nk, v, nv, z, x0 = _inproj(h, mods, base, g_mix[l].reshape(1, d), l, w_in_p, gvec, headmean, cos_t, sin_t,
                                             conv_w[l], conv_b[l].reshape(1, -1), tiles_per_batch, n_batch, seq, n_ctx)

        out_rows = n_lat if last else n_all
        ya = _gqa(q, k, v, out_rows, n_batch, seq, n_ctx, latent=True)
        yc = _na_latent(nq, nk, nv, na_tabs, l, na_masks, out_rows, n_batch, seq, n_ctx)

        fargs = (filt_w1[l], filt_b1[l].reshape(1, -1), filt_w2[l], filt_b2[l].reshape(1, -1), filt_w3[l],
                 filt_freq[l].reshape(1, -1), deltas)
        kfull, l1 = _hfilter(seq, *fargs)
        khat = _filter_spectrum(kfull, tabs, n1_len)
        yconv = _long_conv(z, n_batch, khat, 1.0 / (l1 * (2 * seq)), tabs, n1_len)

        if not last:
            ya_c = _gqa(q, k, v, n_all, n_batch, seq, n_ctx, latent=False)
            yc_c = _gqa(nq, nk, nv, n_all, n_batch, seq, n_ctx, latent=False, kv_chunked=True)
            ya = lax.dynamic_update_slice(ya, ya_c[n_lat:], (n_lat, 0))
            yc = lax.dynamic_update_slice(yc, yc_c[n_lat:], (n_lat, 0))
            kfull_c, l1_c = _hfilter(n_ctx, *fargs)
            yconv = _hyena_ctx(z, kfull_c, l1_c, n_lat // n_ctx, n_batch, n_ctx, yconv)

        h = _ffn(h, mods, base, (6, 7, 8), g_ffn2[l].reshape(1, d), l, *ffn2_w, out_rows, tiles_per_batch, n_batch,
                 mixer=(ya, yconv, z, x0, hyena_skip[l].reshape(1, hy_w), yc, wa, wb, wc))

    return h[:n_lat].reshape(n_batch, seq, d)
```

```python
import functools
import math

import numpy as np
import jax
import jax.numpy as jnp
from jax import lax
from jax.experimental import pallas as pl
from jax.experimental.pallas import tpu as pltpu

F32 = jnp.float32
BF16 = jnp.bfloat16
HIGHEST = lax.Precision.HIGHEST

D_MODEL = 1024
HEAD_DIM = 64
GRID_W = 64
ATTN_HEADS = 8
ATTN_KV_HEADS = 2
NA_HEADS = 4
NA_ROWS = 8
NA_COLS = 16
FILTER_BANDS = 16
DECAY_TARGET = 1e-2
FAST_DECAY_PCT = 0.3
SLOW_DECAY_PCT = 1.5
ROPE_THETA = 10000.0
EPS = 1e-6
N_MOD = 9

LANES = 128
TOKEN_TILE = 512
FFN_CHUNK = 256
ATTN_TQ = 1024
ATTN_TK = 2048
NA_BLOCK_ROWS = 32
NA_SUB_ROWS = 4
NA_KEY_ROWS = 12
DFT_N2 = 128
OUTER_COLS = 8
INNER_K1 = 16
NEG_BIG = -1e30
VMEM_LIMIT = 56 * 1024 * 1024


def _cparams(sem):
    return pltpu.CompilerParams(dimension_semantics=sem, vmem_limit_bytes=VMEM_LIMIT)


def _ada_kernel(c_ref, w_ref, b_ref, o_ref):
    c = c_ref[...]
    s = c * jax.nn.sigmoid(c)
    o_ref[0] = jnp.dot(s, w_ref[0], preferred_element_type=F32) + b_ref[0]


def _adaln_all(cond8, w_ada, b_ada):
    depth, d, nd = w_ada.shape
    tn = 2304
    return pl.pallas_call(
        _ada_kernel,
        grid=(depth, nd // tn),
        in_specs=[
            pl.BlockSpec((8, d), lambda l, j: (0, 0)),
            pl.BlockSpec((1, d, tn), lambda l, j: (l, 0, j)),
            pl.BlockSpec((1, 1, tn), lambda l, j: (l, 0, j)),
        ],
        out_specs=pl.BlockSpec((1, 8, tn), lambda l, j: (l, 0, j)),
        out_shape=jax.ShapeDtypeStruct((depth, 8, nd), F32),
        compiler_params=_cparams(("arbitrary", "arbitrary")),
        name="adaln",
    )(cond8, w_ada, b_ada.reshape(depth, 1, nd))


def _norm_mod(x, g, sh, sc):
    ms = jnp.mean(x * x, axis=-1, keepdims=True)
    y = x * lax.rsqrt(ms + EPS) * g
    return y * (1.0 + sc) + sh


def _mod_specs(tiles_per_batch, n_batch, base, idxs):
    def spec(which):
        return pl.BlockSpec(
            (1, 1, D_MODEL),
            lambda i, w=which: (base + jnp.minimum(i // tiles_per_batch, n_batch) * N_MOD + w, 0, 0))
    return [spec(w) for w in idxs]


def _ffn_kernel(h_ref, *refs, with_mixer, split_tile):
    x = h_ref[...]
    if split_tile is not None:
        x = jnp.where(pl.program_id(0) < split_tile, x, refs[0][...])
        refs = refs[1:]
    if with_mixer:
        gt2_ref, ya_ref, yconv_ref, z_ref, x0_ref, skip_ref, yc_ref, wa_ref, wb_ref, wc_ref = refs[:10]
        refs = refs[10:]
        yb = ((yconv_ref[...] + z_ref[...] * skip_ref[...]) * x0_ref[...]).astype(BF16)
        x = x + gt2_ref[0] * (jnp.dot(ya_ref[...], wa_ref[0], preferred_element_type=F32)
                              + jnp.dot(yb, wb_ref[0], preferred_element_type=F32)
                              + jnp.dot(yc_ref[...], wc_ref[0], preferred_element_type=F32))
    sh_ref, sc_ref, gt_ref, g_ref, wg_ref, wu_ref, wd_ref, o_ref, xb_ref, acc_ref = refs
    x_is_new = with_mixer or split_tile is not None
    if x_is_new:
        o_ref[...] = x
    xb_ref[...] = _norm_mod(x, g_ref[...], sh_ref[0], sc_ref[0]).astype(BF16)

    for k in range(wg_ref.shape[2] // FFN_CHUNK):
        cols = slice(k * FFN_CHUNK, (k + 1) * FFN_CHUNK)
        xb = xb_ref[...]
        a = jnp.dot(xb, wg_ref[0, :, cols], preferred_element_type=F32)
        u = jnp.dot(xb, wu_ref[0, :, cols], preferred_element_type=F32)
        act = (a * jax.nn.sigmoid(a) * u).astype(BF16)
        part = jnp.dot(act, wd_ref[0, cols, :], preferred_element_type=F32)
        if k == 0:
            acc_ref[...] = part
        else:
            acc_ref[...] += part
    resid = o_ref[...] if x_is_new else h_ref[...]
    o_ref[...] = resid + 0.5 * gt_ref[0] * acc_ref[...]


def _ffn(h, mods, mod_base, which, g, layer, wg, wu, wd, n_rows, tiles_per_batch, n_batch, mixer=None, h_tail=None):
    d = h.shape[1]
    of_layer = lambda a: pl.BlockSpec((1,) + a.shape[1:], lambda i: (layer, 0, 0), pipeline_mode=pl.Buffered(1))
    tm = TOKEN_TILE
    row = lambda i: (i, 0)
    const = lambda i: (0, 0)
    rows = lambda a: pl.BlockSpec((tm, a.shape[1]), row)
    full = lambda a: pl.BlockSpec(a.shape, const)
    mix_specs, mix_args = [], ()
    if mixer is not None:
        ya, yconv, z, x0, skip, yc, wa, wb, wc = mixer
        mix_specs = _mod_specs(tiles_per_batch, n_batch, mod_base, (5,)) + [
            rows(ya), rows(yconv), rows(z), rows(x0), full(skip), rows(yc), of_layer(wa), of_layer(wb), of_layer(wc)]
        mix_args = (mods,) + tuple(mixer)
    h_specs, h_args, split = [rows(h)], (h,), None
    if h_tail is not None:
        split = h.shape[0] // tm
        h_specs = [pl.BlockSpec((tm, d), lambda i: (jnp.minimum(i, split - 1), 0)),
                   pl.BlockSpec((tm, d), lambda i: (jnp.maximum(i - split, 0), 0))]
        h_args = (h, h_tail)
    return pl.pallas_call(
        functools.partial(_ffn_kernel, with_mixer=mixer is not None, split_tile=split),
        grid=(n_rows // tm,),
        in_specs=h_specs + mix_specs + _mod_specs(tiles_per_batch, n_batch, mod_base, which)
        + [full(g), of_layer(wg), of_layer(wu), of_layer(wd)],
        out_specs=pl.BlockSpec((tm, d), row),
        out_shape=jax.ShapeDtypeStruct((n_rows, d), F32),
        scratch_shapes=[pltpu.VMEM((tm, d), BF16), pltpu.VMEM((tm, d), F32)],
        compiler_params=_cparams(("arbitrary",)),
        name="ffn_mix" if mixer is not None else "ffn",
    )(*h_args, *mix_args, mods, mods, mods, g, wg, wu, wd)


N_NORM = 1152
N_ROPE = 640
ATTN_Q_CHUNKS = (0, 1, 2, 3)
NA_Q_CHUNKS = (5, 6)
LOG2E = math.log2(math.e)


def _inproj_kernel(h_ref, prev_ref, next_ref, sh_ref, sc_ref, g_ref, w_ref, gv_ref, hm_ref, cos_ref, sin_ref,
                   cw_ref, cb_ref, q_ref, k_ref, nq_ref, nk_ref, v_ref, nv_ref, z_ref, x0_ref, *, seq, n_ctx, n_lat):
    xb = _norm_mod(h_ref[...], g_ref[...], sh_ref[0], sc_ref[0]).astype(BF16)
    hm = hm_ref[...]
    cosv = cos_ref[...]
    sinv = sin_ref[...]
    lane = lax.broadcasted_iota(jnp.int32, cosv.shape, 1)
    first_half = (lane % 32) < 16
    outs = []
    p_norm = jnp.dot(xb, w_ref[0, :, 0:N_NORM], preferred_element_type=F32)
    for c in range(N_NORM // LANES):
        p = p_norm[:, c * LANES:(c + 1) * LANES]
        ms = jnp.dot((p * p).astype(BF16), hm, preferred_element_type=F32)
        y = p * lax.rsqrt(ms + EPS) * gv_ref[:, c * LANES:(c + 1) * LANES]
        if c < N_ROPE // LANES:
            swapped = jnp.where(first_half, pltpu.roll(y, LANES - 16, 1), pltpu.roll(y, 16, 1))
            y = y * cosv + swapped * sinv
        if c in ATTN_Q_CHUNKS or c in NA_Q_CHUNKS:
            y = y * (HEAD_DIM ** -0.5 * LOG2E)
        outs.append(y.astype(BF16))
    for c in range(4):
        q_ref[:, c * LANES:(c + 1) * LANES] = outs[c]
    k_ref[...] = outs[4]
    nq_ref[:, 0:LANES] = outs[5]
    nq_ref[:, LANES:2 * LANES] = outs[6]
    nk_ref[:, 0:LANES] = outs[7]
    nk_ref[:, LANES:2 * LANES] = outs[8]
    p_rest = jnp.dot(xb, w_ref[0, :, N_NORM:], preferred_element_type=F32)
    v = p_rest[:, 0:LANES]
    v_ref[:, 0:LANES] = jnp.where(lane < HEAD_DIM, v, 1.0).astype(BF16)
    v_ref[:, LANES:2 * LANES] = jnp.where(lane < HEAD_DIM, 1.0, v).astype(BF16)
    for c in range(NA_HEADS // 2):
        nv = p_rest[:, (1 + c) * LANES:(2 + c) * LANES]
        nv_ref[:, 2 * c * LANES:(2 * c + 1) * LANES] = jnp.where(lane < HEAD_DIM, nv, 1.0).astype(BF16)
        nv_ref[:, (2 * c + 1) * LANES:(2 * c + 2) * LANES] = jnp.where(lane < HEAD_DIM, 1.0, nv).astype(BF16)

    hy = p_rest[:, 3 * LANES:]
    tm = hy.shape[0]
    halo = jnp.concatenate([prev_ref[...], next_ref[...]], axis=0)
    halo_b = _norm_mod(halo, g_ref[...], sh_ref[0], sc_ref[0]).astype(BF16)
    hy_halo = jnp.dot(halo_b, w_ref[0, :, N_NORM + 3 * LANES:], preferred_element_type=F32)
    row0 = pl.program_id(0) * tm
    is_ctx = row0 >= n_lat
    period = jnp.where(is_ctx, n_ctx, seq)
    local = lax.broadcasted_iota(jnp.int32, (tm, 1), 0)
    pos = jnp.bitwise_and(row0 + local - jnp.where(is_ctx, n_lat, 0), period - 1)
    up = jnp.where(local == 0, hy_halo[7:8, :], pltpu.roll(hy, 1, 0))
    up = jnp.where(pos == 0, 0.0, up)
    dn = jnp.where(local == tm - 1, hy_halo[8:9, :], pltpu.roll(hy, tm - 1, 0))
    dn = jnp.where(pos == period - 1, 0.0, dn)
    uc = up * cw_ref[0:1, :] + hy * cw_ref[1:2, :] + dn * cw_ref[2:3, :] + cb_ref[...]
    cw = uc.shape[1] // 3
    x0_ref[...] = uc[:, :cw]
    z_ref[...] = uc[:, 2 * cw:] * uc[:, cw:2 * cw]


def _inproj(h, mods, mod_base, g, layer, w, gvec, headmean, cos_t, sin_t, conv_w, conv_b, tiles_per_batch, n_batch, seq,
            n_ctx):
    n_rows, d = h.shape
    tm = TOKEN_TILE
    assert seq & (seq - 1) == 0 and n_ctx & (n_ctx - 1) == 0
    row = lambda i: (i, 0)
    const = lambda i: (0, 0)
    widths = (512, 128, 256, 256, 256, 512)
    hy_w = conv_w.shape[1] // 3
    out_shape = [jax.ShapeDtypeStruct((n_rows, wd), BF16) for wd in widths]
    out_shape += [jax.ShapeDtypeStruct((n_rows, hy_w), F32)] * 2
    out_specs = [pl.BlockSpec((tm, wd), row) for wd in widths] + [pl.BlockSpec((tm, hy_w), row)] * 2
    return pl.pallas_call(
        functools.partial(_inproj_kernel, seq=seq, n_ctx=n_ctx, n_lat=n_batch * seq),
        grid=(n_rows // tm,),
        in_specs=[pl.BlockSpec((tm, d), row),
                  pl.BlockSpec((8, d), lambda i: (jnp.maximum(i * (tm // 8) - 1, 0), 0)),
                  pl.BlockSpec((8, d), lambda i: (jnp.minimum((i + 1) * (tm // 8), n_rows // 8 - 1), 0))]
        + _mod_specs(tiles_per_batch, n_batch, mod_base, (3, 4))
        + [
            pl.BlockSpec((1, d), const),
            pl.BlockSpec((1,) + w.shape[1:], lambda i: (layer, 0, 0), pipeline_mode=pl.Buffered(1)),
            pl.BlockSpec(gvec.shape, const),
            pl.BlockSpec(headmean.shape, const),
            pl.BlockSpec((tm, LANES), row),
            pl.BlockSpec((tm, LANES), row),
            pl.BlockSpec(conv_w.shape, const),
            pl.BlockSpec(conv_b.shape, const),
        ],
        out_specs=out_specs,
        out_shape=out_shape,
        compiler_params=_cparams(("arbitrary",)),
        name="inproj",
    )(h, h, h, mods, mods, g, w, gvec, headmean, cos_t, sin_t, conv_w, conv_b)


def _stack_heads(q):
    lane = lax.broadcasted_iota(jnp.int32, q.shape, 1)
    zero = jnp.zeros_like(q)
    return jnp.concatenate([jnp.where(lane < HEAD_DIM, q, zero), jnp.where(lane >= HEAD_DIM, q, zero)], axis=0)


def _qk(q2, k):
    return lax.dot_general(q2, k, (((1,), (1,)), ((), ())), preferred_element_type=F32)


def _gqa_kernel(q_ref, kc_ref, vc_ref, *rest, n_latent_chunks, tk):
    if n_latent_chunks:
        kl_ref, vl_ref, o_ref = rest
    else:
        (o_ref,) = rest
    tq = q_ref.shape[0]
    q2 = _stack_heads(q_ref[...])

    def pv(p, v_ref, rows):
        return jnp.concatenate(
            [jnp.dot(p[:tq], v_ref[rows, 0:LANES], preferred_element_type=F32),
             jnp.dot(p[tq:], v_ref[rows, LANES:2 * LANES], preferred_element_type=F32)], axis=0)

    s = _qk(q2, kc_ref[...])
    m = jnp.max(s, axis=-1, keepdims=True)
    if n_latent_chunks:
        rows0 = pl.ds(0, tk)
        s0 = _qk(q2, kl_ref[rows0, :])
        m = jnp.maximum(m, jnp.max(s0, axis=-1, keepdims=True))
        acc = (pv(jnp.exp2((s - m).astype(BF16)), vc_ref, slice(None))
               + pv(jnp.exp2((s0 - m).astype(BF16)), vl_ref, rows0))
    else:
        acc = pv(jnp.exp2((s - m).astype(BF16)), vc_ref, slice(None))

    if n_latent_chunks:
        def body(j, carry):
            m, acc = carry
            rows = pl.ds(pl.multiple_of(j * tk, tk), tk)
            s = _qk(q2, kl_ref[rows, :])
            m_new = jnp.maximum(m, jnp.max(s, axis=-1, keepdims=True))
            p = jnp.exp2((s - m_new).astype(BF16))
            return m_new, jnp.exp2(m - m_new) * acc + pv(p, vl_ref, rows)

        m, acc = lax.fori_loop(1, n_latent_chunks, body, (m, acc), unroll=True)
    lane = lax.broadcasted_iota(jnp.int32, (tq, LANES), 1)
    num = jnp.where(lane < HEAD_DIM, acc[:tq], acc[tq:])
    den = pltpu.roll(jnp.where(lane < HEAD_DIM, acc[tq:], acc[:tq]), HEAD_DIM, 1)
    o_ref[...] = (num / den).astype(o_ref.dtype)


def _gqa(q, k, v2, n_batch, seq, n_ctx, latent, kv_chunked=False):
    chunks = q.shape[1] // LANES
    ctx_blk0 = n_batch * seq // n_ctx
    tk = min(ATTN_TK, seq)
    kj = (lambda j: j) if kv_chunked else (lambda j: 0)
    if latent:
        tq = ATTN_TQ
        grid = (n_batch, chunks, seq // tq)
        qmap = omap = lambda b, j, i: (b * (seq // tq) + i, j)
        out_rows = n_batch * seq
        kv_specs = [pl.BlockSpec((seq, LANES), lambda b, j, i: (b, kj(j))),
                    pl.BlockSpec((seq, 2 * LANES), lambda b, j, i: (b, kj(j)))]
        kv_args = (k, v2)
    else:
        tq = n_ctx
        grid = (n_batch, chunks, 1)
        qmap = lambda b, j, i: (ctx_blk0 + b, j)
        omap = lambda b, j, i: (b, j)
        out_rows = n_batch * n_ctx
        kv_specs, kv_args = [], ()
    return pl.pallas_call(
        functools.partial(_gqa_kernel, n_latent_chunks=seq // tk if latent else 0, tk=tk),
        grid=grid,
        in_specs=[pl.BlockSpec((tq, LANES), qmap),
                  pl.BlockSpec((n_ctx, LANES), lambda b, j, i: (ctx_blk0 + b, kj(j))),
                  pl.BlockSpec((n_ctx, 2 * LANES), lambda b, j, i: (ctx_blk0 + b, kj(j)))] + kv_specs,
        out_specs=pl.BlockSpec((tq, LANES), omap),
        out_shape=jax.ShapeDtypeStruct((out_rows, q.shape[1]), BF16),
        compiler_params=_cparams(("arbitrary", "arbitrary", "arbitrary")),
        name="attn_latent" if latent else "attn_ctx",
    )(q, k, v2, *kv_args)


def _na_kernel(q_ref, kc_ref, vc_ref, kl_ref, vl_ref, tab_ref, mask_ref, o_ref, *, grid_rows):
    i = pl.program_id(2)
    nk = NA_KEY_ROWS * GRID_W
    n_sub = NA_BLOCK_ROWS // NA_SUB_ROWS
    last_sub = grid_rows // NA_SUB_ROWS - 1
    kc = kc_ref[...]
    sub_rows = NA_SUB_ROWS * GRID_W
    lane = lax.broadcasted_iota(jnp.int32, (sub_rows, LANES), 1)
    low = lane < HEAD_DIM
    for sub in range(n_sub):
        sb = i * n_sub + sub
        r0 = sb * NA_SUB_ROWS
        ks = jnp.clip(r0 - NA_ROWS // 2, 0, grid_rows - NA_KEY_ROWS)
        keys = pl.ds(pl.multiple_of(ks * GRID_W, GRID_W), nk)
        shift = ks - r0 + 2 * NA_ROWS
        variant = jnp.where(sb == 0, 0, jnp.where(sb == last_sub, 2, 1))
        k = kl_ref[keys, :]
        rows = slice(sub * sub_rows, (sub + 1) * sub_rows)
        q = q_ref[rows, :]
        acc = []
        for hh in range(2):
            qh = jnp.where(low if hh == 0 else jnp.logical_not(low), q, jnp.zeros_like(q))
            slabs = []
            for qr in range(NA_SUB_ROWS):
                pairs = [tab_ref[0, hh, shift - qr + 2 * kp] for kp in range(NA_KEY_ROWS // 2)]
                slabs.append(jnp.concatenate(pairs, axis=1) + mask_ref[variant, qr:qr + 1, :])
            s_nb = _qk(qh, k) + jnp.concatenate(slabs, axis=0)
            s_cx = _qk(qh, kc)
            m = jnp.maximum(jnp.max(s_nb, axis=-1, keepdims=True), jnp.max(s_cx, axis=-1, keepdims=True))
            vcols = slice(hh * LANES, (hh + 1) * LANES)
            acc.append(jnp.dot(jnp.exp2((s_nb - m).astype(BF16)), vl_ref[keys, vcols], preferred_element_type=F32)
                       + jnp.dot(jnp.exp2((s_cx - m).astype(BF16)), vc_ref[:, vcols], preferred_element_type=F32))
        num = jnp.where(low, acc[0], acc[1])
        den = pltpu.roll(jnp.where(low, acc[1], acc[0]), HEAD_DIM, 1)
        o_ref[rows, :] = (num / den).astype(o_ref.dtype)


def _na_bias_tables(rpb):
    qc = np.arange(GRID_W)[:, None]
    kc = np.arange(GRID_W)[None, :]
    cs = np.clip(qc - NA_COLS // 2, 0, GRID_W - NA_COLS)
    col_ok = (kc >= cs) & (kc < cs + NA_COLS)
    col_hot = ((kc - qc + NA_COLS - 1)[:, :, None] == np.arange(2 * NA_COLS - 1)) & col_ok[:, :, None]
    by_col = jnp.einsum("abe,hde->hdab", jnp.asarray(col_hot, F32), rpb, precision=HIGHEST)
    by_col = jnp.where(jnp.asarray(col_ok)[None, None], by_col * LOG2E, NEG_BIG)
    t = jnp.pad(by_col, ((0, 0), (NA_ROWS + 1, NA_ROWS), (0, 0), (0, 0)))
    return jnp.concatenate([t[:, :-1], t[:, 1:]], axis=-1)


def _na_row_masks(grid_rows):
    n_sub = grid_rows // NA_SUB_ROWS
    masks = []
    for sb in (0, min(1, n_sub - 1), n_sub - 1):
        qr = sb * NA_SUB_ROWS + np.arange(NA_SUB_ROWS)[:, None]
        ks = int(np.clip(sb * NA_SUB_ROWS - NA_ROWS // 2, 0, grid_rows - NA_KEY_ROWS))
        kr = ks + np.arange(NA_KEY_ROWS)[None, :]
        rs = np.clip(qr - NA_ROWS // 2, 0, grid_rows - NA_ROWS)
        row_ok = (kr >= rs) & (kr < rs + NA_ROWS)
        masks.append(np.repeat(np.where(row_ok, 0.0, NEG_BIG), GRID_W, axis=1))
    return jnp.asarray(np.stack(masks).astype(np.float32))


def _na_latent(nq, nk, nv2, tab, layer, masks, n_batch, seq, n_ctx):
    chunks = nq.shape[1] // LANES
    tq = NA_BLOCK_ROWS * GRID_W
    nblk = seq // tq
    ctx_blk0 = n_batch * seq // n_ctx
    qmap = lambda b, j, i: (b * nblk + i, j)
    tab_spec = pl.BlockSpec((1, 2) + tab.shape[2:], lambda b, j, i: (layer, j, 0, 0, 0))
    mask_spec = pl.BlockSpec(masks.shape, lambda b, j, i: (0, 0, 0))
    return pl.pallas_call(
        functools.partial(_na_kernel, grid_rows=seq // GRID_W),
        grid=(n_batch, chunks, nblk),
        in_specs=[pl.BlockSpec((tq, LANES), qmap),
                  pl.BlockSpec((n_ctx, LANES), lambda b, j, i: (ctx_blk0 + b, j)),
                  pl.BlockSpec((n_ctx, 2 * LANES), lambda b, j, i: (ctx_blk0 + b, j)),
                  pl.BlockSpec((seq, LANES), lambda b, j, i: (b, j)),
                  pl.BlockSpec((seq, 2 * LANES), lambda b, j, i: (b, j)),
                  tab_spec, mask_spec],
        out_specs=pl.BlockSpec((tq, LANES), qmap),
        out_shape=jax.ShapeDtypeStruct((n_batch * seq, nq.shape[1]), BF16),
        compiler_params=_cparams(("arbitrary", "arbitrary", "arbitrary")),
        name="na_latent",
    )(nq, nk, nv2, nk, nv2, tab, masks)


def _hfilt_kernel(ft_ref, tc_ref, w1t_ref, b1_ref, w2t_ref, b2_ref, w3_ref, fr_ref, d_ref, k_ref, l1_ref, *, n):
    i = pl.program_id(0)
    fr = fr_ref[...]
    h1 = jnp.sin(fr * (jnp.dot(w1t_ref[...], ft_ref[...], precision=HIGHEST, preferred_element_type=F32)
                       + b1_ref[...]))
    h2 = jnp.sin(fr * (jnp.dot(w2t_ref[...], h1, precision=HIGHEST, preferred_element_type=F32) + b2_ref[...]))
    contract0 = (((0,), (0,)), ((), ()))
    o = lax.dot_general(h2, w3_ref[...], contract0, precision=HIGHEST, preferred_element_type=F32)
    cw = o.shape[1] // 2
    t = o.shape[0]
    row = i * t + lax.broadcasted_iota(jnp.int32, (t, cw), 0)
    k = jnp.where(row < n, o[:, :cw], o[:, cw:]) * jnp.exp(-tc_ref[...] * d_ref[...])
    k = jnp.where(row == n, 0.0, k)
    k_ref[...] = k

    @pl.when(i == 0)
    def _():
        l1_ref[...] = jnp.zeros_like(l1_ref)

    l1_ref[...] += jnp.sum(jnp.abs(k), axis=0, keepdims=True)


def _filter_feats(n):
    rows = np.arange(2 * n)
    pos = np.where(rows <= n, rows, 2 * n - rows).astype(np.float32)
    pos = jnp.asarray(np.minimum(pos, n - 1))
    t = pos / max(n - 1, 1)
    bands = jnp.linspace(1e-4, FILTER_BANDS - 1, FILTER_BANDS, dtype=F32)
    ang = (2.0 * math.pi / n) * pos[None, :] * bands[:, None]
    feats = jnp.concatenate([t[None, :], jnp.cos(ang), -jnp.sin(ang)], axis=0)
    return jnp.pad(feats, ((0, LANES - feats.shape[0]), (0, 0))), t[:, None]


def _hfilter(n, w1, b1, w2, b2, w3, freq, deltas):
    feats_t, t_col = _filter_feats(n)
    w1t = jnp.pad(w1.T, ((0, 0), (0, LANES - w1.shape[0])))
    col = lambda a: a.reshape(-1, 1)
    rows = feats_t.shape[1]
    t = min(2048, rows)
    cw = w3.shape[1] // 2
    const = lambda i: (0, 0)
    full = lambda a: pl.BlockSpec(a.shape, const)
    args = (w1t, col(b1), w2.T, col(b2), w3, col(freq), jnp.abs(deltas))
    return pl.pallas_call(
        functools.partial(_hfilt_kernel, n=n),
        grid=(rows // t,),
        in_specs=[pl.BlockSpec((LANES, t), lambda i: (0, i)), pl.BlockSpec((t, 1), lambda i: (i, 0))]
        + [full(a) for a in args],
        out_specs=[pl.BlockSpec((t, cw), lambda i: (i, 0)), pl.BlockSpec((1, cw), const)],
        out_shape=[jax.ShapeDtypeStruct((rows, cw), F32), jax.ShapeDtypeStruct((1, cw), F32)],
        compiler_params=_cparams(("arbitrary",)),
        name="hyena_filter",
    )(feats_t, t_col, *args)


def _dft_outer_kernel(f_ref, x_ref, o_ref):
    c = x_ref.shape[-1]
    x = x_ref[...].reshape(f_ref.shape[1], c).astype(BF16)
    o_ref[...] = jnp.dot(f_ref[...], x, preferred_element_type=F32).astype(o_ref.dtype).reshape(o_ref.shape)


def _kron_table(fmat, cols):
    return jnp.asarray(np.kron(np.asarray(fmat, np.float32), np.eye(cols, dtype=np.float32)).astype(BF16))


def _outer_cols(dtype):
    return OUTER_COLS * 4 // jnp.dtype(dtype).itemsize


def _dft_outer_fwd(fmat, x3, nb, n1_len, out_dtype):
    k = fmat.shape[1]
    c = x3.shape[2]
    cols = _outer_cols(out_dtype)
    f = _kron_table(fmat, cols)
    return pl.pallas_call(
        _dft_outer_kernel,
        grid=(nb, DFT_N2 // cols),
        in_specs=[pl.BlockSpec(f.shape, lambda b, j: (0, 0), pipeline_mode=pl.Buffered(1)),
                  pl.BlockSpec((k, cols, c), lambda b, j: (b, j, 0))],
        out_specs=pl.BlockSpec((1, 2, n1_len, cols, c), lambda b, j: (b, 0, 0, j, 0)),
        out_shape=jax.ShapeDtypeStruct((nb, 2, n1_len, DFT_N2, c), out_dtype),
        compiler_params=_cparams(("arbitrary", "arbitrary")),
        name="dft_outer_fwd",
    )(f, x3)


def _dft_outer_inv(fmat, d5):
    nb, _, n1_len, _, c = d5.shape
    half = fmat.shape[0]
    cols = _outer_cols(d5.dtype)
    f = _kron_table(fmat, cols)
    return pl.pallas_call(
        _dft_outer_kernel,
        grid=(nb, DFT_N2 // cols),
        in_specs=[pl.BlockSpec(f.shape, lambda b, j: (0, 0), pipeline_mode=pl.Buffered(1)),
                  pl.BlockSpec((1, 2, n1_len, cols, c), lambda b, j: (b, 0, 0, j, 0))],
        out_specs=pl.BlockSpec((half, cols, c), lambda b, j: (b, j, 0)),
        out_shape=jax.ShapeDtypeStruct((nb * half, DFT_N2, c), F32),
        compiler_params=_cparams(("arbitrary", "arbitrary")),
        name="dft_outer_inv",
    )(f, d5)


def _inner_matrices(fc, fs, tc, ts):
    gr = fc * tc - fs * ts
    gi = -(fc * ts + fs * tc)
    return gr, gi


def _real_form(gr, gi):
    return jnp.concatenate([jnp.concatenate([gr, -gi], axis=1), jnp.concatenate([gi, gr], axis=1)], axis=0)


def _dft_inner_filter_kernel(fc_ref, fs_ref, tc_ref, ts_ref, b_ref, o_ref):
    fc, fs = fc_ref[...], fs_ref[...]
    n2, c = b_ref.shape[3], b_ref.shape[4]

    def body(kk, carry):
        g = _real_form(*_inner_matrices(fc, fs, tc_ref[kk], ts_ref[kk]))
        o_ref[kk] = jnp.dot(g.astype(BF16), b_ref[0, :, kk].reshape(2 * n2, c).astype(BF16),
                            preferred_element_type=F32)
        return carry

    lax.fori_loop(0, b_ref.shape[2], body, 0, unroll=2)


def _dft_inner_conv_kernel(fc_ref, fs_ref, tc_ref, ts_ref, b_ref, kh_ref, sc_ref, o_ref):
    fc, fs = fc_ref[...], fs_ref[...]
    nb, n2, c = b_ref.shape[0], b_ref.shape[3], b_ref.shape[4]
    scale = sc_ref[...]

    def body(kk, carry):
        gr, gi = _inner_matrices(fc, fs, tc_ref[kk], ts_ref[kk])
        g = _real_form(gr, gi).astype(BF16)
        gh = _real_form(gr.T, -gi.T).astype(BF16)
        kh = kh_ref[kk]
        kr, ki = kh[:n2], kh[n2:]
        for b in range(nb):
            x = jnp.dot(g, b_ref[b, :, kk].reshape(2 * n2, c).astype(BF16), preferred_element_type=F32)
            xr, xi = x[:n2], x[n2:]
            y = jnp.concatenate([(xr * kr - xi * ki) * scale, (xr * ki + xi * kr) * scale], axis=0)
            d = jnp.dot(gh, y.astype(BF16), preferred_element_type=F32)
            o_ref[b, :, kk] = d.astype(o_ref.dtype).reshape(2, n2, c)
        return carry

    lax.fori_loop(0, b_ref.shape[2], body, 0, unroll=2)


def _dft_tables(n1_len):
    n_total = n1_len * DFT_N2
    a = np.arange(DFT_N2)
    ang = 2.0 * np.pi * np.outer(a, a) / DFT_N2
    k1 = np.arange(n1_len)
    tw = 2.0 * np.pi * np.outer(k1, a) / n_total
    o = 2.0 * np.pi * np.outer(k1, k1) / n1_len
    f32 = lambda v: jnp.asarray(v.astype(np.float32))
    outer_fwd = np.concatenate([np.cos(o), -np.sin(o)], axis=0)
    outer_inv = np.concatenate([np.cos(o), -np.sin(o)], axis=1)
    return dict(
        fc=f32(np.cos(ang)), fs=f32(np.sin(ang)),
        tc=f32(np.cos(tw)).reshape(n1_len, 1, DFT_N2), ts=f32(np.sin(tw)).reshape(n1_len, 1, DFT_N2),
        outer_fwd_full=outer_fwd, outer_fwd_half=outer_fwd[:, :n1_len // 2], outer_inv_half=outer_inv[:n1_len // 2],
    )


def _inner_specs():
    sq = pl.BlockSpec((DFT_N2, DFT_N2), lambda k: (0, 0))
    tw = pl.BlockSpec((INNER_K1, 1, DFT_N2), lambda k: (k, 0, 0))
    return [sq, sq, tw, tw]


def _filter_spectrum(kfull, tabs, n1_len):
    c = kfull.shape[1]
    b5 = _dft_outer_fwd(tabs["outer_fwd_full"], kfull.reshape(n1_len, DFT_N2, c), 1, n1_len, F32)
    return pl.pallas_call(
        _dft_inner_filter_kernel,
        grid=(n1_len // INNER_K1,),
        in_specs=_inner_specs() + [pl.BlockSpec((1, 2, INNER_K1, DFT_N2, c), lambda k: (0, 0, k, 0, 0))],
        out_specs=pl.BlockSpec((INNER_K1, 2 * DFT_N2, c), lambda k: (k, 0, 0)),
        out_shape=jax.ShapeDtypeStruct((n1_len, 2 * DFT_N2, c), F32),
        compiler_params=_cparams(("arbitrary",)),
        name="dft_inner_filter",
    )(tabs["fc"], tabs["fs"], tabs["tc"], tabs["ts"], b5)


def _long_conv(z_all, nb, khat, scale, tabs, n1_len):
    c = z_all.shape[1]
    bz = _dft_outer_fwd(tabs["outer_fwd_half"], z_all.reshape(z_all.shape[0] // DFT_N2, DFT_N2, c), nb, n1_len, BF16)
    blk = pl.BlockSpec((nb, 2, INNER_K1, DFT_N2, c), lambda k: (0, 0, k, 0, 0))
    d = pl.pallas_call(
        _dft_inner_conv_kernel,
        grid=(n1_len // INNER_K1,),
        in_specs=_inner_specs() + [blk, pl.BlockSpec((INNER_K1, 2 * DFT_N2, c), lambda k: (k, 0, 0)),
                                   pl.BlockSpec((1, c), lambda k: (0, 0))],
        out_specs=blk,
        out_shape=jax.ShapeDtypeStruct(bz.shape, BF16),
        compiler_params=_cparams(("arbitrary",)),
        name="dft_inner_conv",
    )(tabs["fc"], tabs["fs"], tabs["tc"], tabs["ts"], bz, khat, scale)
    y3 = _dft_outer_inv(tabs["outer_inv_half"], d)
    return y3.reshape(y3.shape[0] * DFT_N2, c)


def _hyena_ctx_kernel(ff_ref, fi_ref, k_ref, l1_ref, z_ref, o_ref):
    ff = ff_ref[...]
    n = z_ref.shape[0]
    big = ff.shape[1]
    kh = jnp.dot(ff, k_ref[...], precision=HIGHEST, preferred_element_type=F32)
    x = jnp.dot(ff[:, :n], z_ref[...], precision=HIGHEST, preferred_element_type=F32)
    kr, ki = kh[:big], kh[big:]
    xr, xi = x[:big], x[big:]
    scale = 1.0 / (l1_ref[...] * big)
    y = jnp.concatenate([(xr * kr - xi * ki) * scale, (xr * ki + xi * kr) * scale], axis=0)
    o_ref[...] = jnp.dot(fi_ref[...], y, precision=HIGHEST, preferred_element_type=F32)


def _hyena_ctx(z, kfull, l1, row0_blk, n_batch, n):
    big = 2 * n
    a = np.arange(big)
    ang = 2.0 * np.pi * np.outer(a, a) / big
    ff = jnp.asarray(np.concatenate([np.cos(ang), -np.sin(ang)], axis=0).astype(np.float32))
    fi = jnp.asarray(np.concatenate([np.cos(ang), -np.sin(ang)], axis=1)[:n].astype(np.float32))
    c = z.shape[1]
    const = lambda b: (0, 0)
    return pl.pallas_call(
        _hyena_ctx_kernel,
        grid=(n_batch,),
        in_specs=[pl.BlockSpec(ff.shape, const), pl.BlockSpec(fi.shape, const), pl.BlockSpec(kfull.shape, const),
                  pl.BlockSpec((1, c), const), pl.BlockSpec((n, c), lambda b: (row0_blk + b, 0))],
        out_specs=pl.BlockSpec((n, c), lambda b: (b, 0)),
        out_shape=jax.ShapeDtypeStruct((n_batch * n, c), F32),
        compiler_params=_cparams(("arbitrary",)),
        name="hyena_ctx",
    )(ff, fi, kfull, l1, z)


def _rope_tables(seq, n_batch, n_ctx_rows):
    t = jnp.arange(seq, dtype=jnp.int32)
    rows = (t // GRID_W).astype(F32)
    cols = (t % GRID_W).astype(F32)
    nf = HEAD_DIM // 4
    inv = ROPE_THETA ** (-jnp.arange(nf, dtype=F32) / nf)
    ar, ac = rows[:, None] * inv, cols[:, None] * inv
    cos = jnp.concatenate([jnp.cos(ar), jnp.cos(ar), jnp.cos(ac), jnp.cos(ac)], axis=1)
    sin = jnp.concatenate([-jnp.sin(ar), jnp.sin(ar), -jnp.sin(ac), jnp.sin(ac)], axis=1)
    cos = jnp.tile(cos, (n_batch, LANES // HEAD_DIM))
    sin = jnp.tile(sin, (n_batch, LANES // HEAD_DIM))
    cos = jnp.concatenate([cos, jnp.ones((n_ctx_rows, LANES), F32)], axis=0)
    sin = jnp.concatenate([sin, jnp.zeros((n_ctx_rows, LANES), F32)], axis=0)
    return cos, sin


def _head_mean_matrix():
    a = np.arange(LANES)
    return jnp.asarray((a[:, None] // HEAD_DIM == a[None, :] // HEAD_DIM).astype(np.float32) / HEAD_DIM, dtype=BF16)


def kernel(x, c, ctx, c_ctx, w_ada, b_ada, g_ffn1, w_ffn1_gate, w_ffn1_up, w_ffn1_down, g_mix, w_in, w_out, g_q_attn, g_k_attn, conv_w, conv_b, filt_w1, filt_b1, filt_w2, filt_b2, filt_w3, filt_freq, hyena_skip, g_q_na, g_k_na, na_rpb, g_ffn2, w_ffn2_gate, w_ffn2_up, w_ffn2_down):
    n_batch, seq, d = x.shape
    n_ctx = ctx.shape[1]
    depth = w_ada.shape[0]
    assert d == D_MODEL and n_batch + 1 <= 8 and seq % TOKEN_TILE == 0 and seq % (NA_BLOCK_ROWS * GRID_W) == 0
    assert n_ctx == 256 and (n_batch * n_ctx) % TOKEN_TILE == 0
    n_lat = n_batch * seq
    n_all = n_lat + n_batch * n_ctx
    tiles_per_batch = seq // TOKEN_TILE
    hy_w = conv_w.shape[2] // 3
    n1_len = 2 * seq // DFT_N2

    cond8 = jnp.zeros((8, d), F32).at[:n_batch].set(c).at[n_batch].set(c_ctx)
    mods = _adaln_all(cond8, w_ada, b_ada).reshape(depth * 8 * N_MOD, 1, d)
    h = x.reshape(n_lat, d)
    h_ctx = ctx.reshape(n_batch * n_ctx, d)

    cos_t, sin_t = _rope_tables(seq, n_batch, n_batch * n_ctx)
    headmean = _head_mean_matrix()
    na_masks = _na_row_masks(seq // GRID_W)
    na_tabs = jax.vmap(_na_bias_tables)(na_rpb)
    tabs = _dft_tables(n1_len)
    deltas = jnp.linspace(math.log(DECAY_TARGET) / SLOW_DECAY_PCT, math.log(DECAY_TARGET) / FAST_DECAY_PCT,
                          hy_w, dtype=F32).reshape(1, hy_w)

    q_heads = [hd for pair in range(ATTN_HEADS // 2) for hd in (pair, pair + ATTN_HEADS // 2)]
    q_segs = [(hd * HEAD_DIM, (hd + 1) * HEAD_DIM) for hd in q_heads]
    in_segs = q_segs + [(512, 640), (1536, 1792), (1792, 2048), (640, 768), (2048, 2304), (768, 1536)]
    w_in_p = jnp.concatenate([w_in[:, :, a:b] for a, b in in_segs], axis=2).astype(BF16)
    wa = jnp.concatenate([w_out[:, a:b] for a, b in q_segs], axis=1).astype(BF16)
    wb = w_out[:, 512:512 + hy_w].astype(BF16)
    wc = w_out[:, 512 + hy_w:].astype(BF16)
    ffn1_w = tuple(w.astype(BF16) for w in (w_ffn1_gate, w_ffn1_up, w_ffn1_down))
    ffn2_w = tuple(w.astype(BF16) for w in (w_ffn2_gate, w_ffn2_up, w_ffn2_down))

    for l in range(depth):
        last = l == depth - 1
        base = l * 8 * N_MOD
        rep = lambda g, k: jnp.tile(g, k)
        gvec = jnp.concatenate([rep(g_q_attn[l], ATTN_HEADS), rep(g_k_attn[l], ATTN_KV_HEADS),
                                rep(g_q_na[l], NA_HEADS), rep(g_k_na[l], NA_HEADS)]).reshape(1, N_NORM)

        h = _ffn(h, mods, base, (0, 1, 2), g_ffn1[l].reshape(1, d), l, *ffn1_w, n_all, tiles_per_batch, n_batch,
                 h_tail=h_ctx if l == 0 else None)

        q, k, nq, nk, v, nv, z, x0 = _inproj(h, mods, base, g_mix[l].reshape(1, d), l, w_in_p, gvec, headmean, cos_t, sin_t,
                                             conv_w[l], conv_b[l].reshape(1, -1), tiles_per_batch, n_batch, seq, n_ctx)

        out_rows = n_lat if last else n_all
        ya = _gqa(q, k, v, n_batch, seq, n_ctx, latent=True)
        yc = _na_latent(nq, nk, nv, na_tabs, l, na_masks, n_batch, seq, n_ctx)

        fargs = (filt_w1[l], filt_b1[l].reshape(1, -1), filt_w2[l], filt_b2[l].reshape(1, -1), filt_w3[l],
                 filt_freq[l].reshape(1, -1), deltas)
        kfull, l1 = _hfilter(seq, *fargs)
        khat = _filter_spectrum(kfull, tabs, n1_len)
        yconv = _long_conv(z, n_batch, khat, 1.0 / (l1 * (2 * seq)), tabs, n1_len)

        if not last:
            ya = jnp.concatenate([ya, _gqa(q, k, v, n_batch, seq, n_ctx, latent=False)], axis=0)
            yc = jnp.concatenate([yc, _gqa(nq, nk, nv, n_batch, seq, n_ctx, latent=False, kv_chunked=True)], axis=0)
            kfull_c, l1_c = _hfilter(n_ctx, *fargs)
            yconv = jnp.concatenate([yconv, _hyena_ctx(z, kfull_c, l1_c, n_lat // n_ctx, n_batch, n_ctx)], axis=0)

        h = _ffn(h, mods, base, (6, 7, 8), g_ffn2[l].reshape(1, d), l, *ffn2_w, out_rows, tiles_per_batch, n_batch,
                 mixer=(ya, yconv, z, x0, hyena_skip[l].reshape(1, hy_w), yc, wa, wb, wc))

    return h[:n_lat].reshape(n_batch, seq, d)
```

```python
import functools
import math

import numpy as np
import jax
import jax.numpy as jnp
from jax import lax
from jax.experimental import pallas as pl
from jax.experimental.pallas import tpu as pltpu

F32 = jnp.float32
BF16 = jnp.bfloat16
HIGHEST = lax.Precision.HIGHEST

D_MODEL = 1024
HEAD_DIM = 64
GRID_W = 64
ATTN_HEADS = 8
ATTN_KV_HEADS = 2
NA_HEADS = 4
NA_ROWS = 8
NA_COLS = 16
FILTER_BANDS = 16
DECAY_TARGET = 1e-2
FAST_DECAY_PCT = 0.3
SLOW_DECAY_PCT = 1.5
ROPE_THETA = 10000.0
EPS = 1e-6
N_MOD = 9

LANES = 128
TOKEN_TILE = 512
FFN_CHUNK = 256
ATTN_TQ = 1024
ATTN_TK = 2048
NA_BLOCK_ROWS = 32
NA_SUB_ROWS = 4
NA_KEY_ROWS = 12
DFT_N2 = 128
OUTER_COLS = 8
INNER_K1 = 16
NEG_BIG = -1e30
VMEM_LIMIT = 56 * 1024 * 1024


def _cparams(sem):
    return pltpu.CompilerParams(dimension_semantics=sem, vmem_limit_bytes=VMEM_LIMIT)


def _ada_kernel(c_ref, w_ref, b_ref, o_ref):
    c = c_ref[...]
    s = c * jax.nn.sigmoid(c)
    o_ref[0] = jnp.dot(s, w_ref[0], preferred_element_type=F32) + b_ref[0]


def _adaln_all(cond8, w_ada, b_ada):
    depth, d, nd = w_ada.shape
    tn = 2304
    return pl.pallas_call(
        _ada_kernel,
        grid=(depth, nd // tn),
        in_specs=[
            pl.BlockSpec((8, d), lambda l, j: (0, 0)),
            pl.BlockSpec((1, d, tn), lambda l, j: (l, 0, j)),
            pl.BlockSpec((1, 1, tn), lambda l, j: (l, 0, j)),
        ],
        out_specs=pl.BlockSpec((1, 8, tn), lambda l, j: (l, 0, j)),
        out_shape=jax.ShapeDtypeStruct((depth, 8, nd), F32),
        compiler_params=_cparams(("arbitrary", "arbitrary")),
        name="adaln",
    )(cond8, w_ada, b_ada.reshape(depth, 1, nd))


def _norm_mod(x, g, sh, sc):
    ms = jnp.mean(x * x, axis=-1, keepdims=True)
    y = x * lax.rsqrt(ms + EPS) * g
    return y * (1.0 + sc) + sh


def _mod_specs(tiles_per_batch, n_batch, base, idxs):
    def spec(which):
        return pl.BlockSpec(
            (1, 1, D_MODEL),
            lambda i, w=which: (base + jnp.minimum(i // tiles_per_batch, n_batch) * N_MOD + w, 0, 0))
    return [spec(w) for w in idxs]


def _ffn_kernel(h_ref, *refs, with_mixer, split_tile, mix_split):
    x = h_ref[...]
    if split_tile is not None:
        x = jnp.where(pl.program_id(0) < split_tile, x, refs[0][...])
        refs = refs[1:]
    if with_mixer:
        gt2_ref, ya_ref, yconv_ref, z_ref, x0_ref, skip_ref, yc_ref, wa_ref, wb_ref, wc_ref = refs[:10]
        refs = refs[10:]
        ya, yconv, yc = ya_ref[...], yconv_ref[...], yc_ref[...]
        if mix_split is not None:
            latent = pl.program_id(0) < mix_split
            ya = jnp.where(latent, ya, refs[0][...])
            yconv = jnp.where(latent, yconv, refs[1][...])
            yc = jnp.where(latent, yc, refs[2][...])
            refs = refs[3:]
        yb = ((yconv + z_ref[...] * skip_ref[...]) * x0_ref[...]).astype(BF16)
        x = x + gt2_ref[0] * (jnp.dot(ya, wa_ref[0], preferred_element_type=F32)
                              + jnp.dot(yb, wb_ref[0], preferred_element_type=F32)
                              + jnp.dot(yc, wc_ref[0], preferred_element_type=F32))
    sh_ref, sc_ref, gt_ref, g_ref, wg_ref, wu_ref, wd_ref, o_ref, xb_ref, acc_ref = refs
    x_is_new = with_mixer or split_tile is not None
    if x_is_new:
        o_ref[...] = x
    xb_ref[...] = _norm_mod(x, g_ref[...], sh_ref[0], sc_ref[0]).astype(BF16)

    for k in range(wg_ref.shape[2] // FFN_CHUNK):
        cols = slice(k * FFN_CHUNK, (k + 1) * FFN_CHUNK)
        xb = xb_ref[...]
        a = jnp.dot(xb, wg_ref[0, :, cols], preferred_element_type=F32)
        u = jnp.dot(xb, wu_ref[0, :, cols], preferred_element_type=F32)
        act = (a * jax.nn.sigmoid(a) * u).astype(BF16)
        part = jnp.dot(act, wd_ref[0, cols, :], preferred_element_type=F32)
        if k == 0:
            acc_ref[...] = part
        else:
            acc_ref[...] += part
    resid = o_ref[...] if x_is_new else h_ref[...]
    o_ref[...] = resid + 0.5 * gt_ref[0] * acc_ref[...]


def _ffn(h, mods, mod_base, which, g, layer, wg, wu, wd, n_rows, tiles_per_batch, n_batch, mixer=None, h_tail=None,
         mixer_tail=None):
    d = h.shape[1]
    of_layer = lambda a: pl.BlockSpec((1,) + a.shape[1:], lambda i: (layer, 0, 0), pipeline_mode=pl.Buffered(1))
    tm = TOKEN_TILE
    row = lambda i: (i, 0)
    const = lambda i: (0, 0)
    rows = lambda a: pl.BlockSpec((tm, a.shape[1]), row)
    full = lambda a: pl.BlockSpec(a.shape, const)
    mix_specs, mix_args, mix_split = [], (), None
    if mixer is not None:
        ya, yconv, z, x0, skip, yc, wa, wb, wc = mixer
        head = rows
        if mixer_tail is not None:
            mix_split = ya.shape[0] // tm
            head = lambda a: pl.BlockSpec((tm, a.shape[1]), lambda i: (jnp.minimum(i, mix_split - 1), 0))
        mix_specs = _mod_specs(tiles_per_batch, n_batch, mod_base, (5,)) + [
            head(ya), head(yconv), rows(z), rows(x0), full(skip), head(yc), of_layer(wa), of_layer(wb), of_layer(wc)]
        mix_args = (mods,) + tuple(mixer)
        if mixer_tail is not None:
            mix_specs += [pl.BlockSpec((tm, a.shape[1]), lambda i: (jnp.maximum(i - mix_split, 0), 0))
                          for a in mixer_tail]
            mix_args += tuple(mixer_tail)
    h_specs, h_args, split = [rows(h)], (h,), None
    if h_tail is not None:
        split = h.shape[0] // tm
        h_specs = [pl.BlockSpec((tm, d), lambda i: (jnp.minimum(i, split - 1), 0)),
                   pl.BlockSpec((tm, d), lambda i: (jnp.maximum(i - split, 0), 0))]
        h_args = (h, h_tail)
    return pl.pallas_call(
        functools.partial(_ffn_kernel, with_mixer=mixer is not None, split_tile=split, mix_split=mix_split),
        grid=(n_rows // tm,),
        in_specs=h_specs + mix_specs + _mod_specs(tiles_per_batch, n_batch, mod_base, which)
        + [full(g), of_layer(wg), of_layer(wu), of_layer(wd)],
        out_specs=pl.BlockSpec((tm, d), row),
        out_shape=jax.ShapeDtypeStruct((n_rows, d), F32),
        scratch_shapes=[pltpu.VMEM((tm, d), BF16), pltpu.VMEM((tm, d), F32)],
        compiler_params=_cparams(("arbitrary",)),
        name="ffn_mix" if mixer is not None else "ffn",
    )(*h_args, *mix_args, mods, mods, mods, g, wg, wu, wd)


N_NORM = 1152
N_ROPE = 640
ATTN_Q_CHUNKS = (0, 1, 2, 3)
NA_Q_CHUNKS = (5, 6)
LOG2E = math.log2(math.e)


def _inproj_kernel(h_ref, prev_ref, next_ref, sh_ref, sc_ref, g_ref, w_ref, gv_ref, hm_ref, cos_ref, sin_ref,
                   cw_ref, cb_ref, q_ref, k_ref, nq_ref, nk_ref, v_ref, nv_ref, z_ref, x0_ref, *, seq, n_ctx, n_lat):
    xb = _norm_mod(h_ref[...], g_ref[...], sh_ref[0], sc_ref[0]).astype(BF16)
    hm = hm_ref[...]
    cosv = cos_ref[...]
    sinv = sin_ref[...]
    lane = lax.broadcasted_iota(jnp.int32, cosv.shape, 1)
    first_half = (lane % 32) < 16
    outs = []
    p_norm = jnp.dot(xb, w_ref[0, :, 0:N_NORM], preferred_element_type=F32)
    for c in range(N_NORM // LANES):
        p = p_norm[:, c * LANES:(c + 1) * LANES]
        ms = jnp.dot((p * p).astype(BF16), hm, preferred_element_type=F32)
        y = p * lax.rsqrt(ms + EPS) * gv_ref[:, c * LANES:(c + 1) * LANES]
        if c < N_ROPE // LANES:
            swapped = jnp.where(first_half, pltpu.roll(y, LANES - 16, 1), pltpu.roll(y, 16, 1))
            y = y * cosv + swapped * sinv
        if c in ATTN_Q_CHUNKS or c in NA_Q_CHUNKS:
            y = y * (HEAD_DIM ** -0.5 * LOG2E)
        outs.append(y.astype(BF16))
    for c in range(4):
        q_ref[:, c * LANES:(c + 1) * LANES] = outs[c]
    k_ref[...] = outs[4]
    nq_ref[:, 0:LANES] = outs[5]
    nq_ref[:, LANES:2 * LANES] = outs[6]
    nk_ref[:, 0:LANES] = outs[7]
    nk_ref[:, LANES:2 * LANES] = outs[8]
    p_rest = jnp.dot(xb, w_ref[0, :, N_NORM:], preferred_element_type=F32)
    v = p_rest[:, 0:LANES]
    v_ref[:, 0:LANES] = jnp.where(lane < HEAD_DIM, v, 1.0).astype(BF16)
    v_ref[:, LANES:2 * LANES] = jnp.where(lane < HEAD_DIM, 1.0, v).astype(BF16)
    for c in range(NA_HEADS // 2):
        nv = p_rest[:, (1 + c) * LANES:(2 + c) * LANES]
        nv_ref[:, 2 * c * LANES:(2 * c + 1) * LANES] = jnp.where(lane < HEAD_DIM, nv, 1.0).astype(BF16)
        nv_ref[:, (2 * c + 1) * LANES:(2 * c + 2) * LANES] = jnp.where(lane < HEAD_DIM, 1.0, nv).astype(BF16)

    hy = p_rest[:, 3 * LANES:]
    tm = hy.shape[0]
    halo = jnp.concatenate([prev_ref[...], next_ref[...]], axis=0)
    halo_b = _norm_mod(halo, g_ref[...], sh_ref[0], sc_ref[0]).astype(BF16)
    hy_halo = jnp.dot(halo_b, w_ref[0, :, N_NORM + 3 * LANES:], preferred_element_type=F32)
    row0 = pl.program_id(0) * tm
    is_ctx = row0 >= n_lat
    period = jnp.where(is_ctx, n_ctx, seq)
    local = lax.broadcasted_iota(jnp.int32, (tm, 1), 0)
    pos = jnp.bitwise_and(row0 + local - jnp.where(is_ctx, n_lat, 0), period - 1)
    up = jnp.where(local == 0, hy_halo[7:8, :], pltpu.roll(hy, 1, 0))
    up = jnp.where(pos == 0, 0.0, up)
    dn = jnp.where(local == tm - 1, hy_halo[8:9, :], pltpu.roll(hy, tm - 1, 0))
    dn = jnp.where(pos == period - 1, 0.0, dn)
    uc = up * cw_ref[0:1, :] + hy * cw_ref[1:2, :] + dn * cw_ref[2:3, :] + cb_ref[...]
    cw = uc.shape[1] // 3
    x0_ref[...] = uc[:, :cw]
    z_ref[...] = uc[:, 2 * cw:] * uc[:, cw:2 * cw]


def _inproj(h, mods, mod_base, g, layer, w, gvec, headmean, cos_t, sin_t, conv_w, conv_b, tiles_per_batch, n_batch, seq,
            n_ctx):
    n_rows, d = h.shape
    tm = TOKEN_TILE
    assert seq & (seq - 1) == 0 and n_ctx & (n_ctx - 1) == 0
    row = lambda i: (i, 0)
    const = lambda i: (0, 0)
    widths = (512, 128, 256, 256, 256, 512)
    hy_w = conv_w.shape[1] // 3
    out_shape = [jax.ShapeDtypeStruct((n_rows, wd), BF16) for wd in widths]
    out_shape += [jax.ShapeDtypeStruct((n_rows, hy_w), F32)] * 2
    out_specs = [pl.BlockSpec((tm, wd), row) for wd in widths] + [pl.BlockSpec((tm, hy_w), row)] * 2
    return pl.pallas_call(
        functools.partial(_inproj_kernel, seq=seq, n_ctx=n_ctx, n_lat=n_batch * seq),
        grid=(n_rows // tm,),
        in_specs=[pl.BlockSpec((tm, d), row),
                  pl.BlockSpec((8, d), lambda i: (jnp.maximum(i * (tm // 8) - 1, 0), 0)),
                  pl.BlockSpec((8, d), lambda i: (jnp.minimum((i + 1) * (tm // 8), n_rows // 8 - 1), 0))]
        + _mod_specs(tiles_per_batch, n_batch, mod_base, (3, 4))
        + [
            pl.BlockSpec((1, d), const),
            pl.BlockSpec((1,) + w.shape[1:], lambda i: (layer, 0, 0), pipeline_mode=pl.Buffered(1)),
            pl.BlockSpec(gvec.shape, const),
            pl.BlockSpec(headmean.shape, const),
            pl.BlockSpec((tm, LANES), row),
            pl.BlockSpec((tm, LANES), row),
            pl.BlockSpec(conv_w.shape, const),
            pl.BlockSpec(conv_b.shape, const),
        ],
        out_specs=out_specs,
        out_shape=out_shape,
        compiler_params=_cparams(("arbitrary",)),
        name="inproj",
    )(h, h, h, mods, mods, g, w, gvec, headmean, cos_t, sin_t, conv_w, conv_b)


def _stack_heads(q):
    lane = lax.broadcasted_iota(jnp.int32, q.shape, 1)
    zero = jnp.zeros_like(q)
    return jnp.concatenate([jnp.where(lane < HEAD_DIM, q, zero), jnp.where(lane >= HEAD_DIM, q, zero)], axis=0)


def _qk(q2, k):
    return lax.dot_general(q2, k, (((1,), (1,)), ((), ())), preferred_element_type=F32)


def _gqa_kernel(q_ref, kc_ref, vc_ref, *rest, n_latent_chunks, tk):
    if n_latent_chunks:
        kl_ref, vl_ref, o_ref = rest
    else:
        (o_ref,) = rest
    tq = q_ref.shape[0]
    q2 = _stack_heads(q_ref[...])

    def pv(p, v_ref, rows):
        return jnp.concatenate(
            [jnp.dot(p[:tq], v_ref[rows, 0:LANES], preferred_element_type=F32),
             jnp.dot(p[tq:], v_ref[rows, LANES:2 * LANES], preferred_element_type=F32)], axis=0)

    s = _qk(q2, kc_ref[...])
    m = jnp.max(s, axis=-1, keepdims=True)
    if n_latent_chunks:
        rows0 = pl.ds(0, tk)
        s0 = _qk(q2, kl_ref[rows0, :])
        m = jnp.maximum(m, jnp.max(s0, axis=-1, keepdims=True))
        acc = (pv(jnp.exp2((s - m).astype(BF16)), vc_ref, slice(None))
               + pv(jnp.exp2((s0 - m).astype(BF16)), vl_ref, rows0))
    else:
        acc = pv(jnp.exp2((s - m).astype(BF16)), vc_ref, slice(None))

    if n_latent_chunks:
        def body(j, carry):
            m, acc = carry
            rows = pl.ds(pl.multiple_of(j * tk, tk), tk)
            s = _qk(q2, kl_ref[rows, :])
            m_new = jnp.maximum(m, jnp.max(s, axis=-1, keepdims=True))
            p = jnp.exp2((s - m_new).astype(BF16))
            return m_new, jnp.exp2(m - m_new) * acc + pv(p, vl_ref, rows)

        m, acc = lax.fori_loop(1, n_latent_chunks, body, (m, acc), unroll=True)
    lane = lax.broadcasted_iota(jnp.int32, (tq, LANES), 1)
    num = jnp.where(lane < HEAD_DIM, acc[:tq], acc[tq:])
    den = pltpu.roll(jnp.where(lane < HEAD_DIM, acc[tq:], acc[:tq]), HEAD_DIM, 1)
    o_ref[...] = (num / den).astype(o_ref.dtype)


def _gqa(q, k, v2, n_batch, seq, n_ctx, latent, kv_chunked=False):
    chunks = q.shape[1] // LANES
    ctx_blk0 = n_batch * seq // n_ctx
    tk = min(ATTN_TK, seq)
    kj = (lambda j: j) if kv_chunked else (lambda j: 0)
    if latent:
        tq = ATTN_TQ
        grid = (n_batch, chunks, seq // tq)
        qmap = omap = lambda b, j, i: (b * (seq // tq) + i, j)
        out_rows = n_batch * seq
        kv_specs = [pl.BlockSpec((seq, LANES), lambda b, j, i: (b, kj(j))),
                    pl.BlockSpec((seq, 2 * LANES), lambda b, j, i: (b, kj(j)))]
        kv_args = (k, v2)
    else:
        tq = n_ctx
        grid = (n_batch, chunks, 1)
        qmap = lambda b, j, i: (ctx_blk0 + b, j)
        omap = lambda b, j, i: (b, j)
        out_rows = n_batch * n_ctx
        kv_specs, kv_args = [], ()
    return pl.pallas_call(
        functools.partial(_gqa_kernel, n_latent_chunks=seq // tk if latent else 0, tk=tk),
        grid=grid,
        in_specs=[pl.BlockSpec((tq, LANES), qmap),
                  pl.BlockSpec((n_ctx, LANES), lambda b, j, i: (ctx_blk0 + b, kj(j))),
                  pl.BlockSpec((n_ctx, 2 * LANES), lambda b, j, i: (ctx_blk0 + b, kj(j)))] + kv_specs,
        out_specs=pl.BlockSpec((tq, LANES), omap),
        out_shape=jax.ShapeDtypeStruct((out_rows, q.shape[1]), BF16),
        compiler_params=_cparams(("arbitrary", "arbitrary", "arbitrary")),
        name="attn_latent" if latent else "attn_ctx",
    )(q, k, v2, *kv_args)


def _na_kernel(q_ref, kc_ref, vc_ref, kl_ref, vl_ref, tab_ref, mask_ref, o_ref, *, grid_rows):
    i = pl.program_id(2)
    nk = NA_KEY_ROWS * GRID_W
    n_sub = NA_BLOCK_ROWS // NA_SUB_ROWS
    last_sub = grid_rows // NA_SUB_ROWS - 1
    kc = kc_ref[...]
    sub_rows = NA_SUB_ROWS * GRID_W
    lane = lax.broadcasted_iota(jnp.int32, (sub_rows, LANES), 1)
    low = lane < HEAD_DIM
    for sub in range(n_sub):
        sb = i * n_sub + sub
        r0 = sb * NA_SUB_ROWS
        ks = jnp.clip(r0 - NA_ROWS // 2, 0, grid_rows - NA_KEY_ROWS)
        keys = pl.ds(pl.multiple_of(ks * GRID_W, GRID_W), nk)
        shift = ks - r0 + 2 * NA_ROWS
        variant = jnp.where(sb == 0, 0, jnp.where(sb == last_sub, 2, 1))
        k = kl_ref[keys, :]
        rows = slice(sub * sub_rows, (sub + 1) * sub_rows)
        q = q_ref[rows, :]
        acc = []
        for hh in range(2):
            qh = jnp.where(low if hh == 0 else jnp.logical_not(low), q, jnp.zeros_like(q))
            slabs = []
            for qr in range(NA_SUB_ROWS):
                pairs = [tab_ref[0, hh, shift - qr + 2 * kp] for kp in range(NA_KEY_ROWS // 2)]
                slabs.append(jnp.concatenate(pairs, axis=1) + mask_ref[variant, qr:qr + 1, :])
            s_nb = _qk(qh, k) + jnp.concatenate(slabs, axis=0)
            s_cx = _qk(qh, kc)
            m = jnp.maximum(jnp.max(s_nb, axis=-1, keepdims=True), jnp.max(s_cx, axis=-1, keepdims=True))
            vcols = slice(hh * LANES, (hh + 1) * LANES)
            acc.append(jnp.dot(jnp.exp2((s_nb - m).astype(BF16)), vl_ref[keys, vcols], preferred_element_type=F32)
                       + jnp.dot(jnp.exp2((s_cx - m).astype(BF16)), vc_ref[:, vcols], preferred_element_type=F32))
        num = jnp.where(low, acc[0], acc[1])
        den = pltpu.roll(jnp.where(low, acc[1], acc[0]), HEAD_DIM, 1)
        o_ref[rows, :] = (num / den).astype(o_ref.dtype)


def _na_bias_tables(rpb):
    qc = np.arange(GRID_W)[:, None]
    kc = np.arange(GRID_W)[None, :]
    cs = np.clip(qc - NA_COLS // 2, 0, GRID_W - NA_COLS)
    col_ok = (kc >= cs) & (kc < cs + NA_COLS)
    col_hot = ((kc - qc + NA_COLS - 1)[:, :, None] == np.arange(2 * NA_COLS - 1)) & col_ok[:, :, None]
    by_col = jnp.einsum("abe,hde->hdab", jnp.asarray(col_hot, F32), rpb, precision=HIGHEST)
    by_col = jnp.where(jnp.asarray(col_ok)[None, None], by_col * LOG2E, NEG_BIG)
    t = jnp.pad(by_col, ((0, 0), (NA_ROWS + 1, NA_ROWS), (0, 0), (0, 0)))
    return jnp.concatenate([t[:, :-1], t[:, 1:]], axis=-1)


def _na_row_masks(grid_rows):
    n_sub = grid_rows // NA_SUB_ROWS
    masks = []
    for sb in (0, min(1, n_sub - 1), n_sub - 1):
        qr = sb * NA_SUB_ROWS + np.arange(NA_SUB_ROWS)[:, None]
        ks = int(np.clip(sb * NA_SUB_ROWS - NA_ROWS // 2, 0, grid_rows - NA_KEY_ROWS))
        kr = ks + np.arange(NA_KEY_ROWS)[None, :]
        rs = np.clip(qr - NA_ROWS // 2, 0, grid_rows - NA_ROWS)
        row_ok = (kr >= rs) & (kr < rs + NA_ROWS)
        masks.append(np.repeat(np.where(row_ok, 0.0, NEG_BIG), GRID_W, axis=1))
    return jnp.asarray(np.stack(masks).astype(np.float32))


def _na_latent(nq, nk, nv2, tab, layer, masks, n_batch, seq, n_ctx):
    chunks = nq.shape[1] // LANES
    tq = NA_BLOCK_ROWS * GRID_W
    nblk = seq // tq
    ctx_blk0 = n_batch * seq // n_ctx
    qmap = lambda b, j, i: (b * nblk + i, j)
    tab_spec = pl.BlockSpec((1, 2) + tab.shape[2:], lambda b, j, i: (layer, j, 0, 0, 0))
    mask_spec = pl.BlockSpec(masks.shape, lambda b, j, i: (0, 0, 0))
    return pl.pallas_call(
        functools.partial(_na_kernel, grid_rows=seq // GRID_W),
        grid=(n_batch, chunks, nblk),
        in_specs=[pl.BlockSpec((tq, LANES), qmap),
                  pl.BlockSpec((n_ctx, LANES), lambda b, j, i: (ctx_blk0 + b, j)),
                  pl.BlockSpec((n_ctx, 2 * LANES), lambda b, j, i: (ctx_blk0 + b, j)),
                  pl.BlockSpec((seq, LANES), lambda b, j, i: (b, j)),
                  pl.BlockSpec((seq, 2 * LANES), lambda b, j, i: (b, j)),
                  tab_spec, mask_spec],
        out_specs=pl.BlockSpec((tq, LANES), qmap),
        out_shape=jax.ShapeDtypeStruct((n_batch * seq, nq.shape[1]), BF16),
        compiler_params=_cparams(("arbitrary", "arbitrary", "arbitrary")),
        name="na_latent",
    )(nq, nk, nv2, nk, nv2, tab, masks)


def _hfilt_kernel(ft_ref, tc_ref, w1t_ref, b1_ref, w2t_ref, b2_ref, w3_ref, fr_ref, d_ref, k_ref, l1_ref, *, n):
    i = pl.program_id(0)
    fr = fr_ref[...]
    h1 = jnp.sin(fr * (jnp.dot(w1t_ref[...], ft_ref[...], precision=HIGHEST, preferred_element_type=F32)
                       + b1_ref[...]))
    h2 = jnp.sin(fr * (jnp.dot(w2t_ref[...], h1, precision=HIGHEST, preferred_element_type=F32) + b2_ref[...]))
    contract0 = (((0,), (0,)), ((), ()))
    o = lax.dot_general(h2, w3_ref[...], contract0, precision=HIGHEST, preferred_element_type=F32)
    cw = o.shape[1] // 2
    t = o.shape[0]
    row = i * t + lax.broadcasted_iota(jnp.int32, (t, cw), 0)
    k = jnp.where(row < n, o[:, :cw], o[:, cw:]) * jnp.exp(-tc_ref[...] * d_ref[...])
    k = jnp.where(row == n, 0.0, k)
    k_ref[...] = k

    @pl.when(i == 0)
    def _():
        l1_ref[...] = jnp.zeros_like(l1_ref)

    l1_ref[...] += jnp.sum(jnp.abs(k), axis=0, keepdims=True)


def _filter_feats(n):
    rows = np.arange(2 * n)
    pos = np.where(rows <= n, rows, 2 * n - rows).astype(np.float32)
    pos = jnp.asarray(np.minimum(pos, n - 1))
    t = pos / max(n - 1, 1)
    bands = jnp.linspace(1e-4, FILTER_BANDS - 1, FILTER_BANDS, dtype=F32)
    ang = (2.0 * math.pi / n) * pos[None, :] * bands[:, None]
    feats = jnp.concatenate([t[None, :], jnp.cos(ang), -jnp.sin(ang)], axis=0)
    return jnp.pad(feats, ((0, LANES - feats.shape[0]), (0, 0))), t[:, None]


def _hfilter(n, w1, b1, w2, b2, w3, freq, deltas):
    feats_t, t_col = _filter_feats(n)
    w1t = jnp.pad(w1.T, ((0, 0), (0, LANES - w1.shape[0])))
    col = lambda a: a.reshape(-1, 1)
    rows = feats_t.shape[1]
    t = min(2048, rows)
    cw = w3.shape[1] // 2
    const = lambda i: (0, 0)
    full = lambda a: pl.BlockSpec(a.shape, const)
    args = (w1t, col(b1), w2.T, col(b2), w3, col(freq), jnp.abs(deltas))
    return pl.pallas_call(
        functools.partial(_hfilt_kernel, n=n),
        grid=(rows // t,),
        in_specs=[pl.BlockSpec((LANES, t), lambda i: (0, i)), pl.BlockSpec((t, 1), lambda i: (i, 0))]
        + [full(a) for a in args],
        out_specs=[pl.BlockSpec((t, cw), lambda i: (i, 0)), pl.BlockSpec((1, cw), const)],
        out_shape=[jax.ShapeDtypeStruct((rows, cw), F32), jax.ShapeDtypeStruct((1, cw), F32)],
        compiler_params=_cparams(("arbitrary",)),
        name="hyena_filter",
    )(feats_t, t_col, *args)


def _dft_outer_kernel(f_ref, x_ref, o_ref):
    c = x_ref.shape[-1]
    x = x_ref[...].reshape(f_ref.shape[1], c).astype(BF16)
    o_ref[...] = jnp.dot(f_ref[...], x, preferred_element_type=F32).astype(o_ref.dtype).reshape(o_ref.shape)


def _kron_table(fmat, cols):
    return jnp.asarray(np.kron(np.asarray(fmat, np.float32), np.eye(cols, dtype=np.float32)).astype(BF16))


def _outer_cols(dtype):
    return OUTER_COLS * 4 // jnp.dtype(dtype).itemsize


def _dft_outer_fwd(fmat, x3, nb, n1_len, out_dtype):
    k = fmat.shape[1]
    c = x3.shape[2]
    cols = _outer_cols(out_dtype)
    f = _kron_table(fmat, cols)
    return pl.pallas_call(
        _dft_outer_kernel,
        grid=(nb, DFT_N2 // cols),
        in_specs=[pl.BlockSpec(f.shape, lambda b, j: (0, 0), pipeline_mode=pl.Buffered(1)),
                  pl.BlockSpec((k, cols, c), lambda b, j: (b, j, 0))],
        out_specs=pl.BlockSpec((1, 2, n1_len, cols, c), lambda b, j: (b, 0, 0, j, 0)),
        out_shape=jax.ShapeDtypeStruct((nb, 2, n1_len, DFT_N2, c), out_dtype),
        compiler_params=_cparams(("arbitrary", "arbitrary")),
        name="dft_outer_fwd",
    )(f, x3)


def _dft_outer_inv(fmat, d5):
    nb, _, n1_len, _, c = d5.shape
    half = fmat.shape[0]
    cols = _outer_cols(d5.dtype)
    f = _kron_table(fmat, cols)
    return pl.pallas_call(
        _dft_outer_kernel,
        grid=(nb, DFT_N2 // cols),
        in_specs=[pl.BlockSpec(f.shape, lambda b, j: (0, 0), pipeline_mode=pl.Buffered(1)),
                  pl.BlockSpec((1, 2, n1_len, cols, c), lambda b, j: (b, 0, 0, j, 0))],
        out_specs=pl.BlockSpec((half, cols, c), lambda b, j: (b, j, 0)),
        out_shape=jax.ShapeDtypeStruct((nb * half, DFT_N2, c), F32),
        compiler_params=_cparams(("arbitrary", "arbitrary")),
        name="dft_outer_inv",
    )(f, d5)


def _inner_matrices(fc, fs, tc, ts):
    gr = fc * tc - fs * ts
    gi = -(fc * ts + fs * tc)
    return gr, gi


def _real_form(gr, gi):
    return jnp.concatenate([jnp.concatenate([gr, -gi], axis=1), jnp.concatenate([gi, gr], axis=1)], axis=0)


def _dft_inner_filter_kernel(fc_ref, fs_ref, tc_ref, ts_ref, b_ref, o_ref):
    fc, fs = fc_ref[...], fs_ref[...]
    n2, c = b_ref.shape[3], b_ref.shape[4]

    def body(kk, carry):
        g = _real_form(*_inner_matrices(fc, fs, tc_ref[kk], ts_ref[kk]))
        o_ref[kk] = jnp.dot(g.astype(BF16), b_ref[0, :, kk].reshape(2 * n2, c).astype(BF16),
                            preferred_element_type=F32)
        return carry

    lax.fori_loop(0, b_ref.shape[2], body, 0, unroll=2)


def _dft_inner_conv_kernel(fc_ref, fs_ref, tc_ref, ts_ref, b_ref, kh_ref, sc_ref, o_ref):
    fc, fs = fc_ref[...], fs_ref[...]
    nb, n2, c = b_ref.shape[0], b_ref.shape[3], b_ref.shape[4]
    scale = sc_ref[...]

    def body(kk, carry):
        gr, gi = _inner_matrices(fc, fs, tc_ref[kk], ts_ref[kk])
        g = _real_form(gr, gi).astype(BF16)
        gh = _real_form(gr.T, -gi.T).astype(BF16)
        kh = kh_ref[kk]
        kr, ki = kh[:n2], kh[n2:]
        for b in range(nb):
            x = jnp.dot(g, b_ref[b, :, kk].reshape(2 * n2, c).astype(BF16), preferred_element_type=F32)
            xr, xi = x[:n2], x[n2:]
            y = jnp.concatenate([(xr * kr - xi * ki) * scale, (xr * ki + xi * kr) * scale], axis=0)
            d = jnp.dot(gh, y.astype(BF16), preferred_element_type=F32)
            o_ref[b, :, kk] = d.astype(o_ref.dtype).reshape(2, n2, c)
        return carry

    lax.fori_loop(0, b_ref.shape[2], body, 0, unroll=2)


def _dft_tables(n1_len):
    n_total = n1_len * DFT_N2
    a = np.arange(DFT_N2)
    ang = 2.0 * np.pi * np.outer(a, a) / DFT_N2
    k1 = np.arange(n1_len)
    tw = 2.0 * np.pi * np.outer(k1, a) / n_total
    o = 2.0 * np.pi * np.outer(k1, k1) / n1_len
    f32 = lambda v: jnp.asarray(v.astype(np.float32))
    outer_fwd = np.concatenate([np.cos(o), -np.sin(o)], axis=0)
    outer_inv = np.concatenate([np.cos(o), -np.sin(o)], axis=1)
    return dict(
        fc=f32(np.cos(ang)), fs=f32(np.sin(ang)),
        tc=f32(np.cos(tw)).reshape(n1_len, 1, DFT_N2), ts=f32(np.sin(tw)).reshape(n1_len, 1, DFT_N2),
        outer_fwd_full=outer_fwd, outer_fwd_half=outer_fwd[:, :n1_len // 2], outer_inv_half=outer_inv[:n1_len // 2],
    )


def _inner_specs():
    sq = pl.BlockSpec((DFT_N2, DFT_N2), lambda k: (0, 0))
    tw = pl.BlockSpec((INNER_K1, 1, DFT_N2), lambda k: (k, 0, 0))
    return [sq, sq, tw, tw]


def _filter_spectrum(kfull, tabs, n1_len):
    c = kfull.shape[1]
    b5 = _dft_outer_fwd(tabs["outer_fwd_full"], kfull.reshape(n1_len, DFT_N2, c), 1, n1_len, F32)
    return pl.pallas_call(
        _dft_inner_filter_kernel,
        grid=(n1_len // INNER_K1,),
        in_specs=_inner_specs() + [pl.BlockSpec((1, 2, INNER_K1, DFT_N2, c), lambda k: (0, 0, k, 0, 0))],
        out_specs=pl.BlockSpec((INNER_K1, 2 * DFT_N2, c), lambda k: (k, 0, 0)),
        out_shape=jax.ShapeDtypeStruct((n1_len, 2 * DFT_N2, c), F32),
        compiler_params=_cparams(("arbitrary",)),
        name="dft_inner_filter",
    )(tabs["fc"], tabs["fs"], tabs["tc"], tabs["ts"], b5)


def _long_conv(z_all, nb, khat, scale, tabs, n1_len):
    c = z_all.shape[1]
    bz = _dft_outer_fwd(tabs["outer_fwd_half"], z_all.reshape(z_all.shape[0] // DFT_N2, DFT_N2, c), nb, n1_len, BF16)
    blk = pl.BlockSpec((nb, 2, INNER_K1, DFT_N2, c), lambda k: (0, 0, k, 0, 0))
    d = pl.pallas_call(
        _dft_inner_conv_kernel,
        grid=(n1_len // INNER_K1,),
        in_specs=_inner_specs() + [blk, pl.BlockSpec((INNER_K1, 2 * DFT_N2, c), lambda k: (k, 0, 0)),
                                   pl.BlockSpec((1, c), lambda k: (0, 0))],
        out_specs=blk,
        out_shape=jax.ShapeDtypeStruct(bz.shape, BF16),
        compiler_params=_cparams(("arbitrary",)),
        name="dft_inner_conv",
    )(tabs["fc"], tabs["fs"], tabs["tc"], tabs["ts"], bz, khat, scale)
    y3 = _dft_outer_inv(tabs["outer_inv_half"], d)
    return y3.reshape(y3.shape[0] * DFT_N2, c)


def _hyena_ctx_kernel(ff_ref, fi_ref, k_ref, l1_ref, z_ref, o_ref):
    ff = ff_ref[...]
    n = z_ref.shape[0]
    big = ff.shape[1]
    kh = jnp.dot(ff, k_ref[...], precision=HIGHEST, preferred_element_type=F32)
    x = jnp.dot(ff[:, :n], z_ref[...], precision=HIGHEST, preferred_element_type=F32)
    kr, ki = kh[:big], kh[big:]
    xr, xi = x[:big], x[big:]
    scale = 1.0 / (l1_ref[...] * big)
    y = jnp.concatenate([(xr * kr - xi * ki) * scale, (xr * ki + xi * kr) * scale], axis=0)
    o_ref[...] = jnp.dot(fi_ref[...], y, precision=HIGHEST, preferred_element_type=F32)


def _hyena_ctx(z, kfull, l1, row0_blk, n_batch, n):
    big = 2 * n
    a = np.arange(big)
    ang = 2.0 * np.pi * np.outer(a, a) / big
    ff = jnp.asarray(np.concatenate([np.cos(ang), -np.sin(ang)], axis=0).astype(np.float32))
    fi = jnp.asarray(np.concatenate([np.cos(ang), -np.sin(ang)], axis=1)[:n].astype(np.float32))
    c = z.shape[1]
    const = lambda b: (0, 0)
    return pl.pallas_call(
        _hyena_ctx_kernel,
        grid=(n_batch,),
        in_specs=[pl.BlockSpec(ff.shape, const), pl.BlockSpec(fi.shape, const), pl.BlockSpec(kfull.shape, const),
                  pl.BlockSpec((1, c), const), pl.BlockSpec((n, c), lambda b: (row0_blk + b, 0))],
        out_specs=pl.BlockSpec((n, c), lambda b: (b, 0)),
        out_shape=jax.ShapeDtypeStruct((n_batch * n, c), F32),
        compiler_params=_cparams(("arbitrary",)),
        name="hyena_ctx",
    )(ff, fi, kfull, l1, z)


def _rope_tables(seq, n_batch, n_ctx_rows):
    t = jnp.arange(seq, dtype=jnp.int32)
    rows = (t // GRID_W).astype(F32)
    cols = (t % GRID_W).astype(F32)
    nf = HEAD_DIM // 4
    inv = ROPE_THETA ** (-jnp.arange(nf, dtype=F32) / nf)
    ar, ac = rows[:, None] * inv, cols[:, None] * inv
    cos = jnp.concatenate([jnp.cos(ar), jnp.cos(ar), jnp.cos(ac), jnp.cos(ac)], axis=1)
    sin = jnp.concatenate([-jnp.sin(ar), jnp.sin(ar), -jnp.sin(ac), jnp.sin(ac)], axis=1)
    cos = jnp.tile(cos, (n_batch, LANES // HEAD_DIM))
    sin = jnp.tile(sin, (n_batch, LANES // HEAD_DIM))
    cos = jnp.concatenate([cos, jnp.ones((n_ctx_rows, LANES), F32)], axis=0)
    sin = jnp.concatenate([sin, jnp.zeros((n_ctx_rows, LANES), F32)], axis=0)
    return cos, sin


def _head_mean_matrix():
    a = np.arange(LANES)
    return jnp.asarray((a[:, None] // HEAD_DIM == a[None, :] // HEAD_DIM).astype(np.float32) / HEAD_DIM, dtype=BF16)


def kernel(x, c, ctx, c_ctx, w_ada, b_ada, g_ffn1, w_ffn1_gate, w_ffn1_up, w_ffn1_down, g_mix, w_in, w_out, g_q_attn, g_k_attn, conv_w, conv_b, filt_w1, filt_b1, filt_w2, filt_b2, filt_w3, filt_freq, hyena_skip, g_q_na, g_k_na, na_rpb, g_ffn2, w_ffn2_gate, w_ffn2_up, w_ffn2_down):
    n_batch, seq, d = x.shape
    n_ctx = ctx.shape[1]
    depth = w_ada.shape[0]
    assert d == D_MODEL and n_batch + 1 <= 8 and seq % TOKEN_TILE == 0 and seq % (NA_BLOCK_ROWS * GRID_W) == 0
    assert n_ctx == 256 and (n_batch * n_ctx) % TOKEN_TILE == 0
    n_lat = n_batch * seq
    n_all = n_lat + n_batch * n_ctx
    tiles_per_batch = seq // TOKEN_TILE
    hy_w = conv_w.shape[2] // 3
    n1_len = 2 * seq // DFT_N2

    cond8 = jnp.zeros((8, d), F32).at[:n_batch].set(c).at[n_batch].set(c_ctx)
    mods = _adaln_all(cond8, w_ada, b_ada).reshape(depth * 8 * N_MOD, 1, d)
    h = x.reshape(n_lat, d)
    h_ctx = ctx.reshape(n_batch * n_ctx, d)

    cos_t, sin_t = _rope_tables(seq, n_batch, n_batch * n_ctx)
    headmean = _head_mean_matrix()
    na_masks = _na_row_masks(seq // GRID_W)
    na_tabs = jax.vmap(_na_bias_tables)(na_rpb)
    tabs = _dft_tables(n1_len)
    deltas = jnp.linspace(math.log(DECAY_TARGET) / SLOW_DECAY_PCT, math.log(DECAY_TARGET) / FAST_DECAY_PCT,
                          hy_w, dtype=F32).reshape(1, hy_w)

    q_heads = [hd for pair in range(ATTN_HEADS // 2) for hd in (pair, pair + ATTN_HEADS // 2)]
    q_segs = [(hd * HEAD_DIM, (hd + 1) * HEAD_DIM) for hd in q_heads]
    in_segs = q_segs + [(512, 640), (1536, 1792), (1792, 2048), (640, 768), (2048, 2304), (768, 1536)]
    w_in_p = jnp.concatenate([w_in[:, :, a:b] for a, b in in_segs], axis=2).astype(BF16)
    wa = jnp.concatenate([w_out[:, a:b] for a, b in q_segs], axis=1).astype(BF16)
    wb = w_out[:, 512:512 + hy_w].astype(BF16)
    wc = w_out[:, 512 + hy_w:].astype(BF16)
    ffn1_w = tuple(w.astype(BF16) for w in (w_ffn1_gate, w_ffn1_up, w_ffn1_down))
    ffn2_w = tuple(w.astype(BF16) for w in (w_ffn2_gate, w_ffn2_up, w_ffn2_down))

    for l in range(depth):
        last = l == depth - 1
        base = l * 8 * N_MOD
        rep = lambda g, k: jnp.tile(g, k)
        gvec = jnp.concatenate([rep(g_q_attn[l], ATTN_HEADS), rep(g_k_attn[l], ATTN_KV_HEADS),
                                rep(g_q_na[l], NA_HEADS), rep(g_k_na[l], NA_HEADS)]).reshape(1, N_NORM)

        h = _ffn(h, mods, base, (0, 1, 2), g_ffn1[l].reshape(1, d), l, *ffn1_w, n_all, tiles_per_batch, n_batch,
                 h_tail=h_ctx if l == 0 else None)

        q, k, nq, nk, v, nv, z, x0 = _inproj(h, mods, base, g_mix[l].reshape(1, d), l, w_in_p, gvec, headmean, cos_t, sin_t,
                                             conv_w[l], conv_b[l].reshape(1, -1), tiles_per_batch, n_batch, seq, n_ctx)

        out_rows = n_lat if last else n_all
        ya = _gqa(q, k, v, n_batch, seq, n_ctx, latent=True)
        yc = _na_latent(nq, nk, nv, na_tabs, l, na_masks, n_batch, seq, n_ctx)

        fargs = (filt_w1[l], filt_b1[l].reshape(1, -1), filt_w2[l], filt_b2[l].reshape(1, -1), filt_w3[l],
                 filt_freq[l].reshape(1, -1), deltas)
        kfull, l1 = _hfilter(seq, *fargs)
        khat = _filter_spectrum(kfull, tabs, n1_len)
        yconv = _long_conv(z, n_batch, khat, 1.0 / (l1 * (2 * seq)), tabs, n1_len)

        ctx_out = None
        if not last:
            kfull_c, l1_c = _hfilter(n_ctx, *fargs)
            ctx_out = (_gqa(q, k, v, n_batch, seq, n_ctx, latent=False),
                       _hyena_ctx(z, kfull_c, l1_c, n_lat // n_ctx, n_batch, n_ctx),
                       _gqa(nq, nk, nv, n_batch, seq, n_ctx, latent=False, kv_chunked=True))

        h = _ffn(h, mods, base, (6, 7, 8), g_ffn2[l].reshape(1, d), l, *ffn2_w, out_rows, tiles_per_batch, n_batch,
                 mixer=(ya, yconv, z, x0, hyena_skip[l].reshape(1, hy_w), yc, wa, wb, wc), mixer_tail=ctx_out)

    return h[:n_lat].reshape(n_batch, seq, d)
```
